```python
import math
import jax
import jax.numpy as jnp
from jax import lax
import numpy as np

D_MODEL = 1024
BATCH = 8
SEQ = 2048
DEPTH = 4
DEC_BATCH = 128
DEC_SEQ = 4
PAST_LEN = 16384
PAGE_SIZE = 128

N_META = 16
D_MIX = D_MODEL
D_BRANCH = D_MIX // 4

SSD_HEAD_DIM = 64
SSD_HEADS = D_BRANCH // SSD_HEAD_DIM
SSD_STATE = 64
SSD_GROUPS = 2
SSD_CONV = 4
SSD_CHUNK = 128
SSD_XBC = D_BRANCH + 2 * SSD_GROUPS * SSD_STATE

S5_GROUP = 16
S5_GROUPS = D_BRANCH // S5_GROUP
S5_STATE = 64

GDN_HEAD_DIM = 64
GDN_HEADS = D_BRANCH // GDN_HEAD_DIM
GDN_CONV = 4
GDN_CHUNK = 64
GDN_QKV = 3 * D_BRANCH

HG_HEAD_DIM = 64
HG_HEADS = D_BRANCH // HG_HEAD_DIM
HG_CHUNK = 64

_IN_SIZES = (D_BRANCH, SSD_XBC, SSD_HEADS,
             D_BRANCH, D_BRANCH,
             D_BRANCH, GDN_QKV, GDN_HEADS, GDN_HEADS,
             D_BRANCH, D_BRANCH, D_BRANCH, D_BRANCH)
D_IN = 8 * D_BRANCH + SSD_XBC + GDN_QKV + SSD_HEADS + 2 * GDN_HEADS

DN_ALPHA = (2 * DEPTH) ** 0.25
DN_BETA = (8 * DEPTH) ** -0.25
LN_EPS = 1e-5
RMS_EPS = 1e-6
L2_EPS = 1e-6

kernel_name = 'hybrid_ssd_s5_gdn_hgrn2_step'


def _split_last(t, sizes):
    out = []
    start = 0
    for size in sizes:
        out.append(t[..., start:start + size])
        start += size
    return out


def _layernorm(x, g, b):
    xf = x.astype(jnp.float32)
    mu = jnp.mean(xf, -1, keepdims=True)
    var = jnp.mean(jnp.square(xf - mu), -1, keepdims=True)
    return ((xf - mu) * lax.rsqrt(var + LN_EPS) * g + b).astype(x.dtype)


def _rms_heads(y, g):
    return y * lax.rsqrt(jnp.mean(jnp.square(y), -1, keepdims=True) + RMS_EPS) * g


def _l2norm(t):
    return t * lax.rsqrt(jnp.sum(jnp.square(t), -1, keepdims=True) + L2_EPS)


def _causal_conv(x, buf, w, b):
    c = x.shape[-1]
    width = w.shape[0]
    xx = jnp.concatenate([buf.astype(x.dtype), x], axis=1)
    y = lax.conv_general_dilated(xx, w[:, None, :].astype(x.dtype), window_strides=(1,), padding='VALID',
                                 dimension_numbers=('NWC', 'WIO', 'NWC'), feature_group_count=c)
    return y + b, xx[:, xx.shape[1] - (width - 1):]


def _chunk_len(t_len, chunk):
    return chunk if t_len % chunk == 0 else t_len


def _with_prefix(fn, seqs, s0, prefix, chunk):
    if prefix > 0:
        y0, s = fn(*[a[:, :prefix] for a in seqs], s0, prefix)
        rest = [a[:, prefix:] for a in seqs]
        y1, s = fn(*rest, s, _chunk_len(rest[0].shape[1], chunk))
        return jnp.concatenate([y0, y1], axis=1), s
    return fn(*seqs, s0, _chunk_len(seqs[0].shape[1], chunk))


def _ssd_chunked(x, dt, a, bm, cm, s0, chunk):
    bsz, t_len, n_heads, _ = x.shape
    nc = t_len // chunk
    rep = n_heads // bm.shape[2]
    bh = jnp.repeat(bm, rep, axis=2)
    chh = jnp.repeat(cm, rep, axis=2)

    def ch(t):
        return t.reshape((bsz, nc, chunk) + t.shape[2:])

    xdt = ch(x * dt[..., None])
    bc, cc = ch(bh), ch(chh)
    acum = jnp.cumsum(ch(a), axis=2)
    incl = jnp.tril(jnp.ones((chunk, chunk), dtype=bool))
    seg = acum[:, :, :, None, :] - acum[:, :, None, :, :]
    lmat = jnp.exp(jnp.where(incl[:, :, None], seg, -jnp.inf))
    scores = jnp.einsum('bcihn,bcjhn->bcijh', cc, bc) * lmat
    y_diag = jnp.einsum('bcijh,bcjhp->bcihp', scores, xdt)
    to_end = jnp.exp(acum[:, :, -1:, :] - acum)
    chunk_state = jnp.einsum('bcjhn,bcjh,bcjhp->bchnp', bc, to_end, xdt)
    chunk_decay = jnp.exp(acum[:, :, -1, :])

    def step(s, inp):
        cs, cd = inp
        return s * cd[..., None, None] + cs, s

    s_fin, s_in = lax.scan(step, s0, (chunk_state.swapaxes(0, 1), chunk_decay.swapaxes(0, 1)))
    y_off = jnp.einsum('bcihn,cbhnp,bcih->bcihp', cc, s_in, jnp.exp(acum))
    return (y_diag + y_off).reshape(x.shape), s_fin


def _s5_scan(u, lam_re, lam_im, log_dt, b_re, b_im, c_re, c_im, d, s0_re, s0_im):
    dt = jnp.exp(log_dt)[:, None]
    mag = jnp.exp(lam_re * dt)
    ang = lam_im * dt
    lb_re = mag * jnp.cos(ang)
    lb_im = mag * jnp.sin(ang)
    den = jnp.square(lam_re) + jnp.square(lam_im)
    nr = lb_re - 1.0
    coef_re = (nr * lam_re + lb_im * lam_im) / den
    coef_im = (lb_im * lam_re - nr * lam_im) / den
    bb_re = coef_re[..., None] * b_re - coef_im[..., None] * b_im
    bb_im = coef_re[..., None] * b_im + coef_im[..., None] * b_re
    bu_re = jnp.einsum('btgq,gnq->btgn', u, bb_re)
    bu_im = jnp.einsum('btgq,gnq->btgn', u, bb_im)
    bu_re = bu_re.at[:, 0].add(lb_re * s0_re - lb_im * s0_im)
    bu_im = bu_im.at[:, 0].add(lb_re * s0_im + lb_im * s0_re)
    a_re = jnp.broadcast_to(lb_re, bu_re.shape)
    a_im = jnp.broadcast_to(lb_im, bu_im.shape)

    def combine(e1, e2):
        a1r, a1i, b1r, b1i = e1
        a2r, a2i, b2r, b2i = e2
        return (a2r * a1r - a2i * a1i, a2r * a1i + a2i * a1r,
                a2r * b1r - a2i * b1i + b2r, a2r * b1i + a2i * b1r + b2i)

    _, _, h_re, h_im = lax.associative_scan(combine, (a_re, a_im, bu_re, bu_im), axis=1)
    y = (jnp.einsum('gqn,btgn->btgq', c_re, h_re) - jnp.einsum('gqn,btgn->btgq', c_im, h_im)
         + d * u)
    return y, h_re[:, -1], h_im[:, -1]


def _gdn_chunked(q, k, v, beta, g, s0, chunk):
    bsz, t_len, n_heads, dk = q.shape
    dv = v.shape[-1]
    nc = t_len // chunk

    def ch(t):
        return t.reshape((bsz, nc, chunk) + t.shape[2:])

    q = ch(q * dk ** -0.5)
    k, v, beta = ch(k), ch(v), ch(beta)
    gc = jnp.cumsum(ch(g), axis=2)
    incl = jnp.tril(jnp.ones((chunk, chunk), dtype=bool))
    strict = jnp.tril(jnp.ones((chunk, chunk), dtype=bool), -1)
    diff = gc[:, :, :, None, :] - gc[:, :, None, :, :]
    dec = jnp.exp(jnp.where(incl[:, :, None], diff, -jnp.inf))
    kk = jnp.einsum('bcihd,bcjhd->bcijh', k, k)
    m = jnp.where(strict[:, :, None], kk * dec * beta[:, :, :, None, :], 0.0)
    m = m.transpose(0, 1, 4, 2, 3)
    eye = jnp.eye(chunk, dtype=m.dtype)
    tinv = lax.linalg.triangular_solve(eye + m, jnp.broadcast_to(eye, m.shape),
                                       left_side=True, lower=True, unit_diagonal=True)
    u_val = jnp.einsum('bchij,bcjhd->bcihd', tinv, v * beta[..., None])
    w_key = jnp.einsum('bchij,bcjhd->bcihd', tinv, k * (beta * jnp.exp(gc))[..., None])
    aq = jnp.einsum('bcihd,bcjhd->bcijh', q, k) * dec

    def step(s, inp):
        qi, ki, ui, wi, gi, ai = inp
        v_new = ui - jnp.einsum('blhk,bhkv->blhv', wi, s)
        o = (jnp.einsum('blhk,bhkv->blhv', qi * jnp.exp(gi)[..., None], s)
             + jnp.einsum('blsh,bshv->blhv', ai, v_new))
        gl = gi[:, -1]
        s = (s * jnp.exp(gl)[..., None, None]
             + jnp.einsum('bshk,bshv->bhkv', ki * jnp.exp(gl[:, None] - gi)[..., None], v_new))
        return s, o

    xs = tuple(t.swapaxes(0, 1) for t in (q, k, u_val, w_key, gc, aq))
    s_fin, o = lax.scan(step, s0, xs)
    return o.swapaxes(0, 1).reshape(bsz, t_len, n_heads, dv), s_fin


def _hgrn_chunked(q, k, v, logf, s0, chunk):
    bsz, t_len, n_heads, _ = q.shape
    dv = v.shape[-1]
    nc = t_len // chunk

    def ch(t):
        return t.reshape((bsz, nc, chunk) + t.shape[2:])

    gc = jnp.cumsum(ch(logf), axis=2)
    incl = jnp.tril(jnp.ones((chunk, chunk), dtype=bool))

    def step(s, inp):
        qi, ki, vi, gi = inp
        diff = gi[:, :, None] - gi[:, None]
        dec = jnp.exp(jnp.where(incl[:, :, None, None], diff, -jnp.inf))
        a = jnp.einsum('bihk,bjhk,bijhk->bijh', qi, ki, dec)
        o = (jnp.einsum('bijh,bjhv->bihv', a, vi)
             + jnp.einsum('bihk,bhkv->bihv', qi * jnp.exp(gi), s))
        gl = gi[:, -1]
        s = s * jnp.exp(gl)[..., None] + jnp.einsum('bjhk,bjhv->bhkv', ki * jnp.exp(gl[:, None] - gi), vi)
        return s, o

    xs = tuple(t.swapaxes(0, 1) for t in (ch(q), ch(k), ch(v), gc))
    s_fin, o = lax.scan(step, s0, xs)
    return o.swapaxes(0, 1).reshape(bsz, t_len, n_heads, dv), s_fin


def _hgrn_lower_bounds(lb_raw):
    p = jax.nn.softmax(lb_raw.astype(jnp.float32), axis=0)
    c = jnp.cumsum(p, axis=0)
    return c - c[0]


def _layer(x, st, lw, lb, prefix):
    lw = {k: (v if k in ('w_in', 'w_out') else v.astype(jnp.float32)) for k, v in lw.items()}
    s_ssd, c_ssd, s5r, s5i, s_gdn, c_gdn, s_hg = [s.astype(jnp.float32) for s in st]
    bsz, t_len, _ = x.shape
    u = jnp.einsum('btd,de->bte', x, lw['w_in']).astype(jnp.float32)
    (z_a, xbc, dt_raw, z_b, u_b, z_c, qkv, b_raw, a_raw,
     z_d, q_d, f_d, i_d) = _split_last(u, _IN_SIZES)

    xbc, c_ssd_new = _causal_conv(xbc, c_ssd, lw['ssd_conv_w'], lw['ssd_conv_b'])
    xbc = jax.nn.silu(xbc)
    xs, bm, cm = _split_last(xbc, (D_BRANCH, SSD_GROUPS * SSD_STATE, SSD_GROUPS * SSD_STATE))
    xs = xs.reshape(bsz, t_len, SSD_HEADS, SSD_HEAD_DIM)
    bm = bm.reshape(bsz, t_len, SSD_GROUPS, SSD_STATE)
    cm = cm.reshape(bsz, t_len, SSD_GROUPS, SSD_STATE)
    dt = jax.nn.softplus(dt_raw + lw['ssd_dt_bias'])
    a = -dt * jnp.exp(lw['ssd_a_log'])
    y, s_ssd_new = _with_prefix(_ssd_chunked, (xs, dt, a, bm, cm), s_ssd, prefix, SSD_CHUNK)
    y = y + lw['ssd_d'][:, None] * xs
    y_a = _rms_heads(y, lw['ssd_norm_g']).reshape(bsz, t_len, D_BRANCH) * jax.nn.silu(z_a)

    y5, s5r_new, s5i_new = _s5_scan(u_b.reshape(bsz, t_len, S5_GROUPS, S5_GROUP),
                                    lw['s5_lam_re'], lw['s5_lam_im'], lw['s5_log_dt'],
                                    lw['s5_b_re'], lw['s5_b_im'], lw['s5_c_re'], lw['s5_c_im'],
                                    lw['s5_d'], s5r, s5i)
    y5 = jax.nn.gelu(y5.reshape(bsz, t_len, D_BRANCH))
    y5 = y5 * jax.nn.sigmoid(jnp.einsum('btc,ce->bte', y5, lw['s5_glu_w']) + lw['s5_glu_b'])
    y_b = y5 * jax.nn.silu(z_b)

    qkv, c_gdn_new = _causal_conv(qkv, c_gdn, lw['gdn_conv_w'], lw['gdn_conv_b'])
    qkv = jax.nn.silu(qkv)
    q, k, v = [t.reshape(bsz, t_len, GDN_HEADS, GDN_HEAD_DIM)
               for t in _split_last(qkv, (D_BRANCH, D_BRANCH, D_BRANCH))]
    q, k = _l2norm(q), _l2norm(k)
    beta = jax.nn.sigmoid(b_raw)
    g = -jnp.exp(lw['gdn_a_log']) * jax.nn.softplus(a_raw + lw['gdn_dt_bias'])
    o, s_gdn_new = _with_prefix(_gdn_chunked, (q, k, v, beta, g), s_gdn, prefix, GDN_CHUNK)
    y_c = _rms_heads(o, lw['gdn_norm_g']).reshape(bsz, t_len, D_BRANCH) * jax.nn.silu(z_c)

    lb_h = lb.reshape(HG_HEADS, HG_HEAD_DIM)
    f_h = f_d.reshape(bsz, t_len, HG_HEADS, HG_HEAD_DIM)
    logf = jnp.logaddexp(jnp.log(lb_h), jnp.log1p(-lb_h) + jax.nn.log_sigmoid(f_h))
    k_h = (1.0 - lb_h) * jax.nn.sigmoid(-f_h)
    q_h = jax.nn.silu(q_d).reshape(bsz, t_len, HG_HEADS, HG_HEAD_DIM)
    v_h = i_d.reshape(bsz, t_len, HG_HEADS, HG_HEAD_DIM)
    o, s_hg_new = _with_prefix(_hgrn_chunked, (q_h, k_h, v_h, logf), s_hg, prefix, HG_CHUNK)
    y_d = _rms_heads(o, lw['hg_norm_g']).reshape(bsz, t_len, D_BRANCH) * jax.nn.silu(z_d)

    mix = jnp.concatenate([y_a, y_b, y_c, y_d], axis=-1).astype(x.dtype)
    out = jnp.einsum('btc,cd->btd', mix, lw['w_out'])
    x_new = _layernorm(DN_ALPHA * x + out, lw['ln_g'], lw['ln_b'])
    return x_new, (s_ssd_new, c_ssd_new, s5r_new, s5i_new, s_gdn_new, c_gdn_new, s_hg_new)


def setup_inputs(seed: int = 0) -> dict:
    key = jax.random.key(seed)
    keys = iter(jax.random.split(key, 64))
    f32 = jnp.float32
    L = DEPTH

    def nrm(shape, scale):
        return scale * jax.random.normal(next(keys), shape, f32)

    def unif(shape, lo, hi):
        return jax.random.uniform(next(keys), shape, f32, lo, hi)

    def dt_bias(shape):
        dt = jnp.exp(unif(shape, math.log(1e-3), math.log(1e-1)))
        return dt + jnp.log(-jnp.expm1(-dt))

    inp = {}
    inp['x_prompt'] = nrm((BATCH, SEQ, D_MODEL), 1.0)
    inp['x_sample'] = nrm((DEC_BATCH, DEC_SEQ, D_MODEL), 1.0)
    inp['state_ssd'] = nrm((L, DEC_BATCH, SSD_HEADS, SSD_STATE, SSD_HEAD_DIM), 0.1)
    inp['state_ssd_conv'] = nrm((L, DEC_BATCH, SSD_CONV - 1, SSD_XBC), 1.0)
    inp['state_s5_re'] = nrm((L, DEC_BATCH, S5_GROUPS, S5_STATE), 0.1)
    inp['state_s5_im'] = nrm((L, DEC_BATCH, S5_GROUPS, S5_STATE), 0.1)
    inp['state_gdn'] = nrm((L, DEC_BATCH, GDN_HEADS, GDN_HEAD_DIM, GDN_HEAD_DIM), 0.1)
    inp['state_gdn_conv'] = nrm((L, DEC_BATCH, GDN_CONV - 1, GDN_QKV), 1.0)
    inp['state_hgrn'] = nrm((L, DEC_BATCH, HG_HEADS, HG_HEAD_DIM, HG_HEAD_DIM), 0.1)
    inp['meta_tokens'] = nrm((N_META, D_MODEL), 1.0)
    inp['ln_in_g'] = 1.0 + nrm((D_MODEL,), 0.02)
    inp['ln_in_b'] = nrm((D_MODEL,), 0.02)
    inp['w_in'] = nrm((L, D_MODEL, D_IN), D_MODEL ** -0.5)
    inp['ssd_conv_w'] = nrm((L, SSD_CONV, SSD_XBC), SSD_CONV ** -0.5)
    inp['ssd_conv_b'] = nrm((L, SSD_XBC), 0.02)
    inp['ssd_dt_bias'] = dt_bias((L, SSD_HEADS))
    inp['ssd_a_log'] = jnp.log(unif((L, SSD_HEADS), 1.0, 16.0))
    inp['ssd_d'] = 1.0 + nrm((L, SSD_HEADS), 0.1)
    inp['ssd_norm_g'] = 1.0 + nrm((L, SSD_HEADS, SSD_HEAD_DIM), 0.02)
    inp['s5_lam_re'] = -0.5 + nrm((L, S5_GROUPS, S5_STATE), 0.01)
    inp['s5_lam_im'] = math.pi * jnp.arange(S5_STATE, dtype=f32) + nrm((L, S5_GROUPS, S5_STATE), 0.01)
    inp['s5_log_dt'] = unif((L, S5_GROUPS), math.log(1e-3), math.log(1e-1))
    inp['s5_b_re'] = nrm((L, S5_GROUPS, S5_STATE, S5_GROUP), (2 * S5_GROUP) ** -0.5)
    inp['s5_b_im'] = nrm((L, S5_GROUPS, S5_STATE, S5_GROUP), (2 * S5_GROUP) ** -0.5)
    inp['s5_c_re'] = nrm((L, S5_GROUPS, S5_GROUP, S5_STATE), S5_STATE ** -0.5)
    inp['s5_c_im'] = nrm((L, S5_GROUPS, S5_GROUP, S5_STATE), S5_STATE ** -0.5)
    inp['s5_d'] = nrm((L, S5_GROUPS, S5_GROUP), 1.0)
    inp['s5_glu_w'] = nrm((L, D_BRANCH, D_BRANCH), D_BRANCH ** -0.5)
    inp['s5_glu_b'] = nrm((L, D_BRANCH), 0.02)
    inp['gdn_conv_w'] = nrm((L, GDN_CONV, GDN_QKV), GDN_CONV ** -0.5)
    inp['gdn_conv_b'] = nrm((L, GDN_QKV), 0.02)
    inp['gdn_a_log'] = jnp.log(unif((L, GDN_HEADS), 1.0, 16.0))
    inp['gdn_dt_bias'] = dt_bias((L, GDN_HEADS))
    inp['gdn_norm_g'] = 1.0 + nrm((L, GDN_HEADS, GDN_HEAD_DIM), 0.02)
    inp['hg_lb_raw'] = nrm((L, D_BRANCH), 0.5)
    inp['hg_norm_g'] = 1.0 + nrm((L, HG_HEADS, HG_HEAD_DIM), 0.02)
    inp['w_out'] = nrm((L, D_MIX, D_MODEL), D_MIX ** -0.5 * DN_BETA)
    inp['ln_g'] = 1.0 + nrm((L, D_MODEL), 0.02)
    inp['ln_b'] = nrm((L, D_MODEL), 0.02)
    return inp


def reference(x_prompt, x_sample, state_ssd, state_ssd_conv, state_s5_re, state_s5_im, state_gdn,
              state_gdn_conv, state_hgrn, meta_tokens, ln_in_g, ln_in_b, w_in, ssd_conv_w, ssd_conv_b,
              ssd_dt_bias, ssd_a_log, ssd_d, ssd_norm_g, s5_lam_re, s5_lam_im, s5_log_dt, s5_b_re,
              s5_b_im, s5_c_re, s5_c_im, s5_d, s5_glu_w, s5_glu_b, gdn_conv_w, gdn_conv_b, gdn_a_log,
              gdn_dt_bias, gdn_norm_g, hg_lb_raw, hg_norm_g, w_out, ln_g, ln_b):
    f32 = jnp.float32
    bp = x_prompt.shape[0]
    lbs = _hgrn_lower_bounds(hg_lb_raw)
    meta = jnp.broadcast_to(meta_tokens[None].astype(x_prompt.dtype), (bp, N_META, D_MODEL))
    hp = _layernorm(jnp.concatenate([meta, x_prompt], axis=1), ln_in_g, ln_in_b)
    hs = _layernorm(x_sample, ln_in_g, ln_in_b)
    zero_st = (jnp.zeros((bp, SSD_HEADS, SSD_STATE, SSD_HEAD_DIM), f32),
               jnp.zeros((bp, SSD_CONV - 1, SSD_XBC), f32),
               jnp.zeros((bp, S5_GROUPS, S5_STATE), f32),
               jnp.zeros((bp, S5_GROUPS, S5_STATE), f32),
               jnp.zeros((bp, GDN_HEADS, GDN_HEAD_DIM, GDN_HEAD_DIM), f32),
               jnp.zeros((bp, GDN_CONV - 1, GDN_QKV), f32),
               jnp.zeros((bp, HG_HEADS, HG_HEAD_DIM, HG_HEAD_DIM), f32))
    p_states, s_states = [], []
    for l in range(DEPTH):
        lw = dict(w_in=w_in[l], ssd_conv_w=ssd_conv_w[l], ssd_conv_b=ssd_conv_b[l],
                  ssd_dt_bias=ssd_dt_bias[l], ssd_a_log=ssd_a_log[l], ssd_d=ssd_d[l],
                  ssd_norm_g=ssd_norm_g[l], s5_lam_re=s5_lam_re[l], s5_lam_im=s5_lam_im[l],
                  s5_log_dt=s5_log_dt[l], s5_b_re=s5_b_re[l], s5_b_im=s5_b_im[l], s5_c_re=s5_c_re[l],
                  s5_c_im=s5_c_im[l], s5_d=s5_d[l], s5_glu_w=s5_glu_w[l], s5_glu_b=s5_glu_b[l],
                  gdn_conv_w=gdn_conv_w[l], gdn_conv_b=gdn_conv_b[l], gdn_a_log=gdn_a_log[l],
                  gdn_dt_bias=gdn_dt_bias[l], gdn_norm_g=gdn_norm_g[l], hg_norm_g=hg_norm_g[l],
                  w_out=w_out[l], ln_g=ln_g[l], ln_b=ln_b[l])
        hp, stp = _layer(hp, zero_st, lw, lbs[l], N_META)
        st_in = (state_ssd[l], state_ssd_conv[l], state_s5_re[l], state_s5_im[l],
                 state_gdn[l], state_gdn_conv[l], state_hgrn[l])
        hs, sts = _layer(hs, st_in, lw, lbs[l], 0)
        p_states.append(stp)
        s_states.append(sts)
    p_ssd, p_ssd_conv, p_s5_re, p_s5_im, p_gdn, p_gdn_conv, p_hgrn = [
        jnp.stack([s[i] for s in p_states]) for i in range(7)]
    s_ssd, s_ssd_conv, s_s5_re, s_s5_im, s_gdn, s_gdn_conv, s_hgrn = [
        jnp.stack([s[i] for s in s_states]) for i in range(7)]
    y_prompt = hp[:, N_META:]
    y_sample = hs
    return (y_prompt, y_sample, p_ssd, p_ssd_conv, p_s5_re, p_s5_im, p_gdn, p_gdn_conv, p_hgrn,
            s_ssd, s_ssd_conv, s_s5_re, s_s5_im, s_gdn, s_gdn_conv, s_hgrn)
```

```python
import functools
import math

import numpy as np
import jax
import jax.numpy as jnp
from jax import lax
from jax.experimental import pallas as pl
from jax.experimental.pallas import tpu as pltpu

F32 = jnp.float32
BF16 = jnp.bfloat16

D_MODEL = 1024
DEPTH = 4
N_META = 16
D_BRANCH = 256
HEADS = 4
HEAD_DIM = 64
SSD_GROUPS = 2
SSD_XBC = 512
S5_GROUP = 16
S5_GROUPS = 16
S5_STATE = 64
S5_LANES = S5_GROUPS * S5_STATE
GDN_QKV = 768
CONV_W = 4
D_IN = 3340
DN_ALPHA = (2 * DEPTH) ** 0.25
LN_EPS = 1e-5
RMS_EPS = 1e-6
L2_EPS = 1e-6

CHUNK = 64
SUBLANES = 8
PAD_FRONT = CHUNK - N_META
WA, WB, WC, WD = 896, 512, 1152, 1024
W_IN_COLS = WA + WB + WC + WD
NROWS = 16
ROW_W = 768
VMEM_LIMIT = 56 * 1024 * 1024


def _dot(a, b):
    return jnp.dot(a.astype(BF16), b.astype(BF16), preferred_element_type=F32)


def _dot_nt(a, b):
    return lax.dot_general(a.astype(BF16), b.astype(BF16), (((1,), (1,)), ((), ())),
                           preferred_element_type=F32)


def _dot_tn(a, b):
    return lax.dot_general(a.astype(BF16), b.astype(BF16), (((0,), (0,)), ((), ())),
                           preferred_element_type=F32)


def _dot_hi(a, b):
    return jnp.dot(a, b, precision=lax.Precision.HIGHEST, preferred_element_type=F32)


def _split3(x):
    x1 = x.astype(BF16)
    r1 = x - x1.astype(F32)
    x2 = r1.astype(BF16)
    x3 = (r1 - x2.astype(F32)).astype(BF16)
    return x1, x2, x3


def _dot01_l(w01, x):
    x1, x2, x3 = _split3(x)
    d = lambda v: jnp.dot(w01, v, preferred_element_type=F32)
    return d(x1) + d(x2) + d(x3)


def _dot01_r(x, w01):
    x1, x2, x3 = _split3(x)
    d = lambda v: jnp.dot(v, w01, preferred_element_type=F32)
    return d(x1) + d(x2) + d(x3)


def _dot01_nt(w01, x):
    x1, x2, x3 = _split3(x)
    d = lambda v: lax.dot_general(w01, v, (((1,), (1,)), ((), ())), preferred_element_type=F32)
    return d(x1) + d(x2) + d(x3)


def _sigmoid(x):
    return 1.0 / (1.0 + jnp.exp(-x))


def _silu(x):
    return x * _sigmoid(x)


def _softplus(x):
    return jnp.maximum(x, 0.0) + jnp.log1p(jnp.exp(-jnp.abs(x)))


def _gelu_tanh(x):
    c = math.sqrt(2.0 / math.pi)
    return 0.5 * x * (1.0 + jnp.tanh(c * (x + 0.044715 * (x * x * x))))


def _apply_state(x, s, ns):
    if ns == 1:
        return _dot(x, s[0])
    xb = x.reshape(ns, CHUNK // ns, x.shape[-1])
    r = lax.dot_general(xb.astype(BF16), s.astype(BF16), (((2,), (1,)), ((0,), (0,))),
                        preferred_element_type=F32)
    return r.reshape(CHUNK, s.shape[-1])


def _apply_state_t(x, st, ns):
    if ns == 1:
        return _dot_nt(x, st[0])
    xb = x.reshape(ns, CHUNK // ns, x.shape[-1])
    r = lax.dot_general(xb.astype(BF16), st.astype(BF16), (((2,), (2,)), ((0,), (0,))),
                        preferred_element_type=F32)
    return r.reshape(CHUNK, st.shape[1])


def _outer_state(a, b, ns):
    if ns == 1:
        return _dot_tn(a, b)[None]
    ls = CHUNK // ns
    ab = jnp.swapaxes(a.reshape(ns, ls, a.shape[-1]), 1, 2)
    bb = b.reshape(ns, ls, b.shape[-1])
    return lax.dot_general(ab.astype(BF16), bb.astype(BF16), (((2,), (1,)), ((0,), (0,))),
                           preferred_element_type=F32)


def _per_seq_rows(x, ns):
    if ns == 1:
        return x[0:1][None]
    return x.reshape(ns, CHUNK // ns, x.shape[-1])[:, 0:1, :]


def _layernorm_rows(r, g, b):
    mu = jnp.mean(r, axis=-1, keepdims=True)
    c = r - mu
    var = jnp.mean(c * c, axis=-1, keepdims=True)
    return c * lax.rsqrt(var + LN_EPS) * g + b


def _ln_in_kernel(x_ref, g_ref, b_ref, o_ref):
    o_ref[...] = _layernorm_rows(x_ref[...], g_ref[...], b_ref[...])


def _inproj_kernel(x_ref, w_ref, oa_ref, ob_ref, oc_ref, od_ref):
    x = x_ref[...].astype(BF16)
    off = 0
    for o_ref, width in ((oa_ref, WA), (ob_ref, WB), (oc_ref, WC), (od_ref, WD)):
        o_ref[...] = jnp.dot(x, w_ref[:, off:off + width], preferred_element_type=F32)
        off += width


def _outproj_kernel(mix_ref, x_ref, w_ref, g_ref, b_ref, o_ref):
    out = jnp.dot(mix_ref[...], w_ref[...], preferred_element_type=F32)
    o_ref[...] = _layernorm_rows(DN_ALPHA * x_ref[...] + out, g_ref[...], b_ref[...])


def _row_tiled_call(kernel, n_rows, tm, row_inputs, const_inputs, out_widths, out_dtypes, name):
    in_specs = [pl.BlockSpec((tm, a.shape[1]), lambda i: (i, 0)) for a in row_inputs]
    in_specs += [pl.BlockSpec(a.shape, lambda i, nd=a.ndim: (0,) * nd) for a in const_inputs]
    out_specs = [pl.BlockSpec((tm, w), lambda i: (i, 0)) for w in out_widths]
    out_shape = [jax.ShapeDtypeStruct((n_rows, w), dt) for w, dt in zip(out_widths, out_dtypes)]
    single = len(out_widths) == 1
    res = pl.pallas_call(
        kernel,
        grid=(n_rows // tm,),
        in_specs=in_specs,
        out_specs=out_specs[0] if single else out_specs,
        out_shape=out_shape[0] if single else out_shape,
        compiler_params=pltpu.CompilerParams(dimension_semantics=("arbitrary",),
                                             vmem_limit_bytes=VMEM_LIMIT),
        name=name,
    )(*row_inputs, *const_inputs)
    return res


def _mixer_kernel(*refs, prompt, rows, ns, nsb):
    ls = CHUNK // ns
    lb = rows // nsb
    nch = rows // CHUNK
    nlev = int(math.log2(ls))
    n_in = 16 + (0 if prompt else 7)
    (ua_ref, ub_ref, uc_ref, ud_ref, rows_ref, cwa_ref, cwc_ref, wbre_ref, wbim_ref, cre_ref, cim_ref,
     glu_ref, tab_ref, cmat_ref, cmask_ref, emat_ref) = refs[:16]
    if not prompt:
        (i_ssd, i_ca, i_s5re, i_s5im, i_gdn, i_cc, i_hg) = refs[16:23]
    (mix_ref, o_ssd, o_ca, o_s5re, o_s5im, o_gdn, o_cc, o_hg) = refs[n_in:n_in + 8]
    (xxa_ref, xxc_ref, acta_ref, actc_ref, hre_ref, him_ref) = refs[n_in + 8:]

    j = pl.program_id(1)

    if prompt:
        @pl.when(j == 0)
        def _():
            o_ssd[...] = jnp.zeros_like(o_ssd)
            o_gdn[...] = jnp.zeros_like(o_gdn)
            o_hg[...] = jnp.zeros_like(o_hg)
            o_s5re[...] = jnp.zeros_like(o_s5re)
            o_s5im[...] = jnp.zeros_like(o_s5im)
            xxa_ref[:, 0:SUBLANES, :] = jnp.zeros((nsb, SUBLANES, SSD_XBC), F32)
            xxc_ref[:, 0:SUBLANES, :] = jnp.zeros((nsb, SUBLANES, GDN_QKV), F32)
    else:
        o_ssd[...] = i_ssd[...]
        o_gdn[...] = i_gdn[...]
        o_hg[...] = i_hg[...]
        xxa_ref[:, 0:SUBLANES, :] = i_ca[...]
        xxc_ref[:, 0:SUBLANES, :] = i_cc[...]

    def valid_mask(width):
        r = lax.broadcasted_iota(jnp.int32, (rows, width), 0)
        if prompt:
            return r >= jnp.where(j == 0, PAD_FRONT, 0)
        return (r & (SUBLANES - 1)) < (SUBLANES // 2)

    def row(i, w):
        return rows_ref[i:i + 1, 0:w]

    def conv_block(raw, xx_ref, cw_ref, bias, o_conv, act_ref, width):
        raw = jnp.where(valid_mask(width), raw, 0.0)
        raw3 = raw.reshape(nsb, lb, width)
        xx_ref[:, SUBLANES:SUBLANES + lb, :] = raw3
        acc = bias
        for w in range(CONV_W):
            lo = SUBLANES - (CONV_W - 1) + w
            acc = acc + cw_ref[w:w + 1, :] * xx_ref[:, lo:lo + lb, :]
        tail = raw3[:, lb - SUBLANES:lb, :]
        o_conv[...] = tail
        if prompt:
            xx_ref[:, 0:SUBLANES, :] = tail
        act_ref[...] = _silu(acc).reshape(rows, width)

    conv_block(ua_ref[:, 256:768], xxa_ref, cwa_ref, row(0, SSD_XBC), o_ca, acta_ref, SSD_XBC)
    conv_block(uc_ref[:, 256:1024], xxc_ref, cwc_ref, row(7, GDN_QKV), o_cc, actc_ref, GDN_QKV)

    u_b = jnp.where(valid_mask(D_BRANCH), ub_ref[:, 256:512], 0.0)
    hre_ref[...] = _dot(u_b, wbre_ref[...])
    him_ref[...] = _dot(u_b, wbim_ref[...])

    def s5_group(g, carry):
        r0 = pl.multiple_of(g * SUBLANES, SUBLANES)
        xr = hre_ref[pl.ds(r0, SUBLANES), :]
        xi = him_ref[pl.ds(r0, SUBLANES), :]
        for k, d in enumerate((1, 2, 4)):
            tr = tab_ref[k, :, 0:S5_LANES]
            ti = tab_ref[k, :, S5_LANES:2 * S5_LANES]
            sr = pltpu.roll(xr, d, axis=0)
            si = pltpu.roll(xi, d, axis=0)
            xr, xi = xr + tr * sr - ti * si, xi + tr * si + ti * sr
        if prompt:
            cr, ci = carry
        else:
            cr = jnp.broadcast_to(i_s5re[pl.ds(g, 1), :], (SUBLANES, S5_LANES))
            ci = jnp.broadcast_to(i_s5im[pl.ds(g, 1), :], (SUBLANES, S5_LANES))
        pr = tab_ref[3, :, 0:S5_LANES]
        pi = tab_ref[3, :, S5_LANES:2 * S5_LANES]
        xr, xi = xr + pr * cr - pi * ci, xi + pr * ci + pi * cr
        hre_ref[pl.ds(r0, SUBLANES), :] = xr
        him_ref[pl.ds(r0, SUBLANES), :] = xi
        if prompt:
            return (jnp.broadcast_to(xr[SUBLANES - 1:SUBLANES, :], (SUBLANES, S5_LANES)),
                    jnp.broadcast_to(xi[SUBLANES - 1:SUBLANES, :], (SUBLANES, S5_LANES)))
        return carry

    if prompt:
        cr, ci = lax.fori_loop(0, rows // SUBLANES, s5_group, (o_s5re[0], o_s5im[0]))
        o_s5re[0] = cr
        o_s5im[0] = ci
    else:
        lax.fori_loop(0, rows // SUBLANES, s5_group, 0)
        o_s5re[...] = hre_ref[...]
        o_s5im[...] = him_ref[...]

    y5 = (_dot(hre_ref[...], cre_ref[...]) - _dot(him_ref[...], cim_ref[...])
          + row(5, D_BRANCH) * ub_ref[:, 256:512])
    y5 = _gelu_tanh(y5)
    y5 = y5 * _sigmoid(_dot(y5, glu_ref[...]) + row(6, D_BRANCH))
    mix_ref[:, 256:512] = (y5 * _silu(ub_ref[:, 0:256])).astype(BF16)

    tri_b = cmat_ref[0]
    last_b = cmat_ref[1]
    tri_f = cmask_ref[0]
    strict_f = cmask_ref[1]
    eye_f = cmask_ref[2]
    e_lo = emat_ref[0:128, :]
    e_hi = emat_ref[128:256, :]
    blk = emat_ref[256:512, :]
    selr = emat_ref[512:520, :]

    def head_sum(x):
        return _dot01_r(x, blk)

    def chunk_body(c, _):
        r0 = pl.multiple_of(c * CHUNK, CHUNK)
        rs = pl.ds(r0, CHUNK)
        sq = pl.ds(0, 1) if prompt else pl.ds(c * ns, ns)
        rr = lax.broadcasted_iota(jnp.int32, (CHUNK, 128), 0) + r0
        if prompt:
            valid = rr >= jnp.where(j == 0, PAD_FRONT, 0)
        else:
            valid = (rr & (SUBLANES - 1)) < (SUBLANES // 2)

        xs = acta_ref[rs, 0:256]
        bm = acta_ref[rs, 256:384]
        cmm = acta_ref[rs, 384:512]
        dt = _softplus(ua_ref[rs, 768:896] + row(1, 128))
        dt = jnp.where(valid, dt, 0.0)
        a_s = -dt * row(2, 128)
        dt_full = _dot01_r(dt, e_lo)
        a_full = _dot01_r(a_s, e_lo)
        acum = _dot01_l(tri_b, a_full)
        acum_rows = _dot01_nt(selr, acum)
        alast = _dot01_l(last_b, acum)
        xdt = xs * dt_full
        ya = []
        for h in range(HEADS):
            g = h // (HEADS // SSD_GROUPS)
            hs = slice(h * HEAD_DIM, (h + 1) * HEAD_DIM)
            gs = slice(g * HEAD_DIM, (g + 1) * HEAD_DIM)
            bg, cg = bm[:, gs], cmm[:, gs]
            seg = acum[:, hs] - acum_rows[h:h + 1, :]
            lmat = jnp.exp(jnp.minimum(seg, 0.0)) * tri_f
            s_h = o_ssd[sq, h]
            y_h = _dot(_dot_nt(cg, bg) * lmat, xdt[:, hs])
            y_h = y_h + _apply_state(cg, s_h, ns) * jnp.exp(acum[:, hs])
            to_end = jnp.exp(alast[:, hs] - acum[:, hs])
            s_new = (s_h * jnp.exp(_per_seq_rows(alast[:, hs], ns))
                     + _outer_state(bg * to_end, xdt[:, hs], ns))
            o_ssd[sq, h] = s_new
            ya.append(y_h)
        ya = jnp.concatenate(ya, axis=-1) + row(3, D_BRANCH) * xs
        ya = ya * lax.rsqrt(head_sum(ya * ya) * (1.0 / HEAD_DIM) + RMS_EPS) * row(4, D_BRANCH)
        mix_ref[rs, 0:256] = (ya * _silu(ua_ref[rs, 0:256])).astype(BF16)

        q = actc_ref[rs, 0:256]
        k = actc_ref[rs, 256:512]
        v = actc_ref[rs, 512:768]
        q = q * lax.rsqrt(head_sum(q * q) + L2_EPS) * (HEAD_DIM ** -0.5)
        k = k * lax.rsqrt(head_sum(k * k) + L2_EPS)
        t_ba = uc_ref[rs, 1024:1152] + row(8, 128)
        beta = jnp.where(valid, _sigmoid(t_ba), 0.0)
        gdec = jnp.where(valid, -_softplus(t_ba) * row(9, 128), 0.0)
        beta_full = _dot01_r(beta, e_lo)
        g_full = _dot01_r(gdec, e_hi)
        gc = _dot01_l(tri_b, g_full)
        gc_rows = _dot01_nt(selr, gc)
        glast = _dot01_l(last_b, gc)
        egc = jnp.exp(gc)
        kb = k * beta_full
        vb = v * beta_full
        kbe = kb * egc
        qe = q * egc
        kend = k * jnp.exp(glast - gc)
        yc = []
        for h in range(HEADS):
            hs = slice(h * HEAD_DIM, (h + 1) * HEAD_DIM)
            kh = k[:, hs]
            dec = jnp.exp(jnp.minimum(gc[:, hs] - gc_rows[h:h + 1, :], 0.0)) * tri_f
            m = _dot_nt(kh, kh) * dec * beta_full[:, hs] * strict_f
            p = eye_f - m
            mp = m
            for _lev in range(nlev - 1):
                mp = _dot_hi(mp, mp)
                p = p + _dot_hi(p, mp)
            u_val = _dot(p, vb[:, hs])
            w_key = _dot(p, kbe[:, hs])
            aq = _dot_nt(q[:, hs], kh) * dec
            s_h = o_gdn[sq, h]
            v_new = u_val - _apply_state(w_key, s_h, ns)
            o_h = _apply_state(qe[:, hs], s_h, ns) + _dot(aq, v_new)
            s_new = (s_h * jnp.exp(_per_seq_rows(glast[:, hs], ns))
                     + _outer_state(kend[:, hs], v_new, ns))
            o_gdn[sq, h] = s_new
            yc.append(o_h)
        yc = jnp.concatenate(yc, axis=-1)
        yc = yc * lax.rsqrt(head_sum(yc * yc) * (1.0 / HEAD_DIM) + RMS_EPS) * row(10, D_BRANCH)
        mix_ref[rs, 512:768] = (yc * _silu(uc_ref[rs, 0:256])).astype(BF16)

        valid_w = jnp.concatenate([valid, valid], axis=-1)
        f_d = ud_ref[rs, 512:768]
        lsig = -_softplus(-f_d)
        t1 = row(11, D_BRANCH)
        t2 = row(12, D_BRANCH) + lsig
        logf = jnp.maximum(t1, t2) + jnp.log1p(jnp.exp(-jnp.abs(t1 - t2)))
        logf = jnp.where(valid_w, logf, 0.0)
        kd = row(13, D_BRANCH) * _sigmoid(-f_d)
        qd = _silu(ud_ref[rs, 256:512])
        vd = jnp.where(valid_w, ud_ref[rs, 768:1024], 0.0)
        l1, l2, l3 = _split3(logf)
        l123 = jnp.concatenate([l1, l2, l3], axis=-1)
        xall = jnp.dot(cmat_ref[2:3 + nlev].reshape((1 + nlev) * CHUNK, CHUNK), l123,
                       preferred_element_type=F32)
        xall = xall[:, 0:256] + xall[:, 256:512] + xall[:, 512:768]
        gcd = xall[0:CHUNK, :]
        gld = _dot01_l(last_b, gcd)
        qed = qd * jnp.exp(gcd)
        kend_d = kd * jnp.exp(gld - gcd)
        diag = head_sum(qd * kd)
        amats = [jnp.zeros((CHUNK, CHUNK), F32) for _ in range(HEADS)]
        for lev in range(nlev):
            z = jnp.exp(xall[(1 + lev) * CHUNK:(2 + lev) * CHUNK, :])
            qz = qd * z
            kz = kd * z
            mk = cmask_ref[3 + lev]
            for h in range(HEADS):
                hs = slice(h * HEAD_DIM, (h + 1) * HEAD_DIM)
                amats[h] = amats[h] + _dot_nt(qz[:, hs], kz[:, hs]) * mk
        yd = []
        for h in range(HEADS):
            hs = slice(h * HEAD_DIM, (h + 1) * HEAD_DIM)
            st_h = o_hg[sq, h]
            o_h = _dot(amats[h], vd[:, hs]) + diag[:, hs] * vd[:, hs]
            o_h = o_h + _apply_state_t(qed[:, hs], st_h, ns)
            st_new = (st_h * jnp.exp(_per_seq_rows(gld[:, hs], ns))
                      + _outer_state(vd[:, hs], kend_d[:, hs], ns))
            o_hg[sq, h] = st_new
            yd.append(o_h)
        yd = jnp.concatenate(yd, axis=-1)
        yd = yd * lax.rsqrt(head_sum(yd * yd) * (1.0 / HEAD_DIM) + RMS_EPS) * row(14, D_BRANCH)
        mix_ref[rs, 768:1024] = (yd * _silu(ud_ref[rs, 0:256])).astype(BF16)
        return 0

    lax.fori_loop(0, nch, chunk_body, 0)


def _mixer_constants(ns):
    ls = CHUNK // ns
    nlev = int(math.log2(ls))
    i = np.arange(CHUNK)[:, None]
    jn = np.arange(CHUNK)[None, :]
    same = (i // ls) == (jn // ls)
    tri = (same & (jn <= i)).astype(np.float32)
    strict = (same & (jn < i)).astype(np.float32)
    eye = np.eye(CHUNK, dtype=np.float32)
    last = (jn == (i // ls) * ls + ls - 1).astype(np.float32)
    wlev, mlev = [], []
    for lev in range(nlev):
        b = ls >> (lev + 1)
        blk_i, pos_i = i // (2 * b), i % (2 * b)
        mid = blk_i * 2 * b + b
        upper = pos_i >= b
        w = np.where(upper, (jn >= mid) & (jn <= i), (jn > i) & (jn < mid)).astype(np.float32)
        msk = (((jn // (2 * b)) == blk_i) & upper & ((jn % (2 * b)) < b)).astype(np.float32)
        wlev.append(w)
        mlev.append(msk)
    cmat = np.stack([tri, last, tri] + wlev)
    cmask = np.stack([tri, strict, eye] + mlev)
    e_lo = np.zeros((128, 256), np.float32)
    e_hi = np.zeros((128, 256), np.float32)
    blk = np.zeros((256, 256), np.float32)
    selr = np.zeros((8, 256), np.float32)
    for h in range(HEADS):
        e_lo[h, h * HEAD_DIM:(h + 1) * HEAD_DIM] = 1.0
        e_hi[HEADS + h, h * HEAD_DIM:(h + 1) * HEAD_DIM] = 1.0
        blk[h * HEAD_DIM:(h + 1) * HEAD_DIM, h * HEAD_DIM:(h + 1) * HEAD_DIM] = 1.0
        selr[h, h * HEAD_DIM] = 1.0
    emat = np.concatenate([e_lo, e_hi, blk, selr], axis=0)
    return (jnp.asarray(cmat, BF16), jnp.asarray(cmask, F32), jnp.asarray(emat, BF16))


def _mixer_call(us, lw, consts, states, *, prompt, n_seq, rows):
    ua, ub, uc, ud = us
    n_rows = ua.shape[0]
    if prompt:
        ns, nsb = 1, 1
        nblk = n_rows // n_seq // rows
        grid = (n_seq, nblk)
        rmap = lambda b, j: (b * nblk + j, 0)
        smap4 = lambda b, j: (b, 0, 0, 0)
        smap3 = lambda b, j: (b, 0, 0)
    else:
        ns = CHUNK // SUBLANES
        nsb = rows // SUBLANES
        grid = (1, n_rows // rows)
        rmap = lambda b, j: (j, 0)
        smap4 = lambda b, j: (j, 0, 0, 0)
        smap3 = lambda b, j: (j, 0, 0)
    cmat, cmask, emat = consts
    const_inputs = [lw['rows'], lw['cw_a'], lw['cw_c'], lw['wb_re'], lw['wb_im'], lw['c_re'], lw['c_im'],
                    lw['glu_w'], lw['s5tab'], cmat, cmask, emat]
    in_specs = [pl.BlockSpec((rows, w), rmap) for w in (WA, WB, WC, WD)]
    in_specs += [pl.BlockSpec(a.shape, lambda b, j, nd=a.ndim: (0,) * nd) for a in const_inputs]
    inputs = [ua, ub, uc, ud] + const_inputs
    st4 = (nsb, HEADS, HEAD_DIM, HEAD_DIM)
    if not prompt:
        i_ssd, i_ca, i_s5re, i_s5im, i_gdn, i_cc, i_hg = states
        inputs += [i_ssd, i_ca, i_s5re, i_s5im, i_gdn, i_cc, i_hg]
        in_specs += [pl.BlockSpec(st4, smap4),
                     pl.BlockSpec((nsb, SUBLANES, SSD_XBC), smap3),
                     pl.BlockSpec((nsb, S5_LANES), rmap),
                     pl.BlockSpec((nsb, S5_LANES), rmap),
                     pl.BlockSpec(st4, smap4),
                     pl.BlockSpec((nsb, SUBLANES, GDN_QKV), smap3),
                     pl.BlockSpec(st4, smap4)]
    n_st = n_seq
    if prompt:
        s5_shape, s5_spec = (n_st, SUBLANES, S5_LANES), pl.BlockSpec((1, SUBLANES, S5_LANES), smap3)
    else:
        s5_shape, s5_spec = (n_rows, S5_LANES), pl.BlockSpec((rows, S5_LANES), rmap)
    out_shape = [jax.ShapeDtypeStruct((n_rows, D_MODEL), BF16),
                 jax.ShapeDtypeStruct((n_st,) + st4[1:], F32),
                 jax.ShapeDtypeStruct((n_st, SUBLANES, SSD_XBC), F32),
                 jax.ShapeDtypeStruct(s5_shape, F32),
                 jax.ShapeDtypeStruct(s5_shape, F32),
                 jax.ShapeDtypeStruct((n_st,) + st4[1:], F32),
                 jax.ShapeDtypeStruct((n_st, SUBLANES, GDN_QKV), F32),
                 jax.ShapeDtypeStruct((n_st,) + st4[1:], F32)]
    out_specs = [pl.BlockSpec((rows, D_MODEL), rmap),
                 pl.BlockSpec(st4, smap4),
                 pl.BlockSpec((nsb, SUBLANES, SSD_XBC), smap3),
                 s5_spec, s5_spec,
                 pl.BlockSpec(st4, smap4),
                 pl.BlockSpec((nsb, SUBLANES, GDN_QKV), smap3),
                 pl.BlockSpec(st4, smap4)]
    lb = rows // nsb
    scratch = [pltpu.VMEM((nsb, SUBLANES + lb, SSD_XBC), F32),
               pltpu.VMEM((nsb, SUBLANES + lb, GDN_QKV), F32),
               pltpu.VMEM((rows, SSD_XBC), F32),
               pltpu.VMEM((rows, GDN_QKV), F32),
               pltpu.VMEM((rows, S5_LANES), F32),
               pltpu.VMEM((rows, S5_LANES), F32)]
    return pl.pallas_call(
        functools.partial(_mixer_kernel, prompt=prompt, rows=rows, ns=ns, nsb=nsb),
        grid=grid,
        in_specs=in_specs,
        out_specs=out_specs,
        out_shape=out_shape,
        scratch_shapes=scratch,
        compiler_params=pltpu.CompilerParams(dimension_semantics=("arbitrary", "arbitrary"),
                                             vmem_limit_bytes=VMEM_LIMIT),
        name="mixer_prompt" if prompt else "mixer_sample",
    )(*inputs)


def _pad_row(v, width=ROW_W):
    v = v.astype(F32).reshape(-1)
    return jnp.pad(v, (0, width - v.shape[0]))


def _layer_weights(l, p, lbs):
    w = p['w_in'][l]
    z124 = jnp.zeros((D_MODEL, 124), F32)
    z120 = jnp.zeros((D_MODEL, 120), F32)
    w_r = jnp.concatenate([w[:, :772], z124, w[:, 772:2316], z120, w[:, 2316:]], axis=1).astype(BF16)

    rep = lambda v: jnp.repeat(v.astype(F32), HEAD_DIM)
    lb = lbs[l]
    gdn_bias = jnp.concatenate([jnp.zeros((HEADS,), F32), p['gdn_dt_bias'][l]])
    gdn_expa = jnp.concatenate([jnp.zeros((HEADS,), F32), jnp.exp(p['gdn_a_log'][l])])
    rows = jnp.stack([
        _pad_row(p['ssd_conv_b'][l]), _pad_row(p['ssd_dt_bias'][l]), _pad_row(jnp.exp(p['ssd_a_log'][l])),
        _pad_row(rep(p['ssd_d'][l])), _pad_row(p['ssd_norm_g'][l]), _pad_row(p['s5_d'][l]),
        _pad_row(p['s5_glu_b'][l]), _pad_row(p['gdn_conv_b'][l]), _pad_row(gdn_bias), _pad_row(gdn_expa),
        _pad_row(p['gdn_norm_g'][l]), _pad_row(jnp.log(lb)), _pad_row(jnp.log1p(-lb)), _pad_row(1.0 - lb),
        _pad_row(p['hg_norm_g'][l]), jnp.zeros((ROW_W,), F32)])

    pad_cw = lambda cw: jnp.pad(cw.astype(F32), ((0, SUBLANES - CONV_W), (0, 0)))

    lam_re, lam_im = p['s5_lam_re'][l], p['s5_lam_im'][l]
    dt = jnp.exp(p['s5_log_dt'][l])[:, None]
    mag = jnp.exp(lam_re * dt)
    ang = lam_im * dt
    lb_re, lb_im = mag * jnp.cos(ang), mag * jnp.sin(ang)
    den = jnp.square(lam_re) + jnp.square(lam_im)
    nr = lb_re - 1.0
    coef_re = (nr * lam_re + lb_im * lam_im) / den
    coef_im = (lb_im * lam_re - nr * lam_im) / den
    b_re, b_im = p['s5_b_re'][l], p['s5_b_im'][l]
    bb_re = coef_re[..., None] * b_re - coef_im[..., None] * b_im
    bb_im = coef_re[..., None] * b_im + coef_im[..., None] * b_re
    eye_g = jnp.eye(S5_GROUPS, dtype=F32)
    bd_in = lambda bb: jnp.einsum('gnq,gh->gqhn', bb, eye_g).reshape(D_BRANCH, S5_LANES).astype(BF16)
    bd_out = lambda c: jnp.einsum('gqn,gh->gnhq', c, eye_g).reshape(S5_LANES, D_BRANCH).astype(BF16)

    pr, pi = [lb_re.reshape(-1)], [lb_im.reshape(-1)]
    for _ in range(SUBLANES - 1):
        pr, pi = (pr + [pr[-1] * pr[0] - pi[-1] * pi[0]], pi + [pr[-1] * pi[0] + pi[-1] * pr[0]])
    pw = jnp.stack([jnp.concatenate([a, b]) for a, b in zip(pr, pi)])
    ridx = jnp.arange(SUBLANES)[:, None]
    tabs = [jnp.where(ridx >= d, pw[d - 1][None, :], 0.0) for d in (1, 2, 4)] + [pw]
    s5tab = jnp.stack(tabs).astype(F32)

    return dict(w_in=w_r, rows=rows, cw_a=pad_cw(p['ssd_conv_w'][l]), cw_c=pad_cw(p['gdn_conv_w'][l]),
                wb_re=bd_in(bb_re), wb_im=bd_in(bb_im), c_re=bd_out(p['s5_c_re'][l]),
                c_im=bd_out(p['s5_c_im'][l]), glu_w=p['s5_glu_w'][l].astype(BF16), s5tab=s5tab,
                w_out=p['w_out'][l].astype(BF16),
                ln_g=p['ln_g'][l].astype(F32)[None], ln_b=p['ln_b'][l].astype(F32)[None])


def _pick_tile(n_rows, candidates):
    for t in candidates:
        if n_rows % t == 0:
            return t
    raise ValueError(f"no row tile for {n_rows}")


def kernel(x_prompt, x_sample, state_ssd, state_ssd_conv, state_s5_re, state_s5_im, state_gdn, state_gdn_conv, state_hgrn, meta_tokens, ln_in_g, ln_in_b, w_in, ssd_conv_w, ssd_conv_b, ssd_dt_bias, ssd_a_log, ssd_d, ssd_norm_g, s5_lam_re, s5_lam_im, s5_log_dt, s5_b_re, s5_b_im, s5_c_re, s5_c_im, s5_d, s5_glu_w, s5_glu_b, gdn_conv_w, gdn_conv_b, gdn_a_log, gdn_dt_bias, gdn_norm_g, hg_lb_raw, hg_norm_g, w_out, ln_g, ln_b):
    p = dict(w_in=w_in, ssd_conv_w=ssd_conv_w, ssd_conv_b=ssd_conv_b, ssd_dt_bias=ssd_dt_bias,
             ssd_a_log=ssd_a_log, ssd_d=ssd_d, ssd_norm_g=ssd_norm_g, s5_lam_re=s5_lam_re,
             s5_lam_im=s5_lam_im, s5_log_dt=s5_log_dt, s5_b_re=s5_b_re, s5_b_im=s5_b_im, s5_c_re=s5_c_re,
             s5_c_im=s5_c_im, s5_d=s5_d, s5_glu_w=s5_glu_w, s5_glu_b=s5_glu_b, gdn_conv_w=gdn_conv_w,
             gdn_conv_b=gdn_conv_b, gdn_a_log=gdn_a_log, gdn_dt_bias=gdn_dt_bias, gdn_norm_g=gdn_norm_g,
             hg_norm_g=hg_norm_g, w_out=w_out, ln_g=ln_g, ln_b=ln_b)
    bp, seq, _ = x_prompt.shape
    bs, dseq, _ = x_sample.shape
    assert dseq == SUBLANES // 2
    t_pad = PAD_FRONT + N_META + seq
    rows_p = 192
    assert t_pad % rows_p == 0
    n_p = bp * t_pad
    n_s = bs * SUBLANES
    rows_s = 128 if n_s % 128 == 0 else CHUNK
    assert n_s % rows_s == 0

    soft = jax.nn.softmax(hg_lb_raw.astype(F32), axis=0)
    csum = jnp.cumsum(soft, axis=0)
    lbs = csum - csum[0]

    meta = jnp.broadcast_to(meta_tokens[None].astype(F32), (bp, N_META, D_MODEL))
    xp = jnp.concatenate([jnp.zeros((bp, PAD_FRONT, D_MODEL), F32), meta, x_prompt.astype(F32)], axis=1)
    xp = xp.reshape(n_p, D_MODEL)
    xs = jnp.pad(x_sample.astype(F32), ((0, 0), (0, SUBLANES - dseq), (0, 0))).reshape(n_s, D_MODEL)

    tm_p = _pick_tile(n_p, (704, 512, 384, 192))
    tm_s = _pick_tile(n_s, (512, 256, 128, 64))
    g_in, b_in = ln_in_g.astype(F32)[None], ln_in_b.astype(F32)[None]
    hp = _row_tiled_call(_ln_in_kernel, n_p, tm_p, [xp], [g_in, b_in], [D_MODEL], [F32], "ln_in_prompt")
    hs = _row_tiled_call(_ln_in_kernel, n_s, tm_s, [xs], [g_in, b_in], [D_MODEL], [F32], "ln_in_sample")

    consts_p = _mixer_constants(1)
    consts_s = _mixer_constants(CHUNK // SUBLANES)
    widths = [WA, WB, WC, WD]

    pad_conv = lambda c: jnp.pad(c.astype(F32), ((0, 0), (SUBLANES - (CONV_W - 1), 0), (0, 0)))
    p_out = [[] for _ in range(7)]
    s_out = [[] for _ in range(7)]
    for l in range(DEPTH):
        lw = _layer_weights(l, p, lbs)
        us = _row_tiled_call(_inproj_kernel, n_p, tm_p, [hp], [lw['w_in']], widths, [F32] * 4, "inproj_prompt")
        mix, o_ssd, o_ca, o_re, o_im, o_gdn, o_cc, o_hg = _mixer_call(
            us, lw, consts_p, None, prompt=True, n_seq=bp, rows=rows_p)
        hp = _row_tiled_call(_outproj_kernel, n_p, tm_p, [mix, hp], [lw['w_out'], lw['ln_g'], lw['ln_b']],
                             [D_MODEL], [F32], "outproj_prompt")
        for lst, val in zip(p_out, (o_ssd, o_ca[:, SUBLANES - (CONV_W - 1):],
                                    o_re[:, 0].reshape(bp, S5_GROUPS, S5_STATE),
                                    o_im[:, 0].reshape(bp, S5_GROUPS, S5_STATE),
                                    o_gdn, o_cc[:, SUBLANES - (CONV_W - 1):], jnp.swapaxes(o_hg, -1, -2))):
            lst.append(val)
        st_in = (state_ssd[l].astype(F32), pad_conv(state_ssd_conv[l]),
                 state_s5_re[l].astype(F32).reshape(bs, S5_LANES), state_s5_im[l].astype(F32).reshape(bs, S5_LANES),
                 state_gdn[l].astype(F32), pad_conv(state_gdn_conv[l]),
                 jnp.swapaxes(state_hgrn[l].astype(F32), -1, -2))
        us = _row_tiled_call(_inproj_kernel, n_s, tm_s, [hs], [lw['w_in']], widths, [F32] * 4, "inproj_sample")
        mix, o_ssd, o_ca, o_re, o_im, o_gdn, o_cc, o_hg = _mixer_call(
            us, lw, consts_s, st_in, prompt=False, n_seq=bs, rows=rows_s)
        hs = _row_tiled_call(_outproj_kernel, n_s, tm_s, [mix, hs], [lw['w_out'], lw['ln_g'], lw['ln_b']],
                             [D_MODEL], [F32], "outproj_sample")
        last = dseq - 1
        for lst, val in zip(s_out, (o_ssd, o_ca[:, dseq - (CONV_W - 1):dseq],
                                    o_re.reshape(bs, SUBLANES, S5_GROUPS, S5_STATE)[:, last],
                                    o_im.reshape(bs, SUBLANES, S5_GROUPS, S5_STATE)[:, last],
                                    o_gdn, o_cc[:, dseq - (CONV_W - 1):dseq], jnp.swapaxes(o_hg, -1, -2))):
            lst.append(val)

    y_prompt = hp.reshape(bp, t_pad, D_MODEL)[:, PAD_FRONT + N_META:]
    y_sample = hs.reshape(bs, SUBLANES, D_MODEL)[:, :dseq]
    return (y_prompt, y_sample, *[jnp.stack(v) for v in p_out], *[jnp.stack(v) for v in s_out])
```

```python
import functools
import math

import numpy as np
import jax
import jax.numpy as jnp
from jax import lax
from jax.experimental import pallas as pl
from jax.experimental.pallas import tpu as pltpu

F32 = jnp.float32
BF16 = jnp.bfloat16

D_MODEL = 1024
DEPTH = 4
N_META = 16
D_BRANCH = 256
HEADS = 4
PAIRS = HEADS // 2
HEAD_DIM = 64
SSD_XBC = 512
S5_GROUPS = 16
S5_STATE = 64
S5_LANES = S5_GROUPS * S5_STATE
GDN_QKV = 768
CONV_W = 4
DN_ALPHA = (2 * DEPTH) ** 0.25
LN_EPS = 1e-5
RMS_EPS = 1e-6
L2_EPS = 1e-6

CHUNK = 64
SUBLANES = 8
LANES = 128
PAD_FRONT = CHUNK - N_META
WA, WB, WC, WD = 768, 512, 1152, 1024
NROWS = 16
ROW_W = 768
VMEM_LIMIT = 56 * 1024 * 1024


def _dot(a, b):
    return jnp.dot(a.astype(BF16), b.astype(BF16), preferred_element_type=F32)


def _dot_nt(a, b):
    return lax.dot_general(a.astype(BF16), b.astype(BF16), (((1,), (1,)), ((), ())),
                           preferred_element_type=F32)


def _dot_tn(a, b):
    return lax.dot_general(a.astype(BF16), b.astype(BF16), (((0,), (0,)), ((), ())),
                           preferred_element_type=F32)


def _split(x, pieces):
    out = []
    r = x
    for i in range(pieces):
        xi = r.astype(BF16)
        out.append(xi)
        if i + 1 < pieces:
            r = r - xi.astype(F32)
    return out


def _dot01_l(w01, x, pieces=3):
    n = x.shape[-1]
    r = jnp.dot(w01, jnp.concatenate(_split(x, pieces), axis=-1), preferred_element_type=F32)
    return sum(r[:, i * n:(i + 1) * n] for i in range(pieces))


def _dot01_r(x, w01, pieces=3):
    m = x.shape[0]
    r = jnp.dot(jnp.concatenate(_split(x, pieces), axis=0), w01, preferred_element_type=F32)
    return sum(r[i * m:(i + 1) * m] for i in range(pieces))


def _dot01_nt(w01, x, pieces=3):
    d = lambda v: lax.dot_general(w01, v, (((1,), (1,)), ((), ())), preferred_element_type=F32)
    return sum(d(v) for v in _split(x, pieces))


def _sigmoid(x):
    return 1.0 / (1.0 + jnp.exp(-x))


def _silu(x):
    return x * _sigmoid(x)


def _softplus(x):
    return jnp.maximum(x, 0.0) + jnp.log1p(jnp.exp(-jnp.abs(x)))


def _gelu_tanh(x):
    c = math.sqrt(2.0 / math.pi)
    return 0.5 * x * (1.0 + jnp.tanh(c * (x + 0.044715 * (x * x * x))))


def _unit_lower_inverse(m, eye, levels, mm):
    p = eye - m
    mp = m
    for _ in range(levels):
        mp = mm(mp, mp)
        p = p + mm(p, mp)
    return p


def _apply_state(x, s, ns):
    xb = x.reshape(ns, CHUNK // ns, x.shape[-1])
    r = lax.dot_general(xb.astype(BF16), s.astype(BF16), (((2,), (1,)), ((0,), (0,))),
                        preferred_element_type=F32)
    return r.reshape(CHUNK, s.shape[-1])


def _outer_state(a, b, ns):
    ls = CHUNK // ns
    ab = jnp.swapaxes(a.reshape(ns, ls, a.shape[-1]), 1, 2)
    bb = b.reshape(ns, ls, b.shape[-1])
    return lax.dot_general(ab.astype(BF16), bb.astype(BF16), (((2,), (1,)), ((0,), (0,))),
                           preferred_element_type=F32)


def _per_seq_rows(x, ns):
    return x.reshape(ns, CHUNK // ns, x.shape[-1])[:, 0:1, :]


def _layernorm_rows(r, g, b):
    mu = jnp.mean(r, axis=-1, keepdims=True)
    c = r - mu
    var = jnp.mean(c * c, axis=-1, keepdims=True)
    return c * lax.rsqrt(var + LN_EPS) * g + b


def _ln_in_kernel(x_ref, g_ref, b_ref, o_ref):
    o_ref[...] = _layernorm_rows(x_ref[...], g_ref[...], b_ref[...])


def _inproj_kernel(x_ref, w_ref, oa_ref, ob_ref, oc_ref, od_ref):
    x = x_ref[...].astype(BF16)
    off = 0
    for o_ref, width in ((oa_ref, WA), (ob_ref, WB), (oc_ref, WC), (od_ref, WD)):
        o_ref[...] = jnp.dot(x, w_ref[:, off:off + width], preferred_element_type=F32)
        off += width


def _outproj_kernel(mix_ref, x_ref, w_ref, g_ref, b_ref, o_ref):
    out = jnp.dot(mix_ref[...], w_ref[...], preferred_element_type=F32)
    o_ref[...] = _layernorm_rows(DN_ALPHA * x_ref[...] + out, g_ref[...], b_ref[...])


def _row_tiled_call(kernel, n_rows, tm, row_inputs, const_inputs, out_widths, out_dtypes, name):
    in_specs = [pl.BlockSpec((tm, a.shape[1]), lambda i: (i, 0)) for a in row_inputs]
    in_specs += [pl.BlockSpec(a.shape, lambda i, nd=a.ndim: (0,) * nd) for a in const_inputs]
    out_specs = [pl.BlockSpec((tm, w), lambda i: (i, 0)) for w in out_widths]
    out_shape = [jax.ShapeDtypeStruct((n_rows, w), dt) for w, dt in zip(out_widths, out_dtypes)]
    single = len(out_widths) == 1
    return pl.pallas_call(
        kernel,
        grid=(n_rows // tm,),
        in_specs=in_specs,
        out_specs=out_specs[0] if single else out_specs,
        out_shape=out_shape[0] if single else out_shape,
        compiler_params=pltpu.CompilerParams(dimension_semantics=("arbitrary",),
                                             vmem_limit_bytes=VMEM_LIMIT),
        name=name,
    )(*row_inputs, *const_inputs)


N_MIXER_CONST = 19


def _mixer_kernel(*refs, prompt, rows, ns, nsb):
    ls = CHUNK // ns
    lb = rows // nsb
    nch = rows // CHUNK
    nlev = int(math.log2(ls))
    n_in = 4 + N_MIXER_CONST + (0 if prompt else 7)
    (ua_ref, ub_ref, uc_ref, ud_ref, rows_ref, cwa_ref, cwc_ref, wbre_ref, wbim_ref, cre_ref, cim_ref,
     glu_ref, tab_ref, cmat_ref, cmask_ref, cpair_ref, bdm_ref, lr_ref, ea_ref, eb_ref, blk_ref,
     sel_ref, ones_ref) = refs[:4 + N_MIXER_CONST]
    if not prompt:
        (i_ssd, i_ca, i_s5re, i_s5im, i_gdn, i_cc, i_hg) = refs[4 + N_MIXER_CONST:n_in]
    (mix_ref, o_ssd, o_ca, o_s5re, o_s5im, o_gdn, o_cc, o_hg) = refs[n_in:n_in + 8]
    (xxa_ref, xxc_ref, acta_ref, actc_ref, hre_ref, him_ref, ssd_sc, gdn_sc, hg_sc) = refs[n_in + 8:]

    j = pl.program_id(1)

    if prompt:
        @pl.when(j == 0)
        def _():
            ssd_sc[...] = jnp.zeros_like(ssd_sc)
            gdn_sc[...] = jnp.zeros_like(gdn_sc)
            hg_sc[...] = jnp.zeros_like(hg_sc)
            o_s5re[...] = jnp.zeros_like(o_s5re)
            o_s5im[...] = jnp.zeros_like(o_s5im)
            xxa_ref[:, 0:SUBLANES, :] = jnp.zeros((nsb, SUBLANES, SSD_XBC), F32)
            xxc_ref[:, 0:SUBLANES, :] = jnp.zeros((nsb, SUBLANES, GDN_QKV), F32)
    else:
        o_ssd[...] = i_ssd[...]
        o_gdn[...] = i_gdn[...]
        o_hg[...] = i_hg[...]
        xxa_ref[:, 0:SUBLANES, :] = i_ca[...]
        xxc_ref[:, 0:SUBLANES, :] = i_cc[...]

    first_valid = jnp.where(j == 0, PAD_FRONT, 0) if prompt else None

    def valid_rows(r):
        if prompt:
            return r >= first_valid
        return (r & (SUBLANES - 1)) < (SUBLANES // 2)

    def valid_mask(width):
        return valid_rows(lax.broadcasted_iota(jnp.int32, (rows, width), 0))

    def row(i, w):
        return rows_ref[i:i + 1, 0:w]

    def conv_block(raw, xx_ref, cw_ref, bias, o_conv, act_ref, width):
        raw = jnp.where(valid_mask(width), raw, 0.0)
        raw3 = raw.reshape(nsb, lb, width)
        xx_ref[:, SUBLANES:SUBLANES + lb, :] = raw3
        acc = bias
        for w in range(CONV_W):
            lo = SUBLANES - (CONV_W - 1) + w
            acc = acc + cw_ref[w:w + 1, :] * xx_ref[:, lo:lo + lb, :]
        tail = raw3[:, lb - SUBLANES:lb, :]
        o_conv[...] = tail
        if prompt:
            xx_ref[:, 0:SUBLANES, :] = tail
        act_ref[...] = _silu(acc).reshape(rows, width)

    conv_block(ua_ref[:, 256:768], xxa_ref, cwa_ref, row(0, SSD_XBC), o_ca, acta_ref, SSD_XBC)
    conv_block(uc_ref[:, 256:1024], xxc_ref, cwc_ref, row(7, GDN_QKV), o_cc, actc_ref, GDN_QKV)

    u_b = jnp.where(valid_mask(D_BRANCH), ub_ref[:, 256:512], 0.0)
    hre_ref[...] = _dot(u_b, wbre_ref[...])
    him_ref[...] = _dot(u_b, wbim_ref[...])

    def s5_group(g, carry):
        r0 = pl.multiple_of(g * SUBLANES, SUBLANES)
        xr = hre_ref[pl.ds(r0, SUBLANES), :]
        xi = him_ref[pl.ds(r0, SUBLANES), :]
        for k, d in enumerate((1, 2, 4)):
            tr = tab_ref[k, :, 0:S5_LANES]
            ti = tab_ref[k, :, S5_LANES:2 * S5_LANES]
            sr = pltpu.roll(xr, d, axis=0)
            si = pltpu.roll(xi, d, axis=0)
            xr, xi = xr + tr * sr - ti * si, xi + tr * si + ti * sr
        if prompt:
            cr, ci = carry
        else:
            cr = jnp.broadcast_to(i_s5re[pl.ds(g, 1), :], (SUBLANES, S5_LANES))
            ci = jnp.broadcast_to(i_s5im[pl.ds(g, 1), :], (SUBLANES, S5_LANES))
        pr = tab_ref[3, :, 0:S5_LANES]
        pi = tab_ref[3, :, S5_LANES:2 * S5_LANES]
        xr, xi = xr + pr * cr - pi * ci, xi + pr * ci + pi * cr
        hre_ref[pl.ds(r0, SUBLANES), :] = xr
        him_ref[pl.ds(r0, SUBLANES), :] = xi
        if prompt:
            return (jnp.broadcast_to(xr[SUBLANES - 1:SUBLANES, :], (SUBLANES, S5_LANES)),
                    jnp.broadcast_to(xi[SUBLANES - 1:SUBLANES, :], (SUBLANES, S5_LANES)))
        return carry

    if prompt:
        cr, ci = lax.fori_loop(0, rows // SUBLANES, s5_group, (o_s5re[0], o_s5im[0]))
        o_s5re[0] = cr
        o_s5im[0] = ci
    else:
        lax.fori_loop(0, rows // SUBLANES, s5_group, 0)
        o_s5re[...] = hre_ref[...]
        o_s5im[...] = him_ref[...]

    y5 = (_dot(hre_ref[...], cre_ref[...]) - _dot(him_ref[...], cim_ref[...])
          + row(5, D_BRANCH) * ub_ref[:, 256:512])
    y5 = _gelu_tanh(y5)
    y5 = y5 * _sigmoid(_dot(y5, glu_ref[...]) + row(6, D_BRANCH))
    mix_ref[:, 256:512] = (y5 * _silu(ub_ref[:, 0:256])).astype(BF16)

    tri_b = cmat_ref[0]
    last_b = cmat_ref[1]

    def head_sum(x):
        return jnp.dot(x.astype(BF16), blk_ref[...], preferred_element_type=F32)

    def rms_gate(y, g_row, z):
        return (y * lax.rsqrt(head_sum(y * y) * (1.0 / HEAD_DIM) + RMS_EPS) * g_row * _silu(z)).astype(BF16)

    def head_scalars(rs, r0):
        rr = lax.broadcasted_iota(jnp.int32, (CHUNK, LANES), 0) + r0
        lane = lax.broadcasted_iota(jnp.int32, (CHUNK, LANES), 1)
        valid = valid_rows(rr)
        t = uc_ref[rs, 1024:1152] + row(1, LANES)
        vals = jnp.where((lane >= HEADS) & (lane < 2 * HEADS), _sigmoid(t), _softplus(t) * row(2, LANES))
        vals = jnp.where(valid, vals, 0.0)
        cs = _dot01_l(tri_b, vals)
        ex_v = _dot01_r(vals, ea_ref[...], pieces=2)
        ex_c = _dot01_r(cs, eb_ref[...])
        crow = _dot01_nt(sel_ref[...], cs)
        valid_w = jnp.concatenate([valid, valid], axis=-1)
        return (ex_v[:, 0:256], ex_v[:, 256:512], ex_c[:, 0:256], ex_c[:, 256:512], crow, valid_w)

    def hgrn_inputs(rs, valid_w):
        f_d = ud_ref[rs, 512:768]
        lsig = -_softplus(-f_d)
        t1 = row(11, D_BRANCH)
        t2 = row(12, D_BRANCH) + lsig
        logf = jnp.maximum(t1, t2) + jnp.log1p(jnp.exp(-jnp.abs(t1 - t2)))
        logf = jnp.where(valid_w, logf, 0.0)
        kd = row(13, D_BRANCH) * _sigmoid(-f_d)
        qd = _silu(ud_ref[rs, 256:512])
        vd = jnp.where(valid_w, ud_ref[rs, 768:1024], 0.0)
        xall = _dot01_l(cmat_ref[2:3 + nlev].reshape((1 + nlev) * CHUNK, CHUNK), logf)
        return logf, kd, qd, vd, xall

    def chunk_prompt(c, _):
        r0 = pl.multiple_of(c * CHUNK, CHUNK)
        rs = pl.ds(r0, CHUNK)
        dt_full, beta_full, gc, acum, crow, valid_w = head_scalars(rs, r0)
        tri2, strict2, eye2 = cpair_ref[0], cpair_ref[1], cpair_ref[2]
        bdm = bdm_ref[...]
        lr0, lr1 = lr_ref[0], lr_ref[1]

        def bd(x):
            xb = x.astype(BF16)
            return jnp.concatenate([xb * lr0, xb * lr1], axis=0)

        def mm_pair(a, b):
            return _dot(a, bd(b))

        def pair_row(r):
            return jnp.concatenate([crow[r:r + 1, :], crow[r + 1:r + 2, :]], axis=1)

        pairs = [slice(p * LANES, (p + 1) * LANES) for p in range(PAIRS)]

        xs = acta_ref[rs, 0:256]
        bm = acta_ref[rs, 256:384]
        cmm = acta_ref[rs, 384:512]
        alast = acum[CHUNK - 1:CHUNK, :]
        xdt = xs * dt_full
        xdt_end = xdt * jnp.exp(alast - acum)
        eacum = jnp.exp(acum)
        ya = []
        for p, ps in enumerate(pairs):
            gs = slice(p * HEAD_DIM, (p + 1) * HEAD_DIM)
            bg, cg = bm[:, gs], cmm[:, gs]
            cb2 = _dot_nt(cg, jnp.concatenate([bg, bg], axis=0))
            lm = jnp.exp(jnp.minimum(acum[:, ps] - pair_row(HEADS + 2 * p), 0.0)) * tri2
            s_pk = ssd_sc[p]
            ya.append(mm_pair(cb2 * lm, xdt[:, ps]) + _dot(cg, s_pk) * eacum[:, ps])
            ssd_sc[p] = s_pk * jnp.exp(alast[:, ps]) + _dot_tn(bg, xdt_end[:, ps])
        ya = jnp.concatenate(ya, axis=-1) + row(3, D_BRANCH) * xs
        mix_ref[rs, 0:256] = rms_gate(ya, row(4, D_BRANCH), ua_ref[rs, 0:256])

        q = actc_ref[rs, 0:256]
        k = actc_ref[rs, 256:512]
        v = actc_ref[rs, 512:768]
        q = q * lax.rsqrt(head_sum(q * q) + L2_EPS) * (HEAD_DIM ** -0.5)
        k = k * lax.rsqrt(head_sum(k * k) + L2_EPS)
        glast = gc[CHUNK - 1:CHUNK, :]
        egc = jnp.exp(gc)
        vb = v * beta_full
        kbe = k * beta_full * egc
        qe = q * egc
        kend = k * jnp.exp(glast - gc)
        yc = []
        for p, ps in enumerate(pairs):
            dec = jnp.exp(jnp.minimum(gc[:, ps] - pair_row(2 * p), 0.0)) * tri2
            kq = _dot_nt(jnp.concatenate([k[:, ps], q[:, ps]], axis=0), bd(k[:, ps]))
            m = kq[0:CHUNK] * dec * beta_full[:, ps] * strict2
            aq = kq[CHUNK:2 * CHUNK] * dec
            tinv = _unit_lower_inverse(m, eye2, nlev - 1, mm_pair)
            uw = _dot(tinv, jnp.concatenate([bd(vb[:, ps]), bd(kbe[:, ps])], axis=1))
            s_bd = gdn_sc[p]
            ws = _dot(jnp.concatenate([uw[:, LANES:2 * LANES], qe[:, ps]], axis=0), s_bd)
            v_new = uw[:, 0:LANES] - ws[0:CHUNK]
            yc.append(ws[CHUNK:2 * CHUNK] + mm_pair(aq, v_new))
            gdn_sc[p] = s_bd * jnp.exp(glast[:, ps]) + _dot_tn(kend[:, ps], v_new) * bdm
        yc = jnp.concatenate(yc, axis=-1)
        mix_ref[rs, 512:768] = rms_gate(yc, row(10, D_BRANCH), uc_ref[rs, 0:256])

        logf, kd, qd, vd, xall = hgrn_inputs(rs, valid_w)
        gcd = xall[0:CHUNK, :]
        gld = gcd[CHUNK - 1:CHUNK, :]
        qed = qd * jnp.exp(gcd)
        kend_d = kd * jnp.exp(gld - gcd)
        diag = head_sum(qd * kd)
        g1, g2, g3 = [t.astype(F32) for t in _split(gld, 3)]
        rid = lax.broadcasted_iota(jnp.int32, (2 * SUBLANES, D_BRANCH), 0)
        g16 = jnp.where(rid == 0, g1, jnp.where(rid == 1, g2, jnp.where(rid == 2, g3, 0.0)))
        zs = [jnp.exp(xall[(1 + lev) * CHUNK:(2 + lev) * CHUNK, :]) for lev in range(nlev)]
        yd = []
        for p, ps in enumerate(pairs):
            amat = jnp.zeros((CHUNK, LANES), F32)
            for lev in range(nlev):
                amat = amat + _dot_nt(qd[:, ps] * zs[lev][:, ps], bd(kd[:, ps] * zs[lev][:, ps])) * cpair_ref[3 + lev]
            s_bd = hg_sc[p]
            yd.append(mm_pair(amat, vd[:, ps]) + diag[:, ps] * vd[:, ps] + _dot(qed[:, ps], s_bd))
            dcol = jnp.exp(_dot_tn(g16[:, ps], ones_ref[...]))
            hg_sc[p] = s_bd * dcol + _dot_tn(kend_d[:, ps], vd[:, ps]) * bdm
        yd = jnp.concatenate(yd, axis=-1)
        mix_ref[rs, 768:1024] = rms_gate(yd, row(14, D_BRANCH), ud_ref[rs, 0:256])
        return 0

    def chunk_sample(c, _):
        r0 = pl.multiple_of(c * CHUNK, CHUNK)
        rs = pl.ds(r0, CHUNK)
        sq = pl.ds(c * ns, ns)
        dt_full, beta_full, gc, acum, crow, valid_w = head_scalars(rs, r0)
        tri_f, strict_f, eye_f = cmask_ref[0], cmask_ref[1], cmask_ref[2]
        heads = [slice(h * HEAD_DIM, (h + 1) * HEAD_DIM) for h in range(HEADS)]

        xs = acta_ref[rs, 0:256]
        bm = acta_ref[rs, 256:384]
        cmm = acta_ref[rs, 384:512]
        alast = _dot01_l(last_b, acum)
        xdt = xs * dt_full
        xdt_end = xdt * jnp.exp(alast - acum)
        eacum = jnp.exp(acum)
        ya = []
        for h, hs in enumerate(heads):
            gs = heads[h // 2]
            bg, cg = bm[:, gs], cmm[:, gs]
            lm = jnp.exp(jnp.minimum(acum[:, hs] - crow[HEADS + h:HEADS + h + 1, :], 0.0)) * tri_f
            s_h = o_ssd[sq, h]
            ya.append(_dot(_dot_nt(cg, bg) * lm, xdt[:, hs]) + _apply_state(cg, s_h, ns) * eacum[:, hs])
            o_ssd[sq, h] = (s_h * jnp.exp(_per_seq_rows(alast[:, hs], ns))
                            + _outer_state(bg, xdt_end[:, hs], ns))
        ya = jnp.concatenate(ya, axis=-1) + row(3, D_BRANCH) * xs
        mix_ref[rs, 0:256] = rms_gate(ya, row(4, D_BRANCH), ua_ref[rs, 0:256])

        q = actc_ref[rs, 0:256]
        k = actc_ref[rs, 256:512]
        v = actc_ref[rs, 512:768]
        q = q * lax.rsqrt(head_sum(q * q) + L2_EPS) * (HEAD_DIM ** -0.5)
        k = k * lax.rsqrt(head_sum(k * k) + L2_EPS)
        glast = _dot01_l(last_b, gc)
        egc = jnp.exp(gc)
        vb = v * beta_full
        kbe = k * beta_full * egc
        qe = q * egc
        kend = k * jnp.exp(glast - gc)
        yc = []
        for h, hs in enumerate(heads):
            kh = k[:, hs]
            dec = jnp.exp(jnp.minimum(gc[:, hs] - crow[h:h + 1, :], 0.0)) * tri_f
            m = _dot_nt(kh, kh) * dec * beta_full[:, hs] * strict_f
            tinv = _unit_lower_inverse(m, eye_f, nlev - 1, _dot)
            u_val = _dot(tinv, vb[:, hs])
            w_key = _dot(tinv, kbe[:, hs])
            aq = _dot_nt(q[:, hs], kh) * dec
            s_h = o_gdn[sq, h]
            v_new = u_val - _apply_state(w_key, s_h, ns)
            yc.append(_apply_state(qe[:, hs], s_h, ns) + _dot(aq, v_new))
            o_gdn[sq, h] = (s_h * jnp.exp(_per_seq_rows(glast[:, hs], ns))
                            + _outer_state(kend[:, hs], v_new, ns))
        yc = jnp.concatenate(yc, axis=-1)
        mix_ref[rs, 512:768] = rms_gate(yc, row(10, D_BRANCH), uc_ref[rs, 0:256])

        logf, kd, qd, vd, xall = hgrn_inputs(rs, valid_w)
        gcd = xall[0:CHUNK, :]
        gld = _dot01_l(last_b, gcd)
        qed = qd * jnp.exp(gcd)
        kend_d = kd * jnp.exp(gld - gcd)
        diag = head_sum(qd * kd)
        lpieces = _split(logf, 3)
        ones_h = jnp.ones((CHUNK, HEAD_DIM), BF16)
        zs = [jnp.exp(xall[(1 + lev) * CHUNK:(2 + lev) * CHUNK, :]) for lev in range(nlev)]
        yd = []
        for h, hs in enumerate(heads):
            amat = jnp.zeros((CHUNK, CHUNK), F32)
            for lev in range(nlev):
                amat = amat + _dot_nt(qd[:, hs] * zs[lev][:, hs], kd[:, hs] * zs[lev][:, hs]) * cmask_ref[3 + lev]
            s_h = o_hg[sq, h]
            yd.append(_dot(amat, vd[:, hs]) + diag[:, hs] * vd[:, hs] + _apply_state(qed[:, hs], s_h, ns))
            dcol = jnp.exp(sum(_outer_state(lp[:, hs], ones_h, ns) for lp in lpieces))
            o_hg[sq, h] = s_h * dcol + _outer_state(kend_d[:, hs], vd[:, hs], ns)
        yd = jnp.concatenate(yd, axis=-1)
        mix_ref[rs, 768:1024] = rms_gate(yd, row(14, D_BRANCH), ud_ref[rs, 0:256])
        return 0

    lax.fori_loop(0, nch, chunk_prompt if prompt else chunk_sample, 0)

    if prompt:
        @pl.when(j == pl.num_programs(1) - 1)
        def _():
            for p in range(PAIRS):
                for e in range(2):
                    es = slice(e * HEAD_DIM, (e + 1) * HEAD_DIM)
                    o_ssd[0, 2 * p + e] = ssd_sc[p, :, es]
                    o_gdn[0, 2 * p + e] = gdn_sc[p, es, es]
                    o_hg[0, 2 * p + e] = hg_sc[p, es, es]


def _mixer_constants(ns):
    ls = CHUNK // ns
    nlev = int(math.log2(ls))
    i = np.arange(CHUNK)[:, None]
    jn = np.arange(CHUNK)[None, :]
    same = (i // ls) == (jn // ls)
    tri = (same & (jn <= i)).astype(np.float32)
    strict = (same & (jn < i)).astype(np.float32)
    eye = np.eye(CHUNK, dtype=np.float32)
    last = (jn == (i // ls) * ls + ls - 1).astype(np.float32)
    wlev, mlev = [], []
    for lev in range(nlev):
        b = ls >> (lev + 1)
        blk_i, pos_i = i // (2 * b), i % (2 * b)
        mid = blk_i * 2 * b + b
        upper = pos_i >= b
        w = np.where(upper, (jn >= mid) & (jn <= i), (jn > i) & (jn < mid)).astype(np.float32)
        msk = (((jn // (2 * b)) == blk_i) & upper & ((jn % (2 * b)) < b)).astype(np.float32)
        wlev.append(w)
        mlev.append(msk)
    cmat = np.stack([tri, last, tri] + wlev)
    cmask = np.stack([tri, strict, eye] + mlev)
    cpair = np.concatenate([cmask, cmask], axis=-1)
    bdm = np.kron(np.eye(2, dtype=np.float32), np.ones((HEAD_DIM, HEAD_DIM), np.float32))
    lane = np.arange(LANES)[None, :]
    lr = np.stack([np.broadcast_to(lane < HEAD_DIM, (CHUNK, LANES)),
                   np.broadcast_to(lane >= HEAD_DIM, (CHUNK, LANES))]).astype(np.float32)
    e_a = np.zeros((LANES, 512), np.float32)
    e_b = np.zeros((LANES, 512), np.float32)
    blk = np.zeros((256, 256), np.float32)
    sel = np.zeros((2 * SUBLANES, LANES), np.float32)
    for h in range(HEADS):
        hs = slice(h * HEAD_DIM, (h + 1) * HEAD_DIM)
        e_a[h, hs] = 1.0
        e_a[HEADS + h, 256 + h * HEAD_DIM:256 + (h + 1) * HEAD_DIM] = 1.0
        e_b[2 * HEADS + h, hs] = 1.0
        e_b[3 * HEADS + h, 256 + h * HEAD_DIM:256 + (h + 1) * HEAD_DIM] = 1.0
        blk[hs, hs] = 1.0
        sel[h, 2 * HEADS + h] = 1.0
        sel[HEADS + h, 3 * HEADS + h] = 1.0
    ones = np.ones((2 * SUBLANES, LANES), np.float32)
    return (jnp.asarray(cmat, BF16), jnp.asarray(cmask, F32), jnp.asarray(cpair, F32), jnp.asarray(bdm, F32),
            jnp.asarray(lr, BF16), jnp.asarray(e_a, BF16), jnp.asarray(e_b, BF16), jnp.asarray(blk, BF16),
            jnp.asarray(sel, BF16), jnp.asarray(ones, BF16))


def _mixer_call(us, lw, consts, states, *, prompt, n_seq, rows):
    ua, ub, uc, ud = us
    n_rows = ua.shape[0]
    if prompt:
        ns, nsb = 1, 1
        nblk = n_rows // n_seq // rows
        grid = (n_seq, nblk)
        rmap = lambda b, j: (b * nblk + j, 0)
        smap4 = lambda b, j: (b, 0, 0, 0)
        smap3 = lambda b, j: (b, 0, 0)
    else:
        ns = CHUNK // SUBLANES
        nsb = rows // SUBLANES
        grid = (1, n_rows // rows)
        rmap = lambda b, j: (j, 0)
        smap4 = lambda b, j: (j, 0, 0, 0)
        smap3 = lambda b, j: (j, 0, 0)
    const_inputs = [lw['rows'], lw['cw_a'], lw['cw_c'], lw['wb_re'], lw['wb_im'], lw['c_re'], lw['c_im'],
                    lw['glu_w'], lw['s5tab'], *consts]
    assert len(const_inputs) == N_MIXER_CONST
    in_specs = [pl.BlockSpec((rows, w), rmap) for w in (WA, WB, WC, WD)]
    in_specs += [pl.BlockSpec(a.shape, lambda b, j, nd=a.ndim: (0,) * nd) for a in const_inputs]
    inputs = [ua, ub, uc, ud] + const_inputs
    st4 = (nsb, HEADS, HEAD_DIM, HEAD_DIM)
    if not prompt:
        inputs += list(states)
        in_specs += [pl.BlockSpec(st4, smap4),
                     pl.BlockSpec((nsb, SUBLANES, SSD_XBC), smap3),
                     pl.BlockSpec((nsb, S5_LANES), rmap),
                     pl.BlockSpec((nsb, S5_LANES), rmap),
                     pl.BlockSpec(st4, smap4),
                     pl.BlockSpec((nsb, SUBLANES, GDN_QKV), smap3),
                     pl.BlockSpec(st4, smap4)]
    n_st = n_seq
    if prompt:
        s5_shape, s5_spec = (n_st, SUBLANES, S5_LANES), pl.BlockSpec((1, SUBLANES, S5_LANES), smap3)
    else:
        s5_shape, s5_spec = (n_rows, S5_LANES), pl.BlockSpec((rows, S5_LANES), rmap)
    out_shape = [jax.ShapeDtypeStruct((n_rows, D_MODEL), BF16),
                 jax.ShapeDtypeStruct((n_st,) + st4[1:], F32),
                 jax.ShapeDtypeStruct((n_st, SUBLANES, SSD_XBC), F32),
                 jax.ShapeDtypeStruct(s5_shape, F32),
                 jax.ShapeDtypeStruct(s5_shape, F32),
                 jax.ShapeDtypeStruct((n_st,) + st4[1:], F32),
                 jax.ShapeDtypeStruct((n_st, SUBLANES, GDN_QKV), F32),
                 jax.ShapeDtypeStruct((n_st,) + st4[1:], F32)]
    out_specs = [pl.BlockSpec((rows, D_MODEL), rmap),
                 pl.BlockSpec(st4, smap4),
                 pl.BlockSpec((nsb, SUBLANES, SSD_XBC), smap3),
                 s5_spec, s5_spec,
                 pl.BlockSpec(st4, smap4),
                 pl.BlockSpec((nsb, SUBLANES, GDN_QKV), smap3),
                 pl.BlockSpec(st4, smap4)]
    lb = rows // nsb
    scratch = [pltpu.VMEM((nsb, SUBLANES + lb, SSD_XBC), F32),
               pltpu.VMEM((nsb, SUBLANES + lb, GDN_QKV), F32),
               pltpu.VMEM((rows, SSD_XBC), F32),
               pltpu.VMEM((rows, GDN_QKV), F32),
               pltpu.VMEM((rows, S5_LANES), F32),
               pltpu.VMEM((rows, S5_LANES), F32),
               pltpu.VMEM((PAIRS, HEAD_DIM, LANES), F32),
               pltpu.VMEM((PAIRS, LANES, LANES), F32),
               pltpu.VMEM((PAIRS, LANES, LANES), F32)]
    return pl.pallas_call(
        functools.partial(_mixer_kernel, prompt=prompt, rows=rows, ns=ns, nsb=nsb),
        grid=grid,
        in_specs=in_specs,
        out_specs=out_specs,
        out_shape=out_shape,
        scratch_shapes=scratch,
        compiler_params=pltpu.CompilerParams(dimension_semantics=("arbitrary", "arbitrary"),
                                             vmem_limit_bytes=VMEM_LIMIT),
        name="mixer_prompt" if prompt else "mixer_sample",
    )(*inputs)


def _pad_row(v, width=ROW_W):
    v = v.astype(F32).reshape(-1)
    return jnp.pad(v, (0, width - v.shape[0]))


def _layer_weights(l, p, lbs):
    w = p['w_in'][l]
    small = jnp.concatenate([w[:, 768:772], w[:, 2308:2316], w[:, 768:772],
                             jnp.zeros((D_MODEL, LANES - 4 * HEADS), F32)], axis=1)
    w_r = jnp.concatenate([w[:, 0:768], w[:, 772:1284], w[:, 1284:2308], small, w[:, 2316:3340]],
                          axis=1).astype(BF16)

    rep = lambda v: jnp.repeat(v.astype(F32), HEAD_DIM)
    lb = lbs[l]
    zeros_h = jnp.zeros((HEADS,), F32)
    small_bias = jnp.concatenate([p['ssd_dt_bias'][l], zeros_h, p['gdn_dt_bias'][l], p['ssd_dt_bias'][l]])
    small_scale = jnp.concatenate([jnp.ones((HEADS,), F32), zeros_h, -jnp.exp(p['gdn_a_log'][l]),
                                   -jnp.exp(p['ssd_a_log'][l])])
    zrow = jnp.zeros((ROW_W,), F32)
    rows = jnp.stack([
        _pad_row(p['ssd_conv_b'][l]), _pad_row(small_bias), _pad_row(small_scale),
        _pad_row(rep(p['ssd_d'][l])), _pad_row(p['ssd_norm_g'][l]), _pad_row(p['s5_d'][l]),
        _pad_row(p['s5_glu_b'][l]), _pad_row(p['gdn_conv_b'][l]), zrow, zrow,
        _pad_row(p['gdn_norm_g'][l]), _pad_row(jnp.log(lb)), _pad_row(jnp.log1p(-lb)), _pad_row(1.0 - lb),
        _pad_row(p['hg_norm_g'][l]), zrow])

    pad_cw = lambda cw: jnp.pad(cw.astype(F32), ((0, SUBLANES - CONV_W), (0, 0)))

    lam_re, lam_im = p['s5_lam_re'][l], p['s5_lam_im'][l]
    dt = jnp.exp(p['s5_log_dt'][l])[:, None]
    mag = jnp.exp(lam_re * dt)
    ang = lam_im * dt
    lb_re, lb_im = mag * jnp.cos(ang), mag * jnp.sin(ang)
    den = jnp.square(lam_re) + jnp.square(lam_im)
    nr = lb_re - 1.0
    coef_re = (nr * lam_re + lb_im * lam_im) / den
    coef_im = (lb_im * lam_re - nr * lam_im) / den
    b_re, b_im = p['s5_b_re'][l], p['s5_b_im'][l]
    bb_re = coef_re[..., None] * b_re - coef_im[..., None] * b_im
    bb_im = coef_re[..., None] * b_im + coef_im[..., None] * b_re
    eye_g = jnp.eye(S5_GROUPS, dtype=F32)
    bd_in = lambda bb: jnp.einsum('gnq,gh->gqhn', bb, eye_g).reshape(D_BRANCH, S5_LANES).astype(BF16)
    bd_out = lambda c: jnp.einsum('gqn,gh->gnhq', c, eye_g).reshape(S5_LANES, D_BRANCH).astype(BF16)

    pr, pi = [lb_re.reshape(-1)], [lb_im.reshape(-1)]
    for _ in range(SUBLANES - 1):
        pr, pi = (pr + [pr[-1] * pr[0] - pi[-1] * pi[0]], pi + [pr[-1] * pi[0] + pi[-1] * pr[0]])
    pw = jnp.stack([jnp.concatenate([a, b]) for a, b in zip(pr, pi)])
    ridx = jnp.arange(SUBLANES)[:, None]
    tabs = [jnp.where(ridx >= d, pw[d - 1][None, :], 0.0) for d in (1, 2, 4)] + [pw]
    s5tab = jnp.stack(tabs).astype(F32)

    return dict(w_in=w_r, rows=rows, cw_a=pad_cw(p['ssd_conv_w'][l]), cw_c=pad_cw(p['gdn_conv_w'][l]),
                wb_re=bd_in(bb_re), wb_im=bd_in(bb_im), c_re=bd_out(p['s5_c_re'][l]),
                c_im=bd_out(p['s5_c_im'][l]), glu_w=p['s5_glu_w'][l].astype(BF16), s5tab=s5tab,
                w_out=p['w_out'][l].astype(BF16),
                ln_g=p['ln_g'][l].astype(F32)[None], ln_b=p['ln_b'][l].astype(F32)[None])


def _pick_tile(n_rows, candidates):
    for t in candidates:
        if n_rows % t == 0:
            return t
    raise ValueError(f"no row tile for {n_rows}")


def kernel(x_prompt, x_sample, state_ssd, state_ssd_conv, state_s5_re, state_s5_im, state_gdn, state_gdn_conv, state_hgrn, meta_tokens, ln_in_g, ln_in_b, w_in, ssd_conv_w, ssd_conv_b, ssd_dt_bias, ssd_a_log, ssd_d, ssd_norm_g, s5_lam_re, s5_lam_im, s5_log_dt, s5_b_re, s5_b_im, s5_c_re, s5_c_im, s5_d, s5_glu_w, s5_glu_b, gdn_conv_w, gdn_conv_b, gdn_a_log, gdn_dt_bias, gdn_norm_g, hg_lb_raw, hg_norm_g, w_out, ln_g, ln_b):
    p = dict(w_in=w_in, ssd_conv_w=ssd_conv_w, ssd_conv_b=ssd_conv_b, ssd_dt_bias=ssd_dt_bias,
             ssd_a_log=ssd_a_log, ssd_d=ssd_d, ssd_norm_g=ssd_norm_g, s5_lam_re=s5_lam_re,
             s5_lam_im=s5_lam_im, s5_log_dt=s5_log_dt, s5_b_re=s5_b_re, s5_b_im=s5_b_im, s5_c_re=s5_c_re,
             s5_c_im=s5_c_im, s5_d=s5_d, s5_glu_w=s5_glu_w, s5_glu_b=s5_glu_b, gdn_conv_w=gdn_conv_w,
             gdn_conv_b=gdn_conv_b, gdn_a_log=gdn_a_log, gdn_dt_bias=gdn_dt_bias, gdn_norm_g=gdn_norm_g,
             hg_norm_g=hg_norm_g, w_out=w_out, ln_g=ln_g, ln_b=ln_b)
    bp, seq, _ = x_prompt.shape
    bs, dseq, _ = x_sample.shape
    assert dseq == SUBLANES // 2
    t_pad = PAD_FRONT + N_META + seq
    rows_p = 3 * CHUNK
    assert t_pad % rows_p == 0
    n_p = bp * t_pad
    n_s = bs * SUBLANES
    rows_s = 2 * CHUNK if n_s % (2 * CHUNK) == 0 else CHUNK
    assert n_s % rows_s == 0

    soft = jax.nn.softmax(hg_lb_raw.astype(F32), axis=0)
    csum = jnp.cumsum(soft, axis=0)
    lbs = csum - csum[0]

    meta = jnp.broadcast_to(meta_tokens[None].astype(F32), (bp, N_META, D_MODEL))
    xp = jnp.concatenate([jnp.zeros((bp, PAD_FRONT, D_MODEL), F32), meta, x_prompt.astype(F32)], axis=1)
    xp = xp.reshape(n_p, D_MODEL)
    xs = jnp.pad(x_sample.astype(F32), ((0, 0), (0, SUBLANES - dseq), (0, 0))).reshape(n_s, D_MODEL)

    tm_p = _pick_tile(n_p, (704, 512, 384, 192))
    tm_s = _pick_tile(n_s, (512, 256, 128, 64))
    g_in, b_in = ln_in_g.astype(F32)[None], ln_in_b.astype(F32)[None]
    hp = _row_tiled_call(_ln_in_kernel, n_p, tm_p, [xp], [g_in, b_in], [D_MODEL], [F32], "ln_in_prompt")
    hs = _row_tiled_call(_ln_in_kernel, n_s, tm_s, [xs], [g_in, b_in], [D_MODEL], [F32], "ln_in_sample")

    consts_p = _mixer_constants(1)
    consts_s = _mixer_constants(CHUNK // SUBLANES)
    widths = [WA, WB, WC, WD]

    pad_conv = lambda c: jnp.pad(c.astype(F32), ((0, 0), (SUBLANES - (CONV_W - 1), 0), (0, 0)))
    p_out = [[] for _ in range(7)]
    s_out = [[] for _ in range(7)]
    for l in range(DEPTH):
        lw = _layer_weights(l, p, lbs)
        us = _row_tiled_call(_inproj_kernel, n_p, tm_p, [hp], [lw['w_in']], widths, [F32] * 4, "inproj_prompt")
        mix, o_ssd, o_ca, o_re, o_im, o_gdn, o_cc, o_hg = _mixer_call(
            us, lw, consts_p, None, prompt=True, n_seq=bp, rows=rows_p)
        hp = _row_tiled_call(_outproj_kernel, n_p, tm_p, [mix, hp], [lw['w_out'], lw['ln_g'], lw['ln_b']],
                             [D_MODEL], [F32], "outproj_prompt")
        for lst, val in zip(p_out, (o_ssd, o_ca[:, SUBLANES - (CONV_W - 1):],
                                    o_re[:, 0].reshape(bp, S5_GROUPS, S5_STATE),
                                    o_im[:, 0].reshape(bp, S5_GROUPS, S5_STATE),
                                    o_gdn, o_cc[:, SUBLANES - (CONV_W - 1):], o_hg)):
            lst.append(val)
        st_in = (state_ssd[l].astype(F32), pad_conv(state_ssd_conv[l]),
                 state_s5_re[l].astype(F32).reshape(bs, S5_LANES), state_s5_im[l].astype(F32).reshape(bs, S5_LANES),
                 state_gdn[l].astype(F32), pad_conv(state_gdn_conv[l]), state_hgrn[l].astype(F32))
        us = _row_tiled_call(_inproj_kernel, n_s, tm_s, [hs], [lw['w_in']], widths, [F32] * 4, "inproj_sample")
        mix, o_ssd, o_ca, o_re, o_im, o_gdn, o_cc, o_hg = _mixer_call(
            us, lw, consts_s, st_in, prompt=False, n_seq=bs, rows=rows_s)
        hs = _row_tiled_call(_outproj_kernel, n_s, tm_s, [mix, hs], [lw['w_out'], lw['ln_g'], lw['ln_b']],
                             [D_MODEL], [F32], "outproj_sample")
        last = dseq - 1
        for lst, val in zip(s_out, (o_ssd, o_ca[:, dseq - (CONV_W - 1):dseq],
                                    o_re.reshape(bs, SUBLANES, S5_GROUPS, S5_STATE)[:, last],
                                    o_im.reshape(bs, SUBLANES, S5_GROUPS, S5_STATE)[:, last],
                                    o_gdn, o_cc[:, dseq - (CONV_W - 1):dseq], o_hg)):
            lst.append(val)

    y_prompt = hp.reshape(bp, t_pad, D_MODEL)[:, PAD_FRONT + N_META:]
    y_sample = hs.reshape(bs, SUBLANES, D_MODEL)[:, :dseq]
    return (y_prompt, y_sample, *[jnp.stack(v) for v in p_out], *[jnp.stack(v) for v in s_out])
```

```python
import functools
import math

import numpy as np
import jax
import jax.numpy as jnp
from jax import lax
from jax.experimental import pallas as pl
from jax.experimental.pallas import tpu as pltpu

F32 = jnp.float32
BF16 = jnp.bfloat16

D_MODEL = 1024
DEPTH = 4
N_META = 16
D_BRANCH = 256
HEADS = 4
PAIRS = HEADS // 2
HEAD_DIM = 64
SSD_XBC = 512
S5_GROUPS = 16
S5_STATE = 64
S5_LANES = S5_GROUPS * S5_STATE
GDN_QKV = 768
CONV_W = 4
DN_ALPHA = (2 * DEPTH) ** 0.25
LN_EPS = 1e-5
RMS_EPS = 1e-6
L2_EPS = 1e-6

CHUNK = 64
SUBLANES = 8
LANES = 128
PAD_FRONT = CHUNK - N_META
WA, WB, WC, WD = 768, 512, 1152, 1024
NROWS = 16
ROW_W = 768
VMEM_LIMIT = 56 * 1024 * 1024


def _dot(a, b):
    return jnp.dot(a.astype(BF16), b.astype(BF16), preferred_element_type=F32)


def _dot_nt(a, b):
    return lax.dot_general(a.astype(BF16), b.astype(BF16), (((1,), (1,)), ((), ())),
                           preferred_element_type=F32)


def _dot_tn(a, b):
    return lax.dot_general(a.astype(BF16), b.astype(BF16), (((0,), (0,)), ((), ())),
                           preferred_element_type=F32)


def _split(x, pieces):
    out = []
    r = x
    for i in range(pieces):
        xi = r.astype(BF16)
        out.append(xi)
        if i + 1 < pieces:
            r = r - xi.astype(F32)
    return out


def _dot01_l(w01, x, pieces=3):
    n = x.shape[-1]
    r = jnp.dot(w01, jnp.concatenate(_split(x, pieces), axis=-1), preferred_element_type=F32)
    return sum(r[:, i * n:(i + 1) * n] for i in range(pieces))


def _dot01_r(x, w01, pieces=3):
    m = x.shape[0]
    r = jnp.dot(jnp.concatenate(_split(x, pieces), axis=0), w01, preferred_element_type=F32)
    return sum(r[i * m:(i + 1) * m] for i in range(pieces))


def _dot01_nt(w01, x, pieces=3):
    d = lambda v: lax.dot_general(w01, v, (((1,), (1,)), ((), ())), preferred_element_type=F32)
    return sum(d(v) for v in _split(x, pieces))


def _sigmoid(x):
    return 1.0 / (1.0 + jnp.exp(-x))


def _silu(x):
    return x * _sigmoid(x)


def _softplus(x):
    return jnp.maximum(x, 0.0) + jnp.log1p(jnp.exp(-jnp.abs(x)))


def _gelu_tanh(x):
    c = math.sqrt(2.0 / math.pi)
    return 0.5 * x * (1.0 + jnp.tanh(c * (x + 0.044715 * (x * x * x))))


def _unit_lower_inverse(m, eye, levels, mm):
    p = eye - m
    mp = m
    for _ in range(levels):
        mp = mm(mp, mp)
        p = p + mm(p, mp)
    return p


def _run_interleaved(chains):
    chains = list(chains)
    while chains:
        alive = []
        for ch in chains:
            try:
                next(ch)
                alive.append(ch)
            except StopIteration:
                pass
        chains = alive


def _apply_state(x, s, ns):
    xb = x.reshape(ns, CHUNK // ns, x.shape[-1])
    r = lax.dot_general(xb.astype(BF16), s.astype(BF16), (((2,), (1,)), ((0,), (0,))),
                        preferred_element_type=F32)
    return r.reshape(CHUNK, s.shape[-1])


def _outer_state(a, b, ns):
    ls = CHUNK // ns
    ab = jnp.swapaxes(a.reshape(ns, ls, a.shape[-1]), 1, 2)
    bb = b.reshape(ns, ls, b.shape[-1])
    return lax.dot_general(ab.astype(BF16), bb.astype(BF16), (((2,), (1,)), ((0,), (0,))),
                           preferred_element_type=F32)


def _per_seq_rows(x, ns):
    return x.reshape(ns, CHUNK // ns, x.shape[-1])[:, 0:1, :]


def _layernorm_rows(r, g, b):
    mu = jnp.mean(r, axis=-1, keepdims=True)
    c = r - mu
    var = jnp.mean(c * c, axis=-1, keepdims=True)
    return c * lax.rsqrt(var + LN_EPS) * g + b


def _ln_in_kernel(x_ref, g_ref, b_ref, o_ref):
    o_ref[...] = _layernorm_rows(x_ref[...], g_ref[...], b_ref[...])


def _inproj_kernel(x_ref, w_ref, oa_ref, ob_ref, oc_ref, od_ref):
    x = x_ref[...].astype(BF16)
    off = 0
    for o_ref, width in ((oa_ref, WA), (ob_ref, WB), (oc_ref, WC), (od_ref, WD)):
        o_ref[...] = jnp.dot(x, w_ref[:, off:off + width], preferred_element_type=F32)
        off += width


def _outproj_kernel(mix_ref, x_ref, w_ref, g_ref, b_ref, o_ref):
    out = jnp.dot(mix_ref[...], w_ref[...], preferred_element_type=F32)
    o_ref[...] = _layernorm_rows(DN_ALPHA * x_ref[...] + out, g_ref[...], b_ref[...])


def _row_tiled_call(kernel, n_rows, tm, row_inputs, const_inputs, out_widths, out_dtypes, name):
    in_specs = [pl.BlockSpec((tm, a.shape[1]), lambda i: (i, 0)) for a in row_inputs]
    in_specs += [pl.BlockSpec(a.shape, lambda i, nd=a.ndim: (0,) * nd) for a in const_inputs]
    out_specs = [pl.BlockSpec((tm, w), lambda i: (i, 0)) for w in out_widths]
    out_shape = [jax.ShapeDtypeStruct((n_rows, w), dt) for w, dt in zip(out_widths, out_dtypes)]
    single = len(out_widths) == 1
    return pl.pallas_call(
        kernel,
        grid=(n_rows // tm,),
        in_specs=in_specs,
        out_specs=out_specs[0] if single else out_specs,
        out_shape=out_shape[0] if single else out_shape,
        compiler_params=pltpu.CompilerParams(dimension_semantics=("arbitrary",),
                                             vmem_limit_bytes=VMEM_LIMIT),
        name=name,
    )(*row_inputs, *const_inputs)


N_MIXER_CONST = 19


def _mixer_kernel(*refs, prompt, rows, ns, nsb):
    ls = CHUNK // ns
    lb = rows // nsb
    nch = rows // CHUNK
    nlev = int(math.log2(ls))
    n_in = 4 + N_MIXER_CONST + (0 if prompt else 7)
    (ua_ref, ub_ref, uc_ref, ud_ref, rows_ref, cwa_ref, cwc_ref, wbre_ref, wbim_ref, cre_ref, cim_ref,
     glu_ref, tab_ref, cmat_ref, cmask_ref, cpair_ref, bdm_ref, lr_ref, ea_ref, eb_ref, blk_ref,
     sel_ref, ones_ref) = refs[:4 + N_MIXER_CONST]
    if not prompt:
        (i_ssd, i_ca, i_s5re, i_s5im, i_gdn, i_cc, i_hg) = refs[4 + N_MIXER_CONST:n_in]
    (mix_ref, o_ssd, o_ca, o_s5re, o_s5im, o_gdn, o_cc, o_hg) = refs[n_in:n_in + 8]
    (xxa_ref, xxc_ref, acta_ref, actc_ref, hre_ref, him_ref, ssd_sc, gdn_sc, hg_sc) = refs[n_in + 8:]

    j = pl.program_id(1)

    if prompt:
        @pl.when(j == 0)
        def _():
            ssd_sc[...] = jnp.zeros_like(ssd_sc)
            gdn_sc[...] = jnp.zeros_like(gdn_sc)
            hg_sc[...] = jnp.zeros_like(hg_sc)
            o_s5re[...] = jnp.zeros_like(o_s5re)
            o_s5im[...] = jnp.zeros_like(o_s5im)
            xxa_ref[:, 0:SUBLANES, :] = jnp.zeros((nsb, SUBLANES, SSD_XBC), F32)
            xxc_ref[:, 0:SUBLANES, :] = jnp.zeros((nsb, SUBLANES, GDN_QKV), F32)
    else:
        o_ssd[...] = i_ssd[...]
        o_gdn[...] = i_gdn[...]
        o_hg[...] = i_hg[...]
        xxa_ref[:, 0:SUBLANES, :] = i_ca[...]
        xxc_ref[:, 0:SUBLANES, :] = i_cc[...]

    first_valid = jnp.where(j == 0, PAD_FRONT, 0) if prompt else None

    def valid_rows(r):
        if prompt:
            return r >= first_valid
        return (r & (SUBLANES - 1)) < (SUBLANES // 2)

    def valid_mask(width):
        return valid_rows(lax.broadcasted_iota(jnp.int32, (rows, width), 0))

    def row(i, w):
        return rows_ref[i:i + 1, 0:w]

    def conv_block(raw, xx_ref, cw_ref, bias, o_conv, act_ref, width):
        raw = jnp.where(valid_mask(width), raw, 0.0)
        raw3 = raw.reshape(nsb, lb, width)
        xx_ref[:, SUBLANES:SUBLANES + lb, :] = raw3
        acc = bias
        for w in range(CONV_W):
            lo = SUBLANES - (CONV_W - 1) + w
            acc = acc + cw_ref[w:w + 1, :] * xx_ref[:, lo:lo + lb, :]
        tail = raw3[:, lb - SUBLANES:lb, :]
        o_conv[...] = tail
        if prompt:
            xx_ref[:, 0:SUBLANES, :] = tail
        act_ref[...] = _silu(acc).reshape(rows, width)

    conv_block(ua_ref[:, 256:768], xxa_ref, cwa_ref, row(0, SSD_XBC), o_ca, acta_ref, SSD_XBC)
    conv_block(uc_ref[:, 256:1024], xxc_ref, cwc_ref, row(7, GDN_QKV), o_cc, actc_ref, GDN_QKV)

    u_b = jnp.where(valid_mask(D_BRANCH), ub_ref[:, 256:512], 0.0)
    hre_ref[...] = _dot(u_b, wbre_ref[...])
    him_ref[...] = _dot(u_b, wbim_ref[...])

    def s5_group(g, carry):
        r0 = pl.multiple_of(g * SUBLANES, SUBLANES)
        xr = hre_ref[pl.ds(r0, SUBLANES), :]
        xi = him_ref[pl.ds(r0, SUBLANES), :]
        for k, d in enumerate((1, 2, 4)):
            tr = tab_ref[k, :, 0:S5_LANES]
            ti = tab_ref[k, :, S5_LANES:2 * S5_LANES]
            sr = pltpu.roll(xr, d, axis=0)
            si = pltpu.roll(xi, d, axis=0)
            xr, xi = xr + tr * sr - ti * si, xi + tr * si + ti * sr
        if prompt:
            cr, ci = carry
        else:
            cr = jnp.broadcast_to(i_s5re[pl.ds(g, 1), :], (SUBLANES, S5_LANES))
            ci = jnp.broadcast_to(i_s5im[pl.ds(g, 1), :], (SUBLANES, S5_LANES))
        pr = tab_ref[3, :, 0:S5_LANES]
        pi = tab_ref[3, :, S5_LANES:2 * S5_LANES]
        xr, xi = xr + pr * cr - pi * ci, xi + pr * ci + pi * cr
        hre_ref[pl.ds(r0, SUBLANES), :] = xr
        him_ref[pl.ds(r0, SUBLANES), :] = xi
        if prompt:
            return (jnp.broadcast_to(xr[SUBLANES - 1:SUBLANES, :], (SUBLANES, S5_LANES)),
                    jnp.broadcast_to(xi[SUBLANES - 1:SUBLANES, :], (SUBLANES, S5_LANES)))
        return carry

    if prompt:
        cr, ci = lax.fori_loop(0, rows // SUBLANES, s5_group, (o_s5re[0], o_s5im[0]))
        o_s5re[0] = cr
        o_s5im[0] = ci
    else:
        lax.fori_loop(0, rows // SUBLANES, s5_group, 0)
        o_s5re[...] = hre_ref[...]
        o_s5im[...] = him_ref[...]

    y5 = (_dot(hre_ref[...], cre_ref[...]) - _dot(him_ref[...], cim_ref[...])
          + row(5, D_BRANCH) * ub_ref[:, 256:512])
    y5 = _gelu_tanh(y5)
    y5 = y5 * _sigmoid(_dot(y5, glu_ref[...]) + row(6, D_BRANCH))
    mix_ref[:, 256:512] = (y5 * _silu(ub_ref[:, 0:256])).astype(BF16)

    tri_b = cmat_ref[0]
    last_b = cmat_ref[1]

    def head_sum(x):
        return jnp.dot(x.astype(BF16), blk_ref[...], preferred_element_type=F32)

    def rms_finish(y, ss, g_row, z):
        return (y * lax.rsqrt(ss * (1.0 / HEAD_DIM) + RMS_EPS) * g_row * _silu(z)).astype(BF16)

    def rms_gate(y, g_row, z):
        return rms_finish(y, head_sum(y * y), g_row, z)

    def head_scalars(rs, r0):
        rr = lax.broadcasted_iota(jnp.int32, (CHUNK, LANES), 0) + r0
        lane = lax.broadcasted_iota(jnp.int32, (CHUNK, LANES), 1)
        valid = valid_rows(rr)
        t = uc_ref[rs, 1024:1152] + row(1, LANES)
        vals = jnp.where((lane >= HEADS) & (lane < 2 * HEADS), _sigmoid(t), _softplus(t) * row(2, LANES))
        vals = jnp.where(valid, vals, 0.0)
        cs = _dot01_l(tri_b, vals)
        ex_v = _dot01_r(vals, ea_ref[...], pieces=2)
        ex_c = _dot01_r(cs, eb_ref[...])
        crow = _dot01_nt(sel_ref[...], cs)
        valid_w = jnp.concatenate([valid, valid], axis=-1)
        return (ex_v[:, 0:256], ex_v[:, 256:512], ex_c[:, 0:256], ex_c[:, 256:512], crow, valid_w)

    def hgrn_inputs(rs, valid_w):
        f_d = ud_ref[rs, 512:768]
        lsig = -_softplus(-f_d)
        t1 = row(11, D_BRANCH)
        t2 = row(12, D_BRANCH) + lsig
        logf = jnp.maximum(t1, t2) + jnp.log1p(jnp.exp(-jnp.abs(t1 - t2)))
        logf = jnp.where(valid_w, logf, 0.0)
        kd = row(13, D_BRANCH) * _sigmoid(-f_d)
        qd = _silu(ud_ref[rs, 256:512])
        vd = jnp.where(valid_w, ud_ref[rs, 768:1024], 0.0)
        xall = _dot01_l(cmat_ref[2:3 + nlev].reshape((1 + nlev) * CHUNK, CHUNK), logf)
        return logf, kd, qd, vd, xall

    def prompt_block():
        lr0, lr1 = lr_ref[0], lr_ref[1]
        pairs = [slice(p * LANES, (p + 1) * LANES) for p in range(PAIRS)]
        data = [dict() for _ in range(nch)]

        def bd(x):
            xb = x.astype(BF16)
            return jnp.concatenate([xb * lr0, xb * lr1], axis=0)

        def mm_pair(a, b):
            return _dot(a, bd(b))

        def prologue(c):
            d = data[c]
            rs = slice(c * CHUNK, (c + 1) * CHUNK)
            d['rs'] = rs
            rr = lax.broadcasted_iota(jnp.int32, (CHUNK, LANES), 0) + c * CHUNK
            lane = lax.broadcasted_iota(jnp.int32, (CHUNK, LANES), 1)
            valid = valid_rows(rr)
            t = uc_ref[rs, 1024:1152] + row(1, LANES)
            vals = jnp.where((lane >= HEADS) & (lane < 2 * HEADS), _sigmoid(t), _softplus(t) * row(2, LANES))
            vals = jnp.where(valid, vals, 0.0)
            cs = _dot01_l(tri_b, vals)
            ex_v = _dot01_r(vals, ea_ref[...], pieces=2)
            q = actc_ref[rs, 0:256]
            k = actc_ref[rs, 256:512]
            qss = head_sum(q * q)
            kss = head_sum(k * k)
            valid_w = jnp.concatenate([valid, valid], axis=-1)
            logf, kd, qd, vd, xall = hgrn_inputs(rs, valid_w)
            d.update(kd=kd, qd=qd, vd=vd)
            yield
            ex_c = _dot01_r(cs, eb_ref[...])
            crow = _dot01_nt(sel_ref[...], cs)
            d['diag'] = head_sum(qd * kd)
            yield
            dt_full, beta_full = ex_v[:, 0:256], ex_v[:, 256:512]
            gc, acum = ex_c[:, 0:256], ex_c[:, 256:512]
            d['prow'] = lambda r: jnp.concatenate([crow[r:r + 1, :], crow[r + 1:r + 2, :]], axis=1)
            xs = acta_ref[rs, 0:256]
            alast = acum[CHUNK - 1:CHUNK, :]
            xdt = xs * dt_full
            d.update(acum=acum, alast=alast, xdt=xdt, xdt_end=xdt * jnp.exp(alast - acum), eacum=jnp.exp(acum))
            v = actc_ref[rs, 512:768]
            q = q * lax.rsqrt(qss + L2_EPS) * (HEAD_DIM ** -0.5)
            k = k * lax.rsqrt(kss + L2_EPS)
            glast = gc[CHUNK - 1:CHUNK, :]
            egc = jnp.exp(gc)
            d.update(q=q, k=k, gc=gc, glast=glast, beta=beta_full, vb=v * beta_full, kbe=k * beta_full * egc,
                     qe=q * egc, kend=k * jnp.exp(glast - gc))
            gcd = xall[0:CHUNK, :]
            gld = gcd[CHUNK - 1:CHUNK, :]
            g1, g2, g3 = [t.astype(F32) for t in _split(gld, 3)]
            rid = lax.broadcasted_iota(jnp.int32, (2 * SUBLANES, D_BRANCH), 0)
            d.update(qed=qd * jnp.exp(gcd), kend_d=kd * jnp.exp(gld - gcd), xall=xall,
                     g16=jnp.where(rid == 0, g1, jnp.where(rid == 1, g2, jnp.where(rid == 2, g3, 0.0))))

        def ssd_a(c, p):
            d, ps = data[c], pairs[p]
            gs = slice(256 + p * HEAD_DIM, 256 + (p + 1) * HEAD_DIM)
            bg = acta_ref[d['rs'], gs]
            cg = acta_ref[d['rs'], slice(gs.start + LANES, gs.stop + LANES)]
            cb2 = _dot_nt(cg, jnp.concatenate([bg, bg], axis=0))
            lm = jnp.exp(jnp.minimum(d['acum'][:, ps] - d['prow'](HEADS + 2 * p), 0.0)) * cpair_ref[0]
            d['cbl', p] = cb2 * lm
            d['bg', p], d['cg', p] = bg, cg
            yield

        def gdn_a(c, p):
            d, ps = data[c], pairs[p]
            k = d['k'][:, ps]
            dec = jnp.exp(jnp.minimum(d['gc'][:, ps] - d['prow'](2 * p), 0.0)) * cpair_ref[0]
            kq = _dot_nt(jnp.concatenate([k, d['q'][:, ps]], axis=0), bd(k))
            yield
            m = kq[0:CHUNK] * dec * d['beta'][:, ps] * cpair_ref[1]
            d['aq', p] = kq[CHUNK:2 * CHUNK] * dec
            acc = cpair_ref[2] - m
            mp = mm_pair(m, m)
            yield
            for _ in range(nlev - 2):
                acc_add = mm_pair(acc, mp)
                mp = mm_pair(mp, mp)
                yield
                acc = acc + acc_add
            acc = acc + mm_pair(acc, mp)
            yield
            d['uw', p] = _dot(acc, jnp.concatenate([bd(d['vb'][:, ps]), bd(d['kbe'][:, ps])], axis=1))
            yield

        def hgrn_a(c, p):
            d, ps = data[c], pairs[p]
            qd, kd, xall = d['qd'][:, ps], d['kd'][:, ps], d['xall']
            amat = jnp.zeros((CHUNK, LANES), F32)
            for lev in range(nlev):
                z = jnp.exp(xall[(1 + lev) * CHUNK:(2 + lev) * CHUNK, ps])
                amat = amat + _dot_nt(qd * z, bd(kd * z)) * cpair_ref[3 + lev]
                if lev % 2 == 1:
                    yield
            d['amat', p] = amat
            d['dcol', p] = jnp.exp(_dot_tn(d['g16'][:, ps], ones_ref[...]))
            yield

        def ssd_b():
            for c in range(nch):
                d = data[c]
                ya = []
                for p, ps in enumerate(pairs):
                    s_pk = ssd_sc[p]
                    ya.append(mm_pair(d['cbl', p], d['xdt'][:, ps]) + _dot(d['cg', p], s_pk) * d['eacum'][:, ps])
                    ssd_sc[p] = s_pk * jnp.exp(d['alast'][:, ps]) + _dot_tn(d['bg', p], d['xdt_end'][:, ps])
                yield
                ya = jnp.concatenate(ya, axis=-1) + row(3, D_BRANCH) * acta_ref[d['rs'], 0:256]
                ss = head_sum(ya * ya)
                yield
                mix_ref[d['rs'], 0:256] = rms_finish(ya, ss, row(4, D_BRANCH), ua_ref[d['rs'], 0:256])

        def gdn_b():
            bdm = bdm_ref[...]
            for c in range(nch):
                d = data[c]
                ws, s_old = [], []
                for p, ps in enumerate(pairs):
                    s_bd = gdn_sc[p]
                    s_old.append(s_bd)
                    ws.append(_dot(jnp.concatenate([d['uw', p][:, LANES:2 * LANES], d['qe'][:, ps]], axis=0), s_bd))
                yield
                yc = []
                for p, ps in enumerate(pairs):
                    v_new = d['uw', p][:, 0:LANES] - ws[p][0:CHUNK]
                    yc.append(ws[p][CHUNK:2 * CHUNK] + mm_pair(d['aq', p], v_new))
                    gdn_sc[p] = s_old[p] * jnp.exp(d['glast'][:, ps]) + _dot_tn(d['kend'][:, ps], v_new) * bdm
                yield
                yc = jnp.concatenate(yc, axis=-1)
                ss = head_sum(yc * yc)
                yield
                mix_ref[d['rs'], 512:768] = rms_finish(yc, ss, row(10, D_BRANCH), uc_ref[d['rs'], 0:256])

        def hgrn_b():
            bdm = bdm_ref[...]
            for c in range(nch):
                d = data[c]
                yd = []
                for p, ps in enumerate(pairs):
                    s_bd = hg_sc[p]
                    vd = d['vd'][:, ps]
                    yd.append(mm_pair(d['amat', p], vd) + d['diag'][:, ps] * vd + _dot(d['qed'][:, ps], s_bd))
                    hg_sc[p] = s_bd * d['dcol', p] + _dot_tn(d['kend_d'][:, ps], vd) * bdm
                yield
                yd = jnp.concatenate(yd, axis=-1)
                ss = head_sum(yd * yd)
                yield
                mix_ref[d['rs'], 768:1024] = rms_finish(yd, ss, row(14, D_BRANCH), ud_ref[d['rs'], 0:256])

        _run_interleaved([prologue(c) for c in range(nch)])
        _run_interleaved([f(c, p) for c in range(nch) for p in range(PAIRS) for f in (gdn_a, hgrn_a, ssd_a)])
        _run_interleaved([gdn_b(), hgrn_b(), ssd_b()])

    def chunk_sample(c, _):
        r0 = pl.multiple_of(c * CHUNK, CHUNK)
        rs = pl.ds(r0, CHUNK)
        sq = pl.ds(c * ns, ns)
        dt_full, beta_full, gc, acum, crow, valid_w = head_scalars(rs, r0)
        tri_f, strict_f, eye_f = cmask_ref[0], cmask_ref[1], cmask_ref[2]
        heads = [slice(h * HEAD_DIM, (h + 1) * HEAD_DIM) for h in range(HEADS)]

        xs = acta_ref[rs, 0:256]
        bm = acta_ref[rs, 256:384]
        cmm = acta_ref[rs, 384:512]
        alast = _dot01_l(last_b, acum)
        xdt = xs * dt_full
        xdt_end = xdt * jnp.exp(alast - acum)
        eacum = jnp.exp(acum)
        ya = []
        for h, hs in enumerate(heads):
            gs = heads[h // 2]
            bg, cg = bm[:, gs], cmm[:, gs]
            lm = jnp.exp(jnp.minimum(acum[:, hs] - crow[HEADS + h:HEADS + h + 1, :], 0.0)) * tri_f
            s_h = o_ssd[sq, h]
            ya.append(_dot(_dot_nt(cg, bg) * lm, xdt[:, hs]) + _apply_state(cg, s_h, ns) * eacum[:, hs])
            o_ssd[sq, h] = (s_h * jnp.exp(_per_seq_rows(alast[:, hs], ns))
                            + _outer_state(bg, xdt_end[:, hs], ns))
        ya = jnp.concatenate(ya, axis=-1) + row(3, D_BRANCH) * xs
        mix_ref[rs, 0:256] = rms_gate(ya, row(4, D_BRANCH), ua_ref[rs, 0:256])

        q = actc_ref[rs, 0:256]
        k = actc_ref[rs, 256:512]
        v = actc_ref[rs, 512:768]
        q = q * lax.rsqrt(head_sum(q * q) + L2_EPS) * (HEAD_DIM ** -0.5)
        k = k * lax.rsqrt(head_sum(k * k) + L2_EPS)
        glast = _dot01_l(last_b, gc)
        egc = jnp.exp(gc)
        vb = v * beta_full
        kbe = k * beta_full * egc
        qe = q * egc
        kend = k * jnp.exp(glast - gc)
        yc = []
        for h, hs in enumerate(heads):
            kh = k[:, hs]
            dec = jnp.exp(jnp.minimum(gc[:, hs] - crow[h:h + 1, :], 0.0)) * tri_f
            m = _dot_nt(kh, kh) * dec * beta_full[:, hs] * strict_f
            tinv = _unit_lower_inverse(m, eye_f, nlev - 1, _dot)
            u_val = _dot(tinv, vb[:, hs])
            w_key = _dot(tinv, kbe[:, hs])
            aq = _dot_nt(q[:, hs], kh) * dec
            s_h = o_gdn[sq, h]
            v_new = u_val - _apply_state(w_key, s_h, ns)
            yc.append(_apply_state(qe[:, hs], s_h, ns) + _dot(aq, v_new))
            o_gdn[sq, h] = (s_h * jnp.exp(_per_seq_rows(glast[:, hs], ns))
                            + _outer_state(kend[:, hs], v_new, ns))
        yc = jnp.concatenate(yc, axis=-1)
        mix_ref[rs, 512:768] = rms_gate(yc, row(10, D_BRANCH), uc_ref[rs, 0:256])

        logf, kd, qd, vd, xall = hgrn_inputs(rs, valid_w)
        gcd = xall[0:CHUNK, :]
        gld = _dot01_l(last_b, gcd)
        qed = qd * jnp.exp(gcd)
        kend_d = kd * jnp.exp(gld - gcd)
        diag = head_sum(qd * kd)
        lpieces = _split(logf, 3)
        ones_h = jnp.ones((CHUNK, HEAD_DIM), BF16)
        zs = [jnp.exp(xall[(1 + lev) * CHUNK:(2 + lev) * CHUNK, :]) for lev in range(nlev)]
        yd = []
        for h, hs in enumerate(heads):
            amat = jnp.zeros((CHUNK, CHUNK), F32)
            for lev in range(nlev):
                amat = amat + _dot_nt(qd[:, hs] * zs[lev][:, hs], kd[:, hs] * zs[lev][:, hs]) * cmask_ref[3 + lev]
            s_h = o_hg[sq, h]
            yd.append(_dot(amat, vd[:, hs]) + diag[:, hs] * vd[:, hs] + _apply_state(qed[:, hs], s_h, ns))
            dcol = jnp.exp(sum(_outer_state(lp[:, hs], ones_h, ns) for lp in lpieces))
            o_hg[sq, h] = s_h * dcol + _outer_state(kend_d[:, hs], vd[:, hs], ns)
        yd = jnp.concatenate(yd, axis=-1)
        mix_ref[rs, 768:1024] = rms_gate(yd, row(14, D_BRANCH), ud_ref[rs, 0:256])
        return 0

    if prompt:
        prompt_block()
    else:
        lax.fori_loop(0, nch, chunk_sample, 0)

    if prompt:
        @pl.when(j == pl.num_programs(1) - 1)
        def _():
            for p in range(PAIRS):
                for e in range(2):
                    es = slice(e * HEAD_DIM, (e + 1) * HEAD_DIM)
                    o_ssd[0, 2 * p + e] = ssd_sc[p, :, es]
                    o_gdn[0, 2 * p + e] = gdn_sc[p, es, es]
                    o_hg[0, 2 * p + e] = hg_sc[p, es, es]


def _mixer_constants(ns):
    ls = CHUNK // ns
    nlev = int(math.log2(ls))
    i = np.arange(CHUNK)[:, None]
    jn = np.arange(CHUNK)[None, :]
    same = (i // ls) == (jn // ls)
    tri = (same & (jn <= i)).astype(np.float32)
    strict = (same & (jn < i)).astype(np.float32)
    eye = np.eye(CHUNK, dtype=np.float32)
    last = (jn == (i // ls) * ls + ls - 1).astype(np.float32)
    wlev, mlev = [], []
    for lev in range(nlev):
        b = ls >> (lev + 1)
        blk_i, pos_i = i // (2 * b), i % (2 * b)
        mid = blk_i * 2 * b + b
        upper = pos_i >= b
        w = np.where(upper, (jn >= mid) & (jn <= i), (jn > i) & (jn < mid)).astype(np.float32)
        msk = (((jn // (2 * b)) == blk_i) & upper & ((jn % (2 * b)) < b)).astype(np.float32)
        wlev.append(w)
        mlev.append(msk)
    cmat = np.stack([tri, last, tri] + wlev)
    cmask = np.stack([tri, strict, eye] + mlev)
    cpair = np.concatenate([cmask, cmask], axis=-1)
    bdm = np.kron(np.eye(2, dtype=np.float32), np.ones((HEAD_DIM, HEAD_DIM), np.float32))
    lane = np.arange(LANES)[None, :]
    lr = np.stack([np.broadcast_to(lane < HEAD_DIM, (CHUNK, LANES)),
                   np.broadcast_to(lane >= HEAD_DIM, (CHUNK, LANES))]).astype(np.float32)
    e_a = np.zeros((LANES, 512), np.float32)
    e_b = np.zeros((LANES, 512), np.float32)
    blk = np.zeros((256, 256), np.float32)
    sel = np.zeros((2 * SUBLANES, LANES), np.float32)
    for h in range(HEADS):
        hs = slice(h * HEAD_DIM, (h + 1) * HEAD_DIM)
        e_a[h, hs] = 1.0
        e_a[HEADS + h, 256 + h * HEAD_DIM:256 + (h + 1) * HEAD_DIM] = 1.0
        e_b[2 * HEADS + h, hs] = 1.0
        e_b[3 * HEADS + h, 256 + h * HEAD_DIM:256 + (h + 1) * HEAD_DIM] = 1.0
        blk[hs, hs] = 1.0
        sel[h, 2 * HEADS + h] = 1.0
        sel[HEADS + h, 3 * HEADS + h] = 1.0
    ones = np.ones((2 * SUBLANES, LANES), np.float32)
    return (jnp.asarray(cmat, BF16), jnp.asarray(cmask, F32), jnp.asarray(cpair, F32), jnp.asarray(bdm, F32),
            jnp.asarray(lr, BF16), jnp.asarray(e_a, BF16), jnp.asarray(e_b, BF16), jnp.asarray(blk, BF16),
            jnp.asarray(sel, BF16), jnp.asarray(ones, BF16))


def _mixer_call(us, lw, consts, states, *, prompt, n_seq, rows):
    ua, ub, uc, ud = us
    n_rows = ua.shape[0]
    if prompt:
        ns, nsb = 1, 1
        nblk = n_rows // n_seq // rows
        grid = (n_seq, nblk)
        rmap = lambda b, j: (b * nblk + j, 0)
        smap4 = lambda b, j: (b, 0, 0, 0)
        smap3 = lambda b, j: (b, 0, 0)
    else:
        ns = CHUNK // SUBLANES
        nsb = rows // SUBLANES
        grid = (1, n_rows // rows)
        rmap = lambda b, j: (j, 0)
        smap4 = lambda b, j: (j, 0, 0, 0)
        smap3 = lambda b, j: (j, 0, 0)
    const_inputs = [lw['rows'], lw['cw_a'], lw['cw_c'], lw['wb_re'], lw['wb_im'], lw['c_re'], lw['c_im'],
                    lw['glu_w'], lw['s5tab'], *consts]
    assert len(const_inputs) == N_MIXER_CONST
    in_specs = [pl.BlockSpec((rows, w), rmap) for w in (WA, WB, WC, WD)]
    in_specs += [pl.BlockSpec(a.shape, lambda b, j, nd=a.ndim: (0,) * nd) for a in const_inputs]
    inputs = [ua, ub, uc, ud] + const_inputs
    st4 = (nsb, HEADS, HEAD_DIM, HEAD_DIM)
    if not prompt:
        inputs += list(states)
        in_specs += [pl.BlockSpec(st4, smap4),
                     pl.BlockSpec((nsb, SUBLANES, SSD_XBC), smap3),
                     pl.BlockSpec((nsb, S5_LANES), rmap),
                     pl.BlockSpec((nsb, S5_LANES), rmap),
                     pl.BlockSpec(st4, smap4),
                     pl.BlockSpec((nsb, SUBLANES, GDN_QKV), smap3),
                     pl.BlockSpec(st4, smap4)]
    n_st = n_seq
    if prompt:
        s5_shape, s5_spec = (n_st, SUBLANES, S5_LANES), pl.BlockSpec((1, SUBLANES, S5_LANES), smap3)
    else:
        s5_shape, s5_spec = (n_rows, S5_LANES), pl.BlockSpec((rows, S5_LANES), rmap)
    out_shape = [jax.ShapeDtypeStruct((n_rows, D_MODEL), BF16),
                 jax.ShapeDtypeStruct((n_st,) + st4[1:], F32),
                 jax.ShapeDtypeStruct((n_st, SUBLANES, SSD_XBC), F32),
                 jax.ShapeDtypeStruct(s5_shape, F32),
                 jax.ShapeDtypeStruct(s5_shape, F32),
                 jax.ShapeDtypeStruct((n_st,) + st4[1:], F32),
                 jax.ShapeDtypeStruct((n_st, SUBLANES, GDN_QKV), F32),
                 jax.ShapeDtypeStruct((n_st,) + st4[1:], F32)]
    out_specs = [pl.BlockSpec((rows, D_MODEL), rmap),
                 pl.BlockSpec(st4, smap4),
                 pl.BlockSpec((nsb, SUBLANES, SSD_XBC), smap3),
                 s5_spec, s5_spec,
                 pl.BlockSpec(st4, smap4),
                 pl.BlockSpec((nsb, SUBLANES, GDN_QKV), smap3),
                 pl.BlockSpec(st4, smap4)]
    lb = rows // nsb
    scratch = [pltpu.VMEM((nsb, SUBLANES + lb, SSD_XBC), F32),
               pltpu.VMEM((nsb, SUBLANES + lb, GDN_QKV), F32),
               pltpu.VMEM((rows, SSD_XBC), F32),
               pltpu.VMEM((rows, GDN_QKV), F32),
               pltpu.VMEM((rows, S5_LANES), F32),
               pltpu.VMEM((rows, S5_LANES), F32),
               pltpu.VMEM((PAIRS, HEAD_DIM, LANES), F32),
               pltpu.VMEM((PAIRS, LANES, LANES), F32),
               pltpu.VMEM((PAIRS, LANES, LANES), F32)]
    return pl.pallas_call(
        functools.partial(_mixer_kernel, prompt=prompt, rows=rows, ns=ns, nsb=nsb),
        grid=grid,
        in_specs=in_specs,
        out_specs=out_specs,
        out_shape=out_shape,
        scratch_shapes=scratch,
        compiler_params=pltpu.CompilerParams(dimension_semantics=("arbitrary", "arbitrary"),
                                             vmem_limit_bytes=VMEM_LIMIT),
        name="mixer_prompt" if prompt else "mixer_sample",
    )(*inputs)


def _pad_row(v, width=ROW_W):
    v = v.astype(F32).reshape(-1)
    return jnp.pad(v, (0, width - v.shape[0]))


def _layer_weights(l, p, lbs):
    w = p['w_in'][l]
    small = jnp.concatenate([w[:, 768:772], w[:, 2308:2316], w[:, 768:772],
                             jnp.zeros((D_MODEL, LANES - 4 * HEADS), F32)], axis=1)
    w_r = jnp.concatenate([w[:, 0:768], w[:, 772:1284], w[:, 1284:2308], small, w[:, 2316:3340]],
                          axis=1).astype(BF16)

    rep = lambda v: jnp.repeat(v.astype(F32), HEAD_DIM)
    lb = lbs[l]
    zeros_h = jnp.zeros((HEADS,), F32)
    small_bias = jnp.concatenate([p['ssd_dt_bias'][l], zeros_h, p['gdn_dt_bias'][l], p['ssd_dt_bias'][l]])
    small_scale = jnp.concatenate([jnp.ones((HEADS,), F32), zeros_h, -jnp.exp(p['gdn_a_log'][l]),
                                   -jnp.exp(p['ssd_a_log'][l])])
    zrow = jnp.zeros((ROW_W,), F32)
    rows = jnp.stack([
        _pad_row(p['ssd_conv_b'][l]), _pad_row(small_bias), _pad_row(small_scale),
        _pad_row(rep(p['ssd_d'][l])), _pad_row(p['ssd_norm_g'][l]), _pad_row(p['s5_d'][l]),
        _pad_row(p['s5_glu_b'][l]), _pad_row(p['gdn_conv_b'][l]), zrow, zrow,
        _pad_row(p['gdn_norm_g'][l]), _pad_row(jnp.log(lb)), _pad_row(jnp.log1p(-lb)), _pad_row(1.0 - lb),
        _pad_row(p['hg_norm_g'][l]), zrow])

    pad_cw = lambda cw: jnp.pad(cw.astype(F32), ((0, SUBLANES - CONV_W), (0, 0)))

    lam_re, lam_im = p['s5_lam_re'][l], p['s5_lam_im'][l]
    dt = jnp.exp(p['s5_log_dt'][l])[:, None]
    mag = jnp.exp(lam_re * dt)
    ang = lam_im * dt
    lb_re, lb_im = mag * jnp.cos(ang), mag * jnp.sin(ang)
    den = jnp.square(lam_re) + jnp.square(lam_im)
    nr = lb_re - 1.0
    coef_re = (nr * lam_re + lb_im * lam_im) / den
    coef_im = (lb_im * lam_re - nr * lam_im) / den
    b_re, b_im = p['s5_b_re'][l], p['s5_b_im'][l]
    bb_re = coef_re[..., None] * b_re - coef_im[..., None] * b_im
    bb_im = coef_re[..., None] * b_im + coef_im[..., None] * b_re
    eye_g = jnp.eye(S5_GROUPS, dtype=F32)
    bd_in = lambda bb: jnp.einsum('gnq,gh->gqhn', bb, eye_g).reshape(D_BRANCH, S5_LANES).astype(BF16)
    bd_out = lambda c: jnp.einsum('gqn,gh->gnhq', c, eye_g).reshape(S5_LANES, D_BRANCH).astype(BF16)

    pr, pi = [lb_re.reshape(-1)], [lb_im.reshape(-1)]
    for _ in range(SUBLANES - 1):
        pr, pi = (pr + [pr[-1] * pr[0] - pi[-1] * pi[0]], pi + [pr[-1] * pi[0] + pi[-1] * pr[0]])
    pw = jnp.stack([jnp.concatenate([a, b]) for a, b in zip(pr, pi)])
    ridx = jnp.arange(SUBLANES)[:, None]
    tabs = [jnp.where(ridx >= d, pw[d - 1][None, :], 0.0) for d in (1, 2, 4)] + [pw]
    s5tab = jnp.stack(tabs).astype(F32)

    return dict(w_in=w_r, rows=rows, cw_a=pad_cw(p['ssd_conv_w'][l]), cw_c=pad_cw(p['gdn_conv_w'][l]),
                wb_re=bd_in(bb_re), wb_im=bd_in(bb_im), c_re=bd_out(p['s5_c_re'][l]),
                c_im=bd_out(p['s5_c_im'][l]), glu_w=p['s5_glu_w'][l].astype(BF16), s5tab=s5tab,
                w_out=p['w_out'][l].astype(BF16),
                ln_g=p['ln_g'][l].astype(F32)[None], ln_b=p['ln_b'][l].astype(F32)[None])


def _pick_tile(n_rows, candidates):
    for t in candidates:
        if n_rows % t == 0:
            return t
    raise ValueError(f"no row tile for {n_rows}")


def kernel(x_prompt, x_sample, state_ssd, state_ssd_conv, state_s5_re, state_s5_im, state_gdn, state_gdn_conv, state_hgrn, meta_tokens, ln_in_g, ln_in_b, w_in, ssd_conv_w, ssd_conv_b, ssd_dt_bias, ssd_a_log, ssd_d, ssd_norm_g, s5_lam_re, s5_lam_im, s5_log_dt, s5_b_re, s5_b_im, s5_c_re, s5_c_im, s5_d, s5_glu_w, s5_glu_b, gdn_conv_w, gdn_conv_b, gdn_a_log, gdn_dt_bias, gdn_norm_g, hg_lb_raw, hg_norm_g, w_out, ln_g, ln_b):
    p = dict(w_in=w_in, ssd_conv_w=ssd_conv_w, ssd_conv_b=ssd_conv_b, ssd_dt_bias=ssd_dt_bias,
             ssd_a_log=ssd_a_log, ssd_d=ssd_d, ssd_norm_g=ssd_norm_g, s5_lam_re=s5_lam_re,
             s5_lam_im=s5_lam_im, s5_log_dt=s5_log_dt, s5_b_re=s5_b_re, s5_b_im=s5_b_im, s5_c_re=s5_c_re,
             s5_c_im=s5_c_im, s5_d=s5_d, s5_glu_w=s5_glu_w, s5_glu_b=s5_glu_b, gdn_conv_w=gdn_conv_w,
             gdn_conv_b=gdn_conv_b, gdn_a_log=gdn_a_log, gdn_dt_bias=gdn_dt_bias, gdn_norm_g=gdn_norm_g,
             hg_norm_g=hg_norm_g, w_out=w_out, ln_g=ln_g, ln_b=ln_b)
    bp, seq, _ = x_prompt.shape
    bs, dseq, _ = x_sample.shape
    assert dseq == SUBLANES // 2
    t_pad = PAD_FRONT + N_META + seq
    rows_p = 3 * CHUNK
    assert t_pad % rows_p == 0
    n_p = bp * t_pad
    n_s = bs * SUBLANES
    rows_s = 2 * CHUNK if n_s % (2 * CHUNK) == 0 else CHUNK
    assert n_s % rows_s == 0

    soft = jax.nn.softmax(hg_lb_raw.astype(F32), axis=0)
    csum = jnp.cumsum(soft, axis=0)
    lbs = csum - csum[0]

    meta = jnp.broadcast_to(meta_tokens[None].astype(F32), (bp, N_META, D_MODEL))
    xp = jnp.concatenate([jnp.zeros((bp, PAD_FRONT, D_MODEL), F32), meta, x_prompt.astype(F32)], axis=1)
    xp = xp.reshape(n_p, D_MODEL)
    xs = jnp.pad(x_sample.astype(F32), ((0, 0), (0, SUBLANES - dseq), (0, 0))).reshape(n_s, D_MODEL)

    tm_p = _pick_tile(n_p, (704, 512, 384, 192))
    tm_s = _pick_tile(n_s, (512, 256, 128, 64))
    g_in, b_in = ln_in_g.astype(F32)[None], ln_in_b.astype(F32)[None]
    hp = _row_tiled_call(_ln_in_kernel, n_p, tm_p, [xp], [g_in, b_in], [D_MODEL], [F32], "ln_in_prompt")
    hs = _row_tiled_call(_ln_in_kernel, n_s, tm_s, [xs], [g_in, b_in], [D_MODEL], [F32], "ln_in_sample")

    consts_p = _mixer_constants(1)
    consts_s = _mixer_constants(CHUNK // SUBLANES)
    widths = [WA, WB, WC, WD]

    pad_conv = lambda c: jnp.pad(c.astype(F32), ((0, 0), (SUBLANES - (CONV_W - 1), 0), (0, 0)))
    p_out = [[] for _ in range(7)]
    s_out = [[] for _ in range(7)]
    for l in range(DEPTH):
        lw = _layer_weights(l, p, lbs)
        us = _row_tiled_call(_inproj_kernel, n_p, tm_p, [hp], [lw['w_in']], widths, [F32] * 4, "inproj_prompt")
        mix, o_ssd, o_ca, o_re, o_im, o_gdn, o_cc, o_hg = _mixer_call(
            us, lw, consts_p, None, prompt=True, n_seq=bp, rows=rows_p)
        hp = _row_tiled_call(_outproj_kernel, n_p, tm_p, [mix, hp], [lw['w_out'], lw['ln_g'], lw['ln_b']],
                             [D_MODEL], [F32], "outproj_prompt")
        for lst, val in zip(p_out, (o_ssd, o_ca[:, SUBLANES - (CONV_W - 1):],
                                    o_re[:, 0].reshape(bp, S5_GROUPS, S5_STATE),
                                    o_im[:, 0].reshape(bp, S5_GROUPS, S5_STATE),
                                    o_gdn, o_cc[:, SUBLANES - (CONV_W - 1):], o_hg)):
            lst.append(val)
        st_in = (state_ssd[l].astype(F32), pad_conv(state_ssd_conv[l]),
                 state_s5_re[l].astype(F32).reshape(bs, S5_LANES), state_s5_im[l].astype(F32).reshape(bs, S5_LANES),
                 state_gdn[l].astype(F32), pad_conv(state_gdn_conv[l]), state_hgrn[l].astype(F32))
        us = _row_tiled_call(_inproj_kernel, n_s, tm_s, [hs], [lw['w_in']], widths, [F32] * 4, "inproj_sample")
        mix, o_ssd, o_ca, o_re, o_im, o_gdn, o_cc, o_hg = _mixer_call(
            us, lw, consts_s, st_in, prompt=False, n_seq=bs, rows=rows_s)
        hs = _row_tiled_call(_outproj_kernel, n_s, tm_s, [mix, hs], [lw['w_out'], lw['ln_g'], lw['ln_b']],
                             [D_MODEL], [F32], "outproj_sample")
        last = dseq - 1
        for lst, val in zip(s_out, (o_ssd, o_ca[:, dseq - (CONV_W - 1):dseq],
                                    o_re.reshape(bs, SUBLANES, S5_GROUPS, S5_STATE)[:, last],
                                    o_im.reshape(bs, SUBLANES, S5_GROUPS, S5_STATE)[:, last],
                                    o_gdn, o_cc[:, dseq - (CONV_W - 1):dseq], o_hg)):
            lst.append(val)

    y_prompt = hp.reshape(bp, t_pad, D_MODEL)[:, PAD_FRONT + N_META:]
    y_sample = hs.reshape(bs, SUBLANES, D_MODEL)[:, :dseq]
    return (y_prompt, y_sample, *[jnp.stack(v) for v in p_out], *[jnp.stack(v) for v in s_out])
```

```python
import functools
import math

import numpy as np
import jax
import jax.numpy as jnp
from jax import lax
from jax.experimental import pallas as pl
from jax.experimental.pallas import tpu as pltpu

F32 = jnp.float32
BF16 = jnp.bfloat16

D_MODEL = 1024
DEPTH = 4
N_META = 16
D_BRANCH = 256
HEADS = 4
PAIRS = HEADS // 2
HEAD_DIM = 64
SSD_XBC = 512
S5_GROUPS = 16
S5_STATE = 64
S5_LANES = S5_GROUPS * S5_STATE
GDN_QKV = 768
CONV_W = 4
DN_ALPHA = (2 * DEPTH) ** 0.25
LN_EPS = 1e-5
RMS_EPS = 1e-6
L2_EPS = 1e-6

CHUNK = 64
SUBLANES = 8
LANES = 128
PAD_FRONT = CHUNK - N_META
WA, WB, WC, WD = 768, 512, 1152, 1024
NROWS = 16
ROW_W = 768
VMEM_LIMIT = 56 * 1024 * 1024


def _dot(a, b):
    return jnp.dot(a.astype(BF16), b.astype(BF16), preferred_element_type=F32)


def _dot_nt(a, b):
    return lax.dot_general(a.astype(BF16), b.astype(BF16), (((1,), (1,)), ((), ())),
                           preferred_element_type=F32)


def _dot_tn(a, b):
    return lax.dot_general(a.astype(BF16), b.astype(BF16), (((0,), (0,)), ((), ())),
                           preferred_element_type=F32)


def _split(x, pieces):
    out = []
    r = x
    for i in range(pieces):
        xi = r.astype(BF16)
        out.append(xi)
        if i + 1 < pieces:
            r = r - xi.astype(F32)
    return out


def _dot01_l(w01, x, pieces=3):
    n = x.shape[-1]
    r = jnp.dot(w01, jnp.concatenate(_split(x, pieces), axis=-1), preferred_element_type=F32)
    return sum(r[:, i * n:(i + 1) * n] for i in range(pieces))


def _dot01_r(x, w01, pieces=3):
    m = x.shape[0]
    r = jnp.dot(jnp.concatenate(_split(x, pieces), axis=0), w01, preferred_element_type=F32)
    return sum(r[i * m:(i + 1) * m] for i in range(pieces))


def _dot01_nt(w01, x, pieces=3):
    d = lambda v: lax.dot_general(w01, v, (((1,), (1,)), ((), ())), preferred_element_type=F32)
    return sum(d(v) for v in _split(x, pieces))


def _sigmoid(x):
    return 1.0 / (1.0 + jnp.exp(-x))


def _silu(x):
    return x * _sigmoid(x)


def _softplus(x):
    return jnp.maximum(x, 0.0) + jnp.log1p(jnp.exp(-jnp.abs(x)))


def _gelu_tanh(x):
    c = math.sqrt(2.0 / math.pi)
    return 0.5 * x * (1.0 + jnp.tanh(c * (x + 0.044715 * (x * x * x))))


def _unit_lower_inverse(m, eye, levels, mm):
    p = eye - m
    mp = m
    for _ in range(levels):
        mp = mm(mp, mp)
        p = p + mm(p, mp)
    return p


def _run_interleaved(chains):
    chains = list(chains)
    while chains:
        alive = []
        for ch in chains:
            try:
                next(ch)
                alive.append(ch)
            except StopIteration:
                pass
        chains = alive


def _apply_state(x, s, ns):
    xb = x.reshape(ns, CHUNK // ns, x.shape[-1])
    r = lax.dot_general(xb.astype(BF16), s.astype(BF16), (((2,), (1,)), ((0,), (0,))),
                        preferred_element_type=F32)
    return r.reshape(CHUNK, s.shape[-1])


def _outer_state(a, b, ns):
    ls = CHUNK // ns
    ab = jnp.swapaxes(a.reshape(ns, ls, a.shape[-1]), 1, 2)
    bb = b.reshape(ns, ls, b.shape[-1])
    return lax.dot_general(ab.astype(BF16), bb.astype(BF16), (((2,), (1,)), ((0,), (0,))),
                           preferred_element_type=F32)


def _per_seq_rows(x, ns):
    return x.reshape(ns, CHUNK // ns, x.shape[-1])[:, 0:1, :]


def _layernorm_rows(r, g, b):
    mu = jnp.mean(r, axis=-1, keepdims=True)
    c = r - mu
    var = jnp.mean(c * c, axis=-1, keepdims=True)
    return c * lax.rsqrt(var + LN_EPS) * g + b


def _ln_in_kernel(x_ref, g_ref, b_ref, o_ref):
    o_ref[...] = _layernorm_rows(x_ref[...], g_ref[...], b_ref[...])


def _inproj_kernel(x_ref, w_ref, oa_ref, ob_ref, oc_ref, od_ref):
    x = x_ref[...].astype(BF16)
    off = 0
    for o_ref, width in ((oa_ref, WA), (ob_ref, WB), (oc_ref, WC), (od_ref, WD)):
        o_ref[...] = jnp.dot(x, w_ref[:, off:off + width], preferred_element_type=F32)
        off += width


def _inproj_conv_kernel(x_ref, w_ref, rows_ref, cwa_ref, cwc_ref, oa_ref, ob_ref, oc_ref, od_ref, tail_ref,
                        xx_ref, *, tiles_per_seq):
    tm = x_ref.shape[0]
    t = pl.program_id(0) % tiles_per_seq
    x = x_ref[...].astype(BF16)

    @pl.when(t == 0)
    def _():
        xx_ref[0:SUBLANES, :] = jnp.zeros((SUBLANES, SSD_XBC + GDN_QKV), F32)

    first_valid = jnp.where(t == 0, PAD_FRONT, 0)

    def conv(raw, c0, width, cw_ref, bias_row, o_ref):
        r = lax.broadcasted_iota(jnp.int32, (tm, width), 0)
        raw = jnp.where(r >= first_valid, raw, 0.0)
        xx_ref[SUBLANES:SUBLANES + tm, c0:c0 + width] = raw
        acc = rows_ref[bias_row:bias_row + 1, 0:width]
        for w in range(CONV_W):
            lo = SUBLANES - (CONV_W - 1) + w
            acc = acc + cw_ref[w:w + 1, :] * xx_ref[lo:lo + tm, c0:c0 + width]
        o_ref[:, 256:256 + width] = _silu(acc)
        tail = raw[tm - SUBLANES:tm, :]
        tail_ref[:, c0:c0 + width] = tail
        xx_ref[0:SUBLANES, c0:c0 + width] = tail

    ua = jnp.dot(x, w_ref[:, 0:WA], preferred_element_type=F32)
    oa_ref[:, 0:256] = ua[:, 0:256]
    uc = jnp.dot(x, w_ref[:, WA + WB:WA + WB + WC], preferred_element_type=F32)
    conv(ua[:, 256:WA], 0, SSD_XBC, cwa_ref, 0, oa_ref)
    oc_ref[:, 0:256] = uc[:, 0:256]
    oc_ref[:, 1024:WC] = uc[:, 1024:WC]
    ob_ref[...] = jnp.dot(x, w_ref[:, WA:WA + WB], preferred_element_type=F32)
    conv(uc[:, 256:1024], SSD_XBC, GDN_QKV, cwc_ref, 7, oc_ref)
    od_ref[...] = jnp.dot(x, w_ref[:, WA + WB + WC:WA + WB + WC + WD], preferred_element_type=F32)


def _inproj_conv_call(hp, l, wts, n_seq, tm):
    n_rows = hp.shape[0]
    tiles_per_seq = n_rows // n_seq // tm
    consts = [wts['w_in'], wts['rows'], wts['cw_a'], wts['cw_c']]
    in_specs = [pl.BlockSpec((tm, D_MODEL), lambda i: (i, 0))]
    in_specs += [pl.BlockSpec((None,) + a.shape[1:], lambda i, nd=a.ndim - 1: (l,) + (0,) * nd) for a in consts]
    widths = (WA, WB, WC, WD)
    out_specs = [pl.BlockSpec((tm, w), lambda i: (i, 0)) for w in widths]
    out_specs += [pl.BlockSpec((None, SUBLANES, SSD_XBC + GDN_QKV), lambda i: (i // tiles_per_seq, 0, 0))]
    out_shape = [jax.ShapeDtypeStruct((n_rows, w), F32) for w in widths]
    out_shape += [jax.ShapeDtypeStruct((n_seq, SUBLANES, SSD_XBC + GDN_QKV), F32)]
    *us, tails = pl.pallas_call(
        functools.partial(_inproj_conv_kernel, tiles_per_seq=tiles_per_seq),
        grid=(n_rows // tm,),
        in_specs=in_specs,
        out_specs=out_specs,
        out_shape=out_shape,
        scratch_shapes=[pltpu.VMEM((SUBLANES + tm, SSD_XBC + GDN_QKV), F32)],
        compiler_params=pltpu.CompilerParams(dimension_semantics=("arbitrary",),
                                             vmem_limit_bytes=VMEM_LIMIT),
        name="inproj_conv_prompt",
    )(hp, *consts)
    return us, tails


def _outproj_kernel(mix_ref, x_ref, w_ref, g_ref, b_ref, o_ref):
    out = jnp.dot(mix_ref[...], w_ref[...], preferred_element_type=F32)
    o_ref[...] = _layernorm_rows(DN_ALPHA * x_ref[...] + out, g_ref[...], b_ref[...])


def _row_tiled_call(kernel, n_rows, tm, row_inputs, const_inputs, out_widths, out_dtypes, name, layer=None):
    in_specs = [pl.BlockSpec((tm, a.shape[1]), lambda i: (i, 0)) for a in row_inputs]
    if layer is None:
        in_specs += [pl.BlockSpec(a.shape, lambda i, nd=a.ndim: (0,) * nd) for a in const_inputs]
    else:
        in_specs += [pl.BlockSpec((None,) + a.shape[1:], lambda i, nd=a.ndim - 1: (layer,) + (0,) * nd)
                     for a in const_inputs]
    out_specs = [pl.BlockSpec((tm, w), lambda i: (i, 0)) for w in out_widths]
    out_shape = [jax.ShapeDtypeStruct((n_rows, w), dt) for w, dt in zip(out_widths, out_dtypes)]
    single = len(out_widths) == 1
    return pl.pallas_call(
        kernel,
        grid=(n_rows // tm,),
        in_specs=in_specs,
        out_specs=out_specs[0] if single else out_specs,
        out_shape=out_shape[0] if single else out_shape,
        compiler_params=pltpu.CompilerParams(dimension_semantics=("arbitrary",),
                                             vmem_limit_bytes=VMEM_LIMIT),
        name=name,
    )(*row_inputs, *const_inputs)


N_MIXER_CONST = 19
N_STATES = 7


def _mixer_kernel(*refs, prompt, rows, ns, nsb):
    ls = CHUNK // ns
    lb = rows // nsb
    nch = rows // CHUNK
    nlev = int(math.log2(ls))
    n_states = N_STATES - 2 if prompt else N_STATES
    n_state_in = 0 if prompt else N_STATES
    n_in = 4 + N_MIXER_CONST + n_state_in + n_states
    (ua_ref, ub_ref, uc_ref, ud_ref, rows_ref, cwa_ref, cwc_ref, wbre_ref, wbim_ref, cre_ref, cim_ref,
     glu_ref, tab_ref, cmat_ref, cmask_ref, cpair_ref, bdm_ref, lr_ref, ea_ref, eb_ref, blk_ref,
     sel_ref, ones_ref) = refs[:4 + N_MIXER_CONST]
    if prompt:
        (mix_ref, o_ssd, o_s5re, o_s5im, o_gdn, o_hg) = refs[n_in:n_in + 1 + n_states]
        (hre_ref, him_ref, ssd_sc, gdn_sc, hg_sc) = refs[n_in + 1 + n_states:]
    else:
        (i_ssd, i_ca, i_s5re, i_s5im, i_gdn, i_cc, i_hg) = refs[4 + N_MIXER_CONST:4 + N_MIXER_CONST + N_STATES]
        (mix_ref, o_ssd, o_ca, o_s5re, o_s5im, o_gdn, o_cc, o_hg) = refs[n_in:n_in + 1 + n_states]
        (xxa_ref, xxc_ref, acta_ref, actc_ref, hre_ref, him_ref) = refs[n_in + 1 + n_states:]

    j = pl.program_id(1)

    if prompt:
        @pl.when(j == 0)
        def _():
            ssd_sc[...] = jnp.zeros_like(ssd_sc)
            gdn_sc[...] = jnp.zeros_like(gdn_sc)
            hg_sc[...] = jnp.zeros_like(hg_sc)
            o_s5re[...] = jnp.zeros_like(o_s5re)
            o_s5im[...] = jnp.zeros_like(o_s5im)
    else:
        o_ssd[...] = i_ssd[...]
        o_gdn[...] = i_gdn[...]
        o_hg[...] = i_hg[...]
        xxa_ref[:, SUBLANES - (CONV_W - 1):SUBLANES, :] = i_ca[...]
        xxc_ref[:, SUBLANES - (CONV_W - 1):SUBLANES, :] = i_cc[...]

    first_valid = jnp.where(j == 0, PAD_FRONT, 0) if prompt else None

    def valid_rows(r):
        if prompt:
            return r >= first_valid
        return (r & (SUBLANES - 1)) < (SUBLANES // 2)

    def valid_mask(width):
        return valid_rows(lax.broadcasted_iota(jnp.int32, (rows, width), 0))

    def row(i, w):
        return rows_ref[i:i + 1, 0:w]

    def conv_block(raw, xx_ref, cw_ref, bias, o_conv, act_ref, width):
        raw = jnp.where(valid_mask(width), raw, 0.0)
        raw3 = raw.reshape(nsb, lb, width)
        xx_ref[:, SUBLANES:SUBLANES + lb, :] = raw3
        acc = bias
        for w in range(CONV_W):
            lo = SUBLANES - (CONV_W - 1) + w
            acc = acc + cw_ref[w:w + 1, :] * xx_ref[:, lo:lo + lb, :]
        o_conv[...] = raw3[:, SUBLANES // 2 - (CONV_W - 1):SUBLANES // 2, :]
        act_ref[...] = _silu(acc).reshape(rows, width)

    if prompt:
        act_a = lambda rs, lo, hi: ua_ref[rs, 256 + lo:256 + hi]
        act_c = lambda rs, lo, hi: uc_ref[rs, 256 + lo:256 + hi]
    else:
        conv_block(ua_ref[:, 256:768], xxa_ref, cwa_ref, row(0, SSD_XBC), o_ca, acta_ref, SSD_XBC)
        conv_block(uc_ref[:, 256:1024], xxc_ref, cwc_ref, row(7, GDN_QKV), o_cc, actc_ref, GDN_QKV)
        act_a = lambda rs, lo, hi: acta_ref[rs, lo:hi]
        act_c = lambda rs, lo, hi: actc_ref[rs, lo:hi]

    u_b = jnp.where(valid_mask(D_BRANCH), ub_ref[:, 256:512], 0.0)
    hre_ref[...] = _dot(u_b, wbre_ref[...])
    him_ref[...] = _dot(u_b, wbim_ref[...])

    def s5_group(g, carry):
        r0 = pl.multiple_of(g * SUBLANES, SUBLANES)
        xr = hre_ref[pl.ds(r0, SUBLANES), :]
        xi = him_ref[pl.ds(r0, SUBLANES), :]
        for k, d in enumerate((1, 2, 4)):
            tr = tab_ref[k, :, 0:S5_LANES]
            ti = tab_ref[k, :, S5_LANES:2 * S5_LANES]
            sr = pltpu.roll(xr, d, axis=0)
            si = pltpu.roll(xi, d, axis=0)
            xr, xi = xr + tr * sr - ti * si, xi + tr * si + ti * sr
        if prompt:
            cr, ci = carry
        else:
            cr = jnp.broadcast_to(i_s5re[pl.ds(g, 1), :], (SUBLANES, S5_LANES))
            ci = jnp.broadcast_to(i_s5im[pl.ds(g, 1), :], (SUBLANES, S5_LANES))
        pr = tab_ref[3, :, 0:S5_LANES]
        pi = tab_ref[3, :, S5_LANES:2 * S5_LANES]
        xr, xi = xr + pr * cr - pi * ci, xi + pr * ci + pi * cr
        hre_ref[pl.ds(r0, SUBLANES), :] = xr
        him_ref[pl.ds(r0, SUBLANES), :] = xi
        if prompt:
            return (jnp.broadcast_to(xr[SUBLANES - 1:SUBLANES, :], (SUBLANES, S5_LANES)),
                    jnp.broadcast_to(xi[SUBLANES - 1:SUBLANES, :], (SUBLANES, S5_LANES)))
        o_s5re[pl.ds(g, 1), :] = xr[SUBLANES // 2 - 1:SUBLANES // 2, :]
        o_s5im[pl.ds(g, 1), :] = xi[SUBLANES // 2 - 1:SUBLANES // 2, :]
        return carry

    if prompt:
        cr, ci = lax.fori_loop(0, rows // SUBLANES, s5_group, (o_s5re[0], o_s5im[0]))
        o_s5re[0] = cr
        o_s5im[0] = ci
    else:
        lax.fori_loop(0, rows // SUBLANES, s5_group, 0)

    y5 = (_dot(hre_ref[...], cre_ref[...]) - _dot(him_ref[...], cim_ref[...])
          + row(5, D_BRANCH) * ub_ref[:, 256:512])
    y5 = _gelu_tanh(y5)
    y5 = y5 * _sigmoid(_dot(y5, glu_ref[...]) + row(6, D_BRANCH))
    mix_ref[:, 256:512] = (y5 * _silu(ub_ref[:, 0:256])).astype(BF16)

    tri_b = cmat_ref[0]
    last_b = cmat_ref[1]

    def head_sum(x):
        return jnp.dot(x.astype(BF16), blk_ref[...], preferred_element_type=F32)

    def rms_finish(y, ss, g_row, z):
        return (y * lax.rsqrt(ss * (1.0 / HEAD_DIM) + RMS_EPS) * g_row * _silu(z)).astype(BF16)

    def rms_gate(y, g_row, z):
        return rms_finish(y, head_sum(y * y), g_row, z)

    def head_scalars(rs, r0):
        rr = lax.broadcasted_iota(jnp.int32, (CHUNK, LANES), 0) + r0
        lane = lax.broadcasted_iota(jnp.int32, (CHUNK, LANES), 1)
        valid = valid_rows(rr)
        t = uc_ref[rs, 1024:1152] + row(1, LANES)
        vals = jnp.where((lane >= HEADS) & (lane < 2 * HEADS), _sigmoid(t), _softplus(t) * row(2, LANES))
        vals = jnp.where(valid, vals, 0.0)
        cs = _dot01_l(tri_b, vals)
        ex_v = _dot01_r(vals, ea_ref[...], pieces=2)
        ex_c = _dot01_r(cs, eb_ref[...])
        crow = _dot01_nt(sel_ref[...], cs)
        valid_w = jnp.concatenate([valid, valid], axis=-1)
        return (ex_v[:, 0:256], ex_v[:, 256:512], ex_c[:, 0:256], ex_c[:, 256:512], crow, valid_w)

    def hgrn_inputs(rs, valid_w):
        f_d = ud_ref[rs, 512:768]
        lsig = -_softplus(-f_d)
        t1 = row(11, D_BRANCH)
        t2 = row(12, D_BRANCH) + lsig
        logf = jnp.maximum(t1, t2) + jnp.log1p(jnp.exp(-jnp.abs(t1 - t2)))
        logf = jnp.where(valid_w, logf, 0.0)
        kd = row(13, D_BRANCH) * _sigmoid(-f_d)
        qd = _silu(ud_ref[rs, 256:512])
        vd = jnp.where(valid_w, ud_ref[rs, 768:1024], 0.0)
        xall = _dot01_l(cmat_ref[2:3 + nlev].reshape((1 + nlev) * CHUNK, CHUNK), logf)
        return logf, kd, qd, vd, xall

    def prompt_block():
        lr0, lr1 = lr_ref[0], lr_ref[1]
        pairs = [slice(p * LANES, (p + 1) * LANES) for p in range(PAIRS)]
        data = [dict() for _ in range(nch)]

        def bd(x):
            xb = x.astype(BF16)
            return jnp.concatenate([xb * lr0, xb * lr1], axis=0)

        def mm_pair(a, b):
            return _dot(a, bd(b))

        def prologue(c):
            d = data[c]
            rs = slice(c * CHUNK, (c + 1) * CHUNK)
            d['rs'] = rs
            rr = lax.broadcasted_iota(jnp.int32, (CHUNK, LANES), 0) + c * CHUNK
            lane = lax.broadcasted_iota(jnp.int32, (CHUNK, LANES), 1)
            valid = valid_rows(rr)
            t = uc_ref[rs, 1024:1152] + row(1, LANES)
            vals = jnp.where((lane >= HEADS) & (lane < 2 * HEADS), _sigmoid(t), _softplus(t) * row(2, LANES))
            vals = jnp.where(valid, vals, 0.0)
            cs = _dot01_l(tri_b, vals)
            ex_v = _dot01_r(vals, ea_ref[...], pieces=2)
            q = act_c(rs, 0, 256)
            k = act_c(rs, 256, 512)
            qss = head_sum(q * q)
            kss = head_sum(k * k)
            valid_w = jnp.concatenate([valid, valid], axis=-1)
            logf, kd, qd, vd, xall = hgrn_inputs(rs, valid_w)
            d.update(kd=kd, qd=qd, vd=vd)
            yield
            ex_c = _dot01_r(cs, eb_ref[...])
            crow = _dot01_nt(sel_ref[...], cs)
            d['diag'] = head_sum(qd * kd)
            yield
            dt_full, beta_full = ex_v[:, 0:256], ex_v[:, 256:512]
            gc, acum = ex_c[:, 0:256], ex_c[:, 256:512]
            d['prow'] = lambda r: jnp.concatenate([crow[r:r + 1, :], crow[r + 1:r + 2, :]], axis=1)
            xs = act_a(rs, 0, 256)
            alast = acum[CHUNK - 1:CHUNK, :]
            xdt = xs * dt_full
            d.update(acum=acum, alast=alast, xdt=xdt, xdt_end=xdt * jnp.exp(alast - acum), eacum=jnp.exp(acum))
            v = act_c(rs, 512, 768)
            q = q * lax.rsqrt(qss + L2_EPS) * (HEAD_DIM ** -0.5)
            k = k * lax.rsqrt(kss + L2_EPS)
            glast = gc[CHUNK - 1:CHUNK, :]
            egc = jnp.exp(gc)
            d.update(q=q, k=k, gc=gc, glast=glast, beta=beta_full, vb=v * beta_full, kbe=k * beta_full * egc,
                     qe=q * egc, kend=k * jnp.exp(glast - gc))
            gcd = xall[0:CHUNK, :]
            gld = gcd[CHUNK - 1:CHUNK, :]
            g1, g2, g3 = [t.astype(F32) for t in _split(gld, 3)]
            rid = lax.broadcasted_iota(jnp.int32, (2 * SUBLANES, D_BRANCH), 0)
            d.update(qed=qd * jnp.exp(gcd), kend_d=kd * jnp.exp(gld - gcd), xall=xall,
                     g16=jnp.where(rid == 0, g1, jnp.where(rid == 1, g2, jnp.where(rid == 2, g3, 0.0))))

        def ssd_a(c, p):
            d, ps = data[c], pairs[p]
            g0 = 256 + p * HEAD_DIM
            bg = act_a(d['rs'], g0, g0 + HEAD_DIM)
            cg = act_a(d['rs'], g0 + LANES, g0 + LANES + HEAD_DIM)
            cb2 = _dot_nt(cg, jnp.concatenate([bg, bg], axis=0))
            lm = jnp.exp(jnp.minimum(d['acum'][:, ps] - d['prow'](HEADS + 2 * p), 0.0)) * cpair_ref[0]
            d['cbl', p] = cb2 * lm
            d['bg', p], d['cg', p] = bg, cg
            yield

        def gdn_a(c, p):
            d, ps = data[c], pairs[p]
            k = d['k'][:, ps]
            dec = jnp.exp(jnp.minimum(d['gc'][:, ps] - d['prow'](2 * p), 0.0)) * cpair_ref[0]
            kq = _dot_nt(jnp.concatenate([k, d['q'][:, ps]], axis=0), bd(k))
            yield
            m = kq[0:CHUNK] * dec * d['beta'][:, ps] * cpair_ref[1]
            d['aq', p] = kq[CHUNK:2 * CHUNK] * dec
            acc = cpair_ref[2] - m
            mp = mm_pair(m, m)
            yield
            for _ in range(nlev - 2):
                acc_add = mm_pair(acc, mp)
                mp = mm_pair(mp, mp)
                yield
                acc = acc + acc_add
            acc = acc + mm_pair(acc, mp)
            yield
            d['uw', p] = _dot(acc, jnp.concatenate([bd(d['vb'][:, ps]), bd(d['kbe'][:, ps])], axis=1))
            yield

        def hgrn_a(c, p):
            d, ps = data[c], pairs[p]
            qd, kd, xall = d['qd'][:, ps], d['kd'][:, ps], d['xall']
            amat = jnp.zeros((CHUNK, LANES), F32)
            for lev in range(nlev):
                z = jnp.exp(xall[(1 + lev) * CHUNK:(2 + lev) * CHUNK, ps])
                amat = amat + _dot_nt(qd * z, bd(kd * z)) * cpair_ref[3 + lev]
                if lev % 2 == 1:
                    yield
            d['amat', p] = amat
            d['dcol', p] = jnp.exp(_dot_tn(d['g16'][:, ps], ones_ref[...]))
            yield

        def ssd_b():
            for c in range(nch):
                d = data[c]
                ya = []
                for p, ps in enumerate(pairs):
                    s_pk = ssd_sc[p]
                    ya.append(mm_pair(d['cbl', p], d['xdt'][:, ps]) + _dot(d['cg', p], s_pk) * d['eacum'][:, ps])
                    ssd_sc[p] = s_pk * jnp.exp(d['alast'][:, ps]) + _dot_tn(d['bg', p], d['xdt_end'][:, ps])
                yield
                ya = jnp.concatenate(ya, axis=-1) + row(3, D_BRANCH) * act_a(d['rs'], 0, 256)
                ss = head_sum(ya * ya)
                yield
                mix_ref[d['rs'], 0:256] = rms_finish(ya, ss, row(4, D_BRANCH), ua_ref[d['rs'], 0:256])

        def gdn_b():
            bdm = bdm_ref[...]
            for c in range(nch):
                d = data[c]
                ws, s_old = [], []
                for p, ps in enumerate(pairs):
                    s_bd = gdn_sc[p]
                    s_old.append(s_bd)
                    ws.append(_dot(jnp.concatenate([d['uw', p][:, LANES:2 * LANES], d['qe'][:, ps]], axis=0), s_bd))
                yield
                yc = []
                for p, ps in enumerate(pairs):
                    v_new = d['uw', p][:, 0:LANES] - ws[p][0:CHUNK]
                    yc.append(ws[p][CHUNK:2 * CHUNK] + mm_pair(d['aq', p], v_new))
                    gdn_sc[p] = s_old[p] * jnp.exp(d['glast'][:, ps]) + _dot_tn(d['kend'][:, ps], v_new) * bdm
                yield
                yc = jnp.concatenate(yc, axis=-1)
                ss = head_sum(yc * yc)
                yield
                mix_ref[d['rs'], 512:768] = rms_finish(yc, ss, row(10, D_BRANCH), uc_ref[d['rs'], 0:256])

        def hgrn_b():
            bdm = bdm_ref[...]
            for c in range(nch):
                d = data[c]
                yd = []
                for p, ps in enumerate(pairs):
                    s_bd = hg_sc[p]
                    vd = d['vd'][:, ps]
                    yd.append(mm_pair(d['amat', p], vd) + d['diag'][:, ps] * vd + _dot(d['qed'][:, ps], s_bd))
                    hg_sc[p] = s_bd * d['dcol', p] + _dot_tn(d['kend_d'][:, ps], vd) * bdm
                yield
                yd = jnp.concatenate(yd, axis=-1)
                ss = head_sum(yd * yd)
                yield
                mix_ref[d['rs'], 768:1024] = rms_finish(yd, ss, row(14, D_BRANCH), ud_ref[d['rs'], 0:256])

        _run_interleaved([prologue(c) for c in range(nch)])
        _run_interleaved([f(c, p) for c in range(nch) for p in range(PAIRS) for f in (gdn_a, hgrn_a, ssd_a)])
        _run_interleaved([gdn_b(), hgrn_b(), ssd_b()])

    def chunk_sample(c, _):
        r0 = pl.multiple_of(c * CHUNK, CHUNK)
        rs = pl.ds(r0, CHUNK)
        sq = pl.ds(c * ns, ns)
        dt_full, beta_full, gc, acum, crow, valid_w = head_scalars(rs, r0)
        tri_f, strict_f, eye_f = cmask_ref[0], cmask_ref[1], cmask_ref[2]
        heads = [slice(h * HEAD_DIM, (h + 1) * HEAD_DIM) for h in range(HEADS)]

        xs = acta_ref[rs, 0:256]
        bm = acta_ref[rs, 256:384]
        cmm = acta_ref[rs, 384:512]
        alast = _dot01_l(last_b, acum)
        xdt = xs * dt_full
        xdt_end = xdt * jnp.exp(alast - acum)
        eacum = jnp.exp(acum)
        ya = []
        for h, hs in enumerate(heads):
            gs = heads[h // 2]
            bg, cg = bm[:, gs], cmm[:, gs]
            lm = jnp.exp(jnp.minimum(acum[:, hs] - crow[HEADS + h:HEADS + h + 1, :], 0.0)) * tri_f
            s_h = o_ssd[sq, h]
            ya.append(_dot(_dot_nt(cg, bg) * lm, xdt[:, hs]) + _apply_state(cg, s_h, ns) * eacum[:, hs])
            o_ssd[sq, h] = (s_h * jnp.exp(_per_seq_rows(alast[:, hs], ns))
                            + _outer_state(bg, xdt_end[:, hs], ns))
        ya = jnp.concatenate(ya, axis=-1) + row(3, D_BRANCH) * xs
        mix_ref[rs, 0:256] = rms_gate(ya, row(4, D_BRANCH), ua_ref[rs, 0:256])

        q = actc_ref[rs, 0:256]
        k = actc_ref[rs, 256:512]
        v = actc_ref[rs, 512:768]
        q = q * lax.rsqrt(head_sum(q * q) + L2_EPS) * (HEAD_DIM ** -0.5)
        k = k * lax.rsqrt(head_sum(k * k) + L2_EPS)
        glast = _dot01_l(last_b, gc)
        egc = jnp.exp(gc)
        vb = v * beta_full
        kbe = k * beta_full * egc
        qe = q * egc
        kend = k * jnp.exp(glast - gc)
        yc = []
        for h, hs in enumerate(heads):
            kh = k[:, hs]
            dec = jnp.exp(jnp.minimum(gc[:, hs] - crow[h:h + 1, :], 0.0)) * tri_f
            m = _dot_nt(kh, kh) * dec * beta_full[:, hs] * strict_f
            tinv = _unit_lower_inverse(m, eye_f, nlev - 1, _dot)
            u_val = _dot(tinv, vb[:, hs])
            w_key = _dot(tinv, kbe[:, hs])
            aq = _dot_nt(q[:, hs], kh) * dec
            s_h = o_gdn[sq, h]
            v_new = u_val - _apply_state(w_key, s_h, ns)
            yc.append(_apply_state(qe[:, hs], s_h, ns) + _dot(aq, v_new))
            o_gdn[sq, h] = (s_h * jnp.exp(_per_seq_rows(glast[:, hs], ns))
                            + _outer_state(kend[:, hs], v_new, ns))
        yc = jnp.concatenate(yc, axis=-1)
        mix_ref[rs, 512:768] = rms_gate(yc, row(10, D_BRANCH), uc_ref[rs, 0:256])

        logf, kd, qd, vd, xall = hgrn_inputs(rs, valid_w)
        gcd = xall[0:CHUNK, :]
        gld = _dot01_l(last_b, gcd)
        qed = qd * jnp.exp(gcd)
        kend_d = kd * jnp.exp(gld - gcd)
        diag = head_sum(qd * kd)
        lpieces = _split(logf, 3)
        ones_h = jnp.ones((CHUNK, HEAD_DIM), BF16)
        zs = [jnp.exp(xall[(1 + lev) * CHUNK:(2 + lev) * CHUNK, :]) for lev in range(nlev)]
        yd = []
        for h, hs in enumerate(heads):
            amat = jnp.zeros((CHUNK, CHUNK), F32)
            for lev in range(nlev):
                amat = amat + _dot_nt(qd[:, hs] * zs[lev][:, hs], kd[:, hs] * zs[lev][:, hs]) * cmask_ref[3 + lev]
            s_h = o_hg[sq, h]
            yd.append(_dot(amat, vd[:, hs]) + diag[:, hs] * vd[:, hs] + _apply_state(qed[:, hs], s_h, ns))
            dcol = jnp.exp(sum(_outer_state(lp[:, hs], ones_h, ns) for lp in lpieces))
            o_hg[sq, h] = s_h * dcol + _outer_state(kend_d[:, hs], vd[:, hs], ns)
        yd = jnp.concatenate(yd, axis=-1)
        mix_ref[rs, 768:1024] = rms_gate(yd, row(14, D_BRANCH), ud_ref[rs, 0:256])
        return 0

    if prompt:
        prompt_block()
    else:
        lax.fori_loop(0, nch, chunk_sample, 0)

    if prompt:
        @pl.when(j == pl.num_programs(1) - 1)
        def _():
            for p in range(PAIRS):
                for e in range(2):
                    es = slice(e * HEAD_DIM, (e + 1) * HEAD_DIM)
                    o_ssd[0, 2 * p + e] = ssd_sc[p, :, es]
                    o_gdn[0, 2 * p + e] = gdn_sc[p, es, es]
                    o_hg[0, 2 * p + e] = hg_sc[p, es, es]


def _mixer_constants(ns):
    ls = CHUNK // ns
    nlev = int(math.log2(ls))
    i = np.arange(CHUNK)[:, None]
    jn = np.arange(CHUNK)[None, :]
    same = (i // ls) == (jn // ls)
    tri = (same & (jn <= i)).astype(np.float32)
    strict = (same & (jn < i)).astype(np.float32)
    eye = np.eye(CHUNK, dtype=np.float32)
    last = (jn == (i // ls) * ls + ls - 1).astype(np.float32)
    wlev, mlev = [], []
    for lev in range(nlev):
        b = ls >> (lev + 1)
        blk_i, pos_i = i // (2 * b), i % (2 * b)
        mid = blk_i * 2 * b + b
        upper = pos_i >= b
        w = np.where(upper, (jn >= mid) & (jn <= i), (jn > i) & (jn < mid)).astype(np.float32)
        msk = (((jn // (2 * b)) == blk_i) & upper & ((jn % (2 * b)) < b)).astype(np.float32)
        wlev.append(w)
        mlev.append(msk)
    cmat = np.stack([tri, last, tri] + wlev)
    cmask = np.stack([tri, strict, eye] + mlev)
    cpair = np.concatenate([cmask, cmask], axis=-1)
    bdm = np.kron(np.eye(2, dtype=np.float32), np.ones((HEAD_DIM, HEAD_DIM), np.float32))
    lane = np.arange(LANES)[None, :]
    lr = np.stack([np.broadcast_to(lane < HEAD_DIM, (CHUNK, LANES)),
                   np.broadcast_to(lane >= HEAD_DIM, (CHUNK, LANES))]).astype(np.float32)
    e_a = np.zeros((LANES, 512), np.float32)
    e_b = np.zeros((LANES, 512), np.float32)
    blk = np.zeros((256, 256), np.float32)
    sel = np.zeros((2 * SUBLANES, LANES), np.float32)
    for h in range(HEADS):
        hs = slice(h * HEAD_DIM, (h + 1) * HEAD_DIM)
        e_a[h, hs] = 1.0
        e_a[HEADS + h, 256 + h * HEAD_DIM:256 + (h + 1) * HEAD_DIM] = 1.0
        e_b[2 * HEADS + h, hs] = 1.0
        e_b[3 * HEADS + h, 256 + h * HEAD_DIM:256 + (h + 1) * HEAD_DIM] = 1.0
        blk[hs, hs] = 1.0
        sel[h, 2 * HEADS + h] = 1.0
        sel[HEADS + h, 3 * HEADS + h] = 1.0
    ones = np.ones((2 * SUBLANES, LANES), np.float32)
    return (jnp.asarray(cmat, BF16), jnp.asarray(cmask, F32), jnp.asarray(cpair, F32), jnp.asarray(bdm, F32),
            jnp.asarray(lr, BF16), jnp.asarray(e_a, BF16), jnp.asarray(e_b, BF16), jnp.asarray(blk, BF16),
            jnp.asarray(sel, BF16), jnp.asarray(ones, BF16))


def _layer_spec(a, l):
    nd = a.ndim - 1
    return pl.BlockSpec((None,) + a.shape[1:], lambda b, j: (l,) + (0,) * nd)


def _mixer_call(us, l, wts, consts, states_in, acc, *, prompt, n_seq, rows):
    ua, ub, uc, ud = us
    n_rows = ua.shape[0]
    if prompt:
        ns, nsb = 1, 1
        nblk = n_rows // n_seq // rows
        grid = (n_seq, nblk)
        rmap = lambda b, j: (b * nblk + j, 0)
        seq_blk = lambda b, j: b
    else:
        ns = CHUNK // SUBLANES
        nsb = rows // SUBLANES
        grid = (1, n_rows // rows)
        rmap = lambda b, j: (j, 0)
        seq_blk = lambda b, j: j

    def state_spec(tail):
        return pl.BlockSpec((None, nsb) + tail, lambda b, j: (l, seq_blk(b, j)) + (0,) * len(tail))

    layer_consts = [wts[k] for k in ('rows', 'cw_a', 'cw_c', 'wb_re', 'wb_im', 'c_re', 'c_im', 'glu_w', 's5tab')]
    assert len(layer_consts) + len(consts) == N_MIXER_CONST
    in_specs = [pl.BlockSpec((rows, w), rmap) for w in (WA, WB, WC, WD)]
    in_specs += [_layer_spec(a, l) for a in layer_consts]
    in_specs += [pl.BlockSpec(a.shape, lambda b, j, nd=a.ndim: (0,) * nd) for a in consts]
    inputs = [ua, ub, uc, ud] + layer_consts + list(consts)
    mat = (HEADS, HEAD_DIM, HEAD_DIM)
    lb = rows // nsb
    if prompt:
        s5_tail = (SUBLANES, S5_LANES)
        tails = [mat, s5_tail, s5_tail, mat, mat]
        scratch = [pltpu.VMEM((rows, S5_LANES), F32),
                   pltpu.VMEM((rows, S5_LANES), F32),
                   pltpu.VMEM((PAIRS, HEAD_DIM, LANES), F32),
                   pltpu.VMEM((PAIRS, LANES, LANES), F32),
                   pltpu.VMEM((PAIRS, LANES, LANES), F32)]
    else:
        tails = [mat, (CONV_W - 1, SSD_XBC), (S5_LANES,), (S5_LANES,), mat, (CONV_W - 1, GDN_QKV), mat]
        inputs += list(states_in)
        in_specs += [state_spec(t) for t in tails]
        scratch = [pltpu.VMEM((nsb, SUBLANES + lb, SSD_XBC), F32),
                   pltpu.VMEM((nsb, SUBLANES + lb, GDN_QKV), F32),
                   pltpu.VMEM((rows, SSD_XBC), F32),
                   pltpu.VMEM((rows, GDN_QKV), F32),
                   pltpu.VMEM((rows, S5_LANES), F32),
                   pltpu.VMEM((rows, S5_LANES), F32)]
    assert len(acc) == len(tails)
    n_before_acc = len(inputs)
    inputs += list(acc)
    in_specs += [pl.BlockSpec(memory_space=pl.ANY)] * len(acc)
    out_shape = [jax.ShapeDtypeStruct((n_rows, D_MODEL), BF16)]
    out_shape += [jax.ShapeDtypeStruct(a.shape, a.dtype) for a in acc]
    out_specs = [pl.BlockSpec((rows, D_MODEL), rmap)] + [state_spec(t) for t in tails]
    aliases = {n_before_acc + i: 1 + i for i in range(len(acc))}
    return pl.pallas_call(
        functools.partial(_mixer_kernel, prompt=prompt, rows=rows, ns=ns, nsb=nsb),
        grid=grid,
        in_specs=in_specs,
        out_specs=out_specs,
        out_shape=out_shape,
        scratch_shapes=scratch,
        input_output_aliases=aliases,
        compiler_params=pltpu.CompilerParams(dimension_semantics=("arbitrary", "arbitrary"),
                                             vmem_limit_bytes=VMEM_LIMIT),
        name="mixer_prompt" if prompt else "mixer_sample",
    )(*inputs)


W_IN_ROWS_PER_STEP = 128


def _regroup_w_in_kernel(w_ref, o_ref):
    w = w_ref[...]
    small = jnp.concatenate([w[:, 768:772], w[:, 2308:2316], w[:, 768:772],
                             jnp.zeros((w.shape[0], LANES - 4 * HEADS), F32)], axis=1)
    o_ref[...] = jnp.concatenate([w[:, 0:768], w[:, 772:1284], w[:, 1284:2308], small, w[:, 2316:3340]],
                                 axis=1).astype(BF16)


def _regroup_w_in(w_in):
    depth, d_model, d_in = w_in.shape
    tk = W_IN_ROWS_PER_STEP
    return pl.pallas_call(
        _regroup_w_in_kernel,
        grid=(depth, d_model // tk),
        in_specs=[pl.BlockSpec((None, tk, d_in), lambda l, i: (l, i, 0))],
        out_specs=pl.BlockSpec((None, tk, WA + WB + WC + WD), lambda l, i: (l, i, 0)),
        out_shape=jax.ShapeDtypeStruct((depth, d_model, WA + WB + WC + WD), BF16),
        compiler_params=pltpu.CompilerParams(dimension_semantics=("arbitrary", "arbitrary"),
                                             vmem_limit_bytes=VMEM_LIMIT),
        name="regroup_w_in",
    )(w_in.astype(F32))


def _pad_rows(v, width=ROW_W):
    v = v.astype(F32).reshape(v.shape[0], -1)
    return jnp.pad(v, ((0, 0), (0, width - v.shape[1])))


def _stacked_weights(p, lbs):
    depth = lbs.shape[0]
    rep = lambda v: jnp.repeat(v.astype(F32), HEAD_DIM, axis=-1)
    zeros_h = jnp.zeros((depth, HEADS), F32)
    small_bias = jnp.concatenate([p['ssd_dt_bias'], zeros_h, p['gdn_dt_bias'], p['ssd_dt_bias']], axis=1)
    small_scale = jnp.concatenate([jnp.ones((depth, HEADS), F32), zeros_h, -jnp.exp(p['gdn_a_log']),
                                   -jnp.exp(p['ssd_a_log'])], axis=1)
    zrow = jnp.zeros((depth, ROW_W), F32)
    rows = jnp.stack([
        _pad_rows(p['ssd_conv_b']), _pad_rows(small_bias), _pad_rows(small_scale),
        _pad_rows(rep(p['ssd_d'])), _pad_rows(p['ssd_norm_g']), _pad_rows(p['s5_d']),
        _pad_rows(p['s5_glu_b']), _pad_rows(p['gdn_conv_b']), zrow, zrow,
        _pad_rows(p['gdn_norm_g']), _pad_rows(jnp.log(lbs)), _pad_rows(jnp.log1p(-lbs)), _pad_rows(1.0 - lbs),
        _pad_rows(p['hg_norm_g']), zrow], axis=1)

    pad_cw = lambda cw: jnp.pad(cw.astype(F32), ((0, 0), (0, SUBLANES - CONV_W), (0, 0)))

    lam_re, lam_im = p['s5_lam_re'].astype(F32), p['s5_lam_im'].astype(F32)
    dt = jnp.exp(p['s5_log_dt'].astype(F32))[..., None]
    mag = jnp.exp(lam_re * dt)
    ang = lam_im * dt
    lb_re, lb_im = mag * jnp.cos(ang), mag * jnp.sin(ang)
    den = jnp.square(lam_re) + jnp.square(lam_im)
    nr = lb_re - 1.0
    coef_re = (nr * lam_re + lb_im * lam_im) / den
    coef_im = (lb_im * lam_re - nr * lam_im) / den
    b_re, b_im = p['s5_b_re'].astype(F32), p['s5_b_im'].astype(F32)
    bb_re = coef_re[..., None] * b_re - coef_im[..., None] * b_im
    bb_im = coef_re[..., None] * b_im + coef_im[..., None] * b_re
    eye_g = jnp.eye(S5_GROUPS, dtype=F32)
    bd_in = lambda bb: jnp.einsum('lgnq,gh->lgqhn', bb, eye_g).reshape(depth, D_BRANCH, S5_LANES).astype(BF16)
    bd_out = lambda c: jnp.einsum('lgqn,gh->lgnhq', c.astype(F32), eye_g).reshape(depth, S5_LANES, D_BRANCH).astype(BF16)

    pr, pi = [lb_re.reshape(depth, -1)], [lb_im.reshape(depth, -1)]
    for _ in range(SUBLANES - 1):
        pr, pi = (pr + [pr[-1] * pr[0] - pi[-1] * pi[0]], pi + [pr[-1] * pi[0] + pi[-1] * pr[0]])
    pw = jnp.stack([jnp.concatenate([a, b], axis=-1) for a, b in zip(pr, pi)], axis=1)
    ridx = jnp.arange(SUBLANES)[None, :, None]
    tabs = [jnp.where(ridx >= d, pw[:, d - 1:d, :], 0.0) for d in (1, 2, 4)] + [pw]
    s5tab = jnp.stack(tabs, axis=1).astype(F32)

    return dict(w_in=_regroup_w_in(p['w_in']), rows=rows, cw_a=pad_cw(p['ssd_conv_w']), cw_c=pad_cw(p['gdn_conv_w']),
                wb_re=bd_in(bb_re), wb_im=bd_in(bb_im), c_re=bd_out(p['s5_c_re']), c_im=bd_out(p['s5_c_im']),
                glu_w=p['s5_glu_w'].astype(BF16), s5tab=s5tab, w_out=p['w_out'].astype(BF16),
                ln_g=p['ln_g'].astype(F32)[:, None, :], ln_b=p['ln_b'].astype(F32)[:, None, :])


def _pick_tile(n_rows, candidates):
    for t in candidates:
        if n_rows % t == 0:
            return t
    raise ValueError(f"no row tile for {n_rows}")


def kernel(x_prompt, x_sample, state_ssd, state_ssd_conv, state_s5_re, state_s5_im, state_gdn, state_gdn_conv, state_hgrn, meta_tokens, ln_in_g, ln_in_b, w_in, ssd_conv_w, ssd_conv_b, ssd_dt_bias, ssd_a_log, ssd_d, ssd_norm_g, s5_lam_re, s5_lam_im, s5_log_dt, s5_b_re, s5_b_im, s5_c_re, s5_c_im, s5_d, s5_glu_w, s5_glu_b, gdn_conv_w, gdn_conv_b, gdn_a_log, gdn_dt_bias, gdn_norm_g, hg_lb_raw, hg_norm_g, w_out, ln_g, ln_b):
    p = dict(w_in=w_in, ssd_conv_w=ssd_conv_w, ssd_conv_b=ssd_conv_b, ssd_dt_bias=ssd_dt_bias,
             ssd_a_log=ssd_a_log, ssd_d=ssd_d, ssd_norm_g=ssd_norm_g, s5_lam_re=s5_lam_re,
             s5_lam_im=s5_lam_im, s5_log_dt=s5_log_dt, s5_b_re=s5_b_re, s5_b_im=s5_b_im, s5_c_re=s5_c_re,
             s5_c_im=s5_c_im, s5_d=s5_d, s5_glu_w=s5_glu_w, s5_glu_b=s5_glu_b, gdn_conv_w=gdn_conv_w,
             gdn_conv_b=gdn_conv_b, gdn_a_log=gdn_a_log, gdn_dt_bias=gdn_dt_bias, gdn_norm_g=gdn_norm_g,
             hg_norm_g=hg_norm_g, w_out=w_out, ln_g=ln_g, ln_b=ln_b)
    bp, seq, _ = x_prompt.shape
    bs, dseq, _ = x_sample.shape
    assert dseq == SUBLANES // 2
    t_pad = PAD_FRONT + N_META + seq
    rows_p = 3 * CHUNK
    assert t_pad % rows_p == 0
    n_p = bp * t_pad
    n_s = bs * SUBLANES
    rows_s = 2 * CHUNK if n_s % (2 * CHUNK) == 0 else CHUNK
    assert n_s % rows_s == 0

    soft = jax.nn.softmax(hg_lb_raw.astype(F32), axis=0)
    csum = jnp.cumsum(soft, axis=0)
    lbs = csum - csum[0]

    meta = jnp.broadcast_to(meta_tokens[None].astype(F32), (bp, N_META, D_MODEL))
    xp = jnp.concatenate([jnp.zeros((bp, PAD_FRONT, D_MODEL), F32), meta, x_prompt.astype(F32)], axis=1)
    xp = xp.reshape(n_p, D_MODEL)
    xs = jnp.pad(x_sample.astype(F32), ((0, 0), (0, SUBLANES - dseq), (0, 0))).reshape(n_s, D_MODEL)

    tm_p = _pick_tile(n_p, (704, 512, 384, 192))
    tm_s = _pick_tile(n_s, (512, 256, 128, 64))
    g_in, b_in = ln_in_g.astype(F32)[None], ln_in_b.astype(F32)[None]
    hp = _row_tiled_call(_ln_in_kernel, n_p, tm_p, [xp], [g_in, b_in], [D_MODEL], [F32], "ln_in_prompt")
    hs = _row_tiled_call(_ln_in_kernel, n_s, tm_s, [xs], [g_in, b_in], [D_MODEL], [F32], "ln_in_sample")

    consts_p = _mixer_constants(1)
    consts_s = _mixer_constants(CHUNK // SUBLANES)
    widths = [WA, WB, WC, WD]
    depth = w_in.shape[0]
    wts = _stacked_weights(p, lbs)

    mat = (HEADS, HEAD_DIM, HEAD_DIM)
    zeros = lambda *shape: jnp.zeros((depth,) + shape, F32)
    acc_p = [zeros(bp, *mat), zeros(bp, SUBLANES, S5_LANES), zeros(bp, SUBLANES, S5_LANES), zeros(bp, *mat),
             zeros(bp, *mat)]
    conv_tails = []
    acc_s = [zeros(bs, *mat), zeros(bs, CONV_W - 1, SSD_XBC), zeros(bs, S5_LANES), zeros(bs, S5_LANES),
             zeros(bs, *mat), zeros(bs, CONV_W - 1, GDN_QKV), zeros(bs, *mat)]
    st_in = (state_ssd.astype(F32), state_ssd_conv.astype(F32),
             state_s5_re.astype(F32).reshape(depth, bs, S5_LANES), state_s5_im.astype(F32).reshape(depth, bs, S5_LANES),
             state_gdn.astype(F32), state_gdn_conv.astype(F32), state_hgrn.astype(F32))
    out_consts = [wts['w_out'], wts['ln_g'], wts['ln_b']]
    for l in range(depth):
        us, tails = _inproj_conv_call(hp, l, wts, bp, tm_p)
        conv_tails.append(tails[:, SUBLANES - (CONV_W - 1):, :])
        mix, *acc_p = _mixer_call(us, l, wts, consts_p, None, acc_p, prompt=True, n_seq=bp, rows=rows_p)
        hp = _row_tiled_call(_outproj_kernel, n_p, tm_p, [mix, hp], out_consts, [D_MODEL], [F32],
                             "outproj_prompt", layer=l)
        us = _row_tiled_call(_inproj_kernel, n_s, tm_s, [hs], [wts['w_in']], widths, [F32] * 4,
                             "inproj_sample", layer=l)
        mix, *acc_s = _mixer_call(us, l, wts, consts_s, st_in, acc_s, prompt=False, n_seq=bs, rows=rows_s)
        hs = _row_tiled_call(_outproj_kernel, n_s, tm_s, [mix, hs], out_consts, [D_MODEL], [F32],
                             "outproj_sample", layer=l)

    y_prompt = hp.reshape(bp, t_pad, D_MODEL)[:, PAD_FRONT + N_META:]
    y_sample = hs.reshape(bs, SUBLANES, D_MODEL)[:, :dseq]
    s5_shape = lambda a: a.reshape(depth, -1, S5_GROUPS, S5_STATE)
    p_ssd, p_re, p_im, p_gdn, p_hg = acc_p
    conv_tails = jnp.stack(conv_tails)
    p_ca, p_cc = conv_tails[..., 0:SSD_XBC], conv_tails[..., SSD_XBC:SSD_XBC + GDN_QKV]
    s_ssd, s_ca, s_re, s_im, s_gdn, s_cc, s_hg = acc_s
    return (y_prompt, y_sample, p_ssd, p_ca, s5_shape(p_re[:, :, 0]), s5_shape(p_im[:, :, 0]), p_gdn, p_cc, p_hg,
            s_ssd, s_ca, s5_shape(s_re), s5_shape(s_im), s_gdn, s_cc, s_hg)
```

```python
import functools
import math

import numpy as np
import jax
import jax.numpy as jnp
from jax import lax
from jax.experimental import pallas as pl
from jax.experimental.pallas import tpu as pltpu

F32 = jnp.float32
BF16 = jnp.bfloat16

D_MODEL = 1024
DEPTH = 4
N_META = 16
D_BRANCH = 256
HEADS = 4
PAIRS = HEADS // 2
HEAD_DIM = 64
SSD_XBC = 512
S5_GROUPS = 16
S5_STATE = 64
S5_LANES = S5_GROUPS * S5_STATE
GDN_QKV = 768
CONV_W = 4
DN_ALPHA = (2 * DEPTH) ** 0.25
LN_EPS = 1e-5
RMS_EPS = 1e-6
L2_EPS = 1e-6

CHUNK = 64
SUBLANES = 8
LANES = 128
PAD_FRONT = CHUNK - N_META
WA, WB, WC, WD = 768, 512, 1152, 1024
NROWS = 16
ROW_W = 768
VMEM_LIMIT = 56 * 1024 * 1024


def _dot(a, b):
    return jnp.dot(a.astype(BF16), b.astype(BF16), preferred_element_type=F32)


def _dot_nt(a, b):
    return lax.dot_general(a.astype(BF16), b.astype(BF16), (((1,), (1,)), ((), ())),
                           preferred_element_type=F32)


def _dot_tn(a, b):
    return lax.dot_general(a.astype(BF16), b.astype(BF16), (((0,), (0,)), ((), ())),
                           preferred_element_type=F32)


def _split(x, pieces):
    out = []
    r = x
    for i in range(pieces):
        xi = r.astype(BF16)
        out.append(xi)
        if i + 1 < pieces:
            r = r - xi.astype(F32)
    return out


def _dot01_l(w01, x, pieces=3):
    n = x.shape[-1]
    r = jnp.dot(w01, jnp.concatenate(_split(x, pieces), axis=-1), preferred_element_type=F32)
    return sum(r[:, i * n:(i + 1) * n] for i in range(pieces))


def _dot01_r(x, w01, pieces=3):
    m = x.shape[0]
    r = jnp.dot(jnp.concatenate(_split(x, pieces), axis=0), w01, preferred_element_type=F32)
    return sum(r[i * m:(i + 1) * m] for i in range(pieces))


def _dot01_nt(w01, x, pieces=3):
    d = lambda v: lax.dot_general(w01, v, (((1,), (1,)), ((), ())), preferred_element_type=F32)
    return sum(d(v) for v in _split(x, pieces))


def _sigmoid(x):
    return 1.0 / (1.0 + jnp.exp(-x))


def _silu(x):
    return x * _sigmoid(x)


def _softplus(x):
    return jnp.maximum(x, 0.0) + jnp.log1p(jnp.exp(-jnp.abs(x)))


def _gelu_tanh(x):
    c = math.sqrt(2.0 / math.pi)
    return 0.5 * x * (1.0 + jnp.tanh(c * (x + 0.044715 * (x * x * x))))


def _unit_lower_inverse(m, eye, levels, mm):
    p = eye - m
    mp = m
    for _ in range(levels):
        mp = mm(mp, mp)
        p = p + mm(p, mp)
    return p


def _run_interleaved(chains):
    chains = list(chains)
    while chains:
        alive = []
        for ch in chains:
            try:
                next(ch)
                alive.append(ch)
            except StopIteration:
                pass
        chains = alive


def _apply_state(x, s, ns):
    xb = x.reshape(ns, CHUNK // ns, x.shape[-1])
    r = lax.dot_general(xb.astype(BF16), s.astype(BF16), (((2,), (1,)), ((0,), (0,))),
                        preferred_element_type=F32)
    return r.reshape(CHUNK, s.shape[-1])


def _outer_state(a, b, ns):
    ls = CHUNK // ns
    ab = jnp.swapaxes(a.reshape(ns, ls, a.shape[-1]), 1, 2)
    bb = b.reshape(ns, ls, b.shape[-1])
    return lax.dot_general(ab.astype(BF16), bb.astype(BF16), (((2,), (1,)), ((0,), (0,))),
                           preferred_element_type=F32)


def _per_seq_rows(x, ns):
    return x.reshape(ns, CHUNK // ns, x.shape[-1])[:, 0:1, :]


def _layernorm_rows(r, g, b):
    mu = jnp.mean(r, axis=-1, keepdims=True)
    c = r - mu
    var = jnp.mean(c * c, axis=-1, keepdims=True)
    return c * lax.rsqrt(var + LN_EPS) * g + b


def _ln_in_kernel(x_ref, g_ref, b_ref, o_ref):
    o_ref[...] = _layernorm_rows(x_ref[...], g_ref[...], b_ref[...])


def _inproj_kernel(x_ref, w_ref, oa_ref, ob_ref, oc_ref, od_ref):
    x = x_ref[...].astype(BF16)
    off = 0
    for o_ref, width in ((oa_ref, WA), (ob_ref, WB), (oc_ref, WC), (od_ref, WD)):
        o_ref[...] = jnp.dot(x, w_ref[:, off:off + width], preferred_element_type=F32)
        off += width


def _inproj_conv_kernel(x_ref, w_ref, rows_ref, cwa_ref, cwc_ref, oa_ref, ob_ref, oc_ref, od_ref, tail_ref,
                        xx_ref, *, tiles_per_seq):
    tm = x_ref.shape[0]
    t = pl.program_id(0) % tiles_per_seq
    x = x_ref[...].astype(BF16)

    @pl.when(t == 0)
    def _():
        xx_ref[0:SUBLANES, :] = jnp.zeros((SUBLANES, SSD_XBC + GDN_QKV), F32)

    first_valid = jnp.where(t == 0, PAD_FRONT, 0)

    def conv(raw, c0, width, cw_ref, bias_row, o_ref):
        r = lax.broadcasted_iota(jnp.int32, (tm, width), 0)
        raw = jnp.where(r >= first_valid, raw, 0.0)
        xx_ref[SUBLANES:SUBLANES + tm, c0:c0 + width] = raw
        acc = rows_ref[bias_row:bias_row + 1, 0:width]
        for w in range(CONV_W):
            lo = SUBLANES - (CONV_W - 1) + w
            acc = acc + cw_ref[w:w + 1, :] * xx_ref[lo:lo + tm, c0:c0 + width]
        o_ref[:, 256:256 + width] = _silu(acc)
        tail = raw[tm - SUBLANES:tm, :]
        tail_ref[:, c0:c0 + width] = tail
        xx_ref[0:SUBLANES, c0:c0 + width] = tail

    ua = jnp.dot(x, w_ref[:, 0:WA], preferred_element_type=F32)
    oa_ref[:, 0:256] = ua[:, 0:256]
    uc = jnp.dot(x, w_ref[:, WA + WB:WA + WB + WC], preferred_element_type=F32)
    conv(ua[:, 256:WA], 0, SSD_XBC, cwa_ref, 0, oa_ref)
    oc_ref[:, 0:256] = uc[:, 0:256]
    oc_ref[:, 1024:WC] = uc[:, 1024:WC]
    ob_ref[...] = jnp.dot(x, w_ref[:, WA:WA + WB], preferred_element_type=F32)
    conv(uc[:, 256:1024], SSD_XBC, GDN_QKV, cwc_ref, 7, oc_ref)
    od_ref[...] = jnp.dot(x, w_ref[:, WA + WB + WC:WA + WB + WC + WD], preferred_element_type=F32)


def _inproj_conv_call(hp, l, wts, n_seq, tm):
    n_rows = hp.shape[0]
    tiles_per_seq = n_rows // n_seq // tm
    consts = [wts['w_in'], wts['rows'], wts['cw_a'], wts['cw_c']]
    in_specs = [pl.BlockSpec((tm, D_MODEL), lambda i: (i, 0))]
    in_specs += [pl.BlockSpec((None,) + a.shape[1:], lambda i, nd=a.ndim - 1: (l,) + (0,) * nd) for a in consts]
    widths = (WA, WB, WC, WD)
    out_specs = [pl.BlockSpec((tm, w), lambda i: (i, 0)) for w in widths]
    out_specs += [pl.BlockSpec((None, SUBLANES, SSD_XBC + GDN_QKV), lambda i: (i // tiles_per_seq, 0, 0))]
    out_shape = [jax.ShapeDtypeStruct((n_rows, w), F32) for w in widths]
    out_shape += [jax.ShapeDtypeStruct((n_seq, SUBLANES, SSD_XBC + GDN_QKV), F32)]
    *us, tails = pl.pallas_call(
        functools.partial(_inproj_conv_kernel, tiles_per_seq=tiles_per_seq),
        grid=(n_rows // tm,),
        in_specs=in_specs,
        out_specs=out_specs,
        out_shape=out_shape,
        scratch_shapes=[pltpu.VMEM((SUBLANES + tm, SSD_XBC + GDN_QKV), F32)],
        compiler_params=pltpu.CompilerParams(dimension_semantics=("arbitrary",),
                                             vmem_limit_bytes=VMEM_LIMIT),
        name="inproj_conv_prompt",
    )(hp, *consts)
    return us, tails


def _outproj_kernel(mix_ref, x_ref, w_ref, g_ref, b_ref, o_ref):
    out = jnp.dot(mix_ref[...], w_ref[...], preferred_element_type=F32)
    o_ref[...] = _layernorm_rows(DN_ALPHA * x_ref[...] + out, g_ref[...], b_ref[...])


def _row_tiled_call(kernel, n_rows, tm, row_inputs, const_inputs, out_widths, out_dtypes, name, layer=None):
    in_specs = [pl.BlockSpec((tm, a.shape[1]), lambda i: (i, 0)) for a in row_inputs]
    if layer is None:
        in_specs += [pl.BlockSpec(a.shape, lambda i, nd=a.ndim: (0,) * nd) for a in const_inputs]
    else:
        in_specs += [pl.BlockSpec((None,) + a.shape[1:], lambda i, nd=a.ndim - 1: (layer,) + (0,) * nd)
                     for a in const_inputs]
    out_specs = [pl.BlockSpec((tm, w), lambda i: (i, 0)) for w in out_widths]
    out_shape = [jax.ShapeDtypeStruct((n_rows, w), dt) for w, dt in zip(out_widths, out_dtypes)]
    single = len(out_widths) == 1
    return pl.pallas_call(
        kernel,
        grid=(n_rows // tm,),
        in_specs=in_specs,
        out_specs=out_specs[0] if single else out_specs,
        out_shape=out_shape[0] if single else out_shape,
        compiler_params=pltpu.CompilerParams(dimension_semantics=("arbitrary",),
                                             vmem_limit_bytes=VMEM_LIMIT),
        name=name,
    )(*row_inputs, *const_inputs)


N_MIXER_CONST = 19
N_STATES = 7


def _mixer_kernel(*refs, prompt, rows, ns, nsb):
    ls = CHUNK // ns
    lb = rows // nsb
    nch = rows // CHUNK
    nlev = int(math.log2(ls))
    n_states = N_STATES - 2 if prompt else N_STATES
    n_state_in = 0 if prompt else N_STATES
    n_in = 4 + N_MIXER_CONST + n_state_in + n_states
    (ua_ref, ub_ref, uc_ref, ud_ref, rows_ref, cwa_ref, cwc_ref, wbre_ref, wbim_ref, cre_ref, cim_ref,
     glu_ref, tab_ref, cmat_ref, cmask_ref, cpair_ref, bdm_ref, lr_ref, ea_ref, eb_ref, blk_ref,
     sel_ref, ones_ref) = refs[:4 + N_MIXER_CONST]
    if prompt:
        (mix_ref, o_ssd, o_s5re, o_s5im, o_gdn, o_hg) = refs[n_in:n_in + 1 + n_states]
        (hre_ref, him_ref, ssd_sc, gdn_sc, hg_sc) = refs[n_in + 1 + n_states:]
    else:
        (i_ssd, i_ca, i_s5re, i_s5im, i_gdn, i_cc, i_hg) = refs[4 + N_MIXER_CONST:4 + N_MIXER_CONST + N_STATES]
        (mix_ref, o_ssd, o_ca, o_s5re, o_s5im, o_gdn, o_cc, o_hg) = refs[n_in:n_in + 1 + n_states]
        (xxa_ref, xxc_ref, acta_ref, actc_ref, hre_ref, him_ref) = refs[n_in + 1 + n_states:]

    j = pl.program_id(1)

    if prompt:
        @pl.when(j == 0)
        def _():
            ssd_sc[...] = jnp.zeros_like(ssd_sc)
            gdn_sc[...] = jnp.zeros_like(gdn_sc)
            hg_sc[...] = jnp.zeros_like(hg_sc)
            o_s5re[...] = jnp.zeros_like(o_s5re)
            o_s5im[...] = jnp.zeros_like(o_s5im)
    else:
        o_ssd[...] = i_ssd[...]
        o_gdn[...] = i_gdn[...]
        o_hg[...] = i_hg[...]
        xxa_ref[:, SUBLANES - (CONV_W - 1):SUBLANES, :] = i_ca[...]
        xxc_ref[:, SUBLANES - (CONV_W - 1):SUBLANES, :] = i_cc[...]

    first_valid = jnp.where(j == 0, PAD_FRONT, 0) if prompt else None

    def valid_rows(r):
        if prompt:
            return r >= first_valid
        return (r & (SUBLANES - 1)) < (SUBLANES // 2)

    def valid_mask(width):
        return valid_rows(lax.broadcasted_iota(jnp.int32, (rows, width), 0))

    def row(i, w):
        return rows_ref[i:i + 1, 0:w]

    def conv_block(raw, xx_ref, cw_ref, bias, o_conv, act_ref, width):
        raw = jnp.where(valid_mask(width), raw, 0.0)
        raw3 = raw.reshape(nsb, lb, width)
        xx_ref[:, SUBLANES:SUBLANES + lb, :] = raw3
        acc = bias
        for w in range(CONV_W):
            lo = SUBLANES - (CONV_W - 1) + w
            acc = acc + cw_ref[w:w + 1, :] * xx_ref[:, lo:lo + lb, :]
        o_conv[...] = raw3[:, SUBLANES // 2 - (CONV_W - 1):SUBLANES // 2, :]
        act_ref[...] = _silu(acc).reshape(rows, width)

    if prompt:
        act_a = lambda rs, lo, hi: ua_ref[rs, 256 + lo:256 + hi]
        act_c = lambda rs, lo, hi: uc_ref[rs, 256 + lo:256 + hi]
    else:
        conv_block(ua_ref[:, 256:768], xxa_ref, cwa_ref, row(0, SSD_XBC), o_ca, acta_ref, SSD_XBC)
        conv_block(uc_ref[:, 256:1024], xxc_ref, cwc_ref, row(7, GDN_QKV), o_cc, actc_ref, GDN_QKV)
        act_a = lambda rs, lo, hi: acta_ref[rs, lo:hi]
        act_c = lambda rs, lo, hi: actc_ref[rs, lo:hi]

    u_b = jnp.where(valid_mask(D_BRANCH), ub_ref[:, 256:512], 0.0)
    hre_ref[...] = _dot(u_b, wbre_ref[...])
    him_ref[...] = _dot(u_b, wbim_ref[...])

    def s5_group(g, carry):
        r0 = pl.multiple_of(g * SUBLANES, SUBLANES)
        xr = hre_ref[pl.ds(r0, SUBLANES), :]
        xi = him_ref[pl.ds(r0, SUBLANES), :]
        for k, d in enumerate((1, 2, 4)):
            tr = tab_ref[k, :, 0:S5_LANES]
            ti = tab_ref[k, :, S5_LANES:2 * S5_LANES]
            sr = pltpu.roll(xr, d, axis=0)
            si = pltpu.roll(xi, d, axis=0)
            xr, xi = xr + tr * sr - ti * si, xi + tr * si + ti * sr
        if prompt:
            cr, ci = carry
        else:
            cr = jnp.broadcast_to(i_s5re[pl.ds(g, 1), :], (SUBLANES, S5_LANES))
            ci = jnp.broadcast_to(i_s5im[pl.ds(g, 1), :], (SUBLANES, S5_LANES))
        pr = tab_ref[3, :, 0:S5_LANES]
        pi = tab_ref[3, :, S5_LANES:2 * S5_LANES]
        xr, xi = xr + pr * cr - pi * ci, xi + pr * ci + pi * cr
        hre_ref[pl.ds(r0, SUBLANES), :] = xr
        him_ref[pl.ds(r0, SUBLANES), :] = xi
        if prompt:
            return (jnp.broadcast_to(xr[SUBLANES - 1:SUBLANES, :], (SUBLANES, S5_LANES)),
                    jnp.broadcast_to(xi[SUBLANES - 1:SUBLANES, :], (SUBLANES, S5_LANES)))
        o_s5re[pl.ds(g, 1), :] = xr[SUBLANES // 2 - 1:SUBLANES // 2, :]
        o_s5im[pl.ds(g, 1), :] = xi[SUBLANES // 2 - 1:SUBLANES // 2, :]
        return carry

    if prompt:
        cr, ci = lax.fori_loop(0, rows // SUBLANES, s5_group, (o_s5re[0], o_s5im[0]))
        o_s5re[0] = cr
        o_s5im[0] = ci
    else:
        lax.fori_loop(0, rows // SUBLANES, s5_group, 0)

    y5 = (_dot(hre_ref[...], cre_ref[...]) - _dot(him_ref[...], cim_ref[...])
          + row(5, D_BRANCH) * ub_ref[:, 256:512])
    y5 = _gelu_tanh(y5)
    y5 = y5 * _sigmoid(_dot(y5, glu_ref[...]) + row(6, D_BRANCH))
    mix_ref[:, 256:512] = (y5 * _silu(ub_ref[:, 0:256])).astype(BF16)

    tri_b = cmat_ref[0]
    last_b = cmat_ref[1]

    def head_sum(x):
        return jnp.dot(x.astype(BF16), blk_ref[...], preferred_element_type=F32)

    def rms_finish(y, ss, g_row, z):
        return (y * lax.rsqrt(ss * (1.0 / HEAD_DIM) + RMS_EPS) * g_row * _silu(z)).astype(BF16)

    def rms_gate(y, g_row, z):
        return rms_finish(y, head_sum(y * y), g_row, z)

    def head_scalars(rs, r0):
        rr = lax.broadcasted_iota(jnp.int32, (CHUNK, LANES), 0) + r0
        lane = lax.broadcasted_iota(jnp.int32, (CHUNK, LANES), 1)
        valid = valid_rows(rr)
        t = uc_ref[rs, 1024:1152] + row(1, LANES)
        vals = jnp.where((lane >= HEADS) & (lane < 2 * HEADS), _sigmoid(t), _softplus(t) * row(2, LANES))
        vals = jnp.where(valid, vals, 0.0)
        cs = _dot01_l(tri_b, vals)
        ex_v = _dot01_r(vals, ea_ref[...], pieces=2)
        ex_c = _dot01_r(cs, eb_ref[...])
        crow = _dot01_nt(sel_ref[...], cs)
        valid_w = jnp.concatenate([valid, valid], axis=-1)
        return (ex_v[:, 0:256], ex_v[:, 256:512], ex_c[:, 0:256], ex_c[:, 256:512], crow, valid_w)

    def hgrn_inputs(rs, valid_w):
        f_d = ud_ref[rs, 512:768]
        lsig = -_softplus(-f_d)
        t1 = row(11, D_BRANCH)
        t2 = row(12, D_BRANCH) + lsig
        logf = jnp.maximum(t1, t2) + jnp.log1p(jnp.exp(-jnp.abs(t1 - t2)))
        logf = jnp.where(valid_w, logf, 0.0)
        kd = row(13, D_BRANCH) * _sigmoid(-f_d)
        qd = _silu(ud_ref[rs, 256:512])
        vd = jnp.where(valid_w, ud_ref[rs, 768:1024], 0.0)
        xall = _dot01_l(cmat_ref[2:3 + nlev].reshape((1 + nlev) * CHUNK, CHUNK), logf)
        return logf, kd, qd, vd, xall

    def prompt_block():
        lr0, lr1 = lr_ref[0], lr_ref[1]
        pairs = [slice(p * LANES, (p + 1) * LANES) for p in range(PAIRS)]
        data = [dict() for _ in range(nch)]

        def bd(x):
            xb = x.astype(BF16)
            return jnp.concatenate([xb * lr0, xb * lr1], axis=0)

        def mm_pair(a, b):
            return _dot(a, bd(b))

        def prologue(c):
            d = data[c]
            rs = slice(c * CHUNK, (c + 1) * CHUNK)
            d['rs'] = rs
            rr = lax.broadcasted_iota(jnp.int32, (CHUNK, LANES), 0) + c * CHUNK
            lane = lax.broadcasted_iota(jnp.int32, (CHUNK, LANES), 1)
            valid = valid_rows(rr)
            t = uc_ref[rs, 1024:1152] + row(1, LANES)
            vals = jnp.where((lane >= HEADS) & (lane < 2 * HEADS), _sigmoid(t), _softplus(t) * row(2, LANES))
            vals = jnp.where(valid, vals, 0.0)
            cs = _dot01_l(tri_b, vals)
            ex_v = _dot01_r(vals, ea_ref[...], pieces=2)
            q = act_c(rs, 0, 256)
            k = act_c(rs, 256, 512)
            qss = head_sum(q * q)
            kss = head_sum(k * k)
            valid_w = jnp.concatenate([valid, valid], axis=-1)
            logf, kd, qd, vd, xall = hgrn_inputs(rs, valid_w)
            d.update(kd=kd, qd=qd, vd=vd)
            yield
            ex_c = _dot01_r(cs, eb_ref[...])
            crow = _dot01_nt(sel_ref[...], cs)
            d['diag'] = head_sum(qd * kd)
            yield
            dt_full, beta_full = ex_v[:, 0:256], ex_v[:, 256:512]
            gc, acum = ex_c[:, 0:256], ex_c[:, 256:512]
            d['prow'] = lambda r: jnp.concatenate([crow[r:r + 1, :], crow[r + 1:r + 2, :]], axis=1)
            xs = act_a(rs, 0, 256)
            alast = acum[CHUNK - 1:CHUNK, :]
            xdt = xs * dt_full
            d.update(acum=acum, alast=alast, xdt=xdt, xdt_end=xdt * jnp.exp(alast - acum), eacum=jnp.exp(acum))
            v = act_c(rs, 512, 768)
            q = q * lax.rsqrt(qss + L2_EPS) * (HEAD_DIM ** -0.5)
            k = k * lax.rsqrt(kss + L2_EPS)
            glast = gc[CHUNK - 1:CHUNK, :]
            egc = jnp.exp(gc)
            d.update(q=q, k=k, gc=gc, glast=glast, beta=beta_full, vb=v * beta_full, kbe=k * beta_full * egc,
                     qe=q * egc, kend=k * jnp.exp(glast - gc))
            gcd = xall[0:CHUNK, :]
            gld = gcd[CHUNK - 1:CHUNK, :]
            g1, g2, g3 = [t.astype(F32) for t in _split(gld, 3)]
            rid = lax.broadcasted_iota(jnp.int32, (2 * SUBLANES, D_BRANCH), 0)
            d.update(qed=qd * jnp.exp(gcd), kend_d=kd * jnp.exp(gld - gcd), xall=xall,
                     g16=jnp.where(rid == 0, g1, jnp.where(rid == 1, g2, jnp.where(rid == 2, g3, 0.0))))

        def ssd_a(c, p):
            d, ps = data[c], pairs[p]
            g0 = 256 + p * HEAD_DIM
            bg = act_a(d['rs'], g0, g0 + HEAD_DIM)
            cg = act_a(d['rs'], g0 + LANES, g0 + LANES + HEAD_DIM)
            cb2 = _dot_nt(cg, jnp.concatenate([bg, bg], axis=0))
            lm = jnp.exp(jnp.minimum(d['acum'][:, ps] - d['prow'](HEADS + 2 * p), 0.0)) * cpair_ref[0]
            d['cbl', p] = cb2 * lm
            d['bg', p], d['cg', p] = bg, cg
            yield

        def gdn_a(c, p):
            d, ps = data[c], pairs[p]
            k = d['k'][:, ps]
            dec = jnp.exp(jnp.minimum(d['gc'][:, ps] - d['prow'](2 * p), 0.0)) * cpair_ref[0]
            kq = _dot_nt(jnp.concatenate([k, d['q'][:, ps]], axis=0), bd(k))
            yield
            m = kq[0:CHUNK] * dec * d['beta'][:, ps] * cpair_ref[1]
            d['aq', p] = kq[CHUNK:2 * CHUNK] * dec
            acc = cpair_ref[2] - m
            mp = mm_pair(m, m)
            yield
            for _ in range(nlev - 2):
                acc_add = mm_pair(acc, mp)
                mp = mm_pair(mp, mp)
                yield
                acc = acc + acc_add
            acc = acc + mm_pair(acc, mp)
            yield
            d['uw', p] = _dot(acc, jnp.concatenate([bd(d['vb'][:, ps]), bd(d['kbe'][:, ps])], axis=1))
            yield

        def hgrn_a(c, p):
            d, ps = data[c], pairs[p]
            qd, kd, xall = d['qd'][:, ps], d['kd'][:, ps], d['xall']
            amat = jnp.zeros((CHUNK, LANES), F32)
            for lev in range(nlev):
                z = jnp.exp(xall[(1 + lev) * CHUNK:(2 + lev) * CHUNK, ps])
                amat = amat + _dot_nt(qd * z, bd(kd * z)) * cpair_ref[3 + lev]
                if lev % 2 == 1:
                    yield
            d['amat', p] = amat
            d['dcol', p] = jnp.exp(_dot_tn(d['g16'][:, ps], ones_ref[...]))
            yield

        def ssd_b():
            for c in range(nch):
                d = data[c]
                ya = []
                for p, ps in enumerate(pairs):
                    s_pk = ssd_sc[p]
                    ya.append(mm_pair(d['cbl', p], d['xdt'][:, ps]) + _dot(d['cg', p], s_pk) * d['eacum'][:, ps])
                    ssd_sc[p] = s_pk * jnp.exp(d['alast'][:, ps]) + _dot_tn(d['bg', p], d['xdt_end'][:, ps])
                yield
                ya = jnp.concatenate(ya, axis=-1) + row(3, D_BRANCH) * act_a(d['rs'], 0, 256)
                ss = head_sum(ya * ya)
                yield
                mix_ref[d['rs'], 0:256] = rms_finish(ya, ss, row(4, D_BRANCH), ua_ref[d['rs'], 0:256])

        def gdn_b():
            bdm = bdm_ref[...]
            for c in range(nch):
                d = data[c]
                ws, s_old = [], []
                for p, ps in enumerate(pairs):
                    s_bd = gdn_sc[p]
                    s_old.append(s_bd)
                    ws.append(_dot(jnp.concatenate([d['uw', p][:, LANES:2 * LANES], d['qe'][:, ps]], axis=0), s_bd))
                yield
                yc = []
                for p, ps in enumerate(pairs):
                    v_new = d['uw', p][:, 0:LANES] - ws[p][0:CHUNK]
                    yc.append(ws[p][CHUNK:2 * CHUNK] + mm_pair(d['aq', p], v_new))
                    gdn_sc[p] = s_old[p] * jnp.exp(d['glast'][:, ps]) + _dot_tn(d['kend'][:, ps], v_new) * bdm
                yield
                yc = jnp.concatenate(yc, axis=-1)
                ss = head_sum(yc * yc)
                yield
                mix_ref[d['rs'], 512:768] = rms_finish(yc, ss, row(10, D_BRANCH), uc_ref[d['rs'], 0:256])

        def hgrn_b():
            bdm = bdm_ref[...]
            for c in range(nch):
                d = data[c]
                yd = []
                for p, ps in enumerate(pairs):
                    s_bd = hg_sc[p]
                    vd = d['vd'][:, ps]
                    yd.append(mm_pair(d['amat', p], vd) + d['diag'][:, ps] * vd + _dot(d['qed'][:, ps], s_bd))
                    hg_sc[p] = s_bd * d['dcol', p] + _dot_tn(d['kend_d'][:, ps], vd) * bdm
                yield
                yd = jnp.concatenate(yd, axis=-1)
                ss = head_sum(yd * yd)
                yield
                mix_ref[d['rs'], 768:1024] = rms_finish(yd, ss, row(14, D_BRANCH), ud_ref[d['rs'], 0:256])

        _run_interleaved([prologue(c) for c in range(nch)])
        _run_interleaved([f(c, p) for c in range(nch) for p in range(PAIRS) for f in (gdn_a, hgrn_a, ssd_a)])
        _run_interleaved([gdn_b(), hgrn_b(), ssd_b()])

    def chunk_sample(c, _):
        r0 = pl.multiple_of(c * CHUNK, CHUNK)
        rs = pl.ds(r0, CHUNK)
        sq = pl.ds(c * ns, ns)
        dt_full, beta_full, gc, acum, crow, valid_w = head_scalars(rs, r0)
        tri_f, strict_f, eye_f = cmask_ref[0], cmask_ref[1], cmask_ref[2]
        heads = [slice(h * HEAD_DIM, (h + 1) * HEAD_DIM) for h in range(HEADS)]

        xs = acta_ref[rs, 0:256]
        bm = acta_ref[rs, 256:384]
        cmm = acta_ref[rs, 384:512]
        alast = _dot01_l(last_b, acum)
        xdt = xs * dt_full
        xdt_end = xdt * jnp.exp(alast - acum)
        eacum = jnp.exp(acum)
        ya = []
        for h, hs in enumerate(heads):
            gs = heads[h // 2]
            bg, cg = bm[:, gs], cmm[:, gs]
            lm = jnp.exp(jnp.minimum(acum[:, hs] - crow[HEADS + h:HEADS + h + 1, :], 0.0)) * tri_f
            s_h = o_ssd[sq, h]
            ya.append(_dot(_dot_nt(cg, bg) * lm, xdt[:, hs]) + _apply_state(cg, s_h, ns) * eacum[:, hs])
            o_ssd[sq, h] = (s_h * jnp.exp(_per_seq_rows(alast[:, hs], ns))
                            + _outer_state(bg, xdt_end[:, hs], ns))
        ya = jnp.concatenate(ya, axis=-1) + row(3, D_BRANCH) * xs
        mix_ref[rs, 0:256] = rms_gate(ya, row(4, D_BRANCH), ua_ref[rs, 0:256])

        q = actc_ref[rs, 0:256]
        k = actc_ref[rs, 256:512]
        v = actc_ref[rs, 512:768]
        q = q * lax.rsqrt(head_sum(q * q) + L2_EPS) * (HEAD_DIM ** -0.5)
        k = k * lax.rsqrt(head_sum(k * k) + L2_EPS)
        glast = _dot01_l(last_b, gc)
        egc = jnp.exp(gc)
        vb = v * beta_full
        kbe = k * beta_full * egc
        qe = q * egc
        kend = k * jnp.exp(glast - gc)
        yc = []
        for h, hs in enumerate(heads):
            kh = k[:, hs]
            dec = jnp.exp(jnp.minimum(gc[:, hs] - crow[h:h + 1, :], 0.0)) * tri_f
            m = _dot_nt(kh, kh) * dec * beta_full[:, hs] * strict_f
            tinv = _unit_lower_inverse(m, eye_f, nlev - 1, _dot)
            u_val = _dot(tinv, vb[:, hs])
            w_key = _dot(tinv, kbe[:, hs])
            aq = _dot_nt(q[:, hs], kh) * dec
            s_h = o_gdn[sq, h]
            v_new = u_val - _apply_state(w_key, s_h, ns)
            yc.append(_apply_state(qe[:, hs], s_h, ns) + _dot(aq, v_new))
            o_gdn[sq, h] = (s_h * jnp.exp(_per_seq_rows(glast[:, hs], ns))
                            + _outer_state(kend[:, hs], v_new, ns))
        yc = jnp.concatenate(yc, axis=-1)
        mix_ref[rs, 512:768] = rms_gate(yc, row(10, D_BRANCH), uc_ref[rs, 0:256])

        logf, kd, qd, vd, xall = hgrn_inputs(rs, valid_w)
        gcd = xall[0:CHUNK, :]
        gld = _dot01_l(last_b, gcd)
        qed = qd * jnp.exp(gcd)
        kend_d = kd * jnp.exp(gld - gcd)
        diag = head_sum(qd * kd)
        lpieces = _split(logf, 3)
        ones_h = jnp.ones((CHUNK, HEAD_DIM), BF16)
        zs = [jnp.exp(xall[(1 + lev) * CHUNK:(2 + lev) * CHUNK, :]) for lev in range(nlev)]
        yd = []
        for h, hs in enumerate(heads):
            amat = jnp.zeros((CHUNK, CHUNK), F32)
            for lev in range(nlev):
                amat = amat + _dot_nt(qd[:, hs] * zs[lev][:, hs], kd[:, hs] * zs[lev][:, hs]) * cmask_ref[3 + lev]
            s_h = o_hg[sq, h]
            yd.append(_dot(amat, vd[:, hs]) + diag[:, hs] * vd[:, hs] + _apply_state(qed[:, hs], s_h, ns))
            dcol = jnp.exp(sum(_outer_state(lp[:, hs], ones_h, ns) for lp in lpieces))
            o_hg[sq, h] = s_h * dcol + _outer_state(kend_d[:, hs], vd[:, hs], ns)
        yd = jnp.concatenate(yd, axis=-1)
        mix_ref[rs, 768:1024] = rms_gate(yd, row(14, D_BRANCH), ud_ref[rs, 0:256])
        return 0

    if prompt:
        prompt_block()
    else:
        lax.fori_loop(0, nch, chunk_sample, 0)

    if prompt:
        @pl.when(j == pl.num_programs(1) - 1)
        def _():
            for p in range(PAIRS):
                for e in range(2):
                    es = slice(e * HEAD_DIM, (e + 1) * HEAD_DIM)
                    o_ssd[0, 2 * p + e] = ssd_sc[p, :, es]
                    o_gdn[0, 2 * p + e] = gdn_sc[p, es, es]
                    o_hg[0, 2 * p + e] = hg_sc[p, es, es]


def _mixer_constants(ns):
    ls = CHUNK // ns
    nlev = int(math.log2(ls))
    i = np.arange(CHUNK)[:, None]
    jn = np.arange(CHUNK)[None, :]
    same = (i // ls) == (jn // ls)
    tri = (same & (jn <= i)).astype(np.float32)
    strict = (same & (jn < i)).astype(np.float32)
    eye = np.eye(CHUNK, dtype=np.float32)
    last = (jn == (i // ls) * ls + ls - 1).astype(np.float32)
    wlev, mlev = [], []
    for lev in range(nlev):
        b = ls >> (lev + 1)
        blk_i, pos_i = i // (2 * b), i % (2 * b)
        mid = blk_i * 2 * b + b
        upper = pos_i >= b
        w = np.where(upper, (jn >= mid) & (jn <= i), (jn > i) & (jn < mid)).astype(np.float32)
        msk = (((jn // (2 * b)) == blk_i) & upper & ((jn % (2 * b)) < b)).astype(np.float32)
        wlev.append(w)
        mlev.append(msk)
    cmat = np.stack([tri, last, tri] + wlev)
    cmask = np.stack([tri, strict, eye] + mlev)
    cpair = np.concatenate([cmask, cmask], axis=-1)
    bdm = np.kron(np.eye(2, dtype=np.float32), np.ones((HEAD_DIM, HEAD_DIM), np.float32))
    lane = np.arange(LANES)[None, :]
    lr = np.stack([np.broadcast_to(lane < HEAD_DIM, (CHUNK, LANES)),
                   np.broadcast_to(lane >= HEAD_DIM, (CHUNK, LANES))]).astype(np.float32)
    e_a = np.zeros((LANES, 512), np.float32)
    e_b = np.zeros((LANES, 512), np.float32)
    blk = np.zeros((256, 256), np.float32)
    sel = np.zeros((2 * SUBLANES, LANES), np.float32)
    for h in range(HEADS):
        hs = slice(h * HEAD_DIM, (h + 1) * HEAD_DIM)
        e_a[h, hs] = 1.0
        e_a[HEADS + h, 256 + h * HEAD_DIM:256 + (h + 1) * HEAD_DIM] = 1.0
        e_b[2 * HEADS + h, hs] = 1.0
        e_b[3 * HEADS + h, 256 + h * HEAD_DIM:256 + (h + 1) * HEAD_DIM] = 1.0
        blk[hs, hs] = 1.0
        sel[h, 2 * HEADS + h] = 1.0
        sel[HEADS + h, 3 * HEADS + h] = 1.0
    ones = np.ones((2 * SUBLANES, LANES), np.float32)
    return (jnp.asarray(cmat, BF16), jnp.asarray(cmask, F32), jnp.asarray(cpair, F32), jnp.asarray(bdm, F32),
            jnp.asarray(lr, BF16), jnp.asarray(e_a, BF16), jnp.asarray(e_b, BF16), jnp.asarray(blk, BF16),
            jnp.asarray(sel, BF16), jnp.asarray(ones, BF16))


def _layer_spec(a, l):
    nd = a.ndim - 1
    return pl.BlockSpec((None,) + a.shape[1:], lambda b, j: (l,) + (0,) * nd)


def _mixer_call(us, l, wts, consts, states_in, acc, *, prompt, n_seq, rows):
    ua, ub, uc, ud = us
    n_rows = ua.shape[0]
    if prompt:
        ns, nsb = 1, 1
        nblk = n_rows // n_seq // rows
        grid = (n_seq, nblk)
        rmap = lambda b, j: (b * nblk + j, 0)
        seq_blk = lambda b, j: b
    else:
        ns = CHUNK // SUBLANES
        nsb = rows // SUBLANES
        grid = (1, n_rows // rows)
        rmap = lambda b, j: (j, 0)
        seq_blk = lambda b, j: j

    def state_spec(tail):
        return pl.BlockSpec((None, nsb) + tail, lambda b, j: (l, seq_blk(b, j)) + (0,) * len(tail))

    layer_consts = [wts[k] for k in ('rows', 'cw_a', 'cw_c', 'wb_re', 'wb_im', 'c_re', 'c_im', 'glu_w', 's5tab')]
    assert len(layer_consts) + len(consts) == N_MIXER_CONST
    in_specs = [pl.BlockSpec((rows, w), rmap) for w in (WA, WB, WC, WD)]
    in_specs += [_layer_spec(a, l) for a in layer_consts]
    in_specs += [pl.BlockSpec(a.shape, lambda b, j, nd=a.ndim: (0,) * nd) for a in consts]
    inputs = [ua, ub, uc, ud] + layer_consts + list(consts)
    mat = (HEADS, HEAD_DIM, HEAD_DIM)
    lb = rows // nsb
    if prompt:
        s5_tail = (SUBLANES, S5_LANES)
        tails = [mat, s5_tail, s5_tail, mat, mat]
        scratch = [pltpu.VMEM((rows, S5_LANES), F32),
                   pltpu.VMEM((rows, S5_LANES), F32),
                   pltpu.VMEM((PAIRS, HEAD_DIM, LANES), F32),
                   pltpu.VMEM((PAIRS, LANES, LANES), F32),
                   pltpu.VMEM((PAIRS, LANES, LANES), F32)]
    else:
        tails = [mat, (CONV_W - 1, SSD_XBC), (S5_LANES,), (S5_LANES,), mat, (CONV_W - 1, GDN_QKV), mat]
        inputs += list(states_in)
        in_specs += [state_spec(t) for t in tails]
        scratch = [pltpu.VMEM((nsb, SUBLANES + lb, SSD_XBC), F32),
                   pltpu.VMEM((nsb, SUBLANES + lb, GDN_QKV), F32),
                   pltpu.VMEM((rows, SSD_XBC), F32),
                   pltpu.VMEM((rows, GDN_QKV), F32),
                   pltpu.VMEM((rows, S5_LANES), F32),
                   pltpu.VMEM((rows, S5_LANES), F32)]
    assert len(acc) == len(tails)
    n_before_acc = len(inputs)
    inputs += list(acc)
    in_specs += [pl.BlockSpec(memory_space=pl.ANY)] * len(acc)
    out_shape = [jax.ShapeDtypeStruct((n_rows, D_MODEL), BF16)]
    out_shape += [jax.ShapeDtypeStruct(a.shape, a.dtype) for a in acc]
    out_specs = [pl.BlockSpec((rows, D_MODEL), rmap)] + [state_spec(t) for t in tails]
    aliases = {n_before_acc + i: 1 + i for i in range(len(acc))}
    return pl.pallas_call(
        functools.partial(_mixer_kernel, prompt=prompt, rows=rows, ns=ns, nsb=nsb),
        grid=grid,
        in_specs=in_specs,
        out_specs=out_specs,
        out_shape=out_shape,
        scratch_shapes=scratch,
        input_output_aliases=aliases,
        compiler_params=pltpu.CompilerParams(dimension_semantics=("arbitrary", "arbitrary"),
                                             vmem_limit_bytes=VMEM_LIMIT),
        name="mixer_prompt" if prompt else "mixer_sample",
    )(*inputs)


SAMPLE_COL_WIDTHS = (SSD_XBC, 4 * HEADS, 4 * HEADS, D_BRANCH, D_BRANCH, D_BRANCH, D_BRANCH, GDN_QKV, D_BRANCH,
                     D_BRANCH, D_BRANCH, D_BRANCH, D_BRANCH)
(C_CONVB_A, C_SBIAS, C_SSCALE, C_SSD_D, C_SSD_G, C_S5_D, C_GLU_B, C_CONVB_C, C_GDN_G, C_LOGLB, C_LOG1M, C_ONEM,
 C_HG_G) = [int(v) for v in np.cumsum((0,) + SAMPLE_COL_WIDTHS[:-1])]
U_ZA, U_XBC, U_ZB, U_UB, U_ZC, U_QKV, U_SMALL, U_ZD, U_QD, U_FD, U_ID = (
    0, 256, WA, WA + 256, WA + WB, WA + WB + 256, WA + WB + 1024, WA + WB + WC, WA + WB + WC + 256,
    WA + WB + WC + 512, WA + WB + WC + 768)
V_HALF = HEAD_DIM // 2


def _sample_inproj_kernel(h_ref, wt_ref, o_ref):
    o_ref[...] = lax.dot_general(wt_ref[...], h_ref[...].astype(BF16), (((1,), (1,)), ((), ())),
                                 preferred_element_type=F32)


def _sample_outproj_kernel(mixt_ref, x_ref, w_ref, g_ref, b_ref, o_ref):
    out = lax.dot_general(mixt_ref[...], w_ref[...], (((0,), (0,)), ((), ())), preferred_element_type=F32)
    o_ref[...] = _layernorm_rows(DN_ALPHA * x_ref[...] + out, g_ref[...], b_ref[...])


def _sample_mixer_kernel(ut_ref, cols_ref, cwa_ref, cwc_ref, lb_ref, wbre_ref, wbim_ref, cre_ref, cim_ref, glu_ref,
                         i_ssd, i_ca, i_s5re, i_s5im, i_gdn, i_cc, i_hg, *rest, n_tok, n_seq):
    (mix_ref, o_ssd, o_ca, o_s5re, o_s5im, o_gdn, o_cc, o_hg) = rest[N_STATES:N_STATES + 1 + N_STATES]
    (acta_ref, actc_ref, small_ref, hre_ref, him_ref, gq_ref, gk_ref, hq_ref, hk_ref, hf_ref,
     ya_ref, yc_ref, yd_ref) = rest[N_STATES + 1 + N_STATES:]
    h = pl.program_id(0)
    vh = pl.program_id(1)
    toks = [slice(t * n_seq, (t + 1) * n_seq) for t in range(n_tok)]
    hrow = pl.multiple_of(h * HEAD_DIM, HEAD_DIM)

    def col(c0, n, off=0):
        return cols_ref[pl.ds(c0 + off, n), :]

    @pl.when((h == 0) & (vh == 0))
    def _():
        for u0, width, i_c, o_c, cw_ref, act_ref, cb in ((U_XBC, SSD_XBC, i_ca, o_ca, cwa_ref, acta_ref, C_CONVB_A),
                                                        (U_QKV, GDN_QKV, i_cc, o_cc, cwc_ref, actc_ref, C_CONVB_C)):
            xx = [i_c[i] for i in range(CONV_W - 1)] + [ut_ref[u0:u0 + width, ts] for ts in toks]
            for t in range(n_tok):
                acc = cols_ref[cb:cb + width, :]
                for w in range(CONV_W):
                    acc = acc + cw_ref[w] * xx[t + w]
                act_ref[t] = _silu(acc)
            for i in range(CONV_W - 1):
                o_c[i] = xx[n_tok + i]
        rid = lax.broadcasted_iota(jnp.int32, (4 * HEADS, n_seq), 0)
        for t, ts in enumerate(toks):
            v = ut_ref[U_SMALL:U_SMALL + 4 * HEADS, ts] + cols_ref[C_SBIAS:C_SBIAS + 4 * HEADS, :]
            small_ref[t] = jnp.where((rid >= HEADS) & (rid < 2 * HEADS), _sigmoid(v),
                                     _softplus(v) * cols_ref[C_SSCALE:C_SSCALE + 4 * HEADS, :])
        u_b = ut_ref[U_UB:U_UB + D_BRANCH, :]
        bu_re = _dot(wbre_ref[...], u_b)
        bu_im = _dot(wbim_ref[...], u_b)
        sr, si = i_s5re[...], i_s5im[...]
        lr, li = lb_ref[0], lb_ref[1]
        for t, ts in enumerate(toks):
            sr, si = lr * sr - li * si + bu_re[:, ts], lr * si + li * sr + bu_im[:, ts]
            hre_ref[:, ts] = sr
            him_ref[:, ts] = si
        o_s5re[...] = sr
        o_s5im[...] = si
        y5 = _dot(cre_ref[...], hre_ref[...]) - _dot(cim_ref[...], him_ref[...])
        for ts in toks:
            y = _gelu_tanh(y5[:, ts] + cols_ref[C_S5_D:C_S5_D + D_BRANCH, :] * u_b[:, ts])
            g = _dot(glu_ref[...], y) + cols_ref[C_GLU_B:C_GLU_B + D_BRANCH, :]
            mix_ref[D_BRANCH:2 * D_BRANCH, ts] = (y * _sigmoid(g) * _silu(ut_ref[U_ZB:U_ZB + D_BRANCH, ts])).astype(BF16)

    vrow = pl.multiple_of(vh * V_HALF, V_HALF)
    vsel = pl.ds(hrow + vrow, V_HALF)

    def scal(t, r):
        return small_ref[t, pl.ds(r * HEADS + h, 1), :]

    grp = (h // 2) * HEAD_DIM
    xdt = [acta_ref[t, vsel, :] * scal(t, 0) for t in range(n_tok)]
    ea = [jnp.exp(scal(t, 3)) for t in range(n_tok)]

    def ssd_row(n, ys):
        s = i_ssd[n]
        ys = list(ys)
        for t in range(n_tok):
            s = s * ea[t] + acta_ref[t, pl.ds(256 + grp + n, 1), :] * xdt[t]
            ys[t] = ys[t] + acta_ref[t, pl.ds(384 + grp + n, 1), :] * s
        o_ssd[n] = s
        return tuple(ys)

    zero = jnp.zeros((V_HALF, n_seq), F32)
    ys = lax.fori_loop(0, HEAD_DIM, ssd_row, (zero,) * n_tok)
    for t in range(n_tok):
        ya_ref[t, pl.ds(vrow, V_HALF), :] = ys[t] + col(C_SSD_D, V_HALF, hrow + vrow) * acta_ref[t, vsel, :]

    @pl.when(vh == 0)
    def _():
        for t in range(n_tok):
            q = actc_ref[t, pl.ds(hrow, HEAD_DIM), :]
            k = actc_ref[t, pl.ds(256 + hrow, HEAD_DIM), :]
            gq_ref[t] = q * lax.rsqrt(jnp.sum(q * q, axis=0, keepdims=True) + L2_EPS) * (HEAD_DIM ** -0.5)
            gk_ref[t] = k * lax.rsqrt(jnp.sum(k * k, axis=0, keepdims=True) + L2_EPS)

    for t in range(n_tok):
        alpha = jnp.exp(scal(t, 2))
        src = i_gdn if t == 0 else o_gdn

        def decay_row(kk, pred, alpha=alpha, src=src, t=t):
            s = src[kk] * alpha
            o_gdn[kk] = s
            return pred + gk_ref[t, pl.ds(kk, 1), :] * s

        pred = lax.fori_loop(0, HEAD_DIM, decay_row, zero)
        v_new = scal(t, 1) * (actc_ref[t, pl.ds(512 + hrow + vrow, V_HALF), :] - pred)

        def update_row(kk, o, v_new=v_new, t=t):
            s = o_gdn[kk] + gk_ref[t, pl.ds(kk, 1), :] * v_new
            o_gdn[kk] = s
            return o + gq_ref[t, pl.ds(kk, 1), :] * s

        yc_ref[t, pl.ds(vrow, V_HALF), :] = lax.fori_loop(0, HEAD_DIM, update_row, zero)

    @pl.when(vh == 0)
    def _():
        for t, ts in enumerate(toks):
            f_d = ut_ref[pl.ds(U_FD + hrow, HEAD_DIM), ts]
            t1 = col(C_LOGLB, HEAD_DIM, hrow)
            t2 = col(C_LOG1M, HEAD_DIM, hrow) - _softplus(-f_d)
            logf = jnp.maximum(t1, t2) + jnp.log1p(jnp.exp(-jnp.abs(t1 - t2)))
            hf_ref[t] = jnp.exp(logf)
            hk_ref[t] = col(C_ONEM, HEAD_DIM, hrow) * _sigmoid(-f_d)
            hq_ref[t] = _silu(ut_ref[pl.ds(U_QD + hrow, HEAD_DIM), ts])

    vd = [ut_ref[pl.ds(U_ID + hrow + vrow, V_HALF), ts] for ts in toks]

    def hgrn_row(kk, os_):
        s = i_hg[kk]
        os_ = list(os_)
        for t in range(n_tok):
            s = s * hf_ref[t, pl.ds(kk, 1), :] + hk_ref[t, pl.ds(kk, 1), :] * vd[t]
            os_[t] = os_[t] + hq_ref[t, pl.ds(kk, 1), :] * s
        o_hg[kk] = s
        return tuple(os_)

    os_ = lax.fori_loop(0, HEAD_DIM, hgrn_row, (zero,) * n_tok)
    for t in range(n_tok):
        yd_ref[t, pl.ds(vrow, V_HALF), :] = os_[t]

    @pl.when(vh == HEAD_DIM // V_HALF - 1)
    def _():
        for y_ref, g0, z0, m0 in ((ya_ref, C_SSD_G, U_ZA, 0), (yc_ref, C_GDN_G, U_ZC, 2 * D_BRANCH),
                                  (yd_ref, C_HG_G, U_ZD, 3 * D_BRANCH)):
            for t, ts in enumerate(toks):
                y = y_ref[t]
                ms = jnp.mean(y * y, axis=0, keepdims=True)
                z = ut_ref[pl.ds(z0 + hrow, HEAD_DIM), ts]
                mix_ref[pl.ds(m0 + hrow, HEAD_DIM), ts] = (
                    y * lax.rsqrt(ms + RMS_EPS) * col(g0, HEAD_DIM, hrow) * _silu(z)).astype(BF16)


def _sample_mixer_call(ut, l, sw, states_t, acc, *, n_tok, n_seq):
    consts = [sw[k] for k in ('cols', 'cw_a', 'cw_c', 'lb', 'wb_re', 'wb_im', 'c_re', 'c_im', 'glu_w')]
    whole = lambda a: pl.BlockSpec((None,) + a.shape[1:], lambda h, v, nd=a.ndim - 1: (l,) + (0,) * nd)
    mat_spec = pl.BlockSpec((None, None, HEAD_DIM, V_HALF, n_seq), lambda h, v: (l, h, 0, v, 0))
    state_specs = [mat_spec, whole(states_t[1]), whole(states_t[2]), whole(states_t[3]), mat_spec,
                   whole(states_t[5]), mat_spec]
    n_cols = ut.shape[1]
    in_specs = [pl.BlockSpec(ut.shape, lambda h, v: (0, 0))] + [whole(a) for a in consts] + state_specs
    in_specs += [pl.BlockSpec(memory_space=pl.ANY)] * N_STATES
    inputs = [ut] + consts + list(states_t) + list(acc)
    out_shape = [jax.ShapeDtypeStruct((D_MODEL, n_cols), BF16)] + [jax.ShapeDtypeStruct(a.shape, a.dtype) for a in acc]
    out_specs = [pl.BlockSpec((D_MODEL, n_cols), lambda h, v: (0, 0))] + state_specs
    n_before_acc = 1 + len(consts) + N_STATES
    tok_tile = lambda rows: pltpu.VMEM((n_tok, rows, n_seq), F32)
    scratch = [tok_tile(SSD_XBC), tok_tile(GDN_QKV), tok_tile(4 * HEADS),
               pltpu.VMEM((S5_LANES, n_cols), F32), pltpu.VMEM((S5_LANES, n_cols), F32)]
    scratch += [tok_tile(HEAD_DIM)] * 8
    return pl.pallas_call(
        functools.partial(_sample_mixer_kernel, n_tok=n_tok, n_seq=n_seq),
        grid=(HEADS, HEAD_DIM // V_HALF),
        in_specs=in_specs,
        out_specs=out_specs,
        out_shape=out_shape,
        scratch_shapes=scratch,
        input_output_aliases={n_before_acc + i: 1 + i for i in range(N_STATES)},
        compiler_params=pltpu.CompilerParams(dimension_semantics=("arbitrary", "arbitrary"),
                                             vmem_limit_bytes=VMEM_LIMIT),
        name="mixer_sample",
    )(*inputs)


W_IN_ROWS_PER_STEP = 128


def _regroup_w_in_kernel(w_ref, o_ref, ot_ref):
    w = w_ref[...]
    small = jnp.concatenate([w[:, 768:772], w[:, 2308:2316], w[:, 768:772],
                             jnp.zeros((w.shape[0], LANES - 4 * HEADS), F32)], axis=1)
    wr = jnp.concatenate([w[:, 0:768], w[:, 772:1284], w[:, 1284:2308], small, w[:, 2316:3340]], axis=1)
    o_ref[...] = wr.astype(BF16)
    ot_ref[...] = wr.T.astype(BF16)


def _regroup_w_in(w_in):
    depth, d_model, d_in = w_in.shape
    tk = W_IN_ROWS_PER_STEP
    w_cols = WA + WB + WC + WD
    return pl.pallas_call(
        _regroup_w_in_kernel,
        grid=(depth, d_model // tk),
        in_specs=[pl.BlockSpec((None, tk, d_in), lambda l, i: (l, i, 0))],
        out_specs=[pl.BlockSpec((None, tk, w_cols), lambda l, i: (l, i, 0)),
                   pl.BlockSpec((None, w_cols, tk), lambda l, i: (l, 0, i))],
        out_shape=[jax.ShapeDtypeStruct((depth, d_model, w_cols), BF16),
                   jax.ShapeDtypeStruct((depth, w_cols, d_model), BF16)],
        compiler_params=pltpu.CompilerParams(dimension_semantics=("arbitrary", "arbitrary"),
                                             vmem_limit_bytes=VMEM_LIMIT),
        name="regroup_w_in",
    )(w_in.astype(F32))


def _pad_rows(v, width=ROW_W):
    v = v.astype(F32).reshape(v.shape[0], -1)
    return jnp.pad(v, ((0, 0), (0, width - v.shape[1])))


def _stacked_weights(p, lbs):
    depth = lbs.shape[0]
    rep = lambda v: jnp.repeat(v.astype(F32), HEAD_DIM, axis=-1)
    zeros_h = jnp.zeros((depth, HEADS), F32)
    small_bias = jnp.concatenate([p['ssd_dt_bias'], zeros_h, p['gdn_dt_bias'], p['ssd_dt_bias']], axis=1)
    small_scale = jnp.concatenate([jnp.ones((depth, HEADS), F32), zeros_h, -jnp.exp(p['gdn_a_log']),
                                   -jnp.exp(p['ssd_a_log'])], axis=1)
    zrow = jnp.zeros((depth, ROW_W), F32)
    rows = jnp.stack([
        _pad_rows(p['ssd_conv_b']), _pad_rows(small_bias), _pad_rows(small_scale),
        _pad_rows(rep(p['ssd_d'])), _pad_rows(p['ssd_norm_g']), _pad_rows(p['s5_d']),
        _pad_rows(p['s5_glu_b']), _pad_rows(p['gdn_conv_b']), zrow, zrow,
        _pad_rows(p['gdn_norm_g']), _pad_rows(jnp.log(lbs)), _pad_rows(jnp.log1p(-lbs)), _pad_rows(1.0 - lbs),
        _pad_rows(p['hg_norm_g']), zrow], axis=1)

    pad_cw = lambda cw: jnp.pad(cw.astype(F32), ((0, 0), (0, SUBLANES - CONV_W), (0, 0)))

    lam_re, lam_im = p['s5_lam_re'].astype(F32), p['s5_lam_im'].astype(F32)
    dt = jnp.exp(p['s5_log_dt'].astype(F32))[..., None]
    mag = jnp.exp(lam_re * dt)
    ang = lam_im * dt
    lb_re, lb_im = mag * jnp.cos(ang), mag * jnp.sin(ang)
    den = jnp.square(lam_re) + jnp.square(lam_im)
    nr = lb_re - 1.0
    coef_re = (nr * lam_re + lb_im * lam_im) / den
    coef_im = (lb_im * lam_re - nr * lam_im) / den
    b_re, b_im = p['s5_b_re'].astype(F32), p['s5_b_im'].astype(F32)
    bb_re = coef_re[..., None] * b_re - coef_im[..., None] * b_im
    bb_im = coef_re[..., None] * b_im + coef_im[..., None] * b_re
    eye_g = jnp.eye(S5_GROUPS, dtype=F32)
    bd_in = lambda bb: jnp.einsum('lgnq,gh->lgqhn', bb, eye_g).reshape(depth, D_BRANCH, S5_LANES).astype(BF16)
    bd_out = lambda c: jnp.einsum('lgqn,gh->lgnhq', c.astype(F32), eye_g).reshape(depth, S5_LANES, D_BRANCH).astype(BF16)

    pr, pi = [lb_re.reshape(depth, -1)], [lb_im.reshape(depth, -1)]
    for _ in range(SUBLANES - 1):
        pr, pi = (pr + [pr[-1] * pr[0] - pi[-1] * pi[0]], pi + [pr[-1] * pi[0] + pi[-1] * pr[0]])
    pw = jnp.stack([jnp.concatenate([a, b], axis=-1) for a, b in zip(pr, pi)], axis=1)
    ridx = jnp.arange(SUBLANES)[None, :, None]
    tabs = [jnp.where(ridx >= d, pw[:, d - 1:d, :], 0.0) for d in (1, 2, 4)] + [pw]
    s5tab = jnp.stack(tabs, axis=1).astype(F32)

    lanes = lambda v: jnp.broadcast_to(v.astype(F32)[..., None], v.shape + (LANES,))
    col_vecs = [p['ssd_conv_b'], small_bias, small_scale, rep(p['ssd_d']), p['ssd_norm_g'], p['s5_d'], p['s5_glu_b'],
                p['gdn_conv_b'], p['gdn_norm_g'], jnp.log(lbs), jnp.log1p(-lbs), 1.0 - lbs, p['hg_norm_g']]
    assert [int(np.prod(v.shape[1:])) for v in col_vecs] == list(SAMPLE_COL_WIDTHS)
    cols_s = lanes(jnp.concatenate([v.astype(F32).reshape(depth, -1) for v in col_vecs], axis=1))
    bd_in_t = lambda bb: jnp.einsum('lgnq,gh->lgnhq', bb, eye_g).reshape(depth, S5_LANES, D_BRANCH).astype(BF16)
    bd_out_t = lambda c: jnp.einsum('lgqn,gh->lgqhn', c.astype(F32), eye_g).reshape(depth, D_BRANCH, S5_LANES).astype(BF16)
    sample = dict(cols=cols_s, cw_a=lanes(p['ssd_conv_w']), cw_c=lanes(p['gdn_conv_w']),
                  lb=lanes(jnp.stack([lb_re.reshape(depth, -1), lb_im.reshape(depth, -1)], axis=1)),
                  wb_re=bd_in_t(bb_re), wb_im=bd_in_t(bb_im), c_re=bd_out_t(p['s5_c_re']), c_im=bd_out_t(p['s5_c_im']),
                  glu_w=jnp.swapaxes(p['s5_glu_w'], 1, 2).astype(BF16))

    w_in_r, w_in_t = _regroup_w_in(p['w_in'])
    return dict(w_in=w_in_r, w_in_t=w_in_t, rows=rows, cw_a=pad_cw(p['ssd_conv_w']), cw_c=pad_cw(p['gdn_conv_w']),
                wb_re=bd_in(bb_re), wb_im=bd_in(bb_im), c_re=bd_out(p['s5_c_re']), c_im=bd_out(p['s5_c_im']),
                glu_w=p['s5_glu_w'].astype(BF16), s5tab=s5tab, w_out=p['w_out'].astype(BF16),
                ln_g=p['ln_g'].astype(F32)[:, None, :], ln_b=p['ln_b'].astype(F32)[:, None, :], sample=sample)


def _pick_tile(n_rows, candidates):
    for t in candidates:
        if n_rows % t == 0:
            return t
    raise ValueError(f"no row tile for {n_rows}")


def kernel(x_prompt, x_sample, state_ssd, state_ssd_conv, state_s5_re, state_s5_im, state_gdn, state_gdn_conv, state_hgrn, meta_tokens, ln_in_g, ln_in_b, w_in, ssd_conv_w, ssd_conv_b, ssd_dt_bias, ssd_a_log, ssd_d, ssd_norm_g, s5_lam_re, s5_lam_im, s5_log_dt, s5_b_re, s5_b_im, s5_c_re, s5_c_im, s5_d, s5_glu_w, s5_glu_b, gdn_conv_w, gdn_conv_b, gdn_a_log, gdn_dt_bias, gdn_norm_g, hg_lb_raw, hg_norm_g, w_out, ln_g, ln_b):
    p = dict(w_in=w_in, ssd_conv_w=ssd_conv_w, ssd_conv_b=ssd_conv_b, ssd_dt_bias=ssd_dt_bias,
             ssd_a_log=ssd_a_log, ssd_d=ssd_d, ssd_norm_g=ssd_norm_g, s5_lam_re=s5_lam_re,
             s5_lam_im=s5_lam_im, s5_log_dt=s5_log_dt, s5_b_re=s5_b_re, s5_b_im=s5_b_im, s5_c_re=s5_c_re,
             s5_c_im=s5_c_im, s5_d=s5_d, s5_glu_w=s5_glu_w, s5_glu_b=s5_glu_b, gdn_conv_w=gdn_conv_w,
             gdn_conv_b=gdn_conv_b, gdn_a_log=gdn_a_log, gdn_dt_bias=gdn_dt_bias, gdn_norm_g=gdn_norm_g,
             hg_norm_g=hg_norm_g, w_out=w_out, ln_g=ln_g, ln_b=ln_b)
    bp, seq, _ = x_prompt.shape
    bs, dseq, _ = x_sample.shape
    assert dseq >= CONV_W - 1 and bs % LANES == 0
    t_pad = PAD_FRONT + N_META + seq
    rows_p = 3 * CHUNK
    assert t_pad % rows_p == 0
    n_p = bp * t_pad
    n_s = bs * dseq

    soft = jax.nn.softmax(hg_lb_raw.astype(F32), axis=0)
    csum = jnp.cumsum(soft, axis=0)
    lbs = csum - csum[0]

    meta = jnp.broadcast_to(meta_tokens[None].astype(F32), (bp, N_META, D_MODEL))
    xp = jnp.concatenate([jnp.zeros((bp, PAD_FRONT, D_MODEL), F32), meta, x_prompt.astype(F32)], axis=1)
    xp = xp.reshape(n_p, D_MODEL)
    xs = jnp.swapaxes(x_sample.astype(F32), 0, 1).reshape(n_s, D_MODEL)

    tm_p = _pick_tile(n_p, (704, 512, 384, 192))
    tm_s = n_s
    g_in, b_in = ln_in_g.astype(F32)[None], ln_in_b.astype(F32)[None]
    hp = _row_tiled_call(_ln_in_kernel, n_p, tm_p, [xp], [g_in, b_in], [D_MODEL], [F32], "ln_in_prompt")
    hs = _row_tiled_call(_ln_in_kernel, n_s, tm_s, [xs], [g_in, b_in], [D_MODEL], [F32], "ln_in_sample")

    consts_p = _mixer_constants(1)
    depth = w_in.shape[0]
    wts = _stacked_weights(p, lbs)

    mat = (HEADS, HEAD_DIM, HEAD_DIM)
    zeros = lambda *shape: jnp.zeros((depth,) + shape, F32)
    acc_p = [zeros(bp, *mat), zeros(bp, SUBLANES, S5_LANES), zeros(bp, SUBLANES, S5_LANES), zeros(bp, *mat),
             zeros(bp, *mat)]
    conv_tails = []
    seq_last = lambda a: jnp.moveaxis(a.astype(F32), 1, -1)
    st_in = (seq_last(state_ssd), seq_last(state_ssd_conv),
             seq_last(state_s5_re).reshape(depth, S5_LANES, bs), seq_last(state_s5_im).reshape(depth, S5_LANES, bs),
             seq_last(state_gdn), seq_last(state_gdn_conv), seq_last(state_hgrn))
    acc_s = [jnp.zeros(a.shape, F32) for a in st_in]
    out_consts = [wts['w_out'], wts['ln_g'], wts['ln_b']]
    w_cols = WA + WB + WC + WD
    for l in range(depth):
        us, tails = _inproj_conv_call(hp, l, wts, bp, tm_p)
        conv_tails.append(tails[:, SUBLANES - (CONV_W - 1):, :])
        mix, *acc_p = _mixer_call(us, l, wts, consts_p, None, acc_p, prompt=True, n_seq=bp, rows=rows_p)
        hp = _row_tiled_call(_outproj_kernel, n_p, tm_p, [mix, hp], out_consts, [D_MODEL], [F32],
                             "outproj_prompt", layer=l)

        n_wblk = 6
        ut = pl.pallas_call(
            _sample_inproj_kernel,
            grid=(n_wblk,),
            in_specs=[pl.BlockSpec((n_s, D_MODEL), lambda i: (0, 0)),
                      pl.BlockSpec((None, w_cols // n_wblk, D_MODEL), lambda i: (l, i, 0))],
            out_specs=pl.BlockSpec((w_cols // n_wblk, n_s), lambda i: (i, 0)),
            out_shape=jax.ShapeDtypeStruct((w_cols, n_s), F32),
            compiler_params=pltpu.CompilerParams(dimension_semantics=("arbitrary",), vmem_limit_bytes=VMEM_LIMIT),
            name="inproj_sample",
        )(hs, wts['w_in_t'])
        mixt, *acc_s = _sample_mixer_call(ut, l, wts['sample'], st_in, acc_s, n_tok=dseq, n_seq=bs)
        hs = pl.pallas_call(
            _sample_outproj_kernel,
            grid=(1,),
            in_specs=[pl.BlockSpec((D_MODEL, n_s), lambda i: (0, 0)), pl.BlockSpec((n_s, D_MODEL), lambda i: (0, 0)),
                      pl.BlockSpec((None, D_MODEL, D_MODEL), lambda i: (l, 0, 0)),
                      pl.BlockSpec((None, 1, D_MODEL), lambda i: (l, 0, 0)),
                      pl.BlockSpec((None, 1, D_MODEL), lambda i: (l, 0, 0))],
            out_specs=pl.BlockSpec((n_s, D_MODEL), lambda i: (0, 0)),
            out_shape=jax.ShapeDtypeStruct((n_s, D_MODEL), F32),
            compiler_params=pltpu.CompilerParams(dimension_semantics=("arbitrary",), vmem_limit_bytes=VMEM_LIMIT),
            name="outproj_sample",
        )(mixt, hs, *out_consts)

    y_prompt = hp.reshape(bp, t_pad, D_MODEL)[:, PAD_FRONT + N_META:]
    y_sample = jnp.swapaxes(hs.reshape(dseq, bs, D_MODEL), 0, 1)
    s5_shape = lambda a: a.reshape(depth, -1, S5_GROUPS, S5_STATE)
    p_ssd, p_re, p_im, p_gdn, p_hg = acc_p
    conv_tails = jnp.stack(conv_tails)
    p_ca, p_cc = conv_tails[..., 0:SSD_XBC], conv_tails[..., SSD_XBC:SSD_XBC + GDN_QKV]
    seq_first = lambda a: jnp.moveaxis(a, -1, 1)
    s_ssd, s_ca, s_re, s_im, s_gdn, s_cc, s_hg = [seq_first(a) for a in acc_s]
    return (y_prompt, y_sample, p_ssd, p_ca, s5_shape(p_re[:, :, 0]), s5_shape(p_im[:, :, 0]), p_gdn, p_cc, p_hg,
            s_ssd, s_ca, s5_shape(s_re), s5_shape(s_im), s_gdn, s_cc, s_hg)
```

```python
import functools
import math

import numpy as np
import jax
import jax.numpy as jnp
from jax import lax
from jax.experimental import pallas as pl
from jax.experimental.pallas import tpu as pltpu

F32 = jnp.float32
BF16 = jnp.bfloat16

D_MODEL = 1024
DEPTH = 4
N_META = 16
D_BRANCH = 256
HEADS = 4
PAIRS = HEADS // 2
HEAD_DIM = 64
SSD_XBC = 512
S5_GROUPS = 16
S5_STATE = 64
S5_LANES = S5_GROUPS * S5_STATE
GDN_QKV = 768
CONV_W = 4
DN_ALPHA = (2 * DEPTH) ** 0.25
LN_EPS = 1e-5
RMS_EPS = 1e-6
L2_EPS = 1e-6

CHUNK = 64
SUBLANES = 8
LANES = 128
PAD_FRONT = CHUNK - N_META
WA, WB, WC, WD = 768, 512, 1152, 1024
NROWS = 16
ROW_W = 768
VMEM_LIMIT = 56 * 1024 * 1024


def _dot(a, b):
    return jnp.dot(a.astype(BF16), b.astype(BF16), preferred_element_type=F32)


def _dot_nt(a, b):
    return lax.dot_general(a.astype(BF16), b.astype(BF16), (((1,), (1,)), ((), ())),
                           preferred_element_type=F32)


def _dot_tn(a, b):
    return lax.dot_general(a.astype(BF16), b.astype(BF16), (((0,), (0,)), ((), ())),
                           preferred_element_type=F32)


def _split(x, pieces):
    out = []
    r = x
    for i in range(pieces):
        xi = r.astype(BF16)
        out.append(xi)
        if i + 1 < pieces:
            r = r - xi.astype(F32)
    return out


def _dot01_l(w01, x, pieces=3):
    n = x.shape[-1]
    r = jnp.dot(w01, jnp.concatenate(_split(x, pieces), axis=-1), preferred_element_type=F32)
    return sum(r[:, i * n:(i + 1) * n] for i in range(pieces))


def _dot01_r(x, w01, pieces=3):
    m = x.shape[0]
    r = jnp.dot(jnp.concatenate(_split(x, pieces), axis=0), w01, preferred_element_type=F32)
    return sum(r[i * m:(i + 1) * m] for i in range(pieces))


def _sigmoid(x):
    return 1.0 / (1.0 + jnp.exp(-x))


def _silu(x):
    return x * _sigmoid(x)


def _log1p_exp_neg(a):
    e = jnp.exp(-a)
    u = 1.0 + e
    return jnp.log(u) - ((u - 1.0) - e) / u


def _softplus(x):
    return jnp.maximum(x, 0.0) + _log1p_exp_neg(jnp.abs(x))


def _gelu_tanh(x):
    c = math.sqrt(2.0 / math.pi)
    return 0.5 * x * (1.0 + jnp.tanh(c * (x + 0.044715 * (x * x * x))))


def _run_interleaved(chains):
    chains = list(chains)
    while chains:
        alive = []
        for ch in chains:
            try:
                next(ch)
                alive.append(ch)
            except StopIteration:
                pass
        chains = alive


def _layernorm_rows(r, g, b):
    mu = jnp.mean(r, axis=-1, keepdims=True)
    c = r - mu
    var = jnp.mean(c * c, axis=-1, keepdims=True)
    return c * lax.rsqrt(var + LN_EPS) * g + b


def _ln_in_kernel(x_ref, g_ref, b_ref, o_ref):
    o_ref[...] = _layernorm_rows(x_ref[...], g_ref[...], b_ref[...])


def _ln_in_prompt_kernel(x_ref, meta_ref, g_ref, b_ref, o_ref):
    o_ref[0:PAD_FRONT, :] = jnp.zeros((PAD_FRONT, D_MODEL), F32)
    o_ref[PAD_FRONT:CHUNK, :] = _layernorm_rows(meta_ref[...], g_ref[...], b_ref[...])
    seq = x_ref.shape[0]
    step = math.gcd(seq, 512)
    for r0 in range(0, seq, step):
        o_ref[CHUNK + r0:CHUNK + r0 + step, :] = _layernorm_rows(x_ref[r0:r0 + step, :], g_ref[...], b_ref[...])


def _inproj_conv_kernel(x_ref, w_ref, rows_ref, cwa_ref, cwc_ref, oa_ref, ob_ref, oc_ref, od_ref, tail_ref,
                        xx_ref, *, tiles_per_seq):
    tm = x_ref.shape[0]
    t = pl.program_id(0) % tiles_per_seq
    x = x_ref[...].astype(BF16)

    @pl.when(t == 0)
    def _():
        xx_ref[0:SUBLANES, :] = jnp.zeros((SUBLANES, SSD_XBC + GDN_QKV), F32)

    first_valid = jnp.where(t == 0, PAD_FRONT, 0)

    def conv(raw, c0, width, cw_ref, bias_row, o_ref):
        r = lax.broadcasted_iota(jnp.int32, (tm, width), 0)
        raw = jnp.where(r >= first_valid, raw, 0.0)
        xx_ref[SUBLANES:SUBLANES + tm, c0:c0 + width] = raw
        acc = rows_ref[bias_row:bias_row + 1, 0:width]
        for w in range(CONV_W):
            lo = SUBLANES - (CONV_W - 1) + w
            acc = acc + cw_ref[w:w + 1, :] * xx_ref[lo:lo + tm, c0:c0 + width]
        o_ref[:, 256:256 + width] = _silu(acc)
        tail = raw[tm - SUBLANES:tm, :]
        tail_ref[:, c0:c0 + width] = tail
        xx_ref[0:SUBLANES, c0:c0 + width] = tail

    ua = jnp.dot(x, w_ref[:, 0:WA], preferred_element_type=F32)
    oa_ref[:, 0:256] = ua[:, 0:256]
    uc = jnp.dot(x, w_ref[:, WA + WB:WA + WB + WC], preferred_element_type=F32)
    conv(ua[:, 256:WA], 0, SSD_XBC, cwa_ref, 0, oa_ref)
    oc_ref[:, 0:256] = uc[:, 0:256]
    oc_ref[:, 1024:WC] = uc[:, 1024:WC]
    ob_ref[...] = jnp.dot(x, w_ref[:, WA:WA + WB], preferred_element_type=F32)
    conv(uc[:, 256:1024], SSD_XBC, GDN_QKV, cwc_ref, 7, oc_ref)
    od_ref[...] = jnp.dot(x, w_ref[:, WA + WB + WC:WA + WB + WC + WD], preferred_element_type=F32)


def _inproj_conv_call(hp, l, wts, n_seq, tm):
    n_rows = hp.shape[0]
    tiles_per_seq = n_rows // n_seq // tm
    consts = [wts['w_in'], wts['rows'], wts['cw_a'], wts['cw_c']]
    in_specs = [pl.BlockSpec((tm, D_MODEL), lambda i: (i, 0))]
    in_specs += [pl.BlockSpec((None,) + a.shape[1:], lambda i, nd=a.ndim - 1: (l,) + (0,) * nd) for a in consts]
    widths = (WA, WB, WC, WD)
    out_specs = [pl.BlockSpec((tm, w), lambda i: (i, 0)) for w in widths]
    out_specs += [pl.BlockSpec((None, SUBLANES, SSD_XBC + GDN_QKV), lambda i: (i // tiles_per_seq, 0, 0))]
    out_shape = [jax.ShapeDtypeStruct((n_rows, w), F32) for w in widths]
    out_shape += [jax.ShapeDtypeStruct((n_seq, SUBLANES, SSD_XBC + GDN_QKV), F32)]
    *us, tails = pl.pallas_call(
        functools.partial(_inproj_conv_kernel, tiles_per_seq=tiles_per_seq),
        grid=(n_rows // tm,),
        in_specs=in_specs,
        out_specs=out_specs,
        out_shape=out_shape,
        scratch_shapes=[pltpu.VMEM((SUBLANES + tm, SSD_XBC + GDN_QKV), F32)],
        compiler_params=pltpu.CompilerParams(dimension_semantics=("arbitrary",),
                                             vmem_limit_bytes=VMEM_LIMIT),
        name="inproj_conv_prompt",
    )(hp, *consts)
    return us, tails


def _outproj_kernel(mix_ref, x_ref, w_ref, g_ref, b_ref, o_ref):
    tm = mix_ref.shape[0]
    n_piece = 4 if tm % (4 * 2 * SUBLANES) == 0 else 1
    step = tm // n_piece
    outs = [jnp.dot(mix_ref[i * step:(i + 1) * step, :], w_ref[...], preferred_element_type=F32)
            for i in range(n_piece)]
    for i, out in enumerate(outs):
        rs = slice(i * step, (i + 1) * step)
        o_ref[rs, :] = _layernorm_rows(DN_ALPHA * x_ref[rs, :] + out, g_ref[...], b_ref[...])


def _row_tiled_call(kernel, n_rows, tm, row_inputs, const_inputs, out_widths, out_dtypes, name, layer=None):
    in_specs = [pl.BlockSpec((tm, a.shape[1]), lambda i: (i, 0)) for a in row_inputs]
    if layer is None:
        in_specs += [pl.BlockSpec(a.shape, lambda i, nd=a.ndim: (0,) * nd) for a in const_inputs]
    else:
        in_specs += [pl.BlockSpec((None,) + a.shape[1:], lambda i, nd=a.ndim - 1: (layer,) + (0,) * nd)
                     for a in const_inputs]
    out_specs = [pl.BlockSpec((tm, w), lambda i: (i, 0)) for w in out_widths]
    out_shape = [jax.ShapeDtypeStruct((n_rows, w), dt) for w, dt in zip(out_widths, out_dtypes)]
    single = len(out_widths) == 1
    return pl.pallas_call(
        kernel,
        grid=(n_rows // tm,),
        in_specs=in_specs,
        out_specs=out_specs[0] if single else out_specs,
        out_shape=out_shape[0] if single else out_shape,
        compiler_params=pltpu.CompilerParams(dimension_semantics=("arbitrary",),
                                             vmem_limit_bytes=VMEM_LIMIT),
        name=name,
    )(*row_inputs, *const_inputs)


N_MIXER_CONST = 16
N_STATES = 7
N_PROMPT_STATES = 5
N_LEVELS = 6
MXU_LEVELS = 2


def _mixer_kernel(*refs, rows):
    nch = rows // CHUNK
    nlev = N_LEVELS
    n_in = 4 + N_MIXER_CONST + N_PROMPT_STATES
    (ua_ref, ub_ref, uc_ref, ud_ref, rows_ref, wbre_ref, wbim_ref, cre_ref, cim_ref,
     glu_ref, tab_ref, cmat_ref, cpair_ref, bdm_ref, lr_ref, ea_ref, eb_ref, blk_ref,
     sel_ref, ones_ref) = refs[:4 + N_MIXER_CONST]
    (mix_ref, o_ssd, o_s5re, o_s5im, o_gdn, o_hg) = refs[n_in:n_in + 1 + N_PROMPT_STATES]
    (hre_ref, him_ref, ssd_sc, gdn_sc, hg_sc) = refs[n_in + 1 + N_PROMPT_STATES:]

    j = pl.program_id(1)

    @pl.when(j == 0)
    def _():
        ssd_sc[...] = jnp.zeros_like(ssd_sc)
        gdn_sc[...] = jnp.zeros_like(gdn_sc)
        hg_sc[...] = jnp.zeros_like(hg_sc)
        o_s5re[...] = jnp.zeros_like(o_s5re)
        o_s5im[...] = jnp.zeros_like(o_s5im)

    first_valid = jnp.where(j == 0, PAD_FRONT, 0)

    def valid_rows(r):
        return r >= first_valid

    def row(i, w):
        return rows_ref[i:i + 1, 0:w]

    act_a = lambda rs, lo, hi: ua_ref[rs, 256 + lo:256 + hi]
    act_c = lambda rs, lo, hi: uc_ref[rs, 256 + lo:256 + hi]

    u_b = jnp.where(valid_rows(lax.broadcasted_iota(jnp.int32, (rows, D_BRANCH), 0)), ub_ref[:, 256:512], 0.0)
    hre_ref[...] = _dot(u_b, wbre_ref[...])
    him_ref[...] = _dot(u_b, wbim_ref[...])

    def s5_group(g, carry):
        r0 = pl.multiple_of(g * SUBLANES, SUBLANES)
        xr = hre_ref[pl.ds(r0, SUBLANES), :]
        xi = him_ref[pl.ds(r0, SUBLANES), :]
        for k, d in enumerate((1, 2, 4)):
            tr = tab_ref[k, :, 0:S5_LANES]
            ti = tab_ref[k, :, S5_LANES:2 * S5_LANES]
            sr = pltpu.roll(xr, d, axis=0)
            si = pltpu.roll(xi, d, axis=0)
            xr, xi = xr + tr * sr - ti * si, xi + tr * si + ti * sr
        cr, ci = carry
        pr = tab_ref[3, :, 0:S5_LANES]
        pi = tab_ref[3, :, S5_LANES:2 * S5_LANES]
        xr, xi = xr + pr * cr - pi * ci, xi + pr * ci + pi * cr
        hre_ref[pl.ds(r0, SUBLANES), :] = xr
        him_ref[pl.ds(r0, SUBLANES), :] = xi
        return (jnp.broadcast_to(xr[SUBLANES - 1:SUBLANES, :], (SUBLANES, S5_LANES)),
                jnp.broadcast_to(xi[SUBLANES - 1:SUBLANES, :], (SUBLANES, S5_LANES)))

    cr, ci = lax.fori_loop(0, rows // SUBLANES, s5_group, (o_s5re[0], o_s5im[0]))
    o_s5re[0] = cr
    o_s5im[0] = ci

    y5 = (_dot(hre_ref[...], cre_ref[...]) - _dot(him_ref[...], cim_ref[...])
          + row(5, D_BRANCH) * ub_ref[:, 256:512])
    y5 = _gelu_tanh(y5)
    y5 = y5 * _sigmoid(_dot(y5, glu_ref[...]) + row(6, D_BRANCH))
    mix_ref[:, 256:512] = (y5 * _silu(ub_ref[:, 0:256])).astype(BF16)

    tri_b = cmat_ref[0]

    def head_sum(x):
        return jnp.dot(x.astype(BF16), blk_ref[...], preferred_element_type=F32)

    def rms_finish(y, ss, g_row, z):
        return (y * lax.rsqrt(ss * (1.0 / HEAD_DIM) + RMS_EPS) * g_row * _silu(z)).astype(BF16)

    def hgrn_inputs(rs, valid_w):
        f_d = ud_ref[rs, 512:768]
        lsig = -_softplus(-f_d)
        t1 = row(11, D_BRANCH)
        t2 = row(12, D_BRANCH) + lsig
        logf = jnp.maximum(t1, t2) + _log1p_exp_neg(jnp.abs(t1 - t2))
        logf = jnp.where(valid_w, logf, 0.0)
        kd = row(13, D_BRANCH) * _sigmoid(-f_d)
        qd = _silu(ud_ref[rs, 256:512])
        vd = jnp.where(valid_w, ud_ref[rs, 768:1024], 0.0)
        xmm = _dot01_l(cmat_ref[1:2 + MXU_LEVELS].reshape((1 + MXU_LEVELS) * CHUNK, CHUNK), logf)
        gcd = xmm[0:CHUNK, :]
        xlev = []
        for lev in range(nlev - MXU_LEVELS):
            half = CHUNK >> (lev + 1)
            g3 = gcd.reshape(CHUNK // (2 * half), 2 * half, D_BRANCH)
            bound = jnp.broadcast_to(g3[:, half - 1:half, :], g3.shape).reshape(CHUNK, D_BRANCH)
            xlev.append(-jnp.abs(gcd - bound))
        xlev += [xmm[(1 + i) * CHUNK:(2 + i) * CHUNK, :] for i in range(MXU_LEVELS)]
        return logf, kd, qd, vd, gcd, xlev

    def prompt_block():
        lr0, lr1 = lr_ref[0], lr_ref[1]
        pairs = [slice(p * LANES, (p + 1) * LANES) for p in range(PAIRS)]
        data = [dict() for _ in range(nch)]

        def bd(x):
            xb = x.astype(BF16)
            return jnp.concatenate([xb * lr0, xb * lr1], axis=0)

        def mm_pair(a, b):
            return _dot(a, bd(b))

        def prologue(c):
            d = data[c]
            rs = slice(c * CHUNK, (c + 1) * CHUNK)
            d['rs'] = rs
            rr = lax.broadcasted_iota(jnp.int32, (CHUNK, LANES), 0) + c * CHUNK
            lane = lax.broadcasted_iota(jnp.int32, (CHUNK, LANES), 1)
            valid = valid_rows(rr)
            t = uc_ref[rs, 1024:1152] + row(1, LANES)
            vals = jnp.where((lane >= HEADS) & (lane < 2 * HEADS), _sigmoid(t), _softplus(t) * row(2, LANES))
            vals = jnp.where(valid, vals, 0.0)
            cs = _dot01_l(tri_b, vals)
            ex_v = _dot01_r(vals, ea_ref[...], pieces=2)
            q = act_c(rs, 0, 256)
            k = act_c(rs, 256, 512)
            qss = head_sum(q * q)
            kss = head_sum(k * k)
            valid_w = jnp.concatenate([valid, valid], axis=-1)
            logf, kd, qd, vd, gcd, xlev = hgrn_inputs(rs, valid_w)
            d.update(kd=kd, qd=qd, vd=vd)
            yield
            cs_pieces = _split(cs, 3)
            ex_c_all = jnp.dot(jnp.concatenate(cs_pieces, axis=0), eb_ref[...], preferred_element_type=F32)
            ex_c = sum(ex_c_all[i * CHUNK:(i + 1) * CHUNK] for i in range(3))
            crow = sum(lax.dot_general(sel_ref[...], piece, (((1,), (1,)), ((), ())), preferred_element_type=F32)
                       for piece in cs_pieces)
            d['diag'] = head_sum(qd * kd)
            yield
            dt_full, beta_full = ex_v[:, 0:256], ex_v[:, 256:512]
            gc, acum = ex_c[:, 0:256], ex_c[:, 256:512]
            d['prow'] = lambda r: jnp.concatenate([crow[r:r + 1, :], crow[r + 1:r + 2, :]], axis=1)
            xs = act_a(rs, 0, 256)
            alast = acum[CHUNK - 1:CHUNK, :]
            xdt = xs * dt_full
            d.update(acum=acum, alast=alast, xdt=xdt, xdt_end=xdt * jnp.exp(alast - acum), eacum=jnp.exp(acum))
            v = act_c(rs, 512, 768)
            q = q * lax.rsqrt(qss + L2_EPS) * (HEAD_DIM ** -0.5)
            k = k * lax.rsqrt(kss + L2_EPS)
            glast = gc[CHUNK - 1:CHUNK, :]
            egc = jnp.exp(gc)
            d.update(q=q, k=k, gc=gc, glast=glast, beta=beta_full, vb=v * beta_full, kbe=k * beta_full * egc,
                     qe=q * egc, kend=k * jnp.exp(glast - gc))
            gld = gcd[CHUNK - 1:CHUNK, :]
            g1, g2, g3 = [t.astype(F32) for t in _split(gld, 3)]
            rid = lax.broadcasted_iota(jnp.int32, (2 * SUBLANES, D_BRANCH), 0)
            d.update(qed=qd * jnp.exp(gcd), kend_d=kd * jnp.exp(gld - gcd), xlev=xlev,
                     g16=jnp.where(rid == 0, g1, jnp.where(rid == 1, g2, jnp.where(rid == 2, g3, 0.0))))

        def ssd_a(c, p):
            d, ps = data[c], pairs[p]
            g0 = 256 + p * HEAD_DIM
            bg = act_a(d['rs'], g0, g0 + HEAD_DIM)
            cg = act_a(d['rs'], g0 + LANES, g0 + LANES + HEAD_DIM)
            cb2 = _dot_nt(cg, jnp.concatenate([bg, bg], axis=0))
            lm = jnp.exp(jnp.minimum(d['acum'][:, ps] - d['prow'](HEADS + 2 * p), 0.0)) * cpair_ref[0]
            d['cbl', p] = cb2 * lm
            d['bg', p], d['cg', p] = bg, cg
            yield

        def gdn_a(c, p):
            d, ps = data[c], pairs[p]
            k = d['k'][:, ps]
            dec = jnp.exp(jnp.minimum(d['gc'][:, ps] - d['prow'](2 * p), 0.0)) * cpair_ref[0]
            kq = _dot_nt(jnp.concatenate([k, d['q'][:, ps]], axis=0), bd(k))
            yield
            m = kq[0:CHUNK] * dec * d['beta'][:, ps] * cpair_ref[1]
            d['aq', p] = kq[CHUNK:2 * CHUNK] * dec
            acc = cpair_ref[2] - m
            mp = mm_pair(m, m)
            yield
            for _ in range(nlev - 2):
                acc_add = mm_pair(acc, mp)
                mp = mm_pair(mp, mp)
                yield
                acc = acc + acc_add
            acc = acc + mm_pair(acc, mp)
            yield
            d['uw', p] = _dot(acc, jnp.concatenate([bd(d['vb'][:, ps]), bd(d['kbe'][:, ps])], axis=1))
            yield

        def hgrn_a(c, p):
            d, ps = data[c], pairs[p]
            qd, kd = d['qd'][:, ps], d['kd'][:, ps]
            amat = jnp.zeros((CHUNK, LANES), F32)
            for lev in range(nlev):
                z = jnp.exp(d['xlev'][lev][:, ps])
                amat = amat + _dot_nt(qd * z, bd(kd * z)) * cpair_ref[3 + lev]
                if lev % 2 == 1:
                    yield
            d['amat', p] = amat
            d['dcol', p] = jnp.exp(_dot_tn(d['g16'][:, ps], ones_ref[...]))
            yield

        def ssd_b():
            for c in range(nch):
                d = data[c]
                ya = []
                for p, ps in enumerate(pairs):
                    s_pk = ssd_sc[p]
                    ya.append(mm_pair(d['cbl', p], d['xdt'][:, ps]) + _dot(d['cg', p], s_pk) * d['eacum'][:, ps])
                    ssd_sc[p] = s_pk * jnp.exp(d['alast'][:, ps]) + _dot_tn(d['bg', p], d['xdt_end'][:, ps])
                yield
                ya = jnp.concatenate(ya, axis=-1) + row(3, D_BRANCH) * act_a(d['rs'], 0, 256)
                ss = head_sum(ya * ya)
                yield
                mix_ref[d['rs'], 0:256] = rms_finish(ya, ss, row(4, D_BRANCH), ua_ref[d['rs'], 0:256])

        def gdn_b():
            bdm = bdm_ref[...]
            for c in range(nch):
                d = data[c]
                ws, s_old = [], []
                for p, ps in enumerate(pairs):
                    s_bd = gdn_sc[p]
                    s_old.append(s_bd)
                    ws.append(_dot(jnp.concatenate([d['uw', p][:, LANES:2 * LANES], d['qe'][:, ps]], axis=0), s_bd))
                yield
                yc = []
                for p, ps in enumerate(pairs):
                    v_new = d['uw', p][:, 0:LANES] - ws[p][0:CHUNK]
                    yc.append(ws[p][CHUNK:2 * CHUNK] + mm_pair(d['aq', p], v_new))
                    gdn_sc[p] = s_old[p] * jnp.exp(d['glast'][:, ps]) + _dot_tn(d['kend'][:, ps], v_new) * bdm
                yield
                yc = jnp.concatenate(yc, axis=-1)
                ss = head_sum(yc * yc)
                yield
                mix_ref[d['rs'], 512:768] = rms_finish(yc, ss, row(10, D_BRANCH), uc_ref[d['rs'], 0:256])

        def hgrn_b():
            bdm = bdm_ref[...]
            for c in range(nch):
                d = data[c]
                yd = []
                for p, ps in enumerate(pairs):
                    s_bd = hg_sc[p]
                    vd = d['vd'][:, ps]
                    yd.append(mm_pair(d['amat', p], vd) + d['diag'][:, ps] * vd + _dot(d['qed'][:, ps], s_bd))
                    hg_sc[p] = s_bd * d['dcol', p] + _dot_tn(d['kend_d'][:, ps], vd) * bdm
                yield
                yd = jnp.concatenate(yd, axis=-1)
                ss = head_sum(yd * yd)
                yield
                mix_ref[d['rs'], 768:1024] = rms_finish(yd, ss, row(14, D_BRANCH), ud_ref[d['rs'], 0:256])

        _run_interleaved([prologue(c) for c in range(nch)])
        _run_interleaved([f(c, p) for c in range(nch) for p in range(PAIRS) for f in (gdn_a, hgrn_a, ssd_a)])
        _run_interleaved([gdn_b(), hgrn_b(), ssd_b()])

    prompt_block()

    @pl.when(j == pl.num_programs(1) - 1)
    def _():
        for p in range(PAIRS):
            for e in range(2):
                es = slice(e * HEAD_DIM, (e + 1) * HEAD_DIM)
                o_ssd[0, 2 * p + e] = ssd_sc[p, :, es]
                o_gdn[0, 2 * p + e] = gdn_sc[p, es, es]
                o_hg[0, 2 * p + e] = hg_sc[p, es, es]


def _mixer_constants():
    i = np.arange(CHUNK)[:, None]
    jn = np.arange(CHUNK)[None, :]
    tri = (jn <= i).astype(np.float32)
    strict = (jn < i).astype(np.float32)
    eye = np.eye(CHUNK, dtype=np.float32)
    wlev, mlev = [], []
    for lev in range(N_LEVELS):
        b = CHUNK >> (lev + 1)
        blk_i, pos_i = i // (2 * b), i % (2 * b)
        mid = blk_i * 2 * b + b
        upper = pos_i >= b
        w = np.where(upper, (jn >= mid) & (jn <= i), (jn > i) & (jn < mid)).astype(np.float32)
        msk = (((jn // (2 * b)) == blk_i) & upper & ((jn % (2 * b)) < b)).astype(np.float32)
        wlev.append(w)
        mlev.append(msk)
    cmat = np.stack([tri, tri] + wlev[N_LEVELS - MXU_LEVELS:])
    cmask = np.stack([tri, strict, eye] + mlev)
    cpair = np.concatenate([cmask, cmask], axis=-1)
    bdm = np.kron(np.eye(2, dtype=np.float32), np.ones((HEAD_DIM, HEAD_DIM), np.float32))
    lane = np.arange(LANES)[None, :]
    lr = np.stack([np.broadcast_to(lane < HEAD_DIM, (CHUNK, LANES)),
                   np.broadcast_to(lane >= HEAD_DIM, (CHUNK, LANES))]).astype(np.float32)
    e_a = np.zeros((LANES, 512), np.float32)
    e_b = np.zeros((LANES, 512), np.float32)
    blk = np.zeros((256, 256), np.float32)
    sel = np.zeros((2 * SUBLANES, LANES), np.float32)
    for h in range(HEADS):
        hs = slice(h * HEAD_DIM, (h + 1) * HEAD_DIM)
        e_a[h, hs] = 1.0
        e_a[HEADS + h, 256 + h * HEAD_DIM:256 + (h + 1) * HEAD_DIM] = 1.0
        e_b[2 * HEADS + h, hs] = 1.0
        e_b[3 * HEADS + h, 256 + h * HEAD_DIM:256 + (h + 1) * HEAD_DIM] = 1.0
        blk[hs, hs] = 1.0
        sel[h, 2 * HEADS + h] = 1.0
        sel[HEADS + h, 3 * HEADS + h] = 1.0
    ones = np.ones((2 * SUBLANES, LANES), np.float32)
    return (jnp.asarray(cmat, BF16), jnp.asarray(cpair, F32), jnp.asarray(bdm, F32),
            jnp.asarray(lr, BF16), jnp.asarray(e_a, BF16), jnp.asarray(e_b, BF16), jnp.asarray(blk, BF16),
            jnp.asarray(sel, BF16), jnp.asarray(ones, BF16))


def _layer_spec(a, l):
    nd = a.ndim - 1
    return pl.BlockSpec((None,) + a.shape[1:], lambda b, j: (l,) + (0,) * nd)


def _mixer_call(us, l, wts, consts, acc, *, n_seq, rows):
    ua, ub, uc, ud = us
    n_rows = ua.shape[0]
    nblk = n_rows // n_seq // rows
    rmap = lambda b, j: (b * nblk + j, 0)

    def state_spec(tail):
        return pl.BlockSpec((None, 1) + tail, lambda b, j: (l, b) + (0,) * len(tail))

    layer_consts = [wts[k] for k in ('rows', 'wb_re', 'wb_im', 'c_re', 'c_im', 'glu_w', 's5tab')]
    assert len(layer_consts) + len(consts) == N_MIXER_CONST
    in_specs = [pl.BlockSpec((rows, w), rmap) for w in (WA, WB, WC, WD)]
    in_specs += [_layer_spec(a, l) for a in layer_consts]
    in_specs += [pl.BlockSpec(a.shape, lambda b, j, nd=a.ndim: (0,) * nd) for a in consts]
    inputs = [ua, ub, uc, ud] + layer_consts + list(consts)
    mat = (HEADS, HEAD_DIM, HEAD_DIM)
    s5_tail = (SUBLANES, S5_LANES)
    tails = [mat, s5_tail, s5_tail, mat, mat]
    scratch = [pltpu.VMEM((rows, S5_LANES), F32),
               pltpu.VMEM((rows, S5_LANES), F32),
               pltpu.VMEM((PAIRS, HEAD_DIM, LANES), F32),
               pltpu.VMEM((PAIRS, LANES, LANES), F32),
               pltpu.VMEM((PAIRS, LANES, LANES), F32)]
    assert len(acc) == len(tails) == N_PROMPT_STATES
    n_before_acc = len(inputs)
    inputs += list(acc)
    in_specs += [pl.BlockSpec(memory_space=pl.ANY)] * len(acc)
    out_shape = [jax.ShapeDtypeStruct((n_rows, D_MODEL), BF16)]
    out_shape += [jax.ShapeDtypeStruct(a.shape, a.dtype) for a in acc]
    out_specs = [pl.BlockSpec((rows, D_MODEL), rmap)] + [state_spec(t) for t in tails]
    aliases = {n_before_acc + i: 1 + i for i in range(len(acc))}
    return pl.pallas_call(
        functools.partial(_mixer_kernel, rows=rows),
        grid=(n_seq, nblk),
        in_specs=in_specs,
        out_specs=out_specs,
        out_shape=out_shape,
        scratch_shapes=scratch,
        input_output_aliases=aliases,
        compiler_params=pltpu.CompilerParams(dimension_semantics=("arbitrary", "arbitrary"),
                                             vmem_limit_bytes=VMEM_LIMIT),
        name="mixer_prompt",
    )(*inputs)


SAMPLE_COL_WIDTHS = (SSD_XBC, 4 * HEADS, 4 * HEADS, D_BRANCH, D_BRANCH, D_BRANCH, D_BRANCH, GDN_QKV, D_BRANCH,
                     D_BRANCH, D_BRANCH, D_BRANCH, D_BRANCH)
(C_CONVB_A, C_SBIAS, C_SSCALE, C_SSD_D, C_SSD_G, C_S5_D, C_GLU_B, C_CONVB_C, C_GDN_G, C_LOGLB, C_LOG1M, C_ONEM,
 C_HG_G) = [int(v) for v in np.cumsum((0,) + SAMPLE_COL_WIDTHS[:-1])]
U_ZA, U_XBC, U_ZB, U_UB, U_ZC, U_QKV, U_SMALL, U_ZD, U_QD, U_FD, U_ID = (
    0, 256, WA, WA + 256, WA + WB, WA + WB + 256, WA + WB + 1024, WA + WB + WC, WA + WB + WC + 256,
    WA + WB + WC + 512, WA + WB + WC + 768)
V_HALF = HEAD_DIM // 2


def _sample_inproj_kernel(h_ref, wt_ref, o_ref):
    o_ref[...] = lax.dot_general(wt_ref[...], h_ref[...].astype(BF16), (((1,), (1,)), ((), ())),
                                 preferred_element_type=F32)


def _sample_outproj_kernel(mixt_ref, x_ref, w_ref, g_ref, b_ref, o_ref):
    out = lax.dot_general(mixt_ref[...], w_ref[...], (((0,), (0,)), ((), ())), preferred_element_type=F32)
    o_ref[...] = _layernorm_rows(DN_ALPHA * x_ref[...] + out, g_ref[...], b_ref[...])


def _sample_mixer_kernel(ut_ref, cols_ref, cwa_ref, cwc_ref, lb_ref, wbre_ref, wbim_ref, cre_ref, cim_ref, glu_ref,
                         i_ssd, i_ca, i_s5re, i_s5im, i_gdn, i_cc, i_hg, *rest, n_tok, n_seq):
    (mix_ref, o_ssd, o_ca, o_s5re, o_s5im, o_gdn, o_cc, o_hg) = rest[N_STATES:N_STATES + 1 + N_STATES]
    (acta_ref, actc_ref, small_ref, hre_ref, him_ref, gq_ref, gk_ref, hq_ref, hk_ref, hf_ref,
     ya_ref, yc_ref, yd_ref) = rest[N_STATES + 1 + N_STATES:]
    h = pl.program_id(0)
    vh = pl.program_id(1)
    toks = [slice(t * n_seq, (t + 1) * n_seq) for t in range(n_tok)]
    hrow = pl.multiple_of(h * HEAD_DIM, HEAD_DIM)

    def col(c0, n, off=0):
        return cols_ref[pl.ds(c0 + off, n), :]

    @pl.when((h == 0) & (vh == 0))
    def _():
        for u0, width, i_c, o_c, cw_ref, act_ref, cb in ((U_XBC, SSD_XBC, i_ca, o_ca, cwa_ref, acta_ref, C_CONVB_A),
                                                        (U_QKV, GDN_QKV, i_cc, o_cc, cwc_ref, actc_ref, C_CONVB_C)):
            xx = [i_c[i] for i in range(CONV_W - 1)] + [ut_ref[u0:u0 + width, ts] for ts in toks]
            for t in range(n_tok):
                acc = cols_ref[cb:cb + width, :]
                for w in range(CONV_W):
                    acc = acc + cw_ref[w] * xx[t + w]
                act_ref[t] = _silu(acc)
            for i in range(CONV_W - 1):
                o_c[i] = xx[n_tok + i]
        rid = lax.broadcasted_iota(jnp.int32, (4 * HEADS, n_seq), 0)
        for t, ts in enumerate(toks):
            v = ut_ref[U_SMALL:U_SMALL + 4 * HEADS, ts] + cols_ref[C_SBIAS:C_SBIAS + 4 * HEADS, :]
            small_ref[t] = jnp.where((rid >= HEADS) & (rid < 2 * HEADS), _sigmoid(v),
                                     _softplus(v) * cols_ref[C_SSCALE:C_SSCALE + 4 * HEADS, :])
        u_b = ut_ref[U_UB:U_UB + D_BRANCH, :]
        bu_re = _dot(wbre_ref[...], u_b)
        bu_im = _dot(wbim_ref[...], u_b)
        sr, si = i_s5re[...], i_s5im[...]
        lr, li = lb_ref[0], lb_ref[1]
        for t, ts in enumerate(toks):
            sr, si = lr * sr - li * si + bu_re[:, ts], lr * si + li * sr + bu_im[:, ts]
            hre_ref[:, ts] = sr
            him_ref[:, ts] = si
        o_s5re[...] = sr
        o_s5im[...] = si
        y5 = _dot(cre_ref[...], hre_ref[...]) - _dot(cim_ref[...], him_ref[...])
        for ts in toks:
            y = _gelu_tanh(y5[:, ts] + cols_ref[C_S5_D:C_S5_D + D_BRANCH, :] * u_b[:, ts])
            g = _dot(glu_ref[...], y) + cols_ref[C_GLU_B:C_GLU_B + D_BRANCH, :]
            mix_ref[D_BRANCH:2 * D_BRANCH, ts] = (y * _sigmoid(g) * _silu(ut_ref[U_ZB:U_ZB + D_BRANCH, ts])).astype(BF16)

    vrow = pl.multiple_of(vh * V_HALF, V_HALF)
    vsel = pl.ds(hrow + vrow, V_HALF)

    def scal(t, r):
        return small_ref[t, pl.ds(r * HEADS + h, 1), :]

    grp = (h // 2) * HEAD_DIM
    xdt = [acta_ref[t, vsel, :] * scal(t, 0) for t in range(n_tok)]
    ea = [jnp.exp(scal(t, 3)) for t in range(n_tok)]

    def ssd_row(n, ys):
        s = i_ssd[n]
        ys = list(ys)
        for t in range(n_tok):
            s = s * ea[t] + acta_ref[t, pl.ds(256 + grp + n, 1), :] * xdt[t]
            ys[t] = ys[t] + acta_ref[t, pl.ds(384 + grp + n, 1), :] * s
        o_ssd[n] = s
        return tuple(ys)

    zero = jnp.zeros((V_HALF, n_seq), F32)
    ys = lax.fori_loop(0, HEAD_DIM, ssd_row, (zero,) * n_tok)
    for t in range(n_tok):
        ya_ref[t, pl.ds(vrow, V_HALF), :] = ys[t] + col(C_SSD_D, V_HALF, hrow + vrow) * acta_ref[t, vsel, :]

    @pl.when(vh == 0)
    def _():
        for t in range(n_tok):
            q = actc_ref[t, pl.ds(hrow, HEAD_DIM), :]
            k = actc_ref[t, pl.ds(256 + hrow, HEAD_DIM), :]
            gq_ref[t] = q * lax.rsqrt(jnp.sum(q * q, axis=0, keepdims=True) + L2_EPS) * (HEAD_DIM ** -0.5)
            gk_ref[t] = k * lax.rsqrt(jnp.sum(k * k, axis=0, keepdims=True) + L2_EPS)

    for t in range(n_tok):
        alpha = jnp.exp(scal(t, 2))
        src = i_gdn if t == 0 else o_gdn

        def decay_row(kk, pred, alpha=alpha, src=src, t=t):
            s = src[kk] * alpha
            o_gdn[kk] = s
            return pred + gk_ref[t, pl.ds(kk, 1), :] * s

        pred = lax.fori_loop(0, HEAD_DIM, decay_row, zero)
        v_new = scal(t, 1) * (actc_ref[t, pl.ds(512 + hrow + vrow, V_HALF), :] - pred)

        def update_row(kk, o, v_new=v_new, t=t):
            s = o_gdn[kk] + gk_ref[t, pl.ds(kk, 1), :] * v_new
            o_gdn[kk] = s
            return o + gq_ref[t, pl.ds(kk, 1), :] * s

        yc_ref[t, pl.ds(vrow, V_HALF), :] = lax.fori_loop(0, HEAD_DIM, update_row, zero)

    @pl.when(vh == 0)
    def _():
        for t, ts in enumerate(toks):
            f_d = ut_ref[pl.ds(U_FD + hrow, HEAD_DIM), ts]
            t1 = col(C_LOGLB, HEAD_DIM, hrow)
            t2 = col(C_LOG1M, HEAD_DIM, hrow) - _softplus(-f_d)
            logf = jnp.maximum(t1, t2) + _log1p_exp_neg(jnp.abs(t1 - t2))
            hf_ref[t] = jnp.exp(logf)
            hk_ref[t] = col(C_ONEM, HEAD_DIM, hrow) * _sigmoid(-f_d)
            hq_ref[t] = _silu(ut_ref[pl.ds(U_QD + hrow, HEAD_DIM), ts])

    vd = [ut_ref[pl.ds(U_ID + hrow + vrow, V_HALF), ts] for ts in toks]

    def hgrn_row(kk, os_):
        s = i_hg[kk]
        os_ = list(os_)
        for t in range(n_tok):
            s = s * hf_ref[t, pl.ds(kk, 1), :] + hk_ref[t, pl.ds(kk, 1), :] * vd[t]
            os_[t] = os_[t] + hq_ref[t, pl.ds(kk, 1), :] * s
        o_hg[kk] = s
        return tuple(os_)

    os_ = lax.fori_loop(0, HEAD_DIM, hgrn_row, (zero,) * n_tok)
    for t in range(n_tok):
        yd_ref[t, pl.ds(vrow, V_HALF), :] = os_[t]

    @pl.when(vh == HEAD_DIM // V_HALF - 1)
    def _():
        for y_ref, g0, z0, m0 in ((ya_ref, C_SSD_G, U_ZA, 0), (yc_ref, C_GDN_G, U_ZC, 2 * D_BRANCH),
                                  (yd_ref, C_HG_G, U_ZD, 3 * D_BRANCH)):
            for t, ts in enumerate(toks):
                y = y_ref[t]
                ms = jnp.mean(y * y, axis=0, keepdims=True)
                z = ut_ref[pl.ds(z0 + hrow, HEAD_DIM), ts]
                mix_ref[pl.ds(m0 + hrow, HEAD_DIM), ts] = (
                    y * lax.rsqrt(ms + RMS_EPS) * col(g0, HEAD_DIM, hrow) * _silu(z)).astype(BF16)


def _sample_mixer_call(ut, l, sw, states_t, acc, *, n_tok, n_seq):
    consts = [sw[k] for k in ('cols', 'cw_a', 'cw_c', 'lb', 'wb_re', 'wb_im', 'c_re', 'c_im', 'glu_w')]
    whole = lambda a: pl.BlockSpec((None,) + a.shape[1:], lambda h, v, nd=a.ndim - 1: (l,) + (0,) * nd)
    mat_spec = pl.BlockSpec((None, None, HEAD_DIM, V_HALF, n_seq), lambda h, v: (l, h, 0, v, 0))
    state_specs = [mat_spec, whole(states_t[1]), whole(states_t[2]), whole(states_t[3]), mat_spec,
                   whole(states_t[5]), mat_spec]
    n_cols = ut.shape[1]
    in_specs = [pl.BlockSpec(ut.shape, lambda h, v: (0, 0))] + [whole(a) for a in consts] + state_specs
    in_specs += [pl.BlockSpec(memory_space=pl.ANY)] * N_STATES
    inputs = [ut] + consts + list(states_t) + list(acc)
    out_shape = [jax.ShapeDtypeStruct((D_MODEL, n_cols), BF16)] + [jax.ShapeDtypeStruct(a.shape, a.dtype) for a in acc]
    out_specs = [pl.BlockSpec((D_MODEL, n_cols), lambda h, v: (0, 0))] + state_specs
    n_before_acc = 1 + len(consts) + N_STATES
    tok_tile = lambda rows: pltpu.VMEM((n_tok, rows, n_seq), F32)
    scratch = [tok_tile(SSD_XBC), tok_tile(GDN_QKV), tok_tile(4 * HEADS),
               pltpu.VMEM((S5_LANES, n_cols), F32), pltpu.VMEM((S5_LANES, n_cols), F32)]
    scratch += [tok_tile(HEAD_DIM)] * 8
    return pl.pallas_call(
        functools.partial(_sample_mixer_kernel, n_tok=n_tok, n_seq=n_seq),
        grid=(HEADS, HEAD_DIM // V_HALF),
        in_specs=in_specs,
        out_specs=out_specs,
        out_shape=out_shape,
        scratch_shapes=scratch,
        input_output_aliases={n_before_acc + i: 1 + i for i in range(N_STATES)},
        compiler_params=pltpu.CompilerParams(dimension_semantics=("arbitrary", "arbitrary"),
                                             vmem_limit_bytes=VMEM_LIMIT),
        name="mixer_sample",
    )(*inputs)


W_IN_ROWS_PER_STEP = 128


def _regroup_w_in_kernel(w_ref, o_ref, ot_ref):
    w = w_ref[...]
    small = jnp.concatenate([w[:, 768:772], w[:, 2308:2316], w[:, 768:772],
                             jnp.zeros((w.shape[0], LANES - 4 * HEADS), F32)], axis=1)
    wr = jnp.concatenate([w[:, 0:768], w[:, 772:1284], w[:, 1284:2308], small, w[:, 2316:3340]], axis=1)
    o_ref[...] = wr.astype(BF16)
    ot_ref[...] = wr.T.astype(BF16)


def _regroup_w_in(w_in):
    depth, d_model, d_in = w_in.shape
    tk = W_IN_ROWS_PER_STEP
    w_cols = WA + WB + WC + WD
    return pl.pallas_call(
        _regroup_w_in_kernel,
        grid=(depth, d_model // tk),
        in_specs=[pl.BlockSpec((None, tk, d_in), lambda l, i: (l, i, 0))],
        out_specs=[pl.BlockSpec((None, tk, w_cols), lambda l, i: (l, i, 0)),
                   pl.BlockSpec((None, w_cols, tk), lambda l, i: (l, 0, i))],
        out_shape=[jax.ShapeDtypeStruct((depth, d_model, w_cols), BF16),
                   jax.ShapeDtypeStruct((depth, w_cols, d_model), BF16)],
        compiler_params=pltpu.CompilerParams(dimension_semantics=("arbitrary", "arbitrary"),
                                             vmem_limit_bytes=VMEM_LIMIT),
        name="regroup_w_in",
    )(w_in.astype(F32))


def _pad_rows(v, width=ROW_W):
    v = v.astype(F32).reshape(v.shape[0], -1)
    return jnp.pad(v, ((0, 0), (0, width - v.shape[1])))


def _stacked_weights(p, lbs):
    depth = lbs.shape[0]
    rep = lambda v: jnp.repeat(v.astype(F32), HEAD_DIM, axis=-1)
    zeros_h = jnp.zeros((depth, HEADS), F32)
    small_bias = jnp.concatenate([p['ssd_dt_bias'], zeros_h, p['gdn_dt_bias'], p['ssd_dt_bias']], axis=1)
    small_scale = jnp.concatenate([jnp.ones((depth, HEADS), F32), zeros_h, -jnp.exp(p['gdn_a_log']),
                                   -jnp.exp(p['ssd_a_log'])], axis=1)
    zrow = jnp.zeros((depth, ROW_W), F32)
    rows = jnp.stack([
        _pad_rows(p['ssd_conv_b']), _pad_rows(small_bias), _pad_rows(small_scale),
        _pad_rows(rep(p['ssd_d'])), _pad_rows(p['ssd_norm_g']), _pad_rows(p['s5_d']),
        _pad_rows(p['s5_glu_b']), _pad_rows(p['gdn_conv_b']), zrow, zrow,
        _pad_rows(p['gdn_norm_g']), _pad_rows(jnp.log(lbs)), _pad_rows(jnp.log1p(-lbs)), _pad_rows(1.0 - lbs),
        _pad_rows(p['hg_norm_g']), zrow], axis=1)

    pad_cw = lambda cw: jnp.pad(cw.astype(F32), ((0, 0), (0, SUBLANES - CONV_W), (0, 0)))

    lam_re, lam_im = p['s5_lam_re'].astype(F32), p['s5_lam_im'].astype(F32)
    dt = jnp.exp(p['s5_log_dt'].astype(F32))[..., None]
    mag = jnp.exp(lam_re * dt)
    ang = lam_im * dt
    lb_re, lb_im = mag * jnp.cos(ang), mag * jnp.sin(ang)
    den = jnp.square(lam_re) + jnp.square(lam_im)
    nr = lb_re - 1.0
    coef_re = (nr * lam_re + lb_im * lam_im) / den
    coef_im = (lb_im * lam_re - nr * lam_im) / den
    b_re, b_im = p['s5_b_re'].astype(F32), p['s5_b_im'].astype(F32)
    bb_re = coef_re[..., None] * b_re - coef_im[..., None] * b_im
    bb_im = coef_re[..., None] * b_im + coef_im[..., None] * b_re
    eye_g = jnp.eye(S5_GROUPS, dtype=F32)
    bd_in = lambda bb: jnp.einsum('lgnq,gh->lgqhn', bb, eye_g).reshape(depth, D_BRANCH, S5_LANES).astype(BF16)
    bd_out = lambda c: jnp.einsum('lgqn,gh->lgnhq', c.astype(F32), eye_g).reshape(depth, S5_LANES, D_BRANCH).astype(BF16)

    pr, pi = [lb_re.reshape(depth, -1)], [lb_im.reshape(depth, -1)]
    for _ in range(SUBLANES - 1):
        pr, pi = (pr + [pr[-1] * pr[0] - pi[-1] * pi[0]], pi + [pr[-1] * pi[0] + pi[-1] * pr[0]])
    pw = jnp.stack([jnp.concatenate([a, b], axis=-1) for a, b in zip(pr, pi)], axis=1)
    ridx = jnp.arange(SUBLANES)[None, :, None]
    tabs = [jnp.where(ridx >= d, pw[:, d - 1:d, :], 0.0) for d in (1, 2, 4)] + [pw]
    s5tab = jnp.stack(tabs, axis=1).astype(F32)

    lanes = lambda v: jnp.broadcast_to(v.astype(F32)[..., None], v.shape + (LANES,))
    col_vecs = [p['ssd_conv_b'], small_bias, small_scale, rep(p['ssd_d']), p['ssd_norm_g'], p['s5_d'], p['s5_glu_b'],
                p['gdn_conv_b'], p['gdn_norm_g'], jnp.log(lbs), jnp.log1p(-lbs), 1.0 - lbs, p['hg_norm_g']]
    assert [int(np.prod(v.shape[1:])) for v in col_vecs] == list(SAMPLE_COL_WIDTHS)
    cols_s = lanes(jnp.concatenate([v.astype(F32).reshape(depth, -1) for v in col_vecs], axis=1))
    bd_in_t = lambda bb: jnp.einsum('lgnq,gh->lgnhq', bb, eye_g).reshape(depth, S5_LANES, D_BRANCH).astype(BF16)
    bd_out_t = lambda c: jnp.einsum('lgqn,gh->lgqhn', c.astype(F32), eye_g).reshape(depth, D_BRANCH, S5_LANES).astype(BF16)
    sample = dict(cols=cols_s, cw_a=lanes(p['ssd_conv_w']), cw_c=lanes(p['gdn_conv_w']),
                  lb=lanes(jnp.stack([lb_re.reshape(depth, -1), lb_im.reshape(depth, -1)], axis=1)),
                  wb_re=bd_in_t(bb_re), wb_im=bd_in_t(bb_im), c_re=bd_out_t(p['s5_c_re']), c_im=bd_out_t(p['s5_c_im']),
                  glu_w=jnp.swapaxes(p['s5_glu_w'], 1, 2).astype(BF16))

    w_in_r, w_in_t = _regroup_w_in(p['w_in'])
    return dict(w_in=w_in_r, w_in_t=w_in_t, rows=rows, cw_a=pad_cw(p['ssd_conv_w']), cw_c=pad_cw(p['gdn_conv_w']),
                wb_re=bd_in(bb_re), wb_im=bd_in(bb_im), c_re=bd_out(p['s5_c_re']), c_im=bd_out(p['s5_c_im']),
                glu_w=p['s5_glu_w'].astype(BF16), s5tab=s5tab, w_out=p['w_out'].astype(BF16),
                ln_g=p['ln_g'].astype(F32)[:, None, :], ln_b=p['ln_b'].astype(F32)[:, None, :], sample=sample)


def _pick_tile(n_rows, candidates):
    for t in candidates:
        if n_rows % t == 0:
            return t
    raise ValueError(f"no row tile for {n_rows}")


def kernel(x_prompt, x_sample, state_ssd, state_ssd_conv, state_s5_re, state_s5_im, state_gdn, state_gdn_conv, state_hgrn, meta_tokens, ln_in_g, ln_in_b, w_in, ssd_conv_w, ssd_conv_b, ssd_dt_bias, ssd_a_log, ssd_d, ssd_norm_g, s5_lam_re, s5_lam_im, s5_log_dt, s5_b_re, s5_b_im, s5_c_re, s5_c_im, s5_d, s5_glu_w, s5_glu_b, gdn_conv_w, gdn_conv_b, gdn_a_log, gdn_dt_bias, gdn_norm_g, hg_lb_raw, hg_norm_g, w_out, ln_g, ln_b):
    p = dict(w_in=w_in, ssd_conv_w=ssd_conv_w, ssd_conv_b=ssd_conv_b, ssd_dt_bias=ssd_dt_bias,
             ssd_a_log=ssd_a_log, ssd_d=ssd_d, ssd_norm_g=ssd_norm_g, s5_lam_re=s5_lam_re,
             s5_lam_im=s5_lam_im, s5_log_dt=s5_log_dt, s5_b_re=s5_b_re, s5_b_im=s5_b_im, s5_c_re=s5_c_re,
             s5_c_im=s5_c_im, s5_d=s5_d, s5_glu_w=s5_glu_w, s5_glu_b=s5_glu_b, gdn_conv_w=gdn_conv_w,
             gdn_conv_b=gdn_conv_b, gdn_a_log=gdn_a_log, gdn_dt_bias=gdn_dt_bias, gdn_norm_g=gdn_norm_g,
             hg_norm_g=hg_norm_g, w_out=w_out, ln_g=ln_g, ln_b=ln_b)
    bp, seq, _ = x_prompt.shape
    bs, dseq, _ = x_sample.shape
    assert dseq >= CONV_W - 1 and bs % LANES == 0
    t_pad = PAD_FRONT + N_META + seq
    rows_p = 3 * CHUNK
    assert t_pad % rows_p == 0
    n_p = bp * t_pad
    n_s = bs * dseq

    soft = jax.nn.softmax(hg_lb_raw.astype(F32), axis=0)
    csum = jnp.cumsum(soft, axis=0)
    lbs = csum - csum[0]

    xs = jnp.swapaxes(x_sample.astype(F32), 0, 1).reshape(n_s, D_MODEL)

    tm_p = _pick_tile(n_p, (704, 512, 384, 192))
    tm_s = n_s
    g_in, b_in = ln_in_g.astype(F32)[None], ln_in_b.astype(F32)[None]
    hp = pl.pallas_call(
        _ln_in_prompt_kernel,
        grid=(bp,),
        in_specs=[pl.BlockSpec((None, seq, D_MODEL), lambda b: (b, 0, 0)),
                  pl.BlockSpec((N_META, D_MODEL), lambda b: (0, 0)),
                  pl.BlockSpec((1, D_MODEL), lambda b: (0, 0)), pl.BlockSpec((1, D_MODEL), lambda b: (0, 0))],
        out_specs=pl.BlockSpec((None, t_pad, D_MODEL), lambda b: (b, 0, 0)),
        out_shape=jax.ShapeDtypeStruct((bp, t_pad, D_MODEL), F32),
        compiler_params=pltpu.CompilerParams(dimension_semantics=("arbitrary",), vmem_limit_bytes=VMEM_LIMIT),
        name="ln_in_prompt",
    )(x_prompt.astype(F32), meta_tokens.astype(F32), g_in, b_in).reshape(n_p, D_MODEL)
    hs = _row_tiled_call(_ln_in_kernel, n_s, tm_s, [xs], [g_in, b_in], [D_MODEL], [F32], "ln_in_sample")

    consts_p = _mixer_constants()
    depth = w_in.shape[0]
    wts = _stacked_weights(p, lbs)

    mat = (HEADS, HEAD_DIM, HEAD_DIM)
    zeros = lambda *shape: jnp.zeros((depth,) + shape, F32)
    acc_p = [zeros(bp, *mat), zeros(bp, SUBLANES, S5_LANES), zeros(bp, SUBLANES, S5_LANES), zeros(bp, *mat),
             zeros(bp, *mat)]
    conv_tails = []
    seq_last = lambda a: jnp.moveaxis(a.astype(F32), 1, -1)
    st_in = (seq_last(state_ssd), seq_last(state_ssd_conv),
             seq_last(state_s5_re).reshape(depth, S5_LANES, bs), seq_last(state_s5_im).reshape(depth, S5_LANES, bs),
             seq_last(state_gdn), seq_last(state_gdn_conv), seq_last(state_hgrn))
    acc_s = [jnp.zeros(a.shape, F32) for a in st_in]
    out_consts = [wts['w_out'], wts['ln_g'], wts['ln_b']]
    w_cols = WA + WB + WC + WD
    for l in range(depth):
        us, tails = _inproj_conv_call(hp, l, wts, bp, tm_p)
        conv_tails.append(tails[:, SUBLANES - (CONV_W - 1):, :])
        mix, *acc_p = _mixer_call(us, l, wts, consts_p, acc_p, n_seq=bp, rows=rows_p)
        hp = _row_tiled_call(_outproj_kernel, n_p, tm_p, [mix, hp], out_consts, [D_MODEL], [F32],
                             "outproj_prompt", layer=l)

        n_wblk = 6
        ut = pl.pallas_call(
            _sample_inproj_kernel,
            grid=(n_wblk,),
            in_specs=[pl.BlockSpec((n_s, D_MODEL), lambda i: (0, 0)),
                      pl.BlockSpec((None, w_cols // n_wblk, D_MODEL), lambda i: (l, i, 0))],
            out_specs=pl.BlockSpec((w_cols // n_wblk, n_s), lambda i: (i, 0)),
            out_shape=jax.ShapeDtypeStruct((w_cols, n_s), F32),
            compiler_params=pltpu.CompilerParams(dimension_semantics=("arbitrary",), vmem_limit_bytes=VMEM_LIMIT),
            name="inproj_sample",
        )(hs, wts['w_in_t'])
        mixt, *acc_s = _sample_mixer_call(ut, l, wts['sample'], st_in, acc_s, n_tok=dseq, n_seq=bs)
        hs = pl.pallas_call(
            _sample_outproj_kernel,
            grid=(1,),
            in_specs=[pl.BlockSpec((D_MODEL, n_s), lambda i: (0, 0)), pl.BlockSpec((n_s, D_MODEL), lambda i: (0, 0)),
                      pl.BlockSpec((None, D_MODEL, D_MODEL), lambda i: (l, 0, 0)),
                      pl.BlockSpec((None, 1, D_MODEL), lambda i: (l, 0, 0)),
                      pl.BlockSpec((None, 1, D_MODEL), lambda i: (l, 0, 0))],
            out_specs=pl.BlockSpec((n_s, D_MODEL), lambda i: (0, 0)),
            out_shape=jax.ShapeDtypeStruct((n_s, D_MODEL), F32),
            compiler_params=pltpu.CompilerParams(dimension_semantics=("arbitrary",), vmem_limit_bytes=VMEM_LIMIT),
            name="outproj_sample",
        )(mixt, hs, *out_consts)

    y_prompt = hp.reshape(bp, t_pad, D_MODEL)[:, PAD_FRONT + N_META:]
    y_sample = jnp.swapaxes(hs.reshape(dseq, bs, D_MODEL), 0, 1)
    s5_shape = lambda a: a.reshape(depth, -1, S5_GROUPS, S5_STATE)
    p_ssd, p_re, p_im, p_gdn, p_hg = acc_p
    conv_tails = jnp.stack(conv_tails)
    p_ca, p_cc = conv_tails[..., 0:SSD_XBC], conv_tails[..., SSD_XBC:SSD_XBC + GDN_QKV]
    seq_first = lambda a: jnp.moveaxis(a, -1, 1)
    s_ssd, s_ca, s_re, s_im, s_gdn, s_cc, s_hg = [seq_first(a) for a in acc_s]
    return (y_prompt, y_sample, p_ssd, p_ca, s5_shape(p_re[:, :, 0]), s5_shape(p_im[:, :, 0]), p_gdn, p_cc, p_hg,
            s_ssd, s_ca, s5_shape(s_re), s5_shape(s_im), s_gdn, s_cc, s_hg)
```

```python
import functools
import math

import numpy as np
import jax
import jax.numpy as jnp
from jax import lax
from jax.experimental import pallas as pl
from jax.experimental.pallas import tpu as pltpu

F32 = jnp.float32
BF16 = jnp.bfloat16

D_MODEL = 1024
DEPTH = 4
N_META = 16
D_BRANCH = 256
HEADS = 4
PAIRS = HEADS // 2
HEAD_DIM = 64
SSD_XBC = 512
S5_GROUPS = 16
S5_STATE = 64
S5_LANES = S5_GROUPS * S5_STATE
GDN_QKV = 768
CONV_W = 4
DN_ALPHA = (2 * DEPTH) ** 0.25
LN_EPS = 1e-5
RMS_EPS = 1e-6
L2_EPS = 1e-6

CHUNK = 64
SUBLANES = 8
LANES = 128
PAD_FRONT = CHUNK - N_META
WA, WB, WC, WD = 768, 512, 1152, 1024
NROWS = 16
ROW_W = 768
VMEM_LIMIT = 56 * 1024 * 1024


def _dot(a, b):
    return jnp.dot(a.astype(BF16), b.astype(BF16), preferred_element_type=F32)


def _dot_nt(a, b):
    return lax.dot_general(a.astype(BF16), b.astype(BF16), (((1,), (1,)), ((), ())),
                           preferred_element_type=F32)


def _dot_tn(a, b):
    return lax.dot_general(a.astype(BF16), b.astype(BF16), (((0,), (0,)), ((), ())),
                           preferred_element_type=F32)


def _split(x, pieces):
    out = []
    r = x
    for i in range(pieces):
        xi = r.astype(BF16)
        out.append(xi)
        if i + 1 < pieces:
            r = r - xi.astype(F32)
    return out


def _dot01_l(w01, x, pieces=3):
    n = x.shape[-1]
    r = jnp.dot(w01, jnp.concatenate(_split(x, pieces), axis=-1), preferred_element_type=F32)
    return sum(r[:, i * n:(i + 1) * n] for i in range(pieces))


def _dot01_r(x, w01, pieces=3):
    m = x.shape[0]
    r = jnp.dot(jnp.concatenate(_split(x, pieces), axis=0), w01, preferred_element_type=F32)
    return sum(r[i * m:(i + 1) * m] for i in range(pieces))


def _sigmoid(x):
    return 1.0 / (1.0 + jnp.exp(-x))


def _silu(x):
    return x * _sigmoid(x)


def _log1p_exp_neg(a):
    e = jnp.exp(-a)
    u = 1.0 + e
    return jnp.log(u) - ((u - 1.0) - e) / u


def _softplus(x):
    return jnp.maximum(x, 0.0) + _log1p_exp_neg(jnp.abs(x))


def _gelu_tanh(x):
    c = math.sqrt(2.0 / math.pi)
    return 0.5 * x * (1.0 + jnp.tanh(c * (x + 0.044715 * (x * x * x))))


def _run_interleaved(chains):
    chains = list(chains)
    while chains:
        alive = []
        for ch in chains:
            try:
                next(ch)
                alive.append(ch)
            except StopIteration:
                pass
        chains = alive


def _layernorm_rows(r, g, b):
    mu = jnp.mean(r, axis=-1, keepdims=True)
    c = r - mu
    var = jnp.mean(c * c, axis=-1, keepdims=True)
    return c * lax.rsqrt(var + LN_EPS) * g + b


def _ln_in_kernel(x_ref, g_ref, b_ref, o_ref):
    o_ref[...] = _layernorm_rows(x_ref[...], g_ref[...], b_ref[...])


def _ln_in_prompt_kernel(x_ref, meta_ref, g_ref, b_ref, o_ref):
    o_ref[0:PAD_FRONT, :] = jnp.zeros((PAD_FRONT, D_MODEL), F32)
    o_ref[PAD_FRONT:CHUNK, :] = _layernorm_rows(meta_ref[...], g_ref[...], b_ref[...])
    seq = x_ref.shape[0]
    step = math.gcd(seq, 512)
    for r0 in range(0, seq, step):
        o_ref[CHUNK + r0:CHUNK + r0 + step, :] = _layernorm_rows(x_ref[r0:r0 + step, :], g_ref[...], b_ref[...])


def _inproj_conv_kernel(x_ref, w_ref, rows_ref, cwa_ref, cwc_ref, oa_ref, ob_ref, oc_ref, od_ref, tail_ref,
                        xx_ref, *, tiles_per_seq):
    tm = x_ref.shape[0]
    t = pl.program_id(0) % tiles_per_seq
    x = x_ref[...].astype(BF16)

    @pl.when(t == 0)
    def _():
        xx_ref[0:SUBLANES, :] = jnp.zeros((SUBLANES, SSD_XBC + GDN_QKV), F32)

    first_valid = jnp.where(t == 0, PAD_FRONT, 0)

    def conv(raw, c0, width, cw_ref, bias_row, o_ref):
        r = lax.broadcasted_iota(jnp.int32, (tm, width), 0)
        raw = jnp.where(r >= first_valid, raw, 0.0)
        xx_ref[SUBLANES:SUBLANES + tm, c0:c0 + width] = raw
        acc = rows_ref[bias_row:bias_row + 1, 0:width]
        for w in range(CONV_W):
            lo = SUBLANES - (CONV_W - 1) + w
            acc = acc + cw_ref[w:w + 1, :] * xx_ref[lo:lo + tm, c0:c0 + width]
        o_ref[:, 256:256 + width] = _silu(acc)
        tail = raw[tm - SUBLANES:tm, :]
        tail_ref[:, c0:c0 + width] = tail
        xx_ref[0:SUBLANES, c0:c0 + width] = tail

    ua = jnp.dot(x, w_ref[:, 0:WA], preferred_element_type=F32)
    oa_ref[:, 0:256] = ua[:, 0:256]
    uc = jnp.dot(x, w_ref[:, WA + WB:WA + WB + WC], preferred_element_type=F32)
    conv(ua[:, 256:WA], 0, SSD_XBC, cwa_ref, 0, oa_ref)
    oc_ref[:, 0:256] = uc[:, 0:256]
    oc_ref[:, 1024:WC] = uc[:, 1024:WC]
    ob_ref[...] = jnp.dot(x, w_ref[:, WA:WA + WB], preferred_element_type=F32)
    conv(uc[:, 256:1024], SSD_XBC, GDN_QKV, cwc_ref, 7, oc_ref)
    od_ref[...] = jnp.dot(x, w_ref[:, WA + WB + WC:WA + WB + WC + WD], preferred_element_type=F32)


def _inproj_conv_call(hp, l, wts, n_seq, tm):
    n_rows = hp.shape[0]
    tiles_per_seq = n_rows // n_seq // tm
    consts = [wts['w_in'], wts['rows'], wts['cw_a'], wts['cw_c']]
    in_specs = [pl.BlockSpec((tm, D_MODEL), lambda i: (i, 0))]
    in_specs += [pl.BlockSpec((None,) + a.shape[1:], lambda i, nd=a.ndim - 1: (l,) + (0,) * nd) for a in consts]
    widths = (WA, WB, WC, WD)
    out_specs = [pl.BlockSpec((tm, w), lambda i: (i, 0)) for w in widths]
    out_specs += [pl.BlockSpec((None, SUBLANES, SSD_XBC + GDN_QKV), lambda i: (i // tiles_per_seq, 0, 0))]
    out_shape = [jax.ShapeDtypeStruct((n_rows, w), F32) for w in widths]
    out_shape += [jax.ShapeDtypeStruct((n_seq, SUBLANES, SSD_XBC + GDN_QKV), F32)]
    *us, tails = pl.pallas_call(
        functools.partial(_inproj_conv_kernel, tiles_per_seq=tiles_per_seq),
        grid=(n_rows // tm,),
        in_specs=in_specs,
        out_specs=out_specs,
        out_shape=out_shape,
        scratch_shapes=[pltpu.VMEM((SUBLANES + tm, SSD_XBC + GDN_QKV), F32)],
        compiler_params=pltpu.CompilerParams(dimension_semantics=("arbitrary",),
                                             vmem_limit_bytes=VMEM_LIMIT),
        name="inproj_conv_prompt",
    )(hp, *consts)
    return us, tails


def _outproj_kernel(mix_ref, x_ref, w_ref, g_ref, b_ref, o_ref):
    tm = mix_ref.shape[0]
    n_piece = 4 if tm % (4 * 2 * SUBLANES) == 0 else 1
    step = tm // n_piece
    outs = [jnp.dot(mix_ref[i * step:(i + 1) * step, :], w_ref[...], preferred_element_type=F32)
            for i in range(n_piece)]
    for i, out in enumerate(outs):
        rs = slice(i * step, (i + 1) * step)
        o_ref[rs, :] = _layernorm_rows(DN_ALPHA * x_ref[rs, :] + out, g_ref[...], b_ref[...])


def _row_tiled_call(kernel, n_rows, tm, row_inputs, const_inputs, out_widths, out_dtypes, name, layer=None):
    in_specs = [pl.BlockSpec((tm, a.shape[1]), lambda i: (i, 0)) for a in row_inputs]
    if layer is None:
        in_specs += [pl.BlockSpec(a.shape, lambda i, nd=a.ndim: (0,) * nd) for a in const_inputs]
    else:
        in_specs += [pl.BlockSpec((None,) + a.shape[1:], lambda i, nd=a.ndim - 1: (layer,) + (0,) * nd)
                     for a in const_inputs]
    out_specs = [pl.BlockSpec((tm, w), lambda i: (i, 0)) for w in out_widths]
    out_shape = [jax.ShapeDtypeStruct((n_rows, w), dt) for w, dt in zip(out_widths, out_dtypes)]
    single = len(out_widths) == 1
    return pl.pallas_call(
        kernel,
        grid=(n_rows // tm,),
        in_specs=in_specs,
        out_specs=out_specs[0] if single else out_specs,
        out_shape=out_shape[0] if single else out_shape,
        compiler_params=pltpu.CompilerParams(dimension_semantics=("arbitrary",),
                                             vmem_limit_bytes=VMEM_LIMIT),
        name=name,
    )(*row_inputs, *const_inputs)


N_MIXER_CONST = 16
N_STATES = 7
N_PROMPT_STATES = 5
N_LEVELS = 6
MXU_LEVELS = 2
SEQS_PER_STEP = 2


def _mixer_kernel(*refs, rows, nq):
    n_in = 4 + N_MIXER_CONST + N_PROMPT_STATES
    consts = refs[4:4 + N_MIXER_CONST]
    j = pl.program_id(1)
    stages = [[] for _ in range(nq + 2)]
    finishers = []
    for s in range(nq):
        seq_refs = [r.at[s] for r in refs[:4] + refs[n_in:]]
        prologue, state_free, carried, finish = _mixer_sequence(seq_refs[:4], consts, seq_refs[4:], j, rows)
        stages[s] += prologue
        stages[s + 1] += state_free
        stages[s + 2] += carried
        finishers.append(finish)
    for chains in stages:
        _run_interleaved(chains)
    for finish in finishers:
        finish()


def _mixer_sequence(u_refs, consts, out_refs, j, rows):
    nch = rows // CHUNK
    nlev = N_LEVELS
    (ua_ref, ub_ref, uc_ref, ud_ref) = u_refs
    (rows_ref, wbre_ref, wbim_ref, cre_ref, cim_ref, glu_ref, tab_ref, cmat_ref, cpair_ref, bdm_ref, lr_ref,
     ea_ref, eb_ref, blk_ref, sel_ref, ones_ref) = consts
    (mix_ref, o_ssd, o_s5re, o_s5im, o_gdn, o_hg, hre_ref, him_ref, ssd_sc, gdn_sc, hg_sc) = out_refs

    @pl.when(j == 0)
    def _():
        ssd_sc[...] = jnp.zeros_like(ssd_sc)
        gdn_sc[...] = jnp.zeros_like(gdn_sc)
        hg_sc[...] = jnp.zeros_like(hg_sc)
        o_s5re[...] = jnp.zeros_like(o_s5re)
        o_s5im[...] = jnp.zeros_like(o_s5im)

    first_valid = jnp.where(j == 0, PAD_FRONT, 0)

    def valid_rows(r):
        return r >= first_valid

    def row(i, w):
        return rows_ref[i:i + 1, 0:w]

    act_a = lambda rs, lo, hi: ua_ref[rs, 256 + lo:256 + hi]
    act_c = lambda rs, lo, hi: uc_ref[rs, 256 + lo:256 + hi]

    u_b = jnp.where(valid_rows(lax.broadcasted_iota(jnp.int32, (rows, D_BRANCH), 0)), ub_ref[:, 256:512], 0.0)
    hre_ref[...] = _dot(u_b, wbre_ref[...])
    him_ref[...] = _dot(u_b, wbim_ref[...])

    def s5_group(g, carry):
        r0 = pl.multiple_of(g * SUBLANES, SUBLANES)
        xr = hre_ref[pl.ds(r0, SUBLANES), :]
        xi = him_ref[pl.ds(r0, SUBLANES), :]
        for k, d in enumerate((1, 2, 4)):
            tr = tab_ref[k, :, 0:S5_LANES]
            ti = tab_ref[k, :, S5_LANES:2 * S5_LANES]
            sr = pltpu.roll(xr, d, axis=0)
            si = pltpu.roll(xi, d, axis=0)
            xr, xi = xr + tr * sr - ti * si, xi + tr * si + ti * sr
        cr, ci = carry
        pr = tab_ref[3, :, 0:S5_LANES]
        pi = tab_ref[3, :, S5_LANES:2 * S5_LANES]
        xr, xi = xr + pr * cr - pi * ci, xi + pr * ci + pi * cr
        hre_ref[pl.ds(r0, SUBLANES), :] = xr
        him_ref[pl.ds(r0, SUBLANES), :] = xi
        return (jnp.broadcast_to(xr[SUBLANES - 1:SUBLANES, :], (SUBLANES, S5_LANES)),
                jnp.broadcast_to(xi[SUBLANES - 1:SUBLANES, :], (SUBLANES, S5_LANES)))

    cr, ci = lax.fori_loop(0, rows // SUBLANES, s5_group, (o_s5re[...], o_s5im[...]))
    o_s5re[...] = cr
    o_s5im[...] = ci

    y5 = (_dot(hre_ref[...], cre_ref[...]) - _dot(him_ref[...], cim_ref[...])
          + row(5, D_BRANCH) * ub_ref[:, 256:512])
    y5 = _gelu_tanh(y5)
    y5 = y5 * _sigmoid(_dot(y5, glu_ref[...]) + row(6, D_BRANCH))
    mix_ref[:, 256:512] = (y5 * _silu(ub_ref[:, 0:256])).astype(BF16)

    tri_b = cmat_ref[0]

    def head_sum(x):
        return jnp.dot(x.astype(BF16), blk_ref[...], preferred_element_type=F32)

    def rms_finish(y, ss, g_row, z):
        return (y * lax.rsqrt(ss * (1.0 / HEAD_DIM) + RMS_EPS) * g_row * _silu(z)).astype(BF16)

    def hgrn_inputs(rs, valid_w):
        f_d = ud_ref[rs, 512:768]
        lsig = -_softplus(-f_d)
        t1 = row(11, D_BRANCH)
        t2 = row(12, D_BRANCH) + lsig
        logf = jnp.maximum(t1, t2) + _log1p_exp_neg(jnp.abs(t1 - t2))
        logf = jnp.where(valid_w, logf, 0.0)
        kd = row(13, D_BRANCH) * _sigmoid(-f_d)
        qd = _silu(ud_ref[rs, 256:512])
        vd = jnp.where(valid_w, ud_ref[rs, 768:1024], 0.0)
        xmm = _dot01_l(cmat_ref[1:2 + MXU_LEVELS].reshape((1 + MXU_LEVELS) * CHUNK, CHUNK), logf)
        gcd = xmm[0:CHUNK, :]
        xlev = []
        for lev in range(nlev - MXU_LEVELS):
            half = CHUNK >> (lev + 1)
            g3 = gcd.reshape(CHUNK // (2 * half), 2 * half, D_BRANCH)
            bound = jnp.broadcast_to(g3[:, half - 1:half, :], g3.shape).reshape(CHUNK, D_BRANCH)
            xlev.append(-jnp.abs(gcd - bound))
        xlev += [xmm[(1 + i) * CHUNK:(2 + i) * CHUNK, :] for i in range(MXU_LEVELS)]
        return logf, kd, qd, vd, gcd, xlev

    def prompt_block():
        lr0, lr1 = lr_ref[0], lr_ref[1]
        pairs = [slice(p * LANES, (p + 1) * LANES) for p in range(PAIRS)]
        data = [dict() for _ in range(nch)]

        def bd(x):
            xb = x.astype(BF16)
            return jnp.concatenate([xb * lr0, xb * lr1], axis=0)

        def mm_pair(a, b):
            return _dot(a, bd(b))

        def prologue(c):
            d = data[c]
            rs = slice(c * CHUNK, (c + 1) * CHUNK)
            d['rs'] = rs
            rr = lax.broadcasted_iota(jnp.int32, (CHUNK, LANES), 0) + c * CHUNK
            lane = lax.broadcasted_iota(jnp.int32, (CHUNK, LANES), 1)
            valid = valid_rows(rr)
            t = uc_ref[rs, 1024:1152] + row(1, LANES)
            vals = jnp.where((lane >= HEADS) & (lane < 2 * HEADS), _sigmoid(t), _softplus(t) * row(2, LANES))
            vals = jnp.where(valid, vals, 0.0)
            cs = _dot01_l(tri_b, vals)
            ex_v = _dot01_r(vals, ea_ref[...], pieces=2)
            q = act_c(rs, 0, 256)
            k = act_c(rs, 256, 512)
            qss = head_sum(q * q)
            kss = head_sum(k * k)
            valid_w = jnp.concatenate([valid, valid], axis=-1)
            logf, kd, qd, vd, gcd, xlev = hgrn_inputs(rs, valid_w)
            d.update(kd=kd, qd=qd, vd=vd)
            yield
            cs_pieces = _split(cs, 3)
            ex_c_all = jnp.dot(jnp.concatenate(cs_pieces, axis=0), eb_ref[...], preferred_element_type=F32)
            ex_c = sum(ex_c_all[i * CHUNK:(i + 1) * CHUNK] for i in range(3))
            crow = sum(lax.dot_general(sel_ref[...], piece, (((1,), (1,)), ((), ())), preferred_element_type=F32)
                       for piece in cs_pieces)
            d['diag'] = head_sum(qd * kd)
            yield
            dt_full, beta_full = ex_v[:, 0:256], ex_v[:, 256:512]
            gc, acum = ex_c[:, 0:256], ex_c[:, 256:512]
            d['prow'] = lambda r: jnp.concatenate([crow[r:r + 1, :], crow[r + 1:r + 2, :]], axis=1)
            xs = act_a(rs, 0, 256)
            alast = acum[CHUNK - 1:CHUNK, :]
            xdt = xs * dt_full
            d.update(acum=acum, alast=alast, xdt=xdt, xdt_end=xdt * jnp.exp(alast - acum), eacum=jnp.exp(acum))
            v = act_c(rs, 512, 768)
            q = q * lax.rsqrt(qss + L2_EPS) * (HEAD_DIM ** -0.5)
            k = k * lax.rsqrt(kss + L2_EPS)
            glast = gc[CHUNK - 1:CHUNK, :]
            egc = jnp.exp(gc)
            d.update(q=q, k=k, gc=gc, glast=glast, beta=beta_full, vb=v * beta_full, kbe=k * beta_full * egc,
                     qe=q * egc, kend=k * jnp.exp(glast - gc))
            gld = gcd[CHUNK - 1:CHUNK, :]
            g1, g2, g3 = [t.astype(F32) for t in _split(gld, 3)]
            rid = lax.broadcasted_iota(jnp.int32, (2 * SUBLANES, D_BRANCH), 0)
            d.update(qed=qd * jnp.exp(gcd), kend_d=kd * jnp.exp(gld - gcd), xlev=xlev,
                     g16=jnp.where(rid == 0, g1, jnp.where(rid == 1, g2, jnp.where(rid == 2, g3, 0.0))))

        def ssd_a(c, p):
            d, ps = data[c], pairs[p]
            g0 = 256 + p * HEAD_DIM
            bg = act_a(d['rs'], g0, g0 + HEAD_DIM)
            cg = act_a(d['rs'], g0 + LANES, g0 + LANES + HEAD_DIM)
            cb2 = _dot_nt(cg, jnp.concatenate([bg, bg], axis=0))
            lm = jnp.exp(jnp.minimum(d['acum'][:, ps] - d['prow'](HEADS + 2 * p), 0.0)) * cpair_ref[0]
            d['cbl', p] = cb2 * lm
            d['bg', p], d['cg', p] = bg, cg
            yield

        def gdn_a(c, p):
            d, ps = data[c], pairs[p]
            k = d['k'][:, ps]
            dec = jnp.exp(jnp.minimum(d['gc'][:, ps] - d['prow'](2 * p), 0.0)) * cpair_ref[0]
            kq = _dot_nt(jnp.concatenate([k, d['q'][:, ps]], axis=0), bd(k))
            yield
            m = kq[0:CHUNK] * dec * d['beta'][:, ps] * cpair_ref[1]
            d['aq', p] = kq[CHUNK:2 * CHUNK] * dec
            acc = cpair_ref[2] - m
            mp = mm_pair(m, m)
            yield
            for _ in range(nlev - 2):
                acc_add = mm_pair(acc, mp)
                mp = mm_pair(mp, mp)
                yield
                acc = acc + acc_add
            acc = acc + mm_pair(acc, mp)
            yield
            d['uw', p] = _dot(acc, jnp.concatenate([bd(d['vb'][:, ps]), bd(d['kbe'][:, ps])], axis=1))
            yield

        def hgrn_a(c, p):
            d, ps = data[c], pairs[p]
            qd, kd = d['qd'][:, ps], d['kd'][:, ps]
            amat = jnp.zeros((CHUNK, LANES), F32)
            for lev in range(nlev):
                z = jnp.exp(d['xlev'][lev][:, ps])
                amat = amat + _dot_nt(qd * z, bd(kd * z)) * cpair_ref[3 + lev]
                if lev % 2 == 1:
                    yield
            d['amat', p] = amat
            d['dcol', p] = jnp.exp(_dot_tn(d['g16'][:, ps], ones_ref[...]))
            yield

        def ssd_b():
            for c in range(nch):
                d = data[c]
                ya = []
                for p, ps in enumerate(pairs):
                    s_pk = ssd_sc[p]
                    ya.append(mm_pair(d['cbl', p], d['xdt'][:, ps]) + _dot(d['cg', p], s_pk) * d['eacum'][:, ps])
                    ssd_sc[p] = s_pk * jnp.exp(d['alast'][:, ps]) + _dot_tn(d['bg', p], d['xdt_end'][:, ps])
                yield
                ya = jnp.concatenate(ya, axis=-1) + row(3, D_BRANCH) * act_a(d['rs'], 0, 256)
                ss = head_sum(ya * ya)
                yield
                mix_ref[d['rs'], 0:256] = rms_finish(ya, ss, row(4, D_BRANCH), ua_ref[d['rs'], 0:256])

        def gdn_b():
            bdm = bdm_ref[...]
            for c in range(nch):
                d = data[c]
                ws, s_old = [], []
                for p, ps in enumerate(pairs):
                    s_bd = gdn_sc[p]
                    s_old.append(s_bd)
                    ws.append(_dot(jnp.concatenate([d['uw', p][:, LANES:2 * LANES], d['qe'][:, ps]], axis=0), s_bd))
                yield
                yc = []
                for p, ps in enumerate(pairs):
                    v_new = d['uw', p][:, 0:LANES] - ws[p][0:CHUNK]
                    yc.append(ws[p][CHUNK:2 * CHUNK] + mm_pair(d['aq', p], v_new))
                    gdn_sc[p] = s_old[p] * jnp.exp(d['glast'][:, ps]) + _dot_tn(d['kend'][:, ps], v_new) * bdm
                yield
                yc = jnp.concatenate(yc, axis=-1)
                ss = head_sum(yc * yc)
                yield
                mix_ref[d['rs'], 512:768] = rms_finish(yc, ss, row(10, D_BRANCH), uc_ref[d['rs'], 0:256])

        def hgrn_b():
            bdm = bdm_ref[...]
            for c in range(nch):
                d = data[c]
                yd = []
                for p, ps in enumerate(pairs):
                    s_bd = hg_sc[p]
                    vd = d['vd'][:, ps]
                    yd.append(mm_pair(d['amat', p], vd) + d['diag'][:, ps] * vd + _dot(d['qed'][:, ps], s_bd))
                    hg_sc[p] = s_bd * d['dcol', p] + _dot_tn(d['kend_d'][:, ps], vd) * bdm
                yield
                yd = jnp.concatenate(yd, axis=-1)
                ss = head_sum(yd * yd)
                yield
                mix_ref[d['rs'], 768:1024] = rms_finish(yd, ss, row(14, D_BRANCH), ud_ref[d['rs'], 0:256])

        return ([prologue(c) for c in range(nch)],
                [f(c, p) for c in range(nch) for p in range(PAIRS) for f in (gdn_a, hgrn_a, ssd_a)],
                [gdn_b(), hgrn_b(), ssd_b()])

    def finish():
        @pl.when(j == pl.num_programs(1) - 1)
        def _():
            for p in range(PAIRS):
                for e in range(2):
                    es = slice(e * HEAD_DIM, (e + 1) * HEAD_DIM)
                    o_ssd[2 * p + e] = ssd_sc[p, :, es]
                    o_gdn[2 * p + e] = gdn_sc[p, es, es]
                    o_hg[2 * p + e] = hg_sc[p, es, es]

    return (*prompt_block(), finish)


def _mixer_constants():
    i = np.arange(CHUNK)[:, None]
    jn = np.arange(CHUNK)[None, :]
    tri = (jn <= i).astype(np.float32)
    strict = (jn < i).astype(np.float32)
    eye = np.eye(CHUNK, dtype=np.float32)
    wlev, mlev = [], []
    for lev in range(N_LEVELS):
        b = CHUNK >> (lev + 1)
        blk_i, pos_i = i // (2 * b), i % (2 * b)
        mid = blk_i * 2 * b + b
        upper = pos_i >= b
        w = np.where(upper, (jn >= mid) & (jn <= i), (jn > i) & (jn < mid)).astype(np.float32)
        msk = (((jn // (2 * b)) == blk_i) & upper & ((jn % (2 * b)) < b)).astype(np.float32)
        wlev.append(w)
        mlev.append(msk)
    cmat = np.stack([tri, tri] + wlev[N_LEVELS - MXU_LEVELS:])
    cmask = np.stack([tri, strict, eye] + mlev)
    cpair = np.concatenate([cmask, cmask], axis=-1)
    bdm = np.kron(np.eye(2, dtype=np.float32), np.ones((HEAD_DIM, HEAD_DIM), np.float32))
    lane = np.arange(LANES)[None, :]
    lr = np.stack([np.broadcast_to(lane < HEAD_DIM, (CHUNK, LANES)),
                   np.broadcast_to(lane >= HEAD_DIM, (CHUNK, LANES))]).astype(np.float32)
    e_a = np.zeros((LANES, 512), np.float32)
    e_b = np.zeros((LANES, 512), np.float32)
    blk = np.zeros((256, 256), np.float32)
    sel = np.zeros((2 * SUBLANES, LANES), np.float32)
    for h in range(HEADS):
        hs = slice(h * HEAD_DIM, (h + 1) * HEAD_DIM)
        e_a[h, hs] = 1.0
        e_a[HEADS + h, 256 + h * HEAD_DIM:256 + (h + 1) * HEAD_DIM] = 1.0
        e_b[2 * HEADS + h, hs] = 1.0
        e_b[3 * HEADS + h, 256 + h * HEAD_DIM:256 + (h + 1) * HEAD_DIM] = 1.0
        blk[hs, hs] = 1.0
        sel[h, 2 * HEADS + h] = 1.0
        sel[HEADS + h, 3 * HEADS + h] = 1.0
    ones = np.ones((2 * SUBLANES, LANES), np.float32)
    return (jnp.asarray(cmat, BF16), jnp.asarray(cpair, F32), jnp.asarray(bdm, F32),
            jnp.asarray(lr, BF16), jnp.asarray(e_a, BF16), jnp.asarray(e_b, BF16), jnp.asarray(blk, BF16),
            jnp.asarray(sel, BF16), jnp.asarray(ones, BF16))


def _layer_spec(a, l):
    nd = a.ndim - 1
    return pl.BlockSpec((None,) + a.shape[1:], lambda b, j: (l,) + (0,) * nd)


def _mixer_call(us, l, wts, consts, acc, *, n_seq, rows):
    n_rows = us[0].shape[0]
    t_pad = n_rows // n_seq
    nblk = t_pad // rows
    nq = SEQS_PER_STEP if n_seq % SEQS_PER_STEP == 0 else 1
    us = [u.reshape(n_seq, t_pad, u.shape[1]) for u in us]
    rmap = lambda b, j: (b, j, 0)

    def state_spec(tail):
        return pl.BlockSpec((None, nq) + tail, lambda b, j: (l, b) + (0,) * len(tail))

    layer_consts = [wts[k] for k in ('rows', 'wb_re', 'wb_im', 'c_re', 'c_im', 'glu_w', 's5tab')]
    assert len(layer_consts) + len(consts) == N_MIXER_CONST
    in_specs = [pl.BlockSpec((nq, rows, w), rmap) for w in (WA, WB, WC, WD)]
    in_specs += [_layer_spec(a, l) for a in layer_consts]
    in_specs += [pl.BlockSpec(a.shape, lambda b, j, nd=a.ndim: (0,) * nd) for a in consts]
    inputs = us + layer_consts + list(consts)
    mat = (HEADS, HEAD_DIM, HEAD_DIM)
    s5_tail = (SUBLANES, S5_LANES)
    tails = [mat, s5_tail, s5_tail, mat, mat]
    scratch = [pltpu.VMEM((nq, rows, S5_LANES), F32),
               pltpu.VMEM((nq, rows, S5_LANES), F32),
               pltpu.VMEM((nq, PAIRS, HEAD_DIM, LANES), F32),
               pltpu.VMEM((nq, PAIRS, LANES, LANES), F32),
               pltpu.VMEM((nq, PAIRS, LANES, LANES), F32)]
    assert len(acc) == len(tails) == N_PROMPT_STATES
    n_before_acc = len(inputs)
    inputs += list(acc)
    in_specs += [pl.BlockSpec(memory_space=pl.ANY)] * len(acc)
    out_shape = [jax.ShapeDtypeStruct((n_seq, t_pad, D_MODEL), BF16)]
    out_shape += [jax.ShapeDtypeStruct(a.shape, a.dtype) for a in acc]
    out_specs = [pl.BlockSpec((nq, rows, D_MODEL), rmap)] + [state_spec(t) for t in tails]
    aliases = {n_before_acc + i: 1 + i for i in range(len(acc))}
    return pl.pallas_call(
        functools.partial(_mixer_kernel, rows=rows, nq=nq),
        grid=(n_seq // nq, nblk),
        in_specs=in_specs,
        out_specs=out_specs,
        out_shape=out_shape,
        scratch_shapes=scratch,
        input_output_aliases=aliases,
        compiler_params=pltpu.CompilerParams(dimension_semantics=("arbitrary", "arbitrary"),
                                             vmem_limit_bytes=VMEM_LIMIT),
        name="mixer_prompt",
    )(*inputs)


SAMPLE_COL_WIDTHS = (SSD_XBC, 4 * HEADS, 4 * HEADS, D_BRANCH, D_BRANCH, D_BRANCH, D_BRANCH, GDN_QKV, D_BRANCH,
                     D_BRANCH, D_BRANCH, D_BRANCH, D_BRANCH)
(C_CONVB_A, C_SBIAS, C_SSCALE, C_SSD_D, C_SSD_G, C_S5_D, C_GLU_B, C_CONVB_C, C_GDN_G, C_LOGLB, C_LOG1M, C_ONEM,
 C_HG_G) = [int(v) for v in np.cumsum((0,) + SAMPLE_COL_WIDTHS[:-1])]
U_ZA, U_XBC, U_ZB, U_UB, U_ZC, U_QKV, U_SMALL, U_ZD, U_QD, U_FD, U_ID = (
    0, 256, WA, WA + 256, WA + WB, WA + WB + 256, WA + WB + 1024, WA + WB + WC, WA + WB + WC + 256,
    WA + WB + WC + 512, WA + WB + WC + 768)
V_HALF = HEAD_DIM // 2


def _sample_inproj_kernel(h_ref, wt_ref, o_ref):
    o_ref[...] = lax.dot_general(wt_ref[...], h_ref[...].astype(BF16), (((1,), (1,)), ((), ())),
                                 preferred_element_type=F32)


def _sample_outproj_kernel(mixt_ref, x_ref, w_ref, g_ref, b_ref, o_ref):
    out = lax.dot_general(mixt_ref[...], w_ref[...], (((0,), (0,)), ((), ())), preferred_element_type=F32)
    o_ref[...] = _layernorm_rows(DN_ALPHA * x_ref[...] + out, g_ref[...], b_ref[...])


def _sample_mixer_kernel(ut_ref, cols_ref, cwa_ref, cwc_ref, lb_ref, wbre_ref, wbim_ref, cre_ref, cim_ref, glu_ref,
                         i_ssd, i_ca, i_s5re, i_s5im, i_gdn, i_cc, i_hg, *rest, n_tok, n_seq):
    (mix_ref, o_ssd, o_ca, o_s5re, o_s5im, o_gdn, o_cc, o_hg) = rest[N_STATES:N_STATES + 1 + N_STATES]
    (acta_ref, actc_ref, small_ref, hre_ref, him_ref, gq_ref, gk_ref, hq_ref, hk_ref, hf_ref,
     ya_ref, yc_ref, yd_ref) = rest[N_STATES + 1 + N_STATES:]
    h = pl.program_id(0)
    vh = pl.program_id(1)
    toks = [slice(t * n_seq, (t + 1) * n_seq) for t in range(n_tok)]
    hrow = pl.multiple_of(h * HEAD_DIM, HEAD_DIM)

    def col(c0, n, off=0):
        return cols_ref[pl.ds(c0 + off, n), :]

    @pl.when((h == 0) & (vh == 0))
    def _():
        for u0, width, i_c, o_c, cw_ref, act_ref, cb in ((U_XBC, SSD_XBC, i_ca, o_ca, cwa_ref, acta_ref, C_CONVB_A),
                                                        (U_QKV, GDN_QKV, i_cc, o_cc, cwc_ref, actc_ref, C_CONVB_C)):
            xx = [i_c[i] for i in range(CONV_W - 1)] + [ut_ref[u0:u0 + width, ts] for ts in toks]
            for t in range(n_tok):
                acc = cols_ref[cb:cb + width, :]
                for w in range(CONV_W):
                    acc = acc + cw_ref[w] * xx[t + w]
                act_ref[t] = _silu(acc)
            for i in range(CONV_W - 1):
                o_c[i] = xx[n_tok + i]
        rid = lax.broadcasted_iota(jnp.int32, (4 * HEADS, n_seq), 0)
        for t, ts in enumerate(toks):
            v = ut_ref[U_SMALL:U_SMALL + 4 * HEADS, ts] + cols_ref[C_SBIAS:C_SBIAS + 4 * HEADS, :]
            small_ref[t] = jnp.where((rid >= HEADS) & (rid < 2 * HEADS), _sigmoid(v),
                                     _softplus(v) * cols_ref[C_SSCALE:C_SSCALE + 4 * HEADS, :])
        u_b = ut_ref[U_UB:U_UB + D_BRANCH, :]
        bu_re = _dot(wbre_ref[...], u_b)
        bu_im = _dot(wbim_ref[...], u_b)
        sr, si = i_s5re[...], i_s5im[...]
        lr, li = lb_ref[0], lb_ref[1]
        for t, ts in enumerate(toks):
            sr, si = lr * sr - li * si + bu_re[:, ts], lr * si + li * sr + bu_im[:, ts]
            hre_ref[:, ts] = sr
            him_ref[:, ts] = si
        o_s5re[...] = sr
        o_s5im[...] = si
        y5 = _dot(cre_ref[...], hre_ref[...]) - _dot(cim_ref[...], him_ref[...])
        for ts in toks:
            y = _gelu_tanh(y5[:, ts] + cols_ref[C_S5_D:C_S5_D + D_BRANCH, :] * u_b[:, ts])
            g = _dot(glu_ref[...], y) + cols_ref[C_GLU_B:C_GLU_B + D_BRANCH, :]
            mix_ref[D_BRANCH:2 * D_BRANCH, ts] = (y * _sigmoid(g) * _silu(ut_ref[U_ZB:U_ZB + D_BRANCH, ts])).astype(BF16)

    vrow = pl.multiple_of(vh * V_HALF, V_HALF)
    vsel = pl.ds(hrow + vrow, V_HALF)

    def scal(t, r):
        return small_ref[t, pl.ds(r * HEADS + h, 1), :]

    grp = (h // 2) * HEAD_DIM
    xdt = [acta_ref[t, vsel, :] * scal(t, 0) for t in range(n_tok)]
    ea = [jnp.exp(scal(t, 3)) for t in range(n_tok)]

    def ssd_row(n, ys):
        s = i_ssd[n]
        ys = list(ys)
        for t in range(n_tok):
            s = s * ea[t] + acta_ref[t, pl.ds(256 + grp + n, 1), :] * xdt[t]
            ys[t] = ys[t] + acta_ref[t, pl.ds(384 + grp + n, 1), :] * s
        o_ssd[n] = s
        return tuple(ys)

    zero = jnp.zeros((V_HALF, n_seq), F32)
    ys = lax.fori_loop(0, HEAD_DIM, ssd_row, (zero,) * n_tok, unroll=2)
    for t in range(n_tok):
        ya_ref[t, pl.ds(vrow, V_HALF), :] = ys[t] + col(C_SSD_D, V_HALF, hrow + vrow) * acta_ref[t, vsel, :]

    @pl.when(vh == 0)
    def _():
        for t in range(n_tok):
            q = actc_ref[t, pl.ds(hrow, HEAD_DIM), :]
            k = actc_ref[t, pl.ds(256 + hrow, HEAD_DIM), :]
            gq_ref[t] = q * lax.rsqrt(jnp.sum(q * q, axis=0, keepdims=True) + L2_EPS) * (HEAD_DIM ** -0.5)
            gk_ref[t] = k * lax.rsqrt(jnp.sum(k * k, axis=0, keepdims=True) + L2_EPS)

    for t in range(n_tok):
        alpha = jnp.exp(scal(t, 2))
        src = i_gdn if t == 0 else o_gdn

        def decay_row(kk, pred, alpha=alpha, src=src, t=t):
            s = src[kk] * alpha
            o_gdn[kk] = s
            return pred + gk_ref[t, pl.ds(kk, 1), :] * s

        pred = lax.fori_loop(0, HEAD_DIM, decay_row, zero, unroll=4)
        v_new = scal(t, 1) * (actc_ref[t, pl.ds(512 + hrow + vrow, V_HALF), :] - pred)

        def update_row(kk, o, v_new=v_new, t=t):
            s = o_gdn[kk] + gk_ref[t, pl.ds(kk, 1), :] * v_new
            o_gdn[kk] = s
            return o + gq_ref[t, pl.ds(kk, 1), :] * s

        yc_ref[t, pl.ds(vrow, V_HALF), :] = lax.fori_loop(0, HEAD_DIM, update_row, zero, unroll=4)

    @pl.when(vh == 0)
    def _():
        for t, ts in enumerate(toks):
            f_d = ut_ref[pl.ds(U_FD + hrow, HEAD_DIM), ts]
            t1 = col(C_LOGLB, HEAD_DIM, hrow)
            t2 = col(C_LOG1M, HEAD_DIM, hrow) - _softplus(-f_d)
            logf = jnp.maximum(t1, t2) + _log1p_exp_neg(jnp.abs(t1 - t2))
            hf_ref[t] = jnp.exp(logf)
            hk_ref[t] = col(C_ONEM, HEAD_DIM, hrow) * _sigmoid(-f_d)
            hq_ref[t] = _silu(ut_ref[pl.ds(U_QD + hrow, HEAD_DIM), ts])

    vd = [ut_ref[pl.ds(U_ID + hrow + vrow, V_HALF), ts] for ts in toks]

    def hgrn_row(kk, os_):
        s = i_hg[kk]
        os_ = list(os_)
        for t in range(n_tok):
            s = s * hf_ref[t, pl.ds(kk, 1), :] + hk_ref[t, pl.ds(kk, 1), :] * vd[t]
            os_[t] = os_[t] + hq_ref[t, pl.ds(kk, 1), :] * s
        o_hg[kk] = s
        return tuple(os_)

    os_ = lax.fori_loop(0, HEAD_DIM, hgrn_row, (zero,) * n_tok, unroll=2)
    for t in range(n_tok):
        yd_ref[t, pl.ds(vrow, V_HALF), :] = os_[t]

    @pl.when(vh == HEAD_DIM // V_HALF - 1)
    def _():
        for y_ref, g0, z0, m0 in ((ya_ref, C_SSD_G, U_ZA, 0), (yc_ref, C_GDN_G, U_ZC, 2 * D_BRANCH),
                                  (yd_ref, C_HG_G, U_ZD, 3 * D_BRANCH)):
            for t, ts in enumerate(toks):
                y = y_ref[t]
                ms = jnp.mean(y * y, axis=0, keepdims=True)
                z = ut_ref[pl.ds(z0 + hrow, HEAD_DIM), ts]
                mix_ref[pl.ds(m0 + hrow, HEAD_DIM), ts] = (
                    y * lax.rsqrt(ms + RMS_EPS) * col(g0, HEAD_DIM, hrow) * _silu(z)).astype(BF16)


def _sample_mixer_call(ut, l, sw, states_t, acc, *, n_tok, n_seq):
    consts = [sw[k] for k in ('cols', 'cw_a', 'cw_c', 'lb', 'wb_re', 'wb_im', 'c_re', 'c_im', 'glu_w')]
    whole = lambda a: pl.BlockSpec((None,) + a.shape[1:], lambda h, v, nd=a.ndim - 1: (l,) + (0,) * nd)
    mat_spec = pl.BlockSpec((None, None, HEAD_DIM, V_HALF, n_seq), lambda h, v: (l, h, 0, v, 0))
    state_specs = [mat_spec, whole(states_t[1]), whole(states_t[2]), whole(states_t[3]), mat_spec,
                   whole(states_t[5]), mat_spec]
    n_cols = ut.shape[1]
    in_specs = [pl.BlockSpec(ut.shape, lambda h, v: (0, 0))] + [whole(a) for a in consts] + state_specs
    in_specs += [pl.BlockSpec(memory_space=pl.ANY)] * N_STATES
    inputs = [ut] + consts + list(states_t) + list(acc)
    out_shape = [jax.ShapeDtypeStruct((D_MODEL, n_cols), BF16)] + [jax.ShapeDtypeStruct(a.shape, a.dtype) for a in acc]
    out_specs = [pl.BlockSpec((D_MODEL, n_cols), lambda h, v: (0, 0))] + state_specs
    n_before_acc = 1 + len(consts) + N_STATES
    tok_tile = lambda rows: pltpu.VMEM((n_tok, rows, n_seq), F32)
    scratch = [tok_tile(SSD_XBC), tok_tile(GDN_QKV), tok_tile(4 * HEADS),
               pltpu.VMEM((S5_LANES, n_cols), F32), pltpu.VMEM((S5_LANES, n_cols), F32)]
    scratch += [tok_tile(HEAD_DIM)] * 8
    return pl.pallas_call(
        functools.partial(_sample_mixer_kernel, n_tok=n_tok, n_seq=n_seq),
        grid=(HEADS, HEAD_DIM // V_HALF),
        in_specs=in_specs,
        out_specs=out_specs,
        out_shape=out_shape,
        scratch_shapes=scratch,
        input_output_aliases={n_before_acc + i: 1 + i for i in range(N_STATES)},
        compiler_params=pltpu.CompilerParams(dimension_semantics=("arbitrary", "arbitrary"),
                                             vmem_limit_bytes=VMEM_LIMIT),
        name="mixer_sample",
    )(*inputs)


W_IN_ROWS_PER_STEP = 128


def _regroup_w_in_kernel(w_ref, o_ref, ot_ref):
    w = w_ref[...]
    small = jnp.concatenate([w[:, 768:772], w[:, 2308:2316], w[:, 768:772],
                             jnp.zeros((w.shape[0], LANES - 4 * HEADS), F32)], axis=1)
    wr = jnp.concatenate([w[:, 0:768], w[:, 772:1284], w[:, 1284:2308], small, w[:, 2316:3340]], axis=1)
    o_ref[...] = wr.astype(BF16)
    ot_ref[...] = wr.T.astype(BF16)


def _regroup_w_in(w_in):
    depth, d_model, d_in = w_in.shape
    tk = W_IN_ROWS_PER_STEP
    w_cols = WA + WB + WC + WD
    return pl.pallas_call(
        _regroup_w_in_kernel,
        grid=(depth, d_model // tk),
        in_specs=[pl.BlockSpec((None, tk, d_in), lambda l, i: (l, i, 0))],
        out_specs=[pl.BlockSpec((None, tk, w_cols), lambda l, i: (l, i, 0)),
                   pl.BlockSpec((None, w_cols, tk), lambda l, i: (l, 0, i))],
        out_shape=[jax.ShapeDtypeStruct((depth, d_model, w_cols), BF16),
                   jax.ShapeDtypeStruct((depth, w_cols, d_model), BF16)],
        compiler_params=pltpu.CompilerParams(dimension_semantics=("arbitrary", "arbitrary"),
                                             vmem_limit_bytes=VMEM_LIMIT),
        name="regroup_w_in",
    )(w_in.astype(F32))


def _pad_rows(v, width=ROW_W):
    v = v.astype(F32).reshape(v.shape[0], -1)
    return jnp.pad(v, ((0, 0), (0, width - v.shape[1])))


def _stacked_weights(p, lbs):
    depth = lbs.shape[0]
    rep = lambda v: jnp.repeat(v.astype(F32), HEAD_DIM, axis=-1)
    zeros_h = jnp.zeros((depth, HEADS), F32)
    small_bias = jnp.concatenate([p['ssd_dt_bias'], zeros_h, p['gdn_dt_bias'], p['ssd_dt_bias']], axis=1)
    small_scale = jnp.concatenate([jnp.ones((depth, HEADS), F32), zeros_h, -jnp.exp(p['gdn_a_log']),
                                   -jnp.exp(p['ssd_a_log'])], axis=1)
    zrow = jnp.zeros((depth, ROW_W), F32)
    rows = jnp.stack([
        _pad_rows(p['ssd_conv_b']), _pad_rows(small_bias), _pad_rows(small_scale),
        _pad_rows(rep(p['ssd_d'])), _pad_rows(p['ssd_norm_g']), _pad_rows(p['s5_d']),
        _pad_rows(p['s5_glu_b']), _pad_rows(p['gdn_conv_b']), zrow, zrow,
        _pad_rows(p['gdn_norm_g']), _pad_rows(jnp.log(lbs)), _pad_rows(jnp.log1p(-lbs)), _pad_rows(1.0 - lbs),
        _pad_rows(p['hg_norm_g']), zrow], axis=1)

    pad_cw = lambda cw: jnp.pad(cw.astype(F32), ((0, 0), (0, SUBLANES - CONV_W), (0, 0)))

    lam_re, lam_im = p['s5_lam_re'].astype(F32), p['s5_lam_im'].astype(F32)
    dt = jnp.exp(p['s5_log_dt'].astype(F32))[..., None]
    mag = jnp.exp(lam_re * dt)
    ang = lam_im * dt
    lb_re, lb_im = mag * jnp.cos(ang), mag * jnp.sin(ang)
    den = jnp.square(lam_re) + jnp.square(lam_im)
    nr = lb_re - 1.0
    coef_re = (nr * lam_re + lb_im * lam_im) / den
    coef_im = (lb_im * lam_re - nr * lam_im) / den
    b_re, b_im = p['s5_b_re'].astype(F32), p['s5_b_im'].astype(F32)
    bb_re = coef_re[..., None] * b_re - coef_im[..., None] * b_im
    bb_im = coef_re[..., None] * b_im + coef_im[..., None] * b_re
    eye_g = jnp.eye(S5_GROUPS, dtype=F32)
    bd_in = lambda bb: jnp.einsum('lgnq,gh->lgqhn', bb, eye_g).reshape(depth, D_BRANCH, S5_LANES).astype(BF16)
    bd_out = lambda c: jnp.einsum('lgqn,gh->lgnhq', c.astype(F32), eye_g).reshape(depth, S5_LANES, D_BRANCH).astype(BF16)

    pr, pi = [lb_re.reshape(depth, -1)], [lb_im.reshape(depth, -1)]
    for _ in range(SUBLANES - 1):
        pr, pi = (pr + [pr[-1] * pr[0] - pi[-1] * pi[0]], pi + [pr[-1] * pi[0] + pi[-1] * pr[0]])
    pw = jnp.stack([jnp.concatenate([a, b], axis=-1) for a, b in zip(pr, pi)], axis=1)
    ridx = jnp.arange(SUBLANES)[None, :, None]
    tabs = [jnp.where(ridx >= d, pw[:, d - 1:d, :], 0.0) for d in (1, 2, 4)] + [pw]
    s5tab = jnp.stack(tabs, axis=1).astype(F32)

    lanes = lambda v: jnp.broadcast_to(v.astype(F32)[..., None], v.shape + (LANES,))
    col_vecs = [p['ssd_conv_b'], small_bias, small_scale, rep(p['ssd_d']), p['ssd_norm_g'], p['s5_d'], p['s5_glu_b'],
                p['gdn_conv_b'], p['gdn_norm_g'], jnp.log(lbs), jnp.log1p(-lbs), 1.0 - lbs, p['hg_norm_g']]
    assert [int(np.prod(v.shape[1:])) for v in col_vecs] == list(SAMPLE_COL_WIDTHS)
    cols_s = lanes(jnp.concatenate([v.astype(F32).reshape(depth, -1) for v in col_vecs], axis=1))
    bd_in_t = lambda bb: jnp.einsum('lgnq,gh->lgnhq', bb, eye_g).reshape(depth, S5_LANES, D_BRANCH).astype(BF16)
    bd_out_t = lambda c: jnp.einsum('lgqn,gh->lgqhn', c.astype(F32), eye_g).reshape(depth, D_BRANCH, S5_LANES).astype(BF16)
    sample = dict(cols=cols_s, cw_a=lanes(p['ssd_conv_w']), cw_c=lanes(p['gdn_conv_w']),
                  lb=lanes(jnp.stack([lb_re.reshape(depth, -1), lb_im.reshape(depth, -1)], axis=1)),
                  wb_re=bd_in_t(bb_re), wb_im=bd_in_t(bb_im), c_re=bd_out_t(p['s5_c_re']), c_im=bd_out_t(p['s5_c_im']),
                  glu_w=jnp.swapaxes(p['s5_glu_w'], 1, 2).astype(BF16))

    w_in_r, w_in_t = _regroup_w_in(p['w_in'])
    return dict(w_in=w_in_r, w_in_t=w_in_t, rows=rows, cw_a=pad_cw(p['ssd_conv_w']), cw_c=pad_cw(p['gdn_conv_w']),
                wb_re=bd_in(bb_re), wb_im=bd_in(bb_im), c_re=bd_out(p['s5_c_re']), c_im=bd_out(p['s5_c_im']),
                glu_w=p['s5_glu_w'].astype(BF16), s5tab=s5tab, w_out=p['w_out'].astype(BF16),
                ln_g=p['ln_g'].astype(F32)[:, None, :], ln_b=p['ln_b'].astype(F32)[:, None, :], sample=sample)


def _pick_tile(n_rows, candidates):
    for t in candidates:
        if n_rows % t == 0:
            return t
    raise ValueError(f"no row tile for {n_rows}")


def kernel(x_prompt, x_sample, state_ssd, state_ssd_conv, state_s5_re, state_s5_im, state_gdn, state_gdn_conv, state_hgrn, meta_tokens, ln_in_g, ln_in_b, w_in, ssd_conv_w, ssd_conv_b, ssd_dt_bias, ssd_a_log, ssd_d, ssd_norm_g, s5_lam_re, s5_lam_im, s5_log_dt, s5_b_re, s5_b_im, s5_c_re, s5_c_im, s5_d, s5_glu_w, s5_glu_b, gdn_conv_w, gdn_conv_b, gdn_a_log, gdn_dt_bias, gdn_norm_g, hg_lb_raw, hg_norm_g, w_out, ln_g, ln_b):
    p = dict(w_in=w_in, ssd_conv_w=ssd_conv_w, ssd_conv_b=ssd_conv_b, ssd_dt_bias=ssd_dt_bias,
             ssd_a_log=ssd_a_log, ssd_d=ssd_d, ssd_norm_g=ssd_norm_g, s5_lam_re=s5_lam_re,
             s5_lam_im=s5_lam_im, s5_log_dt=s5_log_dt, s5_b_re=s5_b_re, s5_b_im=s5_b_im, s5_c_re=s5_c_re,
             s5_c_im=s5_c_im, s5_d=s5_d, s5_glu_w=s5_glu_w, s5_glu_b=s5_glu_b, gdn_conv_w=gdn_conv_w,
             gdn_conv_b=gdn_conv_b, gdn_a_log=gdn_a_log, gdn_dt_bias=gdn_dt_bias, gdn_norm_g=gdn_norm_g,
             hg_norm_g=hg_norm_g, w_out=w_out, ln_g=ln_g, ln_b=ln_b)
    bp, seq, _ = x_prompt.shape
    bs, dseq, _ = x_sample.shape
    assert dseq >= CONV_W - 1 and bs % LANES == 0
    t_pad = PAD_FRONT + N_META + seq
    rows_p = 3 * CHUNK
    assert t_pad % rows_p == 0
    n_p = bp * t_pad
    n_s = bs * dseq

    soft = jax.nn.softmax(hg_lb_raw.astype(F32), axis=0)
    csum = jnp.cumsum(soft, axis=0)
    lbs = csum - csum[0]

    xs = jnp.swapaxes(x_sample.astype(F32), 0, 1).reshape(n_s, D_MODEL)

    tm_p = _pick_tile(n_p, (704, 512, 384, 192))
    tm_s = n_s
    g_in, b_in = ln_in_g.astype(F32)[None], ln_in_b.astype(F32)[None]
    hp = pl.pallas_call(
        _ln_in_prompt_kernel,
        grid=(bp,),
        in_specs=[pl.BlockSpec((None, seq, D_MODEL), lambda b: (b, 0, 0)),
                  pl.BlockSpec((N_META, D_MODEL), lambda b: (0, 0)),
                  pl.BlockSpec((1, D_MODEL), lambda b: (0, 0)), pl.BlockSpec((1, D_MODEL), lambda b: (0, 0))],
        out_specs=pl.BlockSpec((None, t_pad, D_MODEL), lambda b: (b, 0, 0)),
        out_shape=jax.ShapeDtypeStruct((bp, t_pad, D_MODEL), F32),
        compiler_params=pltpu.CompilerParams(dimension_semantics=("arbitrary",), vmem_limit_bytes=VMEM_LIMIT),
        name="ln_in_prompt",
    )(x_prompt.astype(F32), meta_tokens.astype(F32), g_in, b_in).reshape(n_p, D_MODEL)
    hs = _row_tiled_call(_ln_in_kernel, n_s, tm_s, [xs], [g_in, b_in], [D_MODEL], [F32], "ln_in_sample")

    consts_p = _mixer_constants()
    depth = w_in.shape[0]
    wts = _stacked_weights(p, lbs)

    mat = (HEADS, HEAD_DIM, HEAD_DIM)
    zeros = lambda *shape: jnp.zeros((depth,) + shape, F32)
    acc_p = [zeros(bp, *mat), zeros(bp, SUBLANES, S5_LANES), zeros(bp, SUBLANES, S5_LANES), zeros(bp, *mat),
             zeros(bp, *mat)]
    conv_tails = []
    seq_last = lambda a: jnp.moveaxis(a.astype(F32), 1, -1)
    st_in = (seq_last(state_ssd), seq_last(state_ssd_conv),
             seq_last(state_s5_re).reshape(depth, S5_LANES, bs), seq_last(state_s5_im).reshape(depth, S5_LANES, bs),
             seq_last(state_gdn), seq_last(state_gdn_conv), seq_last(state_hgrn))
    acc_s = [jnp.zeros(a.shape, F32) for a in st_in]
    out_consts = [wts['w_out'], wts['ln_g'], wts['ln_b']]
    w_cols = WA + WB + WC + WD
    for l in range(depth):
        us, tails = _inproj_conv_call(hp, l, wts, bp, tm_p)
        conv_tails.append(tails[:, SUBLANES - (CONV_W - 1):, :])
        mix, *acc_p = _mixer_call(us, l, wts, consts_p, acc_p, n_seq=bp, rows=rows_p)
        mix = mix.reshape(n_p, D_MODEL)
        hp = _row_tiled_call(_outproj_kernel, n_p, tm_p, [mix, hp], out_consts, [D_MODEL], [F32],
                             "outproj_prompt", layer=l)

        n_wblk = 6
        ut = pl.pallas_call(
            _sample_inproj_kernel,
            grid=(n_wblk,),
            in_specs=[pl.BlockSpec((n_s, D_MODEL), lambda i: (0, 0)),
                      pl.BlockSpec((None, w_cols // n_wblk, D_MODEL), lambda i: (l, i, 0))],
            out_specs=pl.BlockSpec((w_cols // n_wblk, n_s), lambda i: (i, 0)),
            out_shape=jax.ShapeDtypeStruct((w_cols, n_s), F32),
            compiler_params=pltpu.CompilerParams(dimension_semantics=("arbitrary",), vmem_limit_bytes=VMEM_LIMIT),
            name="inproj_sample",
        )(hs, wts['w_in_t'])
        mixt, *acc_s = _sample_mixer_call(ut, l, wts['sample'], st_in, acc_s, n_tok=dseq, n_seq=bs)
        hs = pl.pallas_call(
            _sample_outproj_kernel,
            grid=(1,),
            in_specs=[pl.BlockSpec((D_MODEL, n_s), lambda i: (0, 0)), pl.BlockSpec((n_s, D_MODEL), lambda i: (0, 0)),
                      pl.BlockSpec((None, D_MODEL, D_MODEL), lambda i: (l, 0, 0)),
                      pl.BlockSpec((None, 1, D_MODEL), lambda i: (l, 0, 0)),
                      pl.BlockSpec((None, 1, D_MODEL), lambda i: (l, 0, 0))],
            out_specs=pl.BlockSpec((n_s, D_MODEL), lambda i: (0, 0)),
            out_shape=jax.ShapeDtypeStruct((n_s, D_MODEL), F32),
            compiler_params=pltpu.CompilerParams(dimension_semantics=("arbitrary",), vmem_limit_bytes=VMEM_LIMIT),
            name="outproj_sample",
        )(mixt, hs, *out_consts)

    y_prompt = hp.reshape(bp, t_pad, D_MODEL)[:, PAD_FRONT + N_META:]
    y_sample = jnp.swapaxes(hs.reshape(dseq, bs, D_MODEL), 0, 1)
    s5_shape = lambda a: a.reshape(depth, -1, S5_GROUPS, S5_STATE)
    p_ssd, p_re, p_im, p_gdn, p_hg = acc_p
    conv_tails = jnp.stack(conv_tails)
    p_ca, p_cc = conv_tails[..., 0:SSD_XBC], conv_tails[..., SSD_XBC:SSD_XBC + GDN_QKV]
    seq_first = lambda a: jnp.moveaxis(a, -1, 1)
    s_ssd, s_ca, s_re, s_im, s_gdn, s_cc, s_hg = [seq_first(a) for a in acc_s]
    return (y_prompt, y_sample, p_ssd, p_ca, s5_shape(p_re[:, :, 0]), s5_shape(p_im[:, :, 0]), p_gdn, p_cc, p_hg,
            s_ssd, s_ca, s5_shape(s_re), s5_shape(s_im), s_gdn, s_cc, s_hg)
```

```python
import functools
import math

import numpy as np
import jax
import jax.numpy as jnp
from jax import lax
from jax.experimental import pallas as pl
from jax.experimental.pallas import tpu as pltpu

F32 = jnp.float32
BF16 = jnp.bfloat16

D_MODEL = 1024
DEPTH = 4
N_META = 16
D_BRANCH = 256
HEADS = 4
PAIRS = HEADS // 2
HEAD_DIM = 64
SSD_XBC = 512
S5_GROUPS = 16
S5_STATE = 64
S5_LANES = S5_GROUPS * S5_STATE
GDN_QKV = 768
CONV_W = 4
DN_ALPHA = (2 * DEPTH) ** 0.25
LN_EPS = 1e-5
RMS_EPS = 1e-6
L2_EPS = 1e-6

CHUNK = 64
SUBLANES = 8
LANES = 128
PAD_FRONT = CHUNK - N_META
WA, WB, WC, WD = 768, 512, 1152, 1024
NROWS = 16
ROW_W = 768
VMEM_LIMIT = 56 * 1024 * 1024


def _dot(a, b):
    return jnp.dot(a.astype(BF16), b.astype(BF16), preferred_element_type=F32)


def _dot_nt(a, b):
    return lax.dot_general(a.astype(BF16), b.astype(BF16), (((1,), (1,)), ((), ())),
                           preferred_element_type=F32)


def _dot_tn(a, b):
    return lax.dot_general(a.astype(BF16), b.astype(BF16), (((0,), (0,)), ((), ())),
                           preferred_element_type=F32)


def _split(x, pieces):
    out = []
    r = x
    for i in range(pieces):
        xi = r.astype(BF16)
        out.append(xi)
        if i + 1 < pieces:
            r = r - xi.astype(F32)
    return out


def _dot01_l(w01, x, pieces=3):
    n = x.shape[-1]
    r = jnp.dot(w01, jnp.concatenate(_split(x, pieces), axis=-1), preferred_element_type=F32)
    return sum(r[:, i * n:(i + 1) * n] for i in range(pieces))


def _dot01_r(x, w01, pieces=3):
    m = x.shape[0]
    r = jnp.dot(jnp.concatenate(_split(x, pieces), axis=0), w01, preferred_element_type=F32)
    return sum(r[i * m:(i + 1) * m] for i in range(pieces))


def _sigmoid(x):
    return 1.0 / (1.0 + jnp.exp(-x))


def _silu(x):
    return x * _sigmoid(x)


def _log1p_exp_neg(a):
    e = jnp.exp(-a)
    u = 1.0 + e
    return jnp.log(u) - ((u - 1.0) - e) / u


def _softplus(x):
    return jnp.maximum(x, 0.0) + _log1p_exp_neg(jnp.abs(x))


def _gelu_tanh(x):
    c = math.sqrt(2.0 / math.pi)
    return 0.5 * x * (1.0 + jnp.tanh(c * (x + 0.044715 * (x * x * x))))


def _run_interleaved(chains):
    chains = list(chains)
    while chains:
        alive = []
        for ch in chains:
            try:
                next(ch)
                alive.append(ch)
            except StopIteration:
                pass
        chains = alive


def _layernorm_rows(r, g, b):
    mu = jnp.mean(r, axis=-1, keepdims=True)
    c = r - mu
    var = jnp.mean(c * c, axis=-1, keepdims=True)
    return c * lax.rsqrt(var + LN_EPS) * g + b


def _ln_in_kernel(x_ref, g_ref, b_ref, o_ref):
    o_ref[...] = _layernorm_rows(x_ref[...], g_ref[...], b_ref[...])


def _ln_in_prompt_kernel(x_ref, meta_ref, g_ref, b_ref, o_ref):
    o_ref[0:PAD_FRONT, :] = jnp.zeros((PAD_FRONT, D_MODEL), F32)
    o_ref[PAD_FRONT:CHUNK, :] = _layernorm_rows(meta_ref[...], g_ref[...], b_ref[...])
    seq = x_ref.shape[0]
    step = math.gcd(seq, 512)
    for r0 in range(0, seq, step):
        o_ref[CHUNK + r0:CHUNK + r0 + step, :] = _layernorm_rows(x_ref[r0:r0 + step, :], g_ref[...], b_ref[...])


def _inproj_conv_kernel(x_ref, w_ref, rows_ref, cwa_ref, cwc_ref, oa_ref, ob_ref, oc_ref, od_ref, tail_ref,
                        xx_ref, pp_ref, *, tiles_per_seq):
    tm = x_ref.shape[0]
    t = pl.program_id(0) % tiles_per_seq
    x = x_ref[...].astype(BF16)

    @pl.when(t == 0)
    def _():
        xx_ref[0:SUBLANES, :] = jnp.zeros((SUBLANES, SSD_XBC + GDN_QKV), F32)
        pp_ref[0:SUBLANES, :] = jnp.zeros((SUBLANES, SSD_XBC + GDN_QKV), F32)

    first_valid = jnp.where(t == 0, PAD_FRONT, 0)

    def conv(raw, c0, width, cw_ref, bias_row, o_ref):
        assert CONV_W == 4
        cols = slice(c0, c0 + width)
        r = lax.broadcasted_iota(jnp.int32, (tm, width), 0)
        raw = jnp.where(r >= first_valid, raw, 0.0)
        xx_ref[SUBLANES:SUBLANES + tm, cols] = raw
        x1 = xx_ref[SUBLANES - 1:SUBLANES - 1 + tm, cols]
        p = cw_ref[1:2, :] * raw + cw_ref[0:1, :] * x1
        pp_ref[SUBLANES:SUBLANES + tm, cols] = p
        acc = (rows_ref[bias_row:bias_row + 1, 0:width] + cw_ref[3:4, :] * raw + cw_ref[2:3, :] * x1
               + pp_ref[SUBLANES - 2:SUBLANES - 2 + tm, cols])
        o_ref[:, 256:256 + width] = _silu(acc)
        tail = raw[tm - SUBLANES:tm, :]
        tail_ref[:, cols] = tail
        xx_ref[0:SUBLANES, cols] = tail
        pp_ref[0:SUBLANES, cols] = p[tm - SUBLANES:tm, :]

    ua = jnp.dot(x, w_ref[:, 0:WA], preferred_element_type=F32)
    oa_ref[:, 0:256] = ua[:, 0:256]
    uc = jnp.dot(x, w_ref[:, WA + WB:WA + WB + WC], preferred_element_type=F32)
    conv(ua[:, 256:WA], 0, SSD_XBC, cwa_ref, 0, oa_ref)
    oc_ref[:, 0:256] = uc[:, 0:256]
    oc_ref[:, 1024:WC] = uc[:, 1024:WC]
    ob_ref[...] = jnp.dot(x, w_ref[:, WA:WA + WB], preferred_element_type=F32)
    conv(uc[:, 256:1024], SSD_XBC, GDN_QKV, cwc_ref, 7, oc_ref)
    od_ref[...] = jnp.dot(x, w_ref[:, WA + WB + WC:WA + WB + WC + WD], preferred_element_type=F32)


def _inproj_conv_call(hp, l, wts, n_seq, tm):
    n_rows = hp.shape[0]
    tiles_per_seq = n_rows // n_seq // tm
    consts = [wts['w_in'], wts['rows'], wts['cw_a'], wts['cw_c']]
    in_specs = [pl.BlockSpec((tm, D_MODEL), lambda i: (i, 0))]
    in_specs += [pl.BlockSpec((None,) + a.shape[1:], lambda i, nd=a.ndim - 1: (l,) + (0,) * nd) for a in consts]
    widths = (WA, WB, WC, WD)
    out_specs = [pl.BlockSpec((tm, w), lambda i: (i, 0)) for w in widths]
    out_specs += [pl.BlockSpec((None, SUBLANES, SSD_XBC + GDN_QKV), lambda i: (i // tiles_per_seq, 0, 0))]
    out_shape = [jax.ShapeDtypeStruct((n_rows, w), F32) for w in widths]
    out_shape += [jax.ShapeDtypeStruct((n_seq, SUBLANES, SSD_XBC + GDN_QKV), F32)]
    *us, tails = pl.pallas_call(
        functools.partial(_inproj_conv_kernel, tiles_per_seq=tiles_per_seq),
        grid=(n_rows // tm,),
        in_specs=in_specs,
        out_specs=out_specs,
        out_shape=out_shape,
        scratch_shapes=[pltpu.VMEM((SUBLANES + tm, SSD_XBC + GDN_QKV), F32)] * 2,
        compiler_params=pltpu.CompilerParams(dimension_semantics=("arbitrary",),
                                             vmem_limit_bytes=VMEM_LIMIT),
        name="inproj_conv_prompt",
    )(hp, *consts)
    return us, tails


def _outproj_kernel(mix_ref, x_ref, w_ref, g_ref, b_ref, o_ref):
    tm = mix_ref.shape[0]
    n_piece = 4 if tm % (4 * 2 * SUBLANES) == 0 else 1
    step = tm // n_piece
    outs = [jnp.dot(mix_ref[i * step:(i + 1) * step, :], w_ref[...], preferred_element_type=F32)
            for i in range(n_piece)]
    for i, out in enumerate(outs):
        rs = slice(i * step, (i + 1) * step)
        o_ref[rs, :] = _layernorm_rows(DN_ALPHA * x_ref[rs, :] + out, g_ref[...], b_ref[...])


def _row_tiled_call(kernel, n_rows, tm, row_inputs, const_inputs, out_widths, out_dtypes, name, layer=None):
    in_specs = [pl.BlockSpec((tm, a.shape[1]), lambda i: (i, 0)) for a in row_inputs]
    if layer is None:
        in_specs += [pl.BlockSpec(a.shape, lambda i, nd=a.ndim: (0,) * nd) for a in const_inputs]
    else:
        in_specs += [pl.BlockSpec((None,) + a.shape[1:], lambda i, nd=a.ndim - 1: (layer,) + (0,) * nd)
                     for a in const_inputs]
    out_specs = [pl.BlockSpec((tm, w), lambda i: (i, 0)) for w in out_widths]
    out_shape = [jax.ShapeDtypeStruct((n_rows, w), dt) for w, dt in zip(out_widths, out_dtypes)]
    single = len(out_widths) == 1
    return pl.pallas_call(
        kernel,
        grid=(n_rows // tm,),
        in_specs=in_specs,
        out_specs=out_specs[0] if single else out_specs,
        out_shape=out_shape[0] if single else out_shape,
        compiler_params=pltpu.CompilerParams(dimension_semantics=("arbitrary",),
                                             vmem_limit_bytes=VMEM_LIMIT),
        name=name,
    )(*row_inputs, *const_inputs)


N_MIXER_CONST = 16
N_STATES = 7
N_PROMPT_STATES = 5
N_LEVELS = 6
MXU_LEVELS = 2
SEQS_PER_STEP = 2


def _mixer_kernel(*refs, rows, nq):
    n_in = 4 + N_MIXER_CONST + N_PROMPT_STATES
    consts = refs[4:4 + N_MIXER_CONST]
    j = pl.program_id(1)
    stages = [[] for _ in range(nq + 2)]
    finishers = []
    for s in range(nq):
        seq_refs = [r.at[s] for r in refs[:4] + refs[n_in:]]
        prologue, state_free, carried, finish = _mixer_sequence(seq_refs[:4], consts, seq_refs[4:], j, rows)
        stages[s] += prologue
        stages[s + 1] += state_free
        stages[s + 2] += carried
        finishers.append(finish)
    for chains in stages:
        _run_interleaved(chains)
    for finish in finishers:
        finish()


def _mixer_sequence(u_refs, consts, out_refs, j, rows):
    nch = rows // CHUNK
    nlev = N_LEVELS
    (ua_ref, ub_ref, uc_ref, ud_ref) = u_refs
    (rows_ref, wbre_ref, wbim_ref, cre_ref, cim_ref, glu_ref, tab_ref, cmat_ref, cpair_ref, bdm_ref, lr_ref,
     ea_ref, eb_ref, blk_ref, sel_ref, ones_ref) = consts
    (mix_ref, o_ssd, o_s5re, o_s5im, o_gdn, o_hg, hre_ref, him_ref, ssd_sc, gdn_sc, hg_sc) = out_refs

    @pl.when(j == 0)
    def _():
        ssd_sc[...] = jnp.zeros_like(ssd_sc)
        gdn_sc[...] = jnp.zeros_like(gdn_sc)
        hg_sc[...] = jnp.zeros_like(hg_sc)
        o_s5re[...] = jnp.zeros_like(o_s5re)
        o_s5im[...] = jnp.zeros_like(o_s5im)

    first_valid = jnp.where(j == 0, PAD_FRONT, 0)

    def valid_rows(r):
        return r >= first_valid

    def row(i, w):
        return rows_ref[i:i + 1, 0:w]

    act_a = lambda rs, lo, hi: ua_ref[rs, 256 + lo:256 + hi]
    act_c = lambda rs, lo, hi: uc_ref[rs, 256 + lo:256 + hi]

    u_b = jnp.where(valid_rows(lax.broadcasted_iota(jnp.int32, (rows, D_BRANCH), 0)), ub_ref[:, 256:512], 0.0)
    hre_ref[...] = _dot(u_b, wbre_ref[...])
    him_ref[...] = _dot(u_b, wbim_ref[...])

    def s5_group(g, carry):
        r0 = pl.multiple_of(g * SUBLANES, SUBLANES)
        xr = hre_ref[pl.ds(r0, SUBLANES), :]
        xi = him_ref[pl.ds(r0, SUBLANES), :]
        for k, d in enumerate((1, 2, 4)):
            tr = tab_ref[k, :, 0:S5_LANES]
            ti = tab_ref[k, :, S5_LANES:2 * S5_LANES]
            sr = pltpu.roll(xr, d, axis=0)
            si = pltpu.roll(xi, d, axis=0)
            xr, xi = xr + tr * sr - ti * si, xi + tr * si + ti * sr
        cr, ci = carry
        pr = tab_ref[3, :, 0:S5_LANES]
        pi = tab_ref[3, :, S5_LANES:2 * S5_LANES]
        xr, xi = xr + pr * cr - pi * ci, xi + pr * ci + pi * cr
        hre_ref[pl.ds(r0, SUBLANES), :] = xr
        him_ref[pl.ds(r0, SUBLANES), :] = xi
        return (jnp.broadcast_to(xr[SUBLANES - 1:SUBLANES, :], (SUBLANES, S5_LANES)),
                jnp.broadcast_to(xi[SUBLANES - 1:SUBLANES, :], (SUBLANES, S5_LANES)))

    cr, ci = lax.fori_loop(0, rows // SUBLANES, s5_group, (o_s5re[...], o_s5im[...]))
    o_s5re[...] = cr
    o_s5im[...] = ci

    y5 = (_dot(hre_ref[...], cre_ref[...]) - _dot(him_ref[...], cim_ref[...])
          + row(5, D_BRANCH) * ub_ref[:, 256:512])
    y5 = _gelu_tanh(y5)
    y5 = y5 * _sigmoid(_dot(y5, glu_ref[...]) + row(6, D_BRANCH))
    mix_ref[:, 256:512] = (y5 * _silu(ub_ref[:, 0:256])).astype(BF16)

    tri_b = cmat_ref[0]

    def head_sum(x):
        return jnp.dot(x.astype(BF16), blk_ref[...], preferred_element_type=F32)

    def rms_finish(y, ss, g_row, z):
        return (y * lax.rsqrt(ss * (1.0 / HEAD_DIM) + RMS_EPS) * g_row * _silu(z)).astype(BF16)

    def hgrn_inputs(rs, valid_w):
        f_d = ud_ref[rs, 512:768]
        lsig = -_softplus(-f_d)
        t1 = row(11, D_BRANCH)
        t2 = row(12, D_BRANCH) + lsig
        logf = jnp.maximum(t1, t2) + _log1p_exp_neg(jnp.abs(t1 - t2))
        logf = jnp.where(valid_w, logf, 0.0)
        kd = row(13, D_BRANCH) * _sigmoid(-f_d)
        qd = _silu(ud_ref[rs, 256:512])
        vd = jnp.where(valid_w, ud_ref[rs, 768:1024], 0.0)
        xmm = _dot01_l(cmat_ref[1:2 + MXU_LEVELS].reshape((1 + MXU_LEVELS) * CHUNK, CHUNK), logf)
        gcd = xmm[0:CHUNK, :]
        xlev = []
        for lev in range(nlev - MXU_LEVELS):
            half = CHUNK >> (lev + 1)
            g3 = gcd.reshape(CHUNK // (2 * half), 2 * half, D_BRANCH)
            bound = jnp.broadcast_to(g3[:, half - 1:half, :], g3.shape).reshape(CHUNK, D_BRANCH)
            xlev.append(-jnp.abs(gcd - bound))
        xlev += [xmm[(1 + i) * CHUNK:(2 + i) * CHUNK, :] for i in range(MXU_LEVELS)]
        return logf, kd, qd, vd, gcd, xlev

    def prompt_block():
        lr0, lr1 = lr_ref[0], lr_ref[1]
        pairs = [slice(p * LANES, (p + 1) * LANES) for p in range(PAIRS)]
        data = [dict() for _ in range(nch)]

        def bd(x):
            xb = x.astype(BF16)
            return jnp.concatenate([xb * lr0, xb * lr1], axis=0)

        def mm_pair(a, b):
            return _dot(a, bd(b))

        def prologue(c):
            d = data[c]
            rs = slice(c * CHUNK, (c + 1) * CHUNK)
            d['rs'] = rs
            rr = lax.broadcasted_iota(jnp.int32, (CHUNK, LANES), 0) + c * CHUNK
            lane = lax.broadcasted_iota(jnp.int32, (CHUNK, LANES), 1)
            valid = valid_rows(rr)
            t = uc_ref[rs, 1024:1152] + row(1, LANES)
            vals = jnp.where((lane >= HEADS) & (lane < 2 * HEADS), _sigmoid(t), _softplus(t) * row(2, LANES))
            vals = jnp.where(valid, vals, 0.0)
            cs = _dot01_l(tri_b, vals)
            ex_v = _dot01_r(vals, ea_ref[...], pieces=2)
            q = act_c(rs, 0, 256)
            k = act_c(rs, 256, 512)
            qss = head_sum(q * q)
            kss = head_sum(k * k)
            valid_w = jnp.concatenate([valid, valid], axis=-1)
            logf, kd, qd, vd, gcd, xlev = hgrn_inputs(rs, valid_w)
            d.update(kd=kd, qd=qd, vd=vd)
            yield
            cs_pieces = _split(cs, 3)
            ex_c_all = jnp.dot(jnp.concatenate(cs_pieces, axis=0), eb_ref[...], preferred_element_type=F32)
            ex_c = sum(ex_c_all[i * CHUNK:(i + 1) * CHUNK] for i in range(3))
            crow = sum(lax.dot_general(sel_ref[...], piece, (((1,), (1,)), ((), ())), preferred_element_type=F32)
                       for piece in cs_pieces)
            d['diag'] = head_sum(qd * kd)
            yield
            dt_full, beta_full = ex_v[:, 0:256], ex_v[:, 256:512]
            gc, acum = ex_c[:, 0:256], ex_c[:, 256:512]
            d['prow'] = lambda r: jnp.concatenate([crow[r:r + 1, :], crow[r + 1:r + 2, :]], axis=1)
            xs = act_a(rs, 0, 256)
            alast = acum[CHUNK - 1:CHUNK, :]
            xdt = xs * dt_full
            d.update(acum=acum, alast=alast, xdt=xdt, xdt_end=xdt * jnp.exp(alast - acum), eacum=jnp.exp(acum))
            v = act_c(rs, 512, 768)
            q = q * lax.rsqrt(qss + L2_EPS) * (HEAD_DIM ** -0.5)
            k = k * lax.rsqrt(kss + L2_EPS)
            glast = gc[CHUNK - 1:CHUNK, :]
            egc = jnp.exp(gc)
            d.update(q=q, k=k, gc=gc, glast=glast, beta=beta_full, vb=v * beta_full, kbe=k * beta_full * egc,
                     qe=q * egc, kend=k * jnp.exp(glast - gc))
            gld = gcd[CHUNK - 1:CHUNK, :]
            g1, g2, g3 = [t.astype(F32) for t in _split(gld, 3)]
            rid = lax.broadcasted_iota(jnp.int32, (2 * SUBLANES, D_BRANCH), 0)
            d.update(qed=qd * jnp.exp(gcd), kend_d=kd * jnp.exp(gld - gcd), xlev=xlev,
                     g16=jnp.where(rid == 0, g1, jnp.where(rid == 1, g2, jnp.where(rid == 2, g3, 0.0))))

        def ssd_a(c, p):
            d, ps = data[c], pairs[p]
            g0 = 256 + p * HEAD_DIM
            bg = act_a(d['rs'], g0, g0 + HEAD_DIM)
            cg = act_a(d['rs'], g0 + LANES, g0 + LANES + HEAD_DIM)
            cb2 = _dot_nt(cg, jnp.concatenate([bg, bg], axis=0))
            lm = jnp.exp(jnp.minimum(d['acum'][:, ps] - d['prow'](HEADS + 2 * p), 0.0)) * cpair_ref[0]
            d['cbl', p] = cb2 * lm
            d['bg', p], d['cg', p] = bg, cg
            yield

        def gdn_a(c, p):
            d, ps = data[c], pairs[p]
            k = d['k'][:, ps]
            dec = jnp.exp(jnp.minimum(d['gc'][:, ps] - d['prow'](2 * p), 0.0)) * cpair_ref[0]
            kq = _dot_nt(jnp.concatenate([k, d['q'][:, ps]], axis=0), bd(k))
            yield
            m = kq[0:CHUNK] * dec * d['beta'][:, ps] * cpair_ref[1]
            d['aq', p] = kq[CHUNK:2 * CHUNK] * dec
            acc = cpair_ref[2] - m
            mp = mm_pair(m, m)
            yield
            for _ in range(nlev - 2):
                acc_add = mm_pair(acc, mp)
                mp = mm_pair(mp, mp)
                yield
                acc = acc + acc_add
            acc = acc + mm_pair(acc, mp)
            yield
            d['uw', p] = _dot(acc, jnp.concatenate([bd(d['vb'][:, ps]), bd(d['kbe'][:, ps])], axis=1))
            yield

        def hgrn_a(c, p):
            d, ps = data[c], pairs[p]
            qd, kd = d['qd'][:, ps], d['kd'][:, ps]
            amat = jnp.zeros((CHUNK, LANES), F32)
            for lev in range(nlev):
                z = jnp.exp(d['xlev'][lev][:, ps])
                amat = amat + _dot_nt(qd * z, bd(kd * z)) * cpair_ref[3 + lev]
                if lev % 2 == 1:
                    yield
            d['amat', p] = amat
            d['dcol', p] = jnp.exp(_dot_tn(d['g16'][:, ps], ones_ref[...]))
            yield

        def ssd_b():
            for c in range(nch):
                d = data[c]
                ya = []
                for p, ps in enumerate(pairs):
                    s_pk = ssd_sc[p]
                    ya.append(mm_pair(d['cbl', p], d['xdt'][:, ps]) + _dot(d['cg', p], s_pk) * d['eacum'][:, ps])
                    ssd_sc[p] = s_pk * jnp.exp(d['alast'][:, ps]) + _dot_tn(d['bg', p], d['xdt_end'][:, ps])
                yield
                ya = jnp.concatenate(ya, axis=-1) + row(3, D_BRANCH) * act_a(d['rs'], 0, 256)
                ss = head_sum(ya * ya)
                yield
                mix_ref[d['rs'], 0:256] = rms_finish(ya, ss, row(4, D_BRANCH), ua_ref[d['rs'], 0:256])

        def gdn_b():
            bdm = bdm_ref[...]
            for c in range(nch):
                d = data[c]
                ws, s_old = [], []
                for p, ps in enumerate(pairs):
                    s_bd = gdn_sc[p]
                    s_old.append(s_bd)
                    ws.append(_dot(jnp.concatenate([d['uw', p][:, LANES:2 * LANES], d['qe'][:, ps]], axis=0), s_bd))
                yield
                yc = []
                for p, ps in enumerate(pairs):
                    v_new = d['uw', p][:, 0:LANES] - ws[p][0:CHUNK]
                    yc.append(ws[p][CHUNK:2 * CHUNK] + mm_pair(d['aq', p], v_new))
                    gdn_sc[p] = s_old[p] * jnp.exp(d['glast'][:, ps]) + _dot_tn(d['kend'][:, ps], v_new) * bdm
                yield
                yc = jnp.concatenate(yc, axis=-1)
                ss = head_sum(yc * yc)
                yield
                mix_ref[d['rs'], 512:768] = rms_finish(yc, ss, row(10, D_BRANCH), uc_ref[d['rs'], 0:256])

        def hgrn_b():
            bdm = bdm_ref[...]
            for c in range(nch):
                d = data[c]
                yd = []
                for p, ps in enumerate(pairs):
                    s_bd = hg_sc[p]
                    vd = d['vd'][:, ps]
                    yd.append(mm_pair(d['amat', p], vd) + d['diag'][:, ps] * vd + _dot(d['qed'][:, ps], s_bd))
                    hg_sc[p] = s_bd * d['dcol', p] + _dot_tn(d['kend_d'][:, ps], vd) * bdm
                yield
                yd = jnp.concatenate(yd, axis=-1)
                ss = head_sum(yd * yd)
                yield
                mix_ref[d['rs'], 768:1024] = rms_finish(yd, ss, row(14, D_BRANCH), ud_ref[d['rs'], 0:256])

        return ([prologue(c) for c in range(nch)],
                [f(c, p) for c in range(nch) for p in range(PAIRS) for f in (gdn_a, hgrn_a, ssd_a)],
                [gdn_b(), hgrn_b(), ssd_b()])

    def finish():
        @pl.when(j == pl.num_programs(1) - 1)
        def _():
            for p in range(PAIRS):
                for e in range(2):
                    es = slice(e * HEAD_DIM, (e + 1) * HEAD_DIM)
                    o_ssd[2 * p + e] = ssd_sc[p, :, es]
                    o_gdn[2 * p + e] = gdn_sc[p, es, es]
                    o_hg[2 * p + e] = hg_sc[p, es, es]

    return (*prompt_block(), finish)


def _mixer_constants():
    i = np.arange(CHUNK)[:, None]
    jn = np.arange(CHUNK)[None, :]
    tri = (jn <= i).astype(np.float32)
    strict = (jn < i).astype(np.float32)
    eye = np.eye(CHUNK, dtype=np.float32)
    wlev, mlev = [], []
    for lev in range(N_LEVELS):
        b = CHUNK >> (lev + 1)
        blk_i, pos_i = i // (2 * b), i % (2 * b)
        mid = blk_i * 2 * b + b
        upper = pos_i >= b
        w = np.where(upper, (jn >= mid) & (jn <= i), (jn > i) & (jn < mid)).astype(np.float32)
        msk = (((jn // (2 * b)) == blk_i) & upper & ((jn % (2 * b)) < b)).astype(np.float32)
        wlev.append(w)
        mlev.append(msk)
    cmat = np.stack([tri, tri] + wlev[N_LEVELS - MXU_LEVELS:])
    cmask = np.stack([tri, strict, eye] + mlev)
    cpair = np.concatenate([cmask, cmask], axis=-1)
    bdm = np.kron(np.eye(2, dtype=np.float32), np.ones((HEAD_DIM, HEAD_DIM), np.float32))
    lane = np.arange(LANES)[None, :]
    lr = np.stack([np.broadcast_to(lane < HEAD_DIM, (CHUNK, LANES)),
                   np.broadcast_to(lane >= HEAD_DIM, (CHUNK, LANES))]).astype(np.float32)
    e_a = np.zeros((LANES, 512), np.float32)
    e_b = np.zeros((LANES, 512), np.float32)
    blk = np.zeros((256, 256), np.float32)
    sel = np.zeros((2 * SUBLANES, LANES), np.float32)
    for h in range(HEADS):
        hs = slice(h * HEAD_DIM, (h + 1) * HEAD_DIM)
        e_a[h, hs] = 1.0
        e_a[HEADS + h, 256 + h * HEAD_DIM:256 + (h + 1) * HEAD_DIM] = 1.0
        e_b[2 * HEADS + h, hs] = 1.0
        e_b[3 * HEADS + h, 256 + h * HEAD_DIM:256 + (h + 1) * HEAD_DIM] = 1.0
        blk[hs, hs] = 1.0
        sel[h, 2 * HEADS + h] = 1.0
        sel[HEADS + h, 3 * HEADS + h] = 1.0
    ones = np.ones((2 * SUBLANES, LANES), np.float32)
    return (jnp.asarray(cmat, BF16), jnp.asarray(cpair, F32), jnp.asarray(bdm, F32),
            jnp.asarray(lr, BF16), jnp.asarray(e_a, BF16), jnp.asarray(e_b, BF16), jnp.asarray(blk, BF16),
            jnp.asarray(sel, BF16), jnp.asarray(ones, BF16))


def _layer_spec(a, l):
    nd = a.ndim - 1
    return pl.BlockSpec((None,) + a.shape[1:], lambda b, j: (l,) + (0,) * nd)


def _mixer_call(us, l, wts, consts, acc, *, n_seq, rows):
    n_rows = us[0].shape[0]
    t_pad = n_rows // n_seq
    nblk = t_pad // rows
    nq = SEQS_PER_STEP if n_seq % SEQS_PER_STEP == 0 else 1
    us = [u.reshape(n_seq, t_pad, u.shape[1]) for u in us]
    rmap = lambda b, j: (b, j, 0)

    def state_spec(tail):
        return pl.BlockSpec((None, nq) + tail, lambda b, j: (l, b) + (0,) * len(tail))

    layer_consts = [wts[k] for k in ('rows', 'wb_re', 'wb_im', 'c_re', 'c_im', 'glu_w', 's5tab')]
    assert len(layer_consts) + len(consts) == N_MIXER_CONST
    in_specs = [pl.BlockSpec((nq, rows, w), rmap) for w in (WA, WB, WC, WD)]
    in_specs += [_layer_spec(a, l) for a in layer_consts]
    in_specs += [pl.BlockSpec(a.shape, lambda b, j, nd=a.ndim: (0,) * nd) for a in consts]
    inputs = us + layer_consts + list(consts)
    mat = (HEADS, HEAD_DIM, HEAD_DIM)
    s5_tail = (SUBLANES, S5_LANES)
    tails = [mat, s5_tail, s5_tail, mat, mat]
    scratch = [pltpu.VMEM((nq, rows, S5_LANES), F32),
               pltpu.VMEM((nq, rows, S5_LANES), F32),
               pltpu.VMEM((nq, PAIRS, HEAD_DIM, LANES), F32),
               pltpu.VMEM((nq, PAIRS, LANES, LANES), F32),
               pltpu.VMEM((nq, PAIRS, LANES, LANES), F32)]
    assert len(acc) == len(tails) == N_PROMPT_STATES
    n_before_acc = len(inputs)
    inputs += list(acc)
    in_specs += [pl.BlockSpec(memory_space=pl.ANY)] * len(acc)
    out_shape = [jax.ShapeDtypeStruct((n_seq, t_pad, D_MODEL), BF16)]
    out_shape += [jax.ShapeDtypeStruct(a.shape, a.dtype) for a in acc]
    out_specs = [pl.BlockSpec((nq, rows, D_MODEL), rmap)] + [state_spec(t) for t in tails]
    aliases = {n_before_acc + i: 1 + i for i in range(len(acc))}
    return pl.pallas_call(
        functools.partial(_mixer_kernel, rows=rows, nq=nq),
        grid=(n_seq // nq, nblk),
        in_specs=in_specs,
        out_specs=out_specs,
        out_shape=out_shape,
        scratch_shapes=scratch,
        input_output_aliases=aliases,
        compiler_params=pltpu.CompilerParams(dimension_semantics=("arbitrary", "arbitrary"),
                                             vmem_limit_bytes=VMEM_LIMIT),
        name="mixer_prompt",
    )(*inputs)


SAMPLE_COL_WIDTHS = (SSD_XBC, 4 * HEADS, 4 * HEADS, D_BRANCH, D_BRANCH, D_BRANCH, D_BRANCH, GDN_QKV, D_BRANCH,
                     D_BRANCH, D_BRANCH, D_BRANCH, D_BRANCH)
(C_CONVB_A, C_SBIAS, C_SSCALE, C_SSD_D, C_SSD_G, C_S5_D, C_GLU_B, C_CONVB_C, C_GDN_G, C_LOGLB, C_LOG1M, C_ONEM,
 C_HG_G) = [int(v) for v in np.cumsum((0,) + SAMPLE_COL_WIDTHS[:-1])]
U_ZA, U_XBC, U_ZB, U_UB, U_ZC, U_QKV, U_SMALL, U_ZD, U_QD, U_FD, U_ID = (
    0, 256, WA, WA + 256, WA + WB, WA + WB + 256, WA + WB + 1024, WA + WB + WC, WA + WB + WC + 256,
    WA + WB + WC + 512, WA + WB + WC + 768)
V_HALF = HEAD_DIM // 2


def _sample_inproj_kernel(h_ref, wt_ref, o_ref):
    o_ref[...] = lax.dot_general(wt_ref[...], h_ref[...].astype(BF16), (((1,), (1,)), ((), ())),
                                 preferred_element_type=F32)


def _sample_outproj_kernel(mixt_ref, x_ref, w_ref, g_ref, b_ref, o_ref):
    out = lax.dot_general(mixt_ref[...], w_ref[...], (((0,), (0,)), ((), ())), preferred_element_type=F32)
    o_ref[...] = _layernorm_rows(DN_ALPHA * x_ref[...] + out, g_ref[...], b_ref[...])


def _sample_mixer_kernel(ut_ref, cols_ref, cwa_ref, cwc_ref, lb_ref, wbre_ref, wbim_ref, cre_ref, cim_ref, glu_ref,
                         i_ssd, i_ca, i_s5re, i_s5im, i_gdn, i_cc, i_hg, *rest, n_tok, n_seq):
    (mix_ref, o_ssd, o_ca, o_s5re, o_s5im, o_gdn, o_cc, o_hg) = rest[N_STATES:N_STATES + 1 + N_STATES]
    (acta_ref, actc_ref, small_ref, hre_ref, him_ref, gq_ref, gk_ref, hq_ref, hk_ref, hf_ref,
     ya_ref, yc_ref, yd_ref) = rest[N_STATES + 1 + N_STATES:]
    h = pl.program_id(0)
    vh = pl.program_id(1)
    toks = [slice(t * n_seq, (t + 1) * n_seq) for t in range(n_tok)]
    hrow = pl.multiple_of(h * HEAD_DIM, HEAD_DIM)

    def col(c0, n, off=0):
        return cols_ref[pl.ds(c0 + off, n), :]

    @pl.when((h == 0) & (vh == 0))
    def _():
        for u0, width, i_c, o_c, cw_ref, act_ref, cb in ((U_XBC, SSD_XBC, i_ca, o_ca, cwa_ref, acta_ref, C_CONVB_A),
                                                        (U_QKV, GDN_QKV, i_cc, o_cc, cwc_ref, actc_ref, C_CONVB_C)):
            xx = [i_c[i] for i in range(CONV_W - 1)] + [ut_ref[u0:u0 + width, ts] for ts in toks]
            for t in range(n_tok):
                acc = cols_ref[cb:cb + width, :]
                for w in range(CONV_W):
                    acc = acc + cw_ref[w] * xx[t + w]
                act_ref[t] = _silu(acc)
            for i in range(CONV_W - 1):
                o_c[i] = xx[n_tok + i]
        rid = lax.broadcasted_iota(jnp.int32, (4 * HEADS, n_seq), 0)
        for t, ts in enumerate(toks):
            v = ut_ref[U_SMALL:U_SMALL + 4 * HEADS, ts] + cols_ref[C_SBIAS:C_SBIAS + 4 * HEADS, :]
            small_ref[t] = jnp.where((rid >= HEADS) & (rid < 2 * HEADS), _sigmoid(v),
                                     _softplus(v) * cols_ref[C_SSCALE:C_SSCALE + 4 * HEADS, :])
        u_b = ut_ref[U_UB:U_UB + D_BRANCH, :]
        bu_re = _dot(wbre_ref[...], u_b)
        bu_im = _dot(wbim_ref[...], u_b)
        sr, si = i_s5re[...], i_s5im[...]
        lr, li = lb_ref[0], lb_ref[1]
        for t, ts in enumerate(toks):
            sr, si = lr * sr - li * si + bu_re[:, ts], lr * si + li * sr + bu_im[:, ts]
            hre_ref[:, ts] = sr
            him_ref[:, ts] = si
        o_s5re[...] = sr
        o_s5im[...] = si
        y5 = _dot(cre_ref[...], hre_ref[...]) - _dot(cim_ref[...], him_ref[...])
        for ts in toks:
            y = _gelu_tanh(y5[:, ts] + cols_ref[C_S5_D:C_S5_D + D_BRANCH, :] * u_b[:, ts])
            g = _dot(glu_ref[...], y) + cols_ref[C_GLU_B:C_GLU_B + D_BRANCH, :]
            mix_ref[D_BRANCH:2 * D_BRANCH, ts] = (y * _sigmoid(g) * _silu(ut_ref[U_ZB:U_ZB + D_BRANCH, ts])).astype(BF16)

    vrow = pl.multiple_of(vh * V_HALF, V_HALF)
    vsel = pl.ds(hrow + vrow, V_HALF)

    def scal(t, r):
        return small_ref[t, pl.ds(r * HEADS + h, 1), :]

    grp = (h // 2) * HEAD_DIM
    xdt = [acta_ref[t, vsel, :] * scal(t, 0) for t in range(n_tok)]
    ea = [jnp.exp(scal(t, 3)) for t in range(n_tok)]

    def ssd_row(n, ys):
        s = i_ssd[n]
        ys = list(ys)
        for t in range(n_tok):
            s = s * ea[t] + acta_ref[t, pl.ds(256 + grp + n, 1), :] * xdt[t]
            ys[t] = ys[t] + acta_ref[t, pl.ds(384 + grp + n, 1), :] * s
        o_ssd[n] = s
        return tuple(ys)

    zero = jnp.zeros((V_HALF, n_seq), F32)
    ys = lax.fori_loop(0, HEAD_DIM, ssd_row, (zero,) * n_tok, unroll=2)
    for t in range(n_tok):
        ya_ref[t, pl.ds(vrow, V_HALF), :] = ys[t] + col(C_SSD_D, V_HALF, hrow + vrow) * acta_ref[t, vsel, :]

    @pl.when(vh == 0)
    def _():
        for t in range(n_tok):
            q = actc_ref[t, pl.ds(hrow, HEAD_DIM), :]
            k = actc_ref[t, pl.ds(256 + hrow, HEAD_DIM), :]
            gq_ref[t] = q * lax.rsqrt(jnp.sum(q * q, axis=0, keepdims=True) + L2_EPS) * (HEAD_DIM ** -0.5)
            gk_ref[t] = k * lax.rsqrt(jnp.sum(k * k, axis=0, keepdims=True) + L2_EPS)

    for t in range(n_tok):
        alpha = jnp.exp(scal(t, 2))
        src = i_gdn if t == 0 else o_gdn

        def decay_row(kk, pred, alpha=alpha, src=src, t=t):
            s = src[kk] * alpha
            o_gdn[kk] = s
            return pred + gk_ref[t, pl.ds(kk, 1), :] * s

        pred = lax.fori_loop(0, HEAD_DIM, decay_row, zero, unroll=4)
        v_new = scal(t, 1) * (actc_ref[t, pl.ds(512 + hrow + vrow, V_HALF), :] - pred)

        def update_row(kk, o, v_new=v_new, t=t):
            s = o_gdn[kk] + gk_ref[t, pl.ds(kk, 1), :] * v_new
            o_gdn[kk] = s
            return o + gq_ref[t, pl.ds(kk, 1), :] * s

        yc_ref[t, pl.ds(vrow, V_HALF), :] = lax.fori_loop(0, HEAD_DIM, update_row, zero, unroll=4)

    @pl.when(vh == 0)
    def _():
        for t, ts in enumerate(toks):
            f_d = ut_ref[pl.ds(U_FD + hrow, HEAD_DIM), ts]
            t1 = col(C_LOGLB, HEAD_DIM, hrow)
            t2 = col(C_LOG1M, HEAD_DIM, hrow) - _softplus(-f_d)
            logf = jnp.maximum(t1, t2) + _log1p_exp_neg(jnp.abs(t1 - t2))
            hf_ref[t] = jnp.exp(logf)
            hk_ref[t] = col(C_ONEM, HEAD_DIM, hrow) * _sigmoid(-f_d)
            hq_ref[t] = _silu(ut_ref[pl.ds(U_QD + hrow, HEAD_DIM), ts])

    vd = [ut_ref[pl.ds(U_ID + hrow + vrow, V_HALF), ts] for ts in toks]

    def hgrn_row(kk, os_):
        s = i_hg[kk]
        os_ = list(os_)
        for t in range(n_tok):
            s = s * hf_ref[t, pl.ds(kk, 1), :] + hk_ref[t, pl.ds(kk, 1), :] * vd[t]
            os_[t] = os_[t] + hq_ref[t, pl.ds(kk, 1), :] * s
        o_hg[kk] = s
        return tuple(os_)

    os_ = lax.fori_loop(0, HEAD_DIM, hgrn_row, (zero,) * n_tok, unroll=2)
    for t in range(n_tok):
        yd_ref[t, pl.ds(vrow, V_HALF), :] = os_[t]

    @pl.when(vh == HEAD_DIM // V_HALF - 1)
    def _():
        for y_ref, g0, z0, m0 in ((ya_ref, C_SSD_G, U_ZA, 0), (yc_ref, C_GDN_G, U_ZC, 2 * D_BRANCH),
                                  (yd_ref, C_HG_G, U_ZD, 3 * D_BRANCH)):
            for t, ts in enumerate(toks):
                y = y_ref[t]
                ms = jnp.mean(y * y, axis=0, keepdims=True)
                z = ut_ref[pl.ds(z0 + hrow, HEAD_DIM), ts]
                mix_ref[pl.ds(m0 + hrow, HEAD_DIM), ts] = (
                    y * lax.rsqrt(ms + RMS_EPS) * col(g0, HEAD_DIM, hrow) * _silu(z)).astype(BF16)


def _sample_mixer_call(ut, l, sw, states_t, acc, *, n_tok, n_seq):
    consts = [sw[k] for k in ('cols', 'cw_a', 'cw_c', 'lb', 'wb_re', 'wb_im', 'c_re', 'c_im', 'glu_w')]
    whole = lambda a: pl.BlockSpec((None,) + a.shape[1:], lambda h, v, nd=a.ndim - 1: (l,) + (0,) * nd)
    mat_spec = pl.BlockSpec((None, None, HEAD_DIM, V_HALF, n_seq), lambda h, v: (l, h, 0, v, 0))
    state_specs = [mat_spec, whole(states_t[1]), whole(states_t[2]), whole(states_t[3]), mat_spec,
                   whole(states_t[5]), mat_spec]
    n_cols = ut.shape[1]
    in_specs = [pl.BlockSpec(ut.shape, lambda h, v: (0, 0))] + [whole(a) for a in consts] + state_specs
    in_specs += [pl.BlockSpec(memory_space=pl.ANY)] * N_STATES
    inputs = [ut] + consts + list(states_t) + list(acc)
    out_shape = [jax.ShapeDtypeStruct((D_MODEL, n_cols), BF16)] + [jax.ShapeDtypeStruct(a.shape, a.dtype) for a in acc]
    out_specs = [pl.BlockSpec((D_MODEL, n_cols), lambda h, v: (0, 0))] + state_specs
    n_before_acc = 1 + len(consts) + N_STATES
    tok_tile = lambda rows: pltpu.VMEM((n_tok, rows, n_seq), F32)
    scratch = [tok_tile(SSD_XBC), tok_tile(GDN_QKV), tok_tile(4 * HEADS),
               pltpu.VMEM((S5_LANES, n_cols), F32), pltpu.VMEM((S5_LANES, n_cols), F32)]
    scratch += [tok_tile(HEAD_DIM)] * 8
    return pl.pallas_call(
        functools.partial(_sample_mixer_kernel, n_tok=n_tok, n_seq=n_seq),
        grid=(HEADS, HEAD_DIM // V_HALF),
        in_specs=in_specs,
        out_specs=out_specs,
        out_shape=out_shape,
        scratch_shapes=scratch,
        input_output_aliases={n_before_acc + i: 1 + i for i in range(N_STATES)},
        compiler_params=pltpu.CompilerParams(dimension_semantics=("arbitrary", "arbitrary"),
                                             vmem_limit_bytes=VMEM_LIMIT),
        name="mixer_sample",
    )(*inputs)


W_IN_ROWS_PER_STEP = 128


def _regroup_w_in_kernel(w_ref, o_ref, ot_ref):
    w = w_ref[...]
    small = jnp.concatenate([w[:, 768:772], w[:, 2308:2316], w[:, 768:772],
                             jnp.zeros((w.shape[0], LANES - 4 * HEADS), F32)], axis=1)
    wr = jnp.concatenate([w[:, 0:768], w[:, 772:1284], w[:, 1284:2308], small, w[:, 2316:3340]], axis=1)
    o_ref[...] = wr.astype(BF16)
    ot_ref[...] = wr.T.astype(BF16)


def _regroup_w_in(w_in):
    depth, d_model, d_in = w_in.shape
    tk = W_IN_ROWS_PER_STEP
    w_cols = WA + WB + WC + WD
    return pl.pallas_call(
        _regroup_w_in_kernel,
        grid=(depth, d_model // tk),
        in_specs=[pl.BlockSpec((None, tk, d_in), lambda l, i: (l, i, 0))],
        out_specs=[pl.BlockSpec((None, tk, w_cols), lambda l, i: (l, i, 0)),
                   pl.BlockSpec((None, w_cols, tk), lambda l, i: (l, 0, i))],
        out_shape=[jax.ShapeDtypeStruct((depth, d_model, w_cols), BF16),
                   jax.ShapeDtypeStruct((depth, w_cols, d_model), BF16)],
        compiler_params=pltpu.CompilerParams(dimension_semantics=("arbitrary", "arbitrary"),
                                             vmem_limit_bytes=VMEM_LIMIT),
        name="regroup_w_in",
    )(w_in.astype(F32))


def _pad_rows(v, width=ROW_W):
    v = v.astype(F32).reshape(v.shape[0], -1)
    return jnp.pad(v, ((0, 0), (0, width - v.shape[1])))


def _stacked_weights(p, lbs):
    depth = lbs.shape[0]
    rep = lambda v: jnp.repeat(v.astype(F32), HEAD_DIM, axis=-1)
    zeros_h = jnp.zeros((depth, HEADS), F32)
    small_bias = jnp.concatenate([p['ssd_dt_bias'], zeros_h, p['gdn_dt_bias'], p['ssd_dt_bias']], axis=1)
    small_scale = jnp.concatenate([jnp.ones((depth, HEADS), F32), zeros_h, -jnp.exp(p['gdn_a_log']),
                                   -jnp.exp(p['ssd_a_log'])], axis=1)
    zrow = jnp.zeros((depth, ROW_W), F32)
    rows = jnp.stack([
        _pad_rows(p['ssd_conv_b']), _pad_rows(small_bias), _pad_rows(small_scale),
        _pad_rows(rep(p['ssd_d'])), _pad_rows(p['ssd_norm_g']), _pad_rows(p['s5_d']),
        _pad_rows(p['s5_glu_b']), _pad_rows(p['gdn_conv_b']), zrow, zrow,
        _pad_rows(p['gdn_norm_g']), _pad_rows(jnp.log(lbs)), _pad_rows(jnp.log1p(-lbs)), _pad_rows(1.0 - lbs),
        _pad_rows(p['hg_norm_g']), zrow], axis=1)

    pad_cw = lambda cw: jnp.pad(cw.astype(F32), ((0, 0), (0, SUBLANES - CONV_W), (0, 0)))

    lam_re, lam_im = p['s5_lam_re'].astype(F32), p['s5_lam_im'].astype(F32)
    dt = jnp.exp(p['s5_log_dt'].astype(F32))[..., None]
    mag = jnp.exp(lam_re * dt)
    ang = lam_im * dt
    lb_re, lb_im = mag * jnp.cos(ang), mag * jnp.sin(ang)
    den = jnp.square(lam_re) + jnp.square(lam_im)
    nr = lb_re - 1.0
    coef_re = (nr * lam_re + lb_im * lam_im) / den
    coef_im = (lb_im * lam_re - nr * lam_im) / den
    b_re, b_im = p['s5_b_re'].astype(F32), p['s5_b_im'].astype(F32)
    bb_re = coef_re[..., None] * b_re - coef_im[..., None] * b_im
    bb_im = coef_re[..., None] * b_im + coef_im[..., None] * b_re
    eye_g = jnp.eye(S5_GROUPS, dtype=F32)
    bd_in = lambda bb: jnp.einsum('lgnq,gh->lgqhn', bb, eye_g).reshape(depth, D_BRANCH, S5_LANES).astype(BF16)
    bd_out = lambda c: jnp.einsum('lgqn,gh->lgnhq', c.astype(F32), eye_g).reshape(depth, S5_LANES, D_BRANCH).astype(BF16)

    pr, pi = [lb_re.reshape(depth, -1)], [lb_im.reshape(depth, -1)]
    for _ in range(SUBLANES - 1):
        pr, pi = (pr + [pr[-1] * pr[0] - pi[-1] * pi[0]], pi + [pr[-1] * pi[0] + pi[-1] * pr[0]])
    pw = jnp.stack([jnp.concatenate([a, b], axis=-1) for a, b in zip(pr, pi)], axis=1)
    ridx = jnp.arange(SUBLANES)[None, :, None]
    tabs = [jnp.where(ridx >= d, pw[:, d - 1:d, :], 0.0) for d in (1, 2, 4)] + [pw]
    s5tab = jnp.stack(tabs, axis=1).astype(F32)

    lanes = lambda v: jnp.broadcast_to(v.astype(F32)[..., None], v.shape + (LANES,))
    col_vecs = [p['ssd_conv_b'], small_bias, small_scale, rep(p['ssd_d']), p['ssd_norm_g'], p['s5_d'], p['s5_glu_b'],
                p['gdn_conv_b'], p['gdn_norm_g'], jnp.log(lbs), jnp.log1p(-lbs), 1.0 - lbs, p['hg_norm_g']]
    assert [int(np.prod(v.shape[1:])) for v in col_vecs] == list(SAMPLE_COL_WIDTHS)
    cols_s = lanes(jnp.concatenate([v.astype(F32).reshape(depth, -1) for v in col_vecs], axis=1))
    bd_in_t = lambda bb: jnp.einsum('lgnq,gh->lgnhq', bb, eye_g).reshape(depth, S5_LANES, D_BRANCH).astype(BF16)
    bd_out_t = lambda c: jnp.einsum('lgqn,gh->lgqhn', c.astype(F32), eye_g).reshape(depth, D_BRANCH, S5_LANES).astype(BF16)
    sample = dict(cols=cols_s, cw_a=lanes(p['ssd_conv_w']), cw_c=lanes(p['gdn_conv_w']),
                  lb=lanes(jnp.stack([lb_re.reshape(depth, -1), lb_im.reshape(depth, -1)], axis=1)),
                  wb_re=bd_in_t(bb_re), wb_im=bd_in_t(bb_im), c_re=bd_out_t(p['s5_c_re']), c_im=bd_out_t(p['s5_c_im']),
                  glu_w=jnp.swapaxes(p['s5_glu_w'], 1, 2).astype(BF16))

    w_in_r, w_in_t = _regroup_w_in(p['w_in'])
    return dict(w_in=w_in_r, w_in_t=w_in_t, rows=rows, cw_a=pad_cw(p['ssd_conv_w']), cw_c=pad_cw(p['gdn_conv_w']),
                wb_re=bd_in(bb_re), wb_im=bd_in(bb_im), c_re=bd_out(p['s5_c_re']), c_im=bd_out(p['s5_c_im']),
                glu_w=p['s5_glu_w'].astype(BF16), s5tab=s5tab, w_out=p['w_out'].astype(BF16),
                ln_g=p['ln_g'].astype(F32)[:, None, :], ln_b=p['ln_b'].astype(F32)[:, None, :], sample=sample)


def _pick_tile(n_rows, candidates):
    for t in candidates:
        if n_rows % t == 0:
            return t
    raise ValueError(f"no row tile for {n_rows}")


def kernel(x_prompt, x_sample, state_ssd, state_ssd_conv, state_s5_re, state_s5_im, state_gdn, state_gdn_conv, state_hgrn, meta_tokens, ln_in_g, ln_in_b, w_in, ssd_conv_w, ssd_conv_b, ssd_dt_bias, ssd_a_log, ssd_d, ssd_norm_g, s5_lam_re, s5_lam_im, s5_log_dt, s5_b_re, s5_b_im, s5_c_re, s5_c_im, s5_d, s5_glu_w, s5_glu_b, gdn_conv_w, gdn_conv_b, gdn_a_log, gdn_dt_bias, gdn_norm_g, hg_lb_raw, hg_norm_g, w_out, ln_g, ln_b):
    p = dict(w_in=w_in, ssd_conv_w=ssd_conv_w, ssd_conv_b=ssd_conv_b, ssd_dt_bias=ssd_dt_bias,
             ssd_a_log=ssd_a_log, ssd_d=ssd_d, ssd_norm_g=ssd_norm_g, s5_lam_re=s5_lam_re,
             s5_lam_im=s5_lam_im, s5_log_dt=s5_log_dt, s5_b_re=s5_b_re, s5_b_im=s5_b_im, s5_c_re=s5_c_re,
             s5_c_im=s5_c_im, s5_d=s5_d, s5_glu_w=s5_glu_w, s5_glu_b=s5_glu_b, gdn_conv_w=gdn_conv_w,
             gdn_conv_b=gdn_conv_b, gdn_a_log=gdn_a_log, gdn_dt_bias=gdn_dt_bias, gdn_norm_g=gdn_norm_g,
             hg_norm_g=hg_norm_g, w_out=w_out, ln_g=ln_g, ln_b=ln_b)
    bp, seq, _ = x_prompt.shape
    bs, dseq, _ = x_sample.shape
    assert dseq >= CONV_W - 1 and bs % LANES == 0
    t_pad = PAD_FRONT + N_META + seq
    rows_p = 3 * CHUNK
    assert t_pad % rows_p == 0
    n_p = bp * t_pad
    n_s = bs * dseq

    soft = jax.nn.softmax(hg_lb_raw.astype(F32), axis=0)
    csum = jnp.cumsum(soft, axis=0)
    lbs = csum - csum[0]

    xs = jnp.swapaxes(x_sample.astype(F32), 0, 1).reshape(n_s, D_MODEL)

    tm_p = _pick_tile(n_p, (704, 512, 384, 192))
    tm_s = n_s
    g_in, b_in = ln_in_g.astype(F32)[None], ln_in_b.astype(F32)[None]
    hp = pl.pallas_call(
        _ln_in_prompt_kernel,
        grid=(bp,),
        in_specs=[pl.BlockSpec((None, seq, D_MODEL), lambda b: (b, 0, 0)),
                  pl.BlockSpec((N_META, D_MODEL), lambda b: (0, 0)),
                  pl.BlockSpec((1, D_MODEL), lambda b: (0, 0)), pl.BlockSpec((1, D_MODEL), lambda b: (0, 0))],
        out_specs=pl.BlockSpec((None, t_pad, D_MODEL), lambda b: (b, 0, 0)),
        out_shape=jax.ShapeDtypeStruct((bp, t_pad, D_MODEL), F32),
        compiler_params=pltpu.CompilerParams(dimension_semantics=("arbitrary",), vmem_limit_bytes=VMEM_LIMIT),
        name="ln_in_prompt",
    )(x_prompt.astype(F32), meta_tokens.astype(F32), g_in, b_in).reshape(n_p, D_MODEL)
    hs = _row_tiled_call(_ln_in_kernel, n_s, tm_s, [xs], [g_in, b_in], [D_MODEL], [F32], "ln_in_sample")

    consts_p = _mixer_constants()
    depth = w_in.shape[0]
    wts = _stacked_weights(p, lbs)

    mat = (HEADS, HEAD_DIM, HEAD_DIM)
    zeros = lambda *shape: jnp.zeros((depth,) + shape, F32)
    acc_p = [zeros(bp, *mat), zeros(bp, SUBLANES, S5_LANES), zeros(bp, SUBLANES, S5_LANES), zeros(bp, *mat),
             zeros(bp, *mat)]
    conv_tails = []
    seq_last = lambda a: jnp.moveaxis(a.astype(F32), 1, -1)
    st_in = (seq_last(state_ssd), seq_last(state_ssd_conv),
             seq_last(state_s5_re).reshape(depth, S5_LANES, bs), seq_last(state_s5_im).reshape(depth, S5_LANES, bs),
             seq_last(state_gdn), seq_last(state_gdn_conv), seq_last(state_hgrn))
    acc_s = [jnp.zeros(a.shape, F32) for a in st_in]
    out_consts = [wts['w_out'], wts['ln_g'], wts['ln_b']]
    w_cols = WA + WB + WC + WD
    for l in range(depth):
        us, tails = _inproj_conv_call(hp, l, wts, bp, tm_p)
        conv_tails.append(tails[:, SUBLANES - (CONV_W - 1):, :])
        mix, *acc_p = _mixer_call(us, l, wts, consts_p, acc_p, n_seq=bp, rows=rows_p)
        mix = mix.reshape(n_p, D_MODEL)
        hp = _row_tiled_call(_outproj_kernel, n_p, tm_p, [mix, hp], out_consts, [D_MODEL], [F32],
                             "outproj_prompt", layer=l)

        n_wblk = 6
        ut = pl.pallas_call(
            _sample_inproj_kernel,
            grid=(n_wblk,),
            in_specs=[pl.BlockSpec((n_s, D_MODEL), lambda i: (0, 0)),
                      pl.BlockSpec((None, w_cols // n_wblk, D_MODEL), lambda i: (l, i, 0))],
            out_specs=pl.BlockSpec((w_cols // n_wblk, n_s), lambda i: (i, 0)),
            out_shape=jax.ShapeDtypeStruct((w_cols, n_s), F32),
            compiler_params=pltpu.CompilerParams(dimension_semantics=("arbitrary",), vmem_limit_bytes=VMEM_LIMIT),
            name="inproj_sample",
        )(hs, wts['w_in_t'])
        mixt, *acc_s = _sample_mixer_call(ut, l, wts['sample'], st_in, acc_s, n_tok=dseq, n_seq=bs)
        hs = pl.pallas_call(
            _sample_outproj_kernel,
            grid=(1,),
            in_specs=[pl.BlockSpec((D_MODEL, n_s), lambda i: (0, 0)), pl.BlockSpec((n_s, D_MODEL), lambda i: (0, 0)),
                      pl.BlockSpec((None, D_MODEL, D_MODEL), lambda i: (l, 0, 0)),
                      pl.BlockSpec((None, 1, D_MODEL), lambda i: (l, 0, 0)),
                      pl.BlockSpec((None, 1, D_MODEL), lambda i: (l, 0, 0))],
            out_specs=pl.BlockSpec((n_s, D_MODEL), lambda i: (0, 0)),
            out_shape=jax.ShapeDtypeStruct((n_s, D_MODEL), F32),
            compiler_params=pltpu.CompilerParams(dimension_semantics=("arbitrary",), vmem_limit_bytes=VMEM_LIMIT),
            name="outproj_sample",
        )(mixt, hs, *out_consts)

    y_prompt = hp.reshape(bp, t_pad, D_MODEL)[:, PAD_FRONT + N_META:]
    y_sample = jnp.swapaxes(hs.reshape(dseq, bs, D_MODEL), 0, 1)
    s5_shape = lambda a: a.reshape(depth, -1, S5_GROUPS, S5_STATE)
    p_ssd, p_re, p_im, p_gdn, p_hg = acc_p
    conv_tails = jnp.stack(conv_tails)
    p_ca, p_cc = conv_tails[..., 0:SSD_XBC], conv_tails[..., SSD_XBC:SSD_XBC + GDN_QKV]
    seq_first = lambda a: jnp.moveaxis(a, -1, 1)
    s_ssd, s_ca, s_re, s_im, s_gdn, s_cc, s_hg = [seq_first(a) for a in acc_s]
    return (y_prompt, y_sample, p_ssd, p_ca, s5_shape(p_re[:, :, 0]), s5_shape(p_im[:, :, 0]), p_gdn, p_cc, p_hg,
            s_ssd, s_ca, s5_shape(s_re), s5_shape(s_im), s_gdn, s_cc, s_hg)
```

```python
import functools
import math

import numpy as np
import jax
import jax.numpy as jnp
from jax import lax
from jax.experimental import pallas as pl
from jax.experimental.pallas import tpu as pltpu

F32 = jnp.float32
BF16 = jnp.bfloat16

D_MODEL = 1024
DEPTH = 4
N_META = 16
D_BRANCH = 256
HEADS = 4
PAIRS = HEADS // 2
HEAD_DIM = 64
SSD_XBC = 512
S5_GROUPS = 16
S5_STATE = 64
S5_LANES = S5_GROUPS * S5_STATE
GDN_QKV = 768
CONV_W = 4
DN_ALPHA = (2 * DEPTH) ** 0.25
LN_EPS = 1e-5
RMS_EPS = 1e-6
L2_EPS = 1e-6

CHUNK = 64
SUBLANES = 8
LANES = 128
PAD_FRONT = CHUNK - N_META
WA, WB, WC, WD = 768, 512, 1152, 1024
NROWS = 16
ROW_W = 768
VMEM_LIMIT = 56 * 1024 * 1024


def _dot(a, b):
    return jnp.dot(a.astype(BF16), b.astype(BF16), preferred_element_type=F32)


def _dot_nt(a, b):
    return lax.dot_general(a.astype(BF16), b.astype(BF16), (((1,), (1,)), ((), ())),
                           preferred_element_type=F32)


def _dot_tn(a, b):
    return lax.dot_general(a.astype(BF16), b.astype(BF16), (((0,), (0,)), ((), ())),
                           preferred_element_type=F32)


def _split(x, pieces):
    out = []
    r = x
    for i in range(pieces):
        xi = r.astype(BF16)
        out.append(xi)
        if i + 1 < pieces:
            r = r - xi.astype(F32)
    return out


def _dot01_l(w01, x, pieces=3):
    n = x.shape[-1]
    r = jnp.dot(w01, jnp.concatenate(_split(x, pieces), axis=-1), preferred_element_type=F32)
    return sum(r[:, i * n:(i + 1) * n] for i in range(pieces))


def _dot01_r(x, w01, pieces=3):
    m = x.shape[0]
    r = jnp.dot(jnp.concatenate(_split(x, pieces), axis=0), w01, preferred_element_type=F32)
    return sum(r[i * m:(i + 1) * m] for i in range(pieces))


def _sigmoid(x):
    return 1.0 / (1.0 + jnp.exp(-x))


def _silu(x):
    return x * _sigmoid(x)


def _log1p_exp_neg(a):
    e = jnp.exp(-a)
    u = 1.0 + e
    return jnp.log(u) - ((u - 1.0) - e) / u


def _softplus(x):
    return jnp.maximum(x, 0.0) + _log1p_exp_neg(jnp.abs(x))


def _gelu_tanh(x):
    c = math.sqrt(2.0 / math.pi)
    return 0.5 * x * (1.0 + jnp.tanh(c * (x + 0.044715 * (x * x * x))))


def _run_interleaved(chains):
    chains = list(chains)
    while chains:
        alive = []
        for ch in chains:
            try:
                next(ch)
                alive.append(ch)
            except StopIteration:
                pass
        chains = alive


def _layernorm_rows(r, g, b):
    mu = jnp.mean(r, axis=-1, keepdims=True)
    c = r - mu
    var = jnp.mean(c * c, axis=-1, keepdims=True)
    return c * lax.rsqrt(var + LN_EPS) * g + b


def _ln_in_kernel(x_ref, g_ref, b_ref, o_ref):
    o_ref[...] = _layernorm_rows(x_ref[...], g_ref[...], b_ref[...])


def _ln_in_prompt_kernel(x_ref, meta_ref, g_ref, b_ref, o_ref):
    o_ref[0:PAD_FRONT, :] = jnp.zeros((PAD_FRONT, D_MODEL), F32)
    o_ref[PAD_FRONT:CHUNK, :] = _layernorm_rows(meta_ref[...], g_ref[...], b_ref[...])
    seq = x_ref.shape[0]
    step = math.gcd(seq, 512)
    for r0 in range(0, seq, step):
        o_ref[CHUNK + r0:CHUNK + r0 + step, :] = _layernorm_rows(x_ref[r0:r0 + step, :], g_ref[...], b_ref[...])


def _inproj_conv_kernel(x_ref, w_ref, rows_ref, cwa_ref, cwc_ref, oa_ref, ob_ref, oc_ref, od_ref, tail_ref,
                        xx_ref, pp_ref, *, tiles_per_seq):
    tm = x_ref.shape[0]
    t = pl.program_id(0) % tiles_per_seq
    x = x_ref[...].astype(BF16)

    @pl.when(t == 0)
    def _():
        xx_ref[0:SUBLANES, :] = jnp.zeros((SUBLANES, SSD_XBC + GDN_QKV), F32)
        pp_ref[0:SUBLANES, :] = jnp.zeros((SUBLANES, SSD_XBC + GDN_QKV), F32)

    first_valid = jnp.where(t == 0, PAD_FRONT, 0)

    def conv(raw, c0, width, cw_ref, bias_row, o_ref):
        assert CONV_W == 4
        cols = slice(c0, c0 + width)
        r = lax.broadcasted_iota(jnp.int32, (tm, width), 0)
        raw = jnp.where(r >= first_valid, raw, 0.0)
        xx_ref[SUBLANES:SUBLANES + tm, cols] = raw
        x1 = xx_ref[SUBLANES - 1:SUBLANES - 1 + tm, cols]
        p = cw_ref[1:2, :] * raw + cw_ref[0:1, :] * x1
        pp_ref[SUBLANES:SUBLANES + tm, cols] = p
        acc = (rows_ref[bias_row:bias_row + 1, 0:width] + cw_ref[3:4, :] * raw + cw_ref[2:3, :] * x1
               + pp_ref[SUBLANES - 2:SUBLANES - 2 + tm, cols])
        o_ref[:, 256:256 + width] = _silu(acc)
        tail = raw[tm - SUBLANES:tm, :]
        tail_ref[:, cols] = tail
        xx_ref[0:SUBLANES, cols] = tail
        pp_ref[0:SUBLANES, cols] = p[tm - SUBLANES:tm, :]

    ua = jnp.dot(x, w_ref[:, 0:WA], preferred_element_type=F32)
    oa_ref[:, 0:256] = ua[:, 0:256]
    uc = jnp.dot(x, w_ref[:, WA + WB:WA + WB + WC], preferred_element_type=F32)
    conv(ua[:, 256:WA], 0, SSD_XBC, cwa_ref, 0, oa_ref)
    oc_ref[:, 0:256] = uc[:, 0:256]
    oc_ref[:, 1024:WC] = uc[:, 1024:WC]
    ob_ref[...] = jnp.dot(x, w_ref[:, WA:WA + WB], preferred_element_type=F32)
    conv(uc[:, 256:1024], SSD_XBC, GDN_QKV, cwc_ref, 7, oc_ref)
    od_ref[...] = jnp.dot(x, w_ref[:, WA + WB + WC:WA + WB + WC + WD], preferred_element_type=F32)


def _inproj_conv_call(hp, l, wts, n_seq, tm):
    n_rows = hp.shape[0]
    tiles_per_seq = n_rows // n_seq // tm
    consts = [wts['w_in'], wts['rows'], wts['cw_a'], wts['cw_c']]
    in_specs = [pl.BlockSpec((tm, D_MODEL), lambda i: (i, 0))]
    in_specs += [pl.BlockSpec((None,) + a.shape[1:], lambda i, nd=a.ndim - 1: (l,) + (0,) * nd) for a in consts]
    widths = (WA, WB, WC, WD)
    out_specs = [pl.BlockSpec((tm, w), lambda i: (i, 0)) for w in widths]
    out_specs += [pl.BlockSpec((None, SUBLANES, SSD_XBC + GDN_QKV), lambda i: (i // tiles_per_seq, 0, 0))]
    out_shape = [jax.ShapeDtypeStruct((n_rows, w), F32) for w in widths]
    out_shape += [jax.ShapeDtypeStruct((n_seq, SUBLANES, SSD_XBC + GDN_QKV), F32)]
    *us, tails = pl.pallas_call(
        functools.partial(_inproj_conv_kernel, tiles_per_seq=tiles_per_seq),
        grid=(n_rows // tm,),
        in_specs=in_specs,
        out_specs=out_specs,
        out_shape=out_shape,
        scratch_shapes=[pltpu.VMEM((SUBLANES + tm, SSD_XBC + GDN_QKV), F32)] * 2,
        compiler_params=pltpu.CompilerParams(dimension_semantics=("arbitrary",),
                                             vmem_limit_bytes=VMEM_LIMIT),
        name="inproj_conv_prompt",
    )(hp, *consts)
    return us, tails


def _outproj_seq_kernel(mix_ref, x_ref, w_ref, g_ref, b_ref, o_ref, *, step, skip):
    t_pad = mix_ref.shape[0]
    assert 0 <= skip < step

    def product(i):
        return jnp.dot(mix_ref[i * step:(i + 1) * step, :], w_ref[...], preferred_element_type=F32)

    def finish(i, out):
        res = _layernorm_rows(DN_ALPHA * x_ref[i * step:(i + 1) * step, :] + out, g_ref[...], b_ref[...])
        if i == 0:
            o_ref[0:step - skip, :] = res[skip:step, :]
        else:
            o_ref[i * step - skip:(i + 1) * step - skip, :] = res

    prev = product(0)
    for i in range(1, t_pad // step):
        cur = product(i)
        finish(i - 1, prev)
        prev = cur
    finish(t_pad // step - 1, prev)


def _row_tiled_call(kernel, n_rows, tm, row_inputs, const_inputs, out_widths, out_dtypes, name, layer=None):
    in_specs = [pl.BlockSpec((tm, a.shape[1]), lambda i: (i, 0)) for a in row_inputs]
    if layer is None:
        in_specs += [pl.BlockSpec(a.shape, lambda i, nd=a.ndim: (0,) * nd) for a in const_inputs]
    else:
        in_specs += [pl.BlockSpec((None,) + a.shape[1:], lambda i, nd=a.ndim - 1: (layer,) + (0,) * nd)
                     for a in const_inputs]
    out_specs = [pl.BlockSpec((tm, w), lambda i: (i, 0)) for w in out_widths]
    out_shape = [jax.ShapeDtypeStruct((n_rows, w), dt) for w, dt in zip(out_widths, out_dtypes)]
    single = len(out_widths) == 1
    return pl.pallas_call(
        kernel,
        grid=(n_rows // tm,),
        in_specs=in_specs,
        out_specs=out_specs[0] if single else out_specs,
        out_shape=out_shape[0] if single else out_shape,
        compiler_params=pltpu.CompilerParams(dimension_semantics=("arbitrary",),
                                             vmem_limit_bytes=VMEM_LIMIT),
        name=name,
    )(*row_inputs, *const_inputs)


N_MIXER_CONST = 16
N_STATES = 7
N_PROMPT_STATES = 5
N_LEVELS = 6
MXU_LEVELS = 2
SEQS_PER_STEP = 2


def _mixer_kernel(*refs, rows, nq):
    n_in = 4 + N_MIXER_CONST + N_PROMPT_STATES
    consts = refs[4:4 + N_MIXER_CONST]
    j = pl.program_id(1)
    stages = [[] for _ in range(nq + 2)]
    finishers = []
    for s in range(nq):
        seq_refs = [r.at[s] for r in refs[:4] + refs[n_in:]]
        prologue, state_free, carried, finish = _mixer_sequence(seq_refs[:4], consts, seq_refs[4:], j, rows)
        stages[s] += prologue
        stages[s + 1] += state_free
        stages[s + 2] += carried
        finishers.append(finish)
    for chains in stages:
        _run_interleaved(chains)
    for finish in finishers:
        finish()


def _mixer_sequence(u_refs, consts, out_refs, j, rows):
    nch = rows // CHUNK
    nlev = N_LEVELS
    (ua_ref, ub_ref, uc_ref, ud_ref) = u_refs
    (rows_ref, wbre_ref, wbim_ref, cre_ref, cim_ref, glu_ref, tab_ref, cmat_ref, cpair_ref, bdm_ref, lr_ref,
     ea_ref, eb_ref, blk_ref, sel_ref, ones_ref) = consts
    (mix_ref, o_ssd, o_s5re, o_s5im, o_gdn, o_hg, hre_ref, him_ref, ssd_sc, gdn_sc, hg_sc) = out_refs

    @pl.when(j == 0)
    def _():
        ssd_sc[...] = jnp.zeros_like(ssd_sc)
        gdn_sc[...] = jnp.zeros_like(gdn_sc)
        hg_sc[...] = jnp.zeros_like(hg_sc)
        o_s5re[...] = jnp.zeros_like(o_s5re)
        o_s5im[...] = jnp.zeros_like(o_s5im)

    first_valid = jnp.where(j == 0, PAD_FRONT, 0)

    def valid_rows(r):
        return r >= first_valid

    def row(i, w):
        return rows_ref[i:i + 1, 0:w]

    act_a = lambda rs, lo, hi: ua_ref[rs, 256 + lo:256 + hi]
    act_c = lambda rs, lo, hi: uc_ref[rs, 256 + lo:256 + hi]

    u_b = jnp.where(valid_rows(lax.broadcasted_iota(jnp.int32, (rows, D_BRANCH), 0)), ub_ref[:, 256:512], 0.0)
    hre_ref[...] = _dot(u_b, wbre_ref[...])
    him_ref[...] = _dot(u_b, wbim_ref[...])

    def s5_group(g, carry):
        r0 = pl.multiple_of(g * SUBLANES, SUBLANES)
        xr = hre_ref[pl.ds(r0, SUBLANES), :]
        xi = him_ref[pl.ds(r0, SUBLANES), :]
        for k, d in enumerate((1, 2, 4)):
            tr = tab_ref[k, :, 0:S5_LANES]
            ti = tab_ref[k, :, S5_LANES:2 * S5_LANES]
            sr = pltpu.roll(xr, d, axis=0)
            si = pltpu.roll(xi, d, axis=0)
            xr, xi = xr + tr * sr - ti * si, xi + tr * si + ti * sr
        cr, ci = carry
        pr = tab_ref[3, :, 0:S5_LANES]
        pi = tab_ref[3, :, S5_LANES:2 * S5_LANES]
        xr, xi = xr + pr * cr - pi * ci, xi + pr * ci + pi * cr
        hre_ref[pl.ds(r0, SUBLANES), :] = xr
        him_ref[pl.ds(r0, SUBLANES), :] = xi
        return (jnp.broadcast_to(xr[SUBLANES - 1:SUBLANES, :], (SUBLANES, S5_LANES)),
                jnp.broadcast_to(xi[SUBLANES - 1:SUBLANES, :], (SUBLANES, S5_LANES)))

    cr, ci = lax.fori_loop(0, rows // SUBLANES, s5_group, (o_s5re[...], o_s5im[...]))
    o_s5re[...] = cr
    o_s5im[...] = ci

    y5 = (_dot(hre_ref[...], cre_ref[...]) - _dot(him_ref[...], cim_ref[...])
          + row(5, D_BRANCH) * ub_ref[:, 256:512])
    y5 = _gelu_tanh(y5)
    y5 = y5 * _sigmoid(_dot(y5, glu_ref[...]) + row(6, D_BRANCH))
    mix_ref[:, 256:512] = (y5 * _silu(ub_ref[:, 0:256])).astype(BF16)

    tri_b = cmat_ref[0]

    def head_sum(x):
        return jnp.dot(x.astype(BF16), blk_ref[...], preferred_element_type=F32)

    def rms_finish(y, ss, g_row, z):
        return (y * lax.rsqrt(ss * (1.0 / HEAD_DIM) + RMS_EPS) * g_row * _silu(z)).astype(BF16)

    def hgrn_inputs(rs, valid_w):
        f_d = ud_ref[rs, 512:768]
        lsig = -_softplus(-f_d)
        t1 = row(11, D_BRANCH)
        t2 = row(12, D_BRANCH) + lsig
        logf = jnp.maximum(t1, t2) + _log1p_exp_neg(jnp.abs(t1 - t2))
        logf = jnp.where(valid_w, logf, 0.0)
        kd = row(13, D_BRANCH) * _sigmoid(-f_d)
        qd = _silu(ud_ref[rs, 256:512])
        vd = jnp.where(valid_w, ud_ref[rs, 768:1024], 0.0)
        xmm = _dot01_l(cmat_ref[1:2 + MXU_LEVELS].reshape((1 + MXU_LEVELS) * CHUNK, CHUNK), logf)
        gcd = xmm[0:CHUNK, :]
        xlev = []
        for lev in range(nlev - MXU_LEVELS):
            half = CHUNK >> (lev + 1)
            g3 = gcd.reshape(CHUNK // (2 * half), 2 * half, D_BRANCH)
            bound = jnp.broadcast_to(g3[:, half - 1:half, :], g3.shape).reshape(CHUNK, D_BRANCH)
            xlev.append(-jnp.abs(gcd - bound))
        xlev += [xmm[(1 + i) * CHUNK:(2 + i) * CHUNK, :] for i in range(MXU_LEVELS)]
        return logf, kd, qd, vd, gcd, xlev

    def prompt_block():
        lr0, lr1 = lr_ref[0], lr_ref[1]
        pairs = [slice(p * LANES, (p + 1) * LANES) for p in range(PAIRS)]
        data = [dict() for _ in range(nch)]

        def bd(x):
            xb = x.astype(BF16)
            return jnp.concatenate([xb * lr0, xb * lr1], axis=0)

        def mm_pair(a, b):
            return _dot(a, bd(b))

        def prologue(c):
            d = data[c]
            rs = slice(c * CHUNK, (c + 1) * CHUNK)
            d['rs'] = rs
            rr = lax.broadcasted_iota(jnp.int32, (CHUNK, LANES), 0) + c * CHUNK
            lane = lax.broadcasted_iota(jnp.int32, (CHUNK, LANES), 1)
            valid = valid_rows(rr)
            t = uc_ref[rs, 1024:1152] + row(1, LANES)
            vals = jnp.where((lane >= HEADS) & (lane < 2 * HEADS), _sigmoid(t), _softplus(t) * row(2, LANES))
            vals = jnp.where(valid, vals, 0.0)
            cs = _dot01_l(tri_b, vals)
            ex_v = _dot01_r(vals, ea_ref[...], pieces=2)
            q = act_c(rs, 0, 256)
            k = act_c(rs, 256, 512)
            qss = head_sum(q * q)
            kss = head_sum(k * k)
            valid_w = jnp.concatenate([valid, valid], axis=-1)
            logf, kd, qd, vd, gcd, xlev = hgrn_inputs(rs, valid_w)
            d.update(kd=kd, qd=qd, vd=vd)
            yield
            cs_pieces = _split(cs, 3)
            ex_c_all = jnp.dot(jnp.concatenate(cs_pieces, axis=0), eb_ref[...], preferred_element_type=F32)
            ex_c = sum(ex_c_all[i * CHUNK:(i + 1) * CHUNK] for i in range(3))
            crow = sum(lax.dot_general(sel_ref[...], piece, (((1,), (1,)), ((), ())), preferred_element_type=F32)
                       for piece in cs_pieces)
            d['diag'] = head_sum(qd * kd)
            yield
            dt_full, beta_full = ex_v[:, 0:256], ex_v[:, 256:512]
            gc, acum = ex_c[:, 0:256], ex_c[:, 256:512]
            d['prow'] = lambda r: jnp.concatenate([crow[r:r + 1, :], crow[r + 1:r + 2, :]], axis=1)
            xs = act_a(rs, 0, 256)
            alast = acum[CHUNK - 1:CHUNK, :]
            xdt = xs * dt_full
            d.update(acum=acum, alast=alast, xdt=xdt, xdt_end=xdt * jnp.exp(alast - acum), eacum=jnp.exp(acum))
            v = act_c(rs, 512, 768)
            q = q * lax.rsqrt(qss + L2_EPS) * (HEAD_DIM ** -0.5)
            k = k * lax.rsqrt(kss + L2_EPS)
            glast = gc[CHUNK - 1:CHUNK, :]
            egc = jnp.exp(gc)
            d.update(q=q, k=k, gc=gc, glast=glast, beta=beta_full, vb=v * beta_full, kbe=k * beta_full * egc,
                     qe=q * egc, kend=k * jnp.exp(glast - gc))
            gld = gcd[CHUNK - 1:CHUNK, :]
            g1, g2, g3 = [t.astype(F32) for t in _split(gld, 3)]
            rid = lax.broadcasted_iota(jnp.int32, (2 * SUBLANES, D_BRANCH), 0)
            d.update(qed=qd * jnp.exp(gcd), kend_d=kd * jnp.exp(gld - gcd), xlev=xlev,
                     g16=jnp.where(rid == 0, g1, jnp.where(rid == 1, g2, jnp.where(rid == 2, g3, 0.0))))

        def ssd_a(c, p):
            d, ps = data[c], pairs[p]
            g0 = 256 + p * HEAD_DIM
            bg = act_a(d['rs'], g0, g0 + HEAD_DIM)
            cg = act_a(d['rs'], g0 + LANES, g0 + LANES + HEAD_DIM)
            cb2 = _dot_nt(cg, jnp.concatenate([bg, bg], axis=0))
            lm = jnp.exp(jnp.minimum(d['acum'][:, ps] - d['prow'](HEADS + 2 * p), 0.0)) * cpair_ref[0]
            d['cbl', p] = cb2 * lm
            d['bg', p], d['cg', p] = bg, cg
            yield

        def gdn_a(c, p):
            d, ps = data[c], pairs[p]
            k = d['k'][:, ps]
            dec = jnp.exp(jnp.minimum(d['gc'][:, ps] - d['prow'](2 * p), 0.0)) * cpair_ref[0]
            kq = _dot_nt(jnp.concatenate([k, d['q'][:, ps]], axis=0), bd(k))
            yield
            m = kq[0:CHUNK] * dec * d['beta'][:, ps] * cpair_ref[1]
            d['aq', p] = kq[CHUNK:2 * CHUNK] * dec
            acc = cpair_ref[2] - m
            mp = mm_pair(m, m)
            yield
            for _ in range(nlev - 2):
                acc_add = mm_pair(acc, mp)
                mp = mm_pair(mp, mp)
                yield
                acc = acc + acc_add
            acc = acc + mm_pair(acc, mp)
            yield
            d['uw', p] = _dot(acc, jnp.concatenate([bd(d['vb'][:, ps]), bd(d['kbe'][:, ps])], axis=1))
            yield

        def hgrn_a(c, p):
            d, ps = data[c], pairs[p]
            qd, kd = d['qd'][:, ps], d['kd'][:, ps]
            amat = jnp.zeros((CHUNK, LANES), F32)
            for lev in range(nlev):
                z = jnp.exp(d['xlev'][lev][:, ps])
                amat = amat + _dot_nt(qd * z, bd(kd * z)) * cpair_ref[3 + lev]
                if lev % 2 == 1:
                    yield
            d['amat', p] = amat
            d['dcol', p] = jnp.exp(_dot_tn(d['g16'][:, ps], ones_ref[...]))
            yield

        def ssd_b():
            for c in range(nch):
                d = data[c]
                ya = []
                for p, ps in enumerate(pairs):
                    s_pk = ssd_sc[p]
                    ya.append(mm_pair(d['cbl', p], d['xdt'][:, ps]) + _dot(d['cg', p], s_pk) * d['eacum'][:, ps])
                    ssd_sc[p] = s_pk * jnp.exp(d['alast'][:, ps]) + _dot_tn(d['bg', p], d['xdt_end'][:, ps])
                yield
                ya = jnp.concatenate(ya, axis=-1) + row(3, D_BRANCH) * act_a(d['rs'], 0, 256)
                ss = head_sum(ya * ya)
                yield
                mix_ref[d['rs'], 0:256] = rms_finish(ya, ss, row(4, D_BRANCH), ua_ref[d['rs'], 0:256])

        def gdn_b():
            bdm = bdm_ref[...]
            for c in range(nch):
                d = data[c]
                ws, s_old = [], []
                for p, ps in enumerate(pairs):
                    s_bd = gdn_sc[p]
                    s_old.append(s_bd)
                    ws.append(_dot(jnp.concatenate([d['uw', p][:, LANES:2 * LANES], d['qe'][:, ps]], axis=0), s_bd))
                yield
                yc = []
                for p, ps in enumerate(pairs):
                    v_new = d['uw', p][:, 0:LANES] - ws[p][0:CHUNK]
                    yc.append(ws[p][CHUNK:2 * CHUNK] + mm_pair(d['aq', p], v_new))
                    gdn_sc[p] = s_old[p] * jnp.exp(d['glast'][:, ps]) + _dot_tn(d['kend'][:, ps], v_new) * bdm
                yield
                yc = jnp.concatenate(yc, axis=-1)
                ss = head_sum(yc * yc)
                yield
                mix_ref[d['rs'], 512:768] = rms_finish(yc, ss, row(10, D_BRANCH), uc_ref[d['rs'], 0:256])

        def hgrn_b():
            bdm = bdm_ref[...]
            for c in range(nch):
                d = data[c]
                yd = []
                for p, ps in enumerate(pairs):
                    s_bd = hg_sc[p]
                    vd = d['vd'][:, ps]
                    yd.append(mm_pair(d['amat', p], vd) + d['diag'][:, ps] * vd + _dot(d['qed'][:, ps], s_bd))
                    hg_sc[p] = s_bd * d['dcol', p] + _dot_tn(d['kend_d'][:, ps], vd) * bdm
                yield
                yd = jnp.concatenate(yd, axis=-1)
                ss = head_sum(yd * yd)
                yield
                mix_ref[d['rs'], 768:1024] = rms_finish(yd, ss, row(14, D_BRANCH), ud_ref[d['rs'], 0:256])

        return ([prologue(c) for c in range(nch)],
                [f(c, p) for c in range(nch) for p in range(PAIRS) for f in (gdn_a, hgrn_a, ssd_a)],
                [gdn_b(), hgrn_b(), ssd_b()])

    def finish():
        @pl.when(j == pl.num_programs(1) - 1)
        def _():
            for p in range(PAIRS):
                for e in range(2):
                    es = slice(e * HEAD_DIM, (e + 1) * HEAD_DIM)
                    o_ssd[2 * p + e] = ssd_sc[p, :, es]
                    o_gdn[2 * p + e] = gdn_sc[p, es, es]
                    o_hg[2 * p + e] = hg_sc[p, es, es]

    return (*prompt_block(), finish)


def _mixer_constants():
    i = np.arange(CHUNK)[:, None]
    jn = np.arange(CHUNK)[None, :]
    tri = (jn <= i).astype(np.float32)
    strict = (jn < i).astype(np.float32)
    eye = np.eye(CHUNK, dtype=np.float32)
    wlev, mlev = [], []
    for lev in range(N_LEVELS):
        b = CHUNK >> (lev + 1)
        blk_i, pos_i = i // (2 * b), i % (2 * b)
        mid = blk_i * 2 * b + b
        upper = pos_i >= b
        w = np.where(upper, (jn >= mid) & (jn <= i), (jn > i) & (jn < mid)).astype(np.float32)
        msk = (((jn // (2 * b)) == blk_i) & upper & ((jn % (2 * b)) < b)).astype(np.float32)
        wlev.append(w)
        mlev.append(msk)
    cmat = np.stack([tri, tri] + wlev[N_LEVELS - MXU_LEVELS:])
    cmask = np.stack([tri, strict, eye] + mlev)
    cpair = np.concatenate([cmask, cmask], axis=-1)
    bdm = np.kron(np.eye(2, dtype=np.float32), np.ones((HEAD_DIM, HEAD_DIM), np.float32))
    lane = np.arange(LANES)[None, :]
    lr = np.stack([np.broadcast_to(lane < HEAD_DIM, (CHUNK, LANES)),
                   np.broadcast_to(lane >= HEAD_DIM, (CHUNK, LANES))]).astype(np.float32)
    e_a = np.zeros((LANES, 512), np.float32)
    e_b = np.zeros((LANES, 512), np.float32)
    blk = np.zeros((256, 256), np.float32)
    sel = np.zeros((2 * SUBLANES, LANES), np.float32)
    for h in range(HEADS):
        hs = slice(h * HEAD_DIM, (h + 1) * HEAD_DIM)
        e_a[h, hs] = 1.0
        e_a[HEADS + h, 256 + h * HEAD_DIM:256 + (h + 1) * HEAD_DIM] = 1.0
        e_b[2 * HEADS + h, hs] = 1.0
        e_b[3 * HEADS + h, 256 + h * HEAD_DIM:256 + (h + 1) * HEAD_DIM] = 1.0
        blk[hs, hs] = 1.0
        sel[h, 2 * HEADS + h] = 1.0
        sel[HEADS + h, 3 * HEADS + h] = 1.0
    ones = np.ones((2 * SUBLANES, LANES), np.float32)
    return (jnp.asarray(cmat, BF16), jnp.asarray(cpair, F32), jnp.asarray(bdm, F32),
            jnp.asarray(lr, BF16), jnp.asarray(e_a, BF16), jnp.asarray(e_b, BF16), jnp.asarray(blk, BF16),
            jnp.asarray(sel, BF16), jnp.asarray(ones, BF16))


def _layer_spec(a, l):
    nd = a.ndim - 1
    return pl.BlockSpec((None,) + a.shape[1:], lambda b, j: (l,) + (0,) * nd)


def _mixer_call(us, l, wts, consts, acc, *, n_seq, rows):
    n_rows = us[0].shape[0]
    t_pad = n_rows // n_seq
    nblk = t_pad // rows
    nq = SEQS_PER_STEP if n_seq % SEQS_PER_STEP == 0 else 1
    us = [u.reshape(n_seq, t_pad, u.shape[1]) for u in us]
    rmap = lambda b, j: (b, j, 0)

    def state_spec(tail):
        return pl.BlockSpec((None, nq) + tail, lambda b, j: (l, b) + (0,) * len(tail))

    layer_consts = [wts[k] for k in ('rows', 'wb_re', 'wb_im', 'c_re', 'c_im', 'glu_w', 's5tab')]
    assert len(layer_consts) + len(consts) == N_MIXER_CONST
    in_specs = [pl.BlockSpec((nq, rows, w), rmap) for w in (WA, WB, WC, WD)]
    in_specs += [_layer_spec(a, l) for a in layer_consts]
    in_specs += [pl.BlockSpec(a.shape, lambda b, j, nd=a.ndim: (0,) * nd) for a in consts]
    inputs = us + layer_consts + list(consts)
    mat = (HEADS, HEAD_DIM, HEAD_DIM)
    s5_tail = (SUBLANES, S5_LANES)
    tails = [mat, s5_tail, s5_tail, mat, mat]
    scratch = [pltpu.VMEM((nq, rows, S5_LANES), F32),
               pltpu.VMEM((nq, rows, S5_LANES), F32),
               pltpu.VMEM((nq, PAIRS, HEAD_DIM, LANES), F32),
               pltpu.VMEM((nq, PAIRS, LANES, LANES), F32),
               pltpu.VMEM((nq, PAIRS, LANES, LANES), F32)]
    assert len(acc) == len(tails) == N_PROMPT_STATES
    n_before_acc = len(inputs)
    inputs += list(acc)
    in_specs += [pl.BlockSpec(memory_space=pl.ANY)] * len(acc)
    out_shape = [jax.ShapeDtypeStruct((n_seq, t_pad, D_MODEL), BF16)]
    out_shape += [jax.ShapeDtypeStruct(a.shape, a.dtype) for a in acc]
    out_specs = [pl.BlockSpec((nq, rows, D_MODEL), rmap)] + [state_spec(t) for t in tails]
    aliases = {n_before_acc + i: 1 + i for i in range(len(acc))}
    return pl.pallas_call(
        functools.partial(_mixer_kernel, rows=rows, nq=nq),
        grid=(n_seq // nq, nblk),
        in_specs=in_specs,
        out_specs=out_specs,
        out_shape=out_shape,
        scratch_shapes=scratch,
        input_output_aliases=aliases,
        compiler_params=pltpu.CompilerParams(dimension_semantics=("arbitrary", "arbitrary"),
                                             vmem_limit_bytes=VMEM_LIMIT),
        name="mixer_prompt",
    )(*inputs)


SAMPLE_COL_WIDTHS = (SSD_XBC, 4 * HEADS, 4 * HEADS, D_BRANCH, D_BRANCH, D_BRANCH, D_BRANCH, GDN_QKV, D_BRANCH,
                     D_BRANCH, D_BRANCH, D_BRANCH, D_BRANCH)
(C_CONVB_A, C_SBIAS, C_SSCALE, C_SSD_D, C_SSD_G, C_S5_D, C_GLU_B, C_CONVB_C, C_GDN_G, C_LOGLB, C_LOG1M, C_ONEM,
 C_HG_G) = [int(v) for v in np.cumsum((0,) + SAMPLE_COL_WIDTHS[:-1])]
U_ZA, U_XBC, U_ZB, U_UB, U_ZC, U_QKV, U_SMALL, U_ZD, U_QD, U_FD, U_ID = (
    0, 256, WA, WA + 256, WA + WB, WA + WB + 256, WA + WB + 1024, WA + WB + WC, WA + WB + WC + 256,
    WA + WB + WC + 512, WA + WB + WC + 768)
V_HALF = HEAD_DIM // 2


def _sample_inproj_kernel(h_ref, wt_ref, o_ref):
    o_ref[...] = lax.dot_general(wt_ref[...], h_ref[...].astype(BF16), (((1,), (1,)), ((), ())),
                                 preferred_element_type=F32)


def _sample_outproj_kernel(mixt_ref, x_ref, w_ref, g_ref, b_ref, o_ref):
    out = lax.dot_general(mixt_ref[...], w_ref[...], (((0,), (0,)), ((), ())), preferred_element_type=F32)
    o_ref[...] = _layernorm_rows(DN_ALPHA * x_ref[...] + out, g_ref[...], b_ref[...])


def _sample_mixer_kernel(ut_ref, cols_ref, cwa_ref, cwc_ref, lb_ref, wbre_ref, wbim_ref, cre_ref, cim_ref, glu_ref,
                         i_ssd, i_ca, i_s5re, i_s5im, i_gdn, i_cc, i_hg, *rest, n_tok, n_seq):
    (mix_ref, o_ssd, o_ca, o_s5re, o_s5im, o_gdn, o_cc, o_hg) = rest[N_STATES:N_STATES + 1 + N_STATES]
    (acta_ref, actc_ref, small_ref, hre_ref, him_ref, gq_ref, gk_ref, hq_ref, hk_ref, hf_ref,
     ya_ref, yc_ref, yd_ref) = rest[N_STATES + 1 + N_STATES:]
    h = pl.program_id(0)
    vh = pl.program_id(1)
    toks = [slice(t * n_seq, (t + 1) * n_seq) for t in range(n_tok)]
    hrow = pl.multiple_of(h * HEAD_DIM, HEAD_DIM)

    def col(c0, n, off=0):
        return cols_ref[pl.ds(c0 + off, n), :]

    @pl.when((h == 0) & (vh == 0))
    def _():
        for u0, width, i_c, o_c, cw_ref, act_ref, cb in ((U_XBC, SSD_XBC, i_ca, o_ca, cwa_ref, acta_ref, C_CONVB_A),
                                                        (U_QKV, GDN_QKV, i_cc, o_cc, cwc_ref, actc_ref, C_CONVB_C)):
            xx = [i_c[i] for i in range(CONV_W - 1)] + [ut_ref[u0:u0 + width, ts] for ts in toks]
            for t in range(n_tok):
                acc = cols_ref[cb:cb + width, :]
                for w in range(CONV_W):
                    acc = acc + cw_ref[w] * xx[t + w]
                act_ref[t] = _silu(acc)
            for i in range(CONV_W - 1):
                o_c[i] = xx[n_tok + i]
        rid = lax.broadcasted_iota(jnp.int32, (4 * HEADS, n_seq), 0)
        for t, ts in enumerate(toks):
            v = ut_ref[U_SMALL:U_SMALL + 4 * HEADS, ts] + cols_ref[C_SBIAS:C_SBIAS + 4 * HEADS, :]
            small_ref[t] = jnp.where((rid >= HEADS) & (rid < 2 * HEADS), _sigmoid(v),
                                     _softplus(v) * cols_ref[C_SSCALE:C_SSCALE + 4 * HEADS, :])
        u_b = ut_ref[U_UB:U_UB + D_BRANCH, :]
        bu_re = _dot(wbre_ref[...], u_b)
        bu_im = _dot(wbim_ref[...], u_b)
        sr, si = i_s5re[...], i_s5im[...]
        lr, li = lb_ref[0], lb_ref[1]
        for t, ts in enumerate(toks):
            sr, si = lr * sr - li * si + bu_re[:, ts], lr * si + li * sr + bu_im[:, ts]
            hre_ref[:, ts] = sr
            him_ref[:, ts] = si
        o_s5re[...] = sr
        o_s5im[...] = si
        y5 = _dot(cre_ref[...], hre_ref[...]) - _dot(cim_ref[...], him_ref[...])
        for ts in toks:
            y = _gelu_tanh(y5[:, ts] + cols_ref[C_S5_D:C_S5_D + D_BRANCH, :] * u_b[:, ts])
            g = _dot(glu_ref[...], y) + cols_ref[C_GLU_B:C_GLU_B + D_BRANCH, :]
            mix_ref[D_BRANCH:2 * D_BRANCH, ts] = (y * _sigmoid(g) * _silu(ut_ref[U_ZB:U_ZB + D_BRANCH, ts])).astype(BF16)

    vrow = pl.multiple_of(vh * V_HALF, V_HALF)
    vsel = pl.ds(hrow + vrow, V_HALF)

    def scal(t, r):
        return small_ref[t, pl.ds(r * HEADS + h, 1), :]

    grp = (h // 2) * HEAD_DIM
    xdt = [acta_ref[t, vsel, :] * scal(t, 0) for t in range(n_tok)]
    ea = [jnp.exp(scal(t, 3)) for t in range(n_tok)]

    def ssd_row(n, ys):
        s = i_ssd[n]
        ys = list(ys)
        for t in range(n_tok):
            s = s * ea[t] + acta_ref[t, pl.ds(256 + grp + n, 1), :] * xdt[t]
            ys[t] = ys[t] + acta_ref[t, pl.ds(384 + grp + n, 1), :] * s
        o_ssd[n] = s
        return tuple(ys)

    zero = jnp.zeros((V_HALF, n_seq), F32)
    ys = lax.fori_loop(0, HEAD_DIM, ssd_row, (zero,) * n_tok, unroll=2)
    for t in range(n_tok):
        ya_ref[t, pl.ds(vrow, V_HALF), :] = ys[t] + col(C_SSD_D, V_HALF, hrow + vrow) * acta_ref[t, vsel, :]

    @pl.when(vh == 0)
    def _():
        for t in range(n_tok):
            q = actc_ref[t, pl.ds(hrow, HEAD_DIM), :]
            k = actc_ref[t, pl.ds(256 + hrow, HEAD_DIM), :]
            gq_ref[t] = q * lax.rsqrt(jnp.sum(q * q, axis=0, keepdims=True) + L2_EPS) * (HEAD_DIM ** -0.5)
            gk_ref[t] = k * lax.rsqrt(jnp.sum(k * k, axis=0, keepdims=True) + L2_EPS)

    for t in range(n_tok):
        alpha = jnp.exp(scal(t, 2))
        src = i_gdn if t == 0 else o_gdn

        def decay_row(kk, pred, alpha=alpha, src=src, t=t):
            s = src[kk] * alpha
            o_gdn[kk] = s
            return pred + gk_ref[t, pl.ds(kk, 1), :] * s

        pred = lax.fori_loop(0, HEAD_DIM, decay_row, zero, unroll=4)
        v_new = scal(t, 1) * (actc_ref[t, pl.ds(512 + hrow + vrow, V_HALF), :] - pred)

        def update_row(kk, o, v_new=v_new, t=t):
            s = o_gdn[kk] + gk_ref[t, pl.ds(kk, 1), :] * v_new
            o_gdn[kk] = s
            return o + gq_ref[t, pl.ds(kk, 1), :] * s

        yc_ref[t, pl.ds(vrow, V_HALF), :] = lax.fori_loop(0, HEAD_DIM, update_row, zero, unroll=4)

    @pl.when(vh == 0)
    def _():
        for t, ts in enumerate(toks):
            f_d = ut_ref[pl.ds(U_FD + hrow, HEAD_DIM), ts]
            t1 = col(C_LOGLB, HEAD_DIM, hrow)
            t2 = col(C_LOG1M, HEAD_DIM, hrow) - _softplus(-f_d)
            logf = jnp.maximum(t1, t2) + _log1p_exp_neg(jnp.abs(t1 - t2))
            hf_ref[t] = jnp.exp(logf)
            hk_ref[t] = col(C_ONEM, HEAD_DIM, hrow) * _sigmoid(-f_d)
            hq_ref[t] = _silu(ut_ref[pl.ds(U_QD + hrow, HEAD_DIM), ts])

    vd = [ut_ref[pl.ds(U_ID + hrow + vrow, V_HALF), ts] for ts in toks]

    def hgrn_row(kk, os_):
        s = i_hg[kk]
        os_ = list(os_)
        for t in range(n_tok):
            s = s * hf_ref[t, pl.ds(kk, 1), :] + hk_ref[t, pl.ds(kk, 1), :] * vd[t]
            os_[t] = os_[t] + hq_ref[t, pl.ds(kk, 1), :] * s
        o_hg[kk] = s
        return tuple(os_)

    os_ = lax.fori_loop(0, HEAD_DIM, hgrn_row, (zero,) * n_tok, unroll=2)
    for t in range(n_tok):
        yd_ref[t, pl.ds(vrow, V_HALF), :] = os_[t]

    @pl.when(vh == HEAD_DIM // V_HALF - 1)
    def _():
        for y_ref, g0, z0, m0 in ((ya_ref, C_SSD_G, U_ZA, 0), (yc_ref, C_GDN_G, U_ZC, 2 * D_BRANCH),
                                  (yd_ref, C_HG_G, U_ZD, 3 * D_BRANCH)):
            for t, ts in enumerate(toks):
                y = y_ref[t]
                ms = jnp.mean(y * y, axis=0, keepdims=True)
                z = ut_ref[pl.ds(z0 + hrow, HEAD_DIM), ts]
                mix_ref[pl.ds(m0 + hrow, HEAD_DIM), ts] = (
                    y * lax.rsqrt(ms + RMS_EPS) * col(g0, HEAD_DIM, hrow) * _silu(z)).astype(BF16)


def _sample_mixer_call(ut, l, sw, states_t, acc, *, n_tok, n_seq):
    consts = [sw[k] for k in ('cols', 'cw_a', 'cw_c', 'lb', 'wb_re', 'wb_im', 'c_re', 'c_im', 'glu_w')]
    whole = lambda a: pl.BlockSpec((None,) + a.shape[1:], lambda h, v, nd=a.ndim - 1: (l,) + (0,) * nd)
    mat_spec = pl.BlockSpec((None, None, HEAD_DIM, V_HALF, n_seq), lambda h, v: (l, h, 0, v, 0))
    state_specs = [mat_spec, whole(states_t[1]), whole(states_t[2]), whole(states_t[3]), mat_spec,
                   whole(states_t[5]), mat_spec]
    n_cols = ut.shape[1]
    in_specs = [pl.BlockSpec(ut.shape, lambda h, v: (0, 0))] + [whole(a) for a in consts] + state_specs
    in_specs += [pl.BlockSpec(memory_space=pl.ANY)] * N_STATES
    inputs = [ut] + consts + list(states_t) + list(acc)
    out_shape = [jax.ShapeDtypeStruct((D_MODEL, n_cols), BF16)] + [jax.ShapeDtypeStruct(a.shape, a.dtype) for a in acc]
    out_specs = [pl.BlockSpec((D_MODEL, n_cols), lambda h, v: (0, 0))] + state_specs
    n_before_acc = 1 + len(consts) + N_STATES
    tok_tile = lambda rows: pltpu.VMEM((n_tok, rows, n_seq), F32)
    scratch = [tok_tile(SSD_XBC), tok_tile(GDN_QKV), tok_tile(4 * HEADS),
               pltpu.VMEM((S5_LANES, n_cols), F32), pltpu.VMEM((S5_LANES, n_cols), F32)]
    scratch += [tok_tile(HEAD_DIM)] * 8
    return pl.pallas_call(
        functools.partial(_sample_mixer_kernel, n_tok=n_tok, n_seq=n_seq),
        grid=(HEADS, HEAD_DIM // V_HALF),
        in_specs=in_specs,
        out_specs=out_specs,
        out_shape=out_shape,
        scratch_shapes=scratch,
        input_output_aliases={n_before_acc + i: 1 + i for i in range(N_STATES)},
        compiler_params=pltpu.CompilerParams(dimension_semantics=("arbitrary", "arbitrary"),
                                             vmem_limit_bytes=VMEM_LIMIT),
        name="mixer_sample",
    )(*inputs)


W_IN_ROWS_PER_STEP = 128


def _regroup_w_in_kernel(w_ref, o_ref, ot_ref):
    w = w_ref[...]
    small = jnp.concatenate([w[:, 768:772], w[:, 2308:2316], w[:, 768:772],
                             jnp.zeros((w.shape[0], LANES - 4 * HEADS), F32)], axis=1)
    wr = jnp.concatenate([w[:, 0:768], w[:, 772:1284], w[:, 1284:2308], small, w[:, 2316:3340]], axis=1)
    o_ref[...] = wr.astype(BF16)
    ot_ref[...] = wr.T.astype(BF16)


def _regroup_w_in(w_in):
    depth, d_model, d_in = w_in.shape
    tk = W_IN_ROWS_PER_STEP
    w_cols = WA + WB + WC + WD
    return pl.pallas_call(
        _regroup_w_in_kernel,
        grid=(depth, d_model // tk),
        in_specs=[pl.BlockSpec((None, tk, d_in), lambda l, i: (l, i, 0))],
        out_specs=[pl.BlockSpec((None, tk, w_cols), lambda l, i: (l, i, 0)),
                   pl.BlockSpec((None, w_cols, tk), lambda l, i: (l, 0, i))],
        out_shape=[jax.ShapeDtypeStruct((depth, d_model, w_cols), BF16),
                   jax.ShapeDtypeStruct((depth, w_cols, d_model), BF16)],
        compiler_params=pltpu.CompilerParams(dimension_semantics=("arbitrary", "arbitrary"),
                                             vmem_limit_bytes=VMEM_LIMIT),
        name="regroup_w_in",
    )(w_in.astype(F32))


def _pad_rows(v, width=ROW_W):
    v = v.astype(F32).reshape(v.shape[0], -1)
    return jnp.pad(v, ((0, 0), (0, width - v.shape[1])))


def _stacked_weights(p, lbs):
    depth = lbs.shape[0]
    rep = lambda v: jnp.repeat(v.astype(F32), HEAD_DIM, axis=-1)
    zeros_h = jnp.zeros((depth, HEADS), F32)
    small_bias = jnp.concatenate([p['ssd_dt_bias'], zeros_h, p['gdn_dt_bias'], p['ssd_dt_bias']], axis=1)
    small_scale = jnp.concatenate([jnp.ones((depth, HEADS), F32), zeros_h, -jnp.exp(p['gdn_a_log']),
                                   -jnp.exp(p['ssd_a_log'])], axis=1)
    zrow = jnp.zeros((depth, ROW_W), F32)
    rows = jnp.stack([
        _pad_rows(p['ssd_conv_b']), _pad_rows(small_bias), _pad_rows(small_scale),
        _pad_rows(rep(p['ssd_d'])), _pad_rows(p['ssd_norm_g']), _pad_rows(p['s5_d']),
        _pad_rows(p['s5_glu_b']), _pad_rows(p['gdn_conv_b']), zrow, zrow,
        _pad_rows(p['gdn_norm_g']), _pad_rows(jnp.log(lbs)), _pad_rows(jnp.log1p(-lbs)), _pad_rows(1.0 - lbs),
        _pad_rows(p['hg_norm_g']), zrow], axis=1)

    pad_cw = lambda cw: jnp.pad(cw.astype(F32), ((0, 0), (0, SUBLANES - CONV_W), (0, 0)))

    lam_re, lam_im = p['s5_lam_re'].astype(F32), p['s5_lam_im'].astype(F32)
    dt = jnp.exp(p['s5_log_dt'].astype(F32))[..., None]
    mag = jnp.exp(lam_re * dt)
    ang = lam_im * dt
    lb_re, lb_im = mag * jnp.cos(ang), mag * jnp.sin(ang)
    den = jnp.square(lam_re) + jnp.square(lam_im)
    nr = lb_re - 1.0
    coef_re = (nr * lam_re + lb_im * lam_im) / den
    coef_im = (lb_im * lam_re - nr * lam_im) / den
    b_re, b_im = p['s5_b_re'].astype(F32), p['s5_b_im'].astype(F32)
    bb_re = coef_re[..., None] * b_re - coef_im[..., None] * b_im
    bb_im = coef_re[..., None] * b_im + coef_im[..., None] * b_re
    eye_g = jnp.eye(S5_GROUPS, dtype=F32)
    bd_in = lambda bb: jnp.einsum('lgnq,gh->lgqhn', bb, eye_g).reshape(depth, D_BRANCH, S5_LANES).astype(BF16)
    bd_out = lambda c: jnp.einsum('lgqn,gh->lgnhq', c.astype(F32), eye_g).reshape(depth, S5_LANES, D_BRANCH).astype(BF16)

    pr, pi = [lb_re.reshape(depth, -1)], [lb_im.reshape(depth, -1)]
    for _ in range(SUBLANES - 1):
        pr, pi = (pr + [pr[-1] * pr[0] - pi[-1] * pi[0]], pi + [pr[-1] * pi[0] + pi[-1] * pr[0]])
    pw = jnp.stack([jnp.concatenate([a, b], axis=-1) for a, b in zip(pr, pi)], axis=1)
    ridx = jnp.arange(SUBLANES)[None, :, None]
    tabs = [jnp.where(ridx >= d, pw[:, d - 1:d, :], 0.0) for d in (1, 2, 4)] + [pw]
    s5tab = jnp.stack(tabs, axis=1).astype(F32)

    lanes = lambda v: jnp.broadcast_to(v.astype(F32)[..., None], v.shape + (LANES,))
    col_vecs = [p['ssd_conv_b'], small_bias, small_scale, rep(p['ssd_d']), p['ssd_norm_g'], p['s5_d'], p['s5_glu_b'],
                p['gdn_conv_b'], p['gdn_norm_g'], jnp.log(lbs), jnp.log1p(-lbs), 1.0 - lbs, p['hg_norm_g']]
    assert [int(np.prod(v.shape[1:])) for v in col_vecs] == list(SAMPLE_COL_WIDTHS)
    cols_s = lanes(jnp.concatenate([v.astype(F32).reshape(depth, -1) for v in col_vecs], axis=1))
    bd_in_t = lambda bb: jnp.einsum('lgnq,gh->lgnhq', bb, eye_g).reshape(depth, S5_LANES, D_BRANCH).astype(BF16)
    bd_out_t = lambda c: jnp.einsum('lgqn,gh->lgqhn', c.astype(F32), eye_g).reshape(depth, D_BRANCH, S5_LANES).astype(BF16)
    sample = dict(cols=cols_s, cw_a=lanes(p['ssd_conv_w']), cw_c=lanes(p['gdn_conv_w']),
                  lb=lanes(jnp.stack([lb_re.reshape(depth, -1), lb_im.reshape(depth, -1)], axis=1)),
                  wb_re=bd_in_t(bb_re), wb_im=bd_in_t(bb_im), c_re=bd_out_t(p['s5_c_re']), c_im=bd_out_t(p['s5_c_im']),
                  glu_w=jnp.swapaxes(p['s5_glu_w'], 1, 2).astype(BF16))

    w_in_r, w_in_t = _regroup_w_in(p['w_in'])
    return dict(w_in=w_in_r, w_in_t=w_in_t, rows=rows, cw_a=pad_cw(p['ssd_conv_w']), cw_c=pad_cw(p['gdn_conv_w']),
                wb_re=bd_in(bb_re), wb_im=bd_in(bb_im), c_re=bd_out(p['s5_c_re']), c_im=bd_out(p['s5_c_im']),
                glu_w=p['s5_glu_w'].astype(BF16), s5tab=s5tab, w_out=p['w_out'].astype(BF16),
                ln_g=p['ln_g'].astype(F32)[:, None, :], ln_b=p['ln_b'].astype(F32)[:, None, :], sample=sample)


def _pick_tile(n_rows, candidates):
    for t in candidates:
        if n_rows % t == 0:
            return t
    raise ValueError(f"no row tile for {n_rows}")


def kernel(x_prompt, x_sample, state_ssd, state_ssd_conv, state_s5_re, state_s5_im, state_gdn, state_gdn_conv, state_hgrn, meta_tokens, ln_in_g, ln_in_b, w_in, ssd_conv_w, ssd_conv_b, ssd_dt_bias, ssd_a_log, ssd_d, ssd_norm_g, s5_lam_re, s5_lam_im, s5_log_dt, s5_b_re, s5_b_im, s5_c_re, s5_c_im, s5_d, s5_glu_w, s5_glu_b, gdn_conv_w, gdn_conv_b, gdn_a_log, gdn_dt_bias, gdn_norm_g, hg_lb_raw, hg_norm_g, w_out, ln_g, ln_b):
    p = dict(w_in=w_in, ssd_conv_w=ssd_conv_w, ssd_conv_b=ssd_conv_b, ssd_dt_bias=ssd_dt_bias,
             ssd_a_log=ssd_a_log, ssd_d=ssd_d, ssd_norm_g=ssd_norm_g, s5_lam_re=s5_lam_re,
             s5_lam_im=s5_lam_im, s5_log_dt=s5_log_dt, s5_b_re=s5_b_re, s5_b_im=s5_b_im, s5_c_re=s5_c_re,
             s5_c_im=s5_c_im, s5_d=s5_d, s5_glu_w=s5_glu_w, s5_glu_b=s5_glu_b, gdn_conv_w=gdn_conv_w,
             gdn_conv_b=gdn_conv_b, gdn_a_log=gdn_a_log, gdn_dt_bias=gdn_dt_bias, gdn_norm_g=gdn_norm_g,
             hg_norm_g=hg_norm_g, w_out=w_out, ln_g=ln_g, ln_b=ln_b)
    bp, seq, _ = x_prompt.shape
    bs, dseq, _ = x_sample.shape
    assert dseq >= CONV_W - 1 and bs % LANES == 0
    t_pad = PAD_FRONT + N_META + seq
    rows_p = 3 * CHUNK
    assert t_pad % rows_p == 0
    n_p = bp * t_pad
    n_s = bs * dseq

    soft = jax.nn.softmax(hg_lb_raw.astype(F32), axis=0)
    csum = jnp.cumsum(soft, axis=0)
    lbs = csum - csum[0]

    xs = jnp.swapaxes(x_sample.astype(F32), 0, 1).reshape(n_s, D_MODEL)

    tm_p = _pick_tile(n_p, (704, 512, 384, 192))
    tm_s = n_s
    g_in, b_in = ln_in_g.astype(F32)[None], ln_in_b.astype(F32)[None]
    hp = pl.pallas_call(
        _ln_in_prompt_kernel,
        grid=(bp,),
        in_specs=[pl.BlockSpec((None, seq, D_MODEL), lambda b: (b, 0, 0)),
                  pl.BlockSpec((N_META, D_MODEL), lambda b: (0, 0)),
                  pl.BlockSpec((1, D_MODEL), lambda b: (0, 0)), pl.BlockSpec((1, D_MODEL), lambda b: (0, 0))],
        out_specs=pl.BlockSpec((None, t_pad, D_MODEL), lambda b: (b, 0, 0)),
        out_shape=jax.ShapeDtypeStruct((bp, t_pad, D_MODEL), F32),
        compiler_params=pltpu.CompilerParams(dimension_semantics=("arbitrary",), vmem_limit_bytes=VMEM_LIMIT),
        name="ln_in_prompt",
    )(x_prompt.astype(F32), meta_tokens.astype(F32), g_in, b_in).reshape(n_p, D_MODEL)
    hs = _row_tiled_call(_ln_in_kernel, n_s, tm_s, [xs], [g_in, b_in], [D_MODEL], [F32], "ln_in_sample")

    consts_p = _mixer_constants()
    depth = w_in.shape[0]
    wts = _stacked_weights(p, lbs)

    mat = (HEADS, HEAD_DIM, HEAD_DIM)
    zeros = lambda *shape: jnp.zeros((depth,) + shape, F32)
    acc_p = [zeros(bp, *mat), zeros(bp, SUBLANES, S5_LANES), zeros(bp, SUBLANES, S5_LANES), zeros(bp, *mat),
             zeros(bp, *mat)]
    conv_tails = []
    seq_last = lambda a: jnp.moveaxis(a.astype(F32), 1, -1)
    st_in = (seq_last(state_ssd), seq_last(state_ssd_conv),
             seq_last(state_s5_re).reshape(depth, S5_LANES, bs), seq_last(state_s5_im).reshape(depth, S5_LANES, bs),
             seq_last(state_gdn), seq_last(state_gdn_conv), seq_last(state_hgrn))
    acc_s = [jnp.zeros(a.shape, F32) for a in st_in]
    out_consts = [wts['w_out'], wts['ln_g'], wts['ln_b']]
    w_cols = WA + WB + WC + WD
    for l in range(depth):
        us, tails = _inproj_conv_call(hp, l, wts, bp, tm_p)
        conv_tails.append(tails[:, SUBLANES - (CONV_W - 1):, :])
        mix, *acc_p = _mixer_call(us, l, wts, consts_p, acc_p, n_seq=bp, rows=rows_p)
        skip = PAD_FRONT + N_META if l + 1 == depth else 0
        seq_rows = lambda b: (b, 0, 0)
        layer_blk = lambda a: pl.BlockSpec((None,) + a.shape[1:], lambda b, nd=a.ndim - 1: (l,) + (0,) * nd)
        hp = pl.pallas_call(
            functools.partial(_outproj_seq_kernel, step=rows_p, skip=skip),
            grid=(bp,),
            in_specs=[pl.BlockSpec((None, t_pad, D_MODEL), seq_rows), pl.BlockSpec((None, t_pad, D_MODEL), seq_rows)]
            + [layer_blk(a) for a in out_consts],
            out_specs=pl.BlockSpec((None, t_pad - skip, D_MODEL), seq_rows),
            out_shape=jax.ShapeDtypeStruct((bp, t_pad - skip, D_MODEL), F32),
            compiler_params=pltpu.CompilerParams(dimension_semantics=("arbitrary",), vmem_limit_bytes=VMEM_LIMIT),
            name="outproj_prompt",
        )(mix, hp.reshape(bp, t_pad, D_MODEL), *out_consts)
        if skip:
            y_prompt = hp
        else:
            hp = hp.reshape(n_p, D_MODEL)

        n_wblk = 6
        ut = pl.pallas_call(
            _sample_inproj_kernel,
            grid=(n_wblk,),
            in_specs=[pl.BlockSpec((n_s, D_MODEL), lambda i: (0, 0)),
                      pl.BlockSpec((None, w_cols // n_wblk, D_MODEL), lambda i: (l, i, 0))],
            out_specs=pl.BlockSpec((w_cols // n_wblk, n_s), lambda i: (i, 0)),
            out_shape=jax.ShapeDtypeStruct((w_cols, n_s), F32),
            compiler_params=pltpu.CompilerParams(dimension_semantics=("arbitrary",), vmem_limit_bytes=VMEM_LIMIT),
            name="inproj_sample",
        )(hs, wts['w_in_t'])
        mixt, *acc_s = _sample_mixer_call(ut, l, wts['sample'], st_in, acc_s, n_tok=dseq, n_seq=bs)
        hs = pl.pallas_call(
            _sample_outproj_kernel,
            grid=(1,),
            in_specs=[pl.BlockSpec((D_MODEL, n_s), lambda i: (0, 0)), pl.BlockSpec((n_s, D_MODEL), lambda i: (0, 0)),
                      pl.BlockSpec((None, D_MODEL, D_MODEL), lambda i: (l, 0, 0)),
                      pl.BlockSpec((None, 1, D_MODEL), lambda i: (l, 0, 0)),
                      pl.BlockSpec((None, 1, D_MODEL), lambda i: (l, 0, 0))],
            out_specs=pl.BlockSpec((n_s, D_MODEL), lambda i: (0, 0)),
            out_shape=jax.ShapeDtypeStruct((n_s, D_MODEL), F32),
            compiler_params=pltpu.CompilerParams(dimension_semantics=("arbitrary",), vmem_limit_bytes=VMEM_LIMIT),
            name="outproj_sample",
        )(mixt, hs, *out_consts)

    y_sample = jnp.swapaxes(hs.reshape(dseq, bs, D_MODEL), 0, 1)
    s5_shape = lambda a: a.reshape(depth, -1, S5_GROUPS, S5_STATE)
    p_ssd, p_re, p_im, p_gdn, p_hg = acc_p
    conv_tails = jnp.stack(conv_tails)
    p_ca, p_cc = conv_tails[..., 0:SSD_XBC], conv_tails[..., SSD_XBC:SSD_XBC + GDN_QKV]
    seq_first = lambda a: jnp.moveaxis(a, -1, 1)
    s_ssd, s_ca, s_re, s_im, s_gdn, s_cc, s_hg = [seq_first(a) for a in acc_s]
    return (y_prompt, y_sample, p_ssd, p_ca, s5_shape(p_re[:, :, 0]), s5_shape(p_im[:, :, 0]), p_gdn, p_cc, p_hg,
            s_ssd, s_ca, s5_shape(s_re), s5_shape(s_im), s_gdn, s_cc, s_hg)
```

```python
import functools
import math

import numpy as np
import jax
import jax.numpy as jnp
from jax import lax
from jax.experimental import pallas as pl
from jax.experimental.pallas import tpu as pltpu

F32 = jnp.float32
BF16 = jnp.bfloat16

D_MODEL = 1024
DEPTH = 4
N_META = 16
D_BRANCH = 256
HEADS = 4
PAIRS = HEADS // 2
HEAD_DIM = 64
SSD_XBC = 512
S5_GROUPS = 16
S5_STATE = 64
S5_LANES = S5_GROUPS * S5_STATE
GDN_QKV = 768
CONV_W = 4
DN_ALPHA = (2 * DEPTH) ** 0.25
LN_EPS = 1e-5
RMS_EPS = 1e-6
L2_EPS = 1e-6

CHUNK = 64
SUBLANES = 8
LANES = 128
PAD_FRONT = CHUNK - N_META
WA, WB, WC, WD = 768, 512, 1152, 1024
NROWS = 16
ROW_W = 768
VMEM_LIMIT = 56 * 1024 * 1024


def _dot(a, b):
    return jnp.dot(a.astype(BF16), b.astype(BF16), preferred_element_type=F32)


def _dot_nt(a, b):
    return lax.dot_general(a.astype(BF16), b.astype(BF16), (((1,), (1,)), ((), ())),
                           preferred_element_type=F32)


def _dot_tn(a, b):
    return lax.dot_general(a.astype(BF16), b.astype(BF16), (((0,), (0,)), ((), ())),
                           preferred_element_type=F32)


def _split(x, pieces):
    out = []
    r = x
    for i in range(pieces):
        xi = r.astype(BF16)
        out.append(xi)
        if i + 1 < pieces:
            r = r - xi.astype(F32)
    return out


def _dot01_l(w01, x, pieces=3):
    n = x.shape[-1]
    r = jnp.dot(w01, jnp.concatenate(_split(x, pieces), axis=-1), preferred_element_type=F32)
    return sum(r[:, i * n:(i + 1) * n] for i in range(pieces))


def _dot01_r(x, w01, pieces=3):
    m = x.shape[0]
    r = jnp.dot(jnp.concatenate(_split(x, pieces), axis=0), w01, preferred_element_type=F32)
    return sum(r[i * m:(i + 1) * m] for i in range(pieces))


def _sigmoid(x):
    return 1.0 / (1.0 + jnp.exp(-x))


def _silu(x):
    return x * _sigmoid(x)


def _log1p_exp_neg(a):
    e = jnp.exp(-a)
    u = 1.0 + e
    return jnp.log(u) - ((u - 1.0) - e) / u


def _softplus(x):
    return jnp.maximum(x, 0.0) + _log1p_exp_neg(jnp.abs(x))


def _gelu_tanh(x):
    c = math.sqrt(2.0 / math.pi)
    return 0.5 * x * (1.0 + jnp.tanh(c * (x + 0.044715 * (x * x * x))))


def _run_interleaved(chains):
    chains = list(chains)
    while chains:
        alive = []
        for ch in chains:
            try:
                next(ch)
                alive.append(ch)
            except StopIteration:
                pass
        chains = alive


def _layernorm_rows(r, g, b):
    mu = jnp.mean(r, axis=-1, keepdims=True)
    c = r - mu
    var = jnp.mean(c * c, axis=-1, keepdims=True)
    return c * lax.rsqrt(var + LN_EPS) * g + b


def _ln_in_kernel(x_ref, g_ref, b_ref, o_ref):
    o_ref[...] = _layernorm_rows(x_ref[...], g_ref[...], b_ref[...])


def _ln_in_prompt_kernel(x_ref, meta_ref, g_ref, b_ref, o_ref):
    o_ref[0:PAD_FRONT, :] = jnp.zeros((PAD_FRONT, D_MODEL), F32)
    o_ref[PAD_FRONT:CHUNK, :] = _layernorm_rows(meta_ref[...], g_ref[...], b_ref[...])
    seq = x_ref.shape[0]
    step = math.gcd(seq, 512)
    for r0 in range(0, seq, step):
        o_ref[CHUNK + r0:CHUNK + r0 + step, :] = _layernorm_rows(x_ref[r0:r0 + step, :], g_ref[...], b_ref[...])


def _inproj_conv_kernel(x_ref, w_ref, rows_ref, cwa_ref, cwc_ref, oa_ref, ob_ref, oc_ref, od_ref, tail_ref,
                        xx_ref, pp_ref, *, tiles_per_seq):
    tm = x_ref.shape[0]
    t = pl.program_id(0) % tiles_per_seq
    x = x_ref[...].astype(BF16)

    @pl.when(t == 0)
    def _():
        xx_ref[0:SUBLANES, :] = jnp.zeros((SUBLANES, SSD_XBC + GDN_QKV), F32)
        pp_ref[0:SUBLANES, :] = jnp.zeros((SUBLANES, SSD_XBC + GDN_QKV), F32)

    first_valid = jnp.where(t == 0, PAD_FRONT, 0)

    def conv(raw, c0, width, cw_ref, bias_row, o_ref):
        assert CONV_W == 4
        cols = slice(c0, c0 + width)
        r = lax.broadcasted_iota(jnp.int32, (tm, width), 0)
        raw = jnp.where(r >= first_valid, raw, 0.0)
        xx_ref[SUBLANES:SUBLANES + tm, cols] = raw
        x1 = xx_ref[SUBLANES - 1:SUBLANES - 1 + tm, cols]
        p = cw_ref[1:2, :] * raw + cw_ref[0:1, :] * x1
        pp_ref[SUBLANES:SUBLANES + tm, cols] = p
        acc = (rows_ref[bias_row:bias_row + 1, 0:width] + cw_ref[3:4, :] * raw + cw_ref[2:3, :] * x1
               + pp_ref[SUBLANES - 2:SUBLANES - 2 + tm, cols])
        o_ref[:, 256:256 + width] = _silu(acc)
        tail = raw[tm - SUBLANES:tm, :]
        tail_ref[:, cols] = tail
        xx_ref[0:SUBLANES, cols] = tail
        pp_ref[0:SUBLANES, cols] = p[tm - SUBLANES:tm, :]

    ua = jnp.dot(x, w_ref[:, 0:WA], preferred_element_type=F32)
    oa_ref[:, 0:256] = ua[:, 0:256]
    uc = jnp.dot(x, w_ref[:, WA + WB:WA + WB + WC], preferred_element_type=F32)
    conv(ua[:, 256:WA], 0, SSD_XBC, cwa_ref, 0, oa_ref)
    oc_ref[:, 0:256] = uc[:, 0:256]
    oc_ref[:, 1024:WC] = uc[:, 1024:WC]
    ob_ref[...] = jnp.dot(x, w_ref[:, WA:WA + WB], preferred_element_type=F32)
    conv(uc[:, 256:1024], SSD_XBC, GDN_QKV, cwc_ref, 7, oc_ref)
    od_ref[...] = jnp.dot(x, w_ref[:, WA + WB + WC:WA + WB + WC + WD], preferred_element_type=F32)


def _inproj_conv_call(hp, l, wts, n_seq, tm):
    n_rows = hp.shape[0]
    tiles_per_seq = n_rows // n_seq // tm
    consts = [wts['w_in'], wts['rows'], wts['cw_a'], wts['cw_c']]
    in_specs = [pl.BlockSpec((tm, D_MODEL), lambda i: (i, 0))]
    in_specs += [pl.BlockSpec((None,) + a.shape[1:], lambda i, nd=a.ndim - 1: (l,) + (0,) * nd) for a in consts]
    widths = (WA, WB, WC, WD)
    out_specs = [pl.BlockSpec((tm, w), lambda i: (i, 0)) for w in widths]
    out_specs += [pl.BlockSpec((None, SUBLANES, SSD_XBC + GDN_QKV), lambda i: (i // tiles_per_seq, 0, 0))]
    out_shape = [jax.ShapeDtypeStruct((n_rows, w), F32) for w in widths]
    out_shape += [jax.ShapeDtypeStruct((n_seq, SUBLANES, SSD_XBC + GDN_QKV), F32)]
    *us, tails = pl.pallas_call(
        functools.partial(_inproj_conv_kernel, tiles_per_seq=tiles_per_seq),
        grid=(n_rows // tm,),
        in_specs=in_specs,
        out_specs=out_specs,
        out_shape=out_shape,
        scratch_shapes=[pltpu.VMEM((SUBLANES + tm, SSD_XBC + GDN_QKV), F32)] * 2,
        compiler_params=pltpu.CompilerParams(dimension_semantics=("arbitrary",),
                                             vmem_limit_bytes=VMEM_LIMIT),
        name="inproj_conv_prompt",
    )(hp, *consts)
    return us, tails


def _outproj_seq_kernel(mix_ref, x_ref, w_ref, g_ref, b_ref, o_ref, *, step, skip):
    t_pad = mix_ref.shape[0]
    assert 0 <= skip < step

    def product(i):
        return jnp.dot(mix_ref[i * step:(i + 1) * step, :], w_ref[...], preferred_element_type=F32)

    def finish(i, out):
        res = _layernorm_rows(DN_ALPHA * x_ref[i * step:(i + 1) * step, :] + out, g_ref[...], b_ref[...])
        if i == 0:
            o_ref[0:step - skip, :] = res[skip:step, :]
        else:
            o_ref[i * step - skip:(i + 1) * step - skip, :] = res

    prev = product(0)
    for i in range(1, t_pad // step):
        cur = product(i)
        finish(i - 1, prev)
        prev = cur
    finish(t_pad // step - 1, prev)


def _row_tiled_call(kernel, n_rows, tm, row_inputs, const_inputs, out_widths, out_dtypes, name, layer=None):
    in_specs = [pl.BlockSpec((tm, a.shape[1]), lambda i: (i, 0)) for a in row_inputs]
    if layer is None:
        in_specs += [pl.BlockSpec(a.shape, lambda i, nd=a.ndim: (0,) * nd) for a in const_inputs]
    else:
        in_specs += [pl.BlockSpec((None,) + a.shape[1:], lambda i, nd=a.ndim - 1: (layer,) + (0,) * nd)
                     for a in const_inputs]
    out_specs = [pl.BlockSpec((tm, w), lambda i: (i, 0)) for w in out_widths]
    out_shape = [jax.ShapeDtypeStruct((n_rows, w), dt) for w, dt in zip(out_widths, out_dtypes)]
    single = len(out_widths) == 1
    return pl.pallas_call(
        kernel,
        grid=(n_rows // tm,),
        in_specs=in_specs,
        out_specs=out_specs[0] if single else out_specs,
        out_shape=out_shape[0] if single else out_shape,
        compiler_params=pltpu.CompilerParams(dimension_semantics=("arbitrary",),
                                             vmem_limit_bytes=VMEM_LIMIT),
        name=name,
    )(*row_inputs, *const_inputs)


N_MIXER_U = 3
N_MIXER_CONST = 10
N_STATES = 7
N_MIXER_STATES = 3
N_LEVELS = 6
MXU_LEVELS = 2
SEQS_PER_STEP = 2


def _mixer_kernel(*refs, rows, nq):
    n_in = N_MIXER_U + N_MIXER_CONST + N_MIXER_STATES
    consts = refs[N_MIXER_U:N_MIXER_U + N_MIXER_CONST]
    j = pl.program_id(1)
    stages = [[] for _ in range(nq + 2)]
    finishers = []
    for s in range(nq):
        seq_refs = [r.at[s] for r in refs[:N_MIXER_U] + refs[n_in:]]
        prologue, state_free, carried, finish = _mixer_sequence(seq_refs[:N_MIXER_U], consts, seq_refs[N_MIXER_U:],
                                                                j, rows)
        stages[s] += prologue
        stages[s + 1] += state_free
        stages[s + 2] += carried
        finishers.append(finish)
    for chains in stages:
        _run_interleaved(chains)
    for finish in finishers:
        finish()


def _mixer_sequence(u_refs, consts, out_refs, j, rows):
    nch = rows // CHUNK
    nlev = N_LEVELS
    (ua_ref, uc_ref, ud_ref) = u_refs
    (rows_ref, cmat_ref, cpair_ref, bdm_ref, lr_ref, ea_ref, eb_ref, blk_ref, sel_ref, ones_ref) = consts
    (mix_ref, o_ssd, o_gdn, o_hg, ssd_sc, gdn_sc, hg_sc) = out_refs

    @pl.when(j == 0)
    def _():
        ssd_sc[...] = jnp.zeros_like(ssd_sc)
        gdn_sc[...] = jnp.zeros_like(gdn_sc)
        hg_sc[...] = jnp.zeros_like(hg_sc)

    mix_ref[:, 256:512] = jnp.zeros((rows, D_BRANCH), BF16)
    first_valid = jnp.where(j == 0, PAD_FRONT, 0)

    def valid_rows(r):
        return r >= first_valid

    def row(i, w):
        return rows_ref[i:i + 1, 0:w]

    act_a = lambda rs, lo, hi: ua_ref[rs, 256 + lo:256 + hi]
    act_c = lambda rs, lo, hi: uc_ref[rs, 256 + lo:256 + hi]

    tri_b = cmat_ref[0]

    def head_sum(x):
        return jnp.dot(x.astype(BF16), blk_ref[...], preferred_element_type=F32)

    def rms_finish(y, ss, g_row, z):
        return (y * lax.rsqrt(ss * (1.0 / HEAD_DIM) + RMS_EPS) * g_row * _silu(z)).astype(BF16)

    def hgrn_inputs(rs, valid_w):
        f_d = ud_ref[rs, 512:768]
        lsig = -_softplus(-f_d)
        t1 = row(11, D_BRANCH)
        t2 = row(12, D_BRANCH) + lsig
        logf = jnp.maximum(t1, t2) + _log1p_exp_neg(jnp.abs(t1 - t2))
        logf = jnp.where(valid_w, logf, 0.0)
        kd = row(13, D_BRANCH) * _sigmoid(-f_d)
        qd = _silu(ud_ref[rs, 256:512])
        vd = jnp.where(valid_w, ud_ref[rs, 768:1024], 0.0)
        xmm = _dot01_l(cmat_ref[1:2 + MXU_LEVELS].reshape((1 + MXU_LEVELS) * CHUNK, CHUNK), logf)
        gcd = xmm[0:CHUNK, :]
        xlev = []
        for lev in range(nlev - MXU_LEVELS):
            half = CHUNK >> (lev + 1)
            g3 = gcd.reshape(CHUNK // (2 * half), 2 * half, D_BRANCH)
            bound = jnp.broadcast_to(g3[:, half - 1:half, :], g3.shape).reshape(CHUNK, D_BRANCH)
            xlev.append(-jnp.abs(gcd - bound))
        xlev += [xmm[(1 + i) * CHUNK:(2 + i) * CHUNK, :] for i in range(MXU_LEVELS)]
        return logf, kd, qd, vd, gcd, xlev

    def prompt_block():
        lr0, lr1 = lr_ref[0], lr_ref[1]
        pairs = [slice(p * LANES, (p + 1) * LANES) for p in range(PAIRS)]
        data = [dict() for _ in range(nch)]

        def bd(x):
            xb = x.astype(BF16)
            return jnp.concatenate([xb * lr0, xb * lr1], axis=0)

        def mm_pair(a, b):
            return _dot(a, bd(b))

        def prologue(c):
            d = data[c]
            rs = slice(c * CHUNK, (c + 1) * CHUNK)
            d['rs'] = rs
            rr = lax.broadcasted_iota(jnp.int32, (CHUNK, LANES), 0) + c * CHUNK
            lane = lax.broadcasted_iota(jnp.int32, (CHUNK, LANES), 1)
            valid = valid_rows(rr)
            t = uc_ref[rs, 1024:1152] + row(1, LANES)
            vals = jnp.where((lane >= HEADS) & (lane < 2 * HEADS), _sigmoid(t), _softplus(t) * row(2, LANES))
            vals = jnp.where(valid, vals, 0.0)
            cs = _dot01_l(tri_b, vals)
            ex_v = _dot01_r(vals, ea_ref[...], pieces=2)
            q = act_c(rs, 0, 256)
            k = act_c(rs, 256, 512)
            qss = head_sum(q * q)
            kss = head_sum(k * k)
            valid_w = jnp.concatenate([valid, valid], axis=-1)
            logf, kd, qd, vd, gcd, xlev = hgrn_inputs(rs, valid_w)
            d.update(kd=kd, qd=qd, vd=vd)
            yield
            cs_pieces = _split(cs, 3)
            ex_c_all = jnp.dot(jnp.concatenate(cs_pieces, axis=0), eb_ref[...], preferred_element_type=F32)
            ex_c = sum(ex_c_all[i * CHUNK:(i + 1) * CHUNK] for i in range(3))
            crow = sum(lax.dot_general(sel_ref[...], piece, (((1,), (1,)), ((), ())), preferred_element_type=F32)
                       for piece in cs_pieces)
            d['diag'] = head_sum(qd * kd)
            yield
            dt_full, beta_full = ex_v[:, 0:256], ex_v[:, 256:512]
            gc, acum = ex_c[:, 0:256], ex_c[:, 256:512]
            d['prow'] = lambda r: jnp.concatenate([crow[r:r + 1, :], crow[r + 1:r + 2, :]], axis=1)
            xs = act_a(rs, 0, 256)
            alast = acum[CHUNK - 1:CHUNK, :]
            xdt = xs * dt_full
            d.update(acum=acum, alast=alast, xdt=xdt, xdt_end=xdt * jnp.exp(alast - acum), eacum=jnp.exp(acum))
            v = act_c(rs, 512, 768)
            q = q * lax.rsqrt(qss + L2_EPS) * (HEAD_DIM ** -0.5)
            k = k * lax.rsqrt(kss + L2_EPS)
            glast = gc[CHUNK - 1:CHUNK, :]
            egc = jnp.exp(gc)
            d.update(q=q, k=k, gc=gc, glast=glast, beta=beta_full, vb=v * beta_full, kbe=k * beta_full * egc,
                     qe=q * egc, kend=k * jnp.exp(glast - gc))
            gld = gcd[CHUNK - 1:CHUNK, :]
            g1, g2, g3 = [t.astype(F32) for t in _split(gld, 3)]
            rid = lax.broadcasted_iota(jnp.int32, (2 * SUBLANES, D_BRANCH), 0)
            d.update(qed=qd * jnp.exp(gcd), kend_d=kd * jnp.exp(gld - gcd), xlev=xlev,
                     g16=jnp.where(rid == 0, g1, jnp.where(rid == 1, g2, jnp.where(rid == 2, g3, 0.0))))

        def ssd_a(c, p):
            d, ps = data[c], pairs[p]
            g0 = 256 + p * HEAD_DIM
            bg = act_a(d['rs'], g0, g0 + HEAD_DIM)
            cg = act_a(d['rs'], g0 + LANES, g0 + LANES + HEAD_DIM)
            cb2 = _dot_nt(cg, jnp.concatenate([bg, bg], axis=0))
            lm = jnp.exp(jnp.minimum(d['acum'][:, ps] - d['prow'](HEADS + 2 * p), 0.0)) * cpair_ref[0]
            d['cbl', p] = cb2 * lm
            d['bg', p], d['cg', p] = bg, cg
            yield

        def gdn_a(c, p):
            d, ps = data[c], pairs[p]
            k = d['k'][:, ps]
            dec = jnp.exp(jnp.minimum(d['gc'][:, ps] - d['prow'](2 * p), 0.0)) * cpair_ref[0]
            kq = _dot_nt(jnp.concatenate([k, d['q'][:, ps]], axis=0), bd(k))
            yield
            m = kq[0:CHUNK] * dec * d['beta'][:, ps] * cpair_ref[1]
            d['aq', p] = kq[CHUNK:2 * CHUNK] * dec
            acc = cpair_ref[2] - m
            mp = mm_pair(m, m)
            yield
            for _ in range(nlev - 2):
                acc_add = mm_pair(acc, mp)
                mp = mm_pair(mp, mp)
                yield
                acc = acc + acc_add
            acc = acc + mm_pair(acc, mp)
            yield
            d['uw', p] = _dot(acc, jnp.concatenate([bd(d['vb'][:, ps]), bd(d['kbe'][:, ps])], axis=1))
            yield

        def hgrn_a(c, p):
            d, ps = data[c], pairs[p]
            qd, kd = d['qd'][:, ps], d['kd'][:, ps]
            amat = jnp.zeros((CHUNK, LANES), F32)
            for lev in range(nlev):
                z = jnp.exp(d['xlev'][lev][:, ps])
                amat = amat + _dot_nt(qd * z, bd(kd * z)) * cpair_ref[3 + lev]
                if lev % 2 == 1:
                    yield
            d['amat', p] = amat
            d['dcol', p] = jnp.exp(_dot_tn(d['g16'][:, ps], ones_ref[...]))
            yield

        def ssd_b():
            for c in range(nch):
                d = data[c]
                ya = []
                for p, ps in enumerate(pairs):
                    s_pk = ssd_sc[p]
                    ya.append(mm_pair(d['cbl', p], d['xdt'][:, ps]) + _dot(d['cg', p], s_pk) * d['eacum'][:, ps])
                    ssd_sc[p] = s_pk * jnp.exp(d['alast'][:, ps]) + _dot_tn(d['bg', p], d['xdt_end'][:, ps])
                yield
                ya = jnp.concatenate(ya, axis=-1) + row(3, D_BRANCH) * act_a(d['rs'], 0, 256)
                ss = head_sum(ya * ya)
                yield
                mix_ref[d['rs'], 0:256] = rms_finish(ya, ss, row(4, D_BRANCH), ua_ref[d['rs'], 0:256])

        def gdn_b():
            bdm = bdm_ref[...]
            for c in range(nch):
                d = data[c]
                ws, s_old = [], []
                for p, ps in enumerate(pairs):
                    s_bd = gdn_sc[p]
                    s_old.append(s_bd)
                    ws.append(_dot(jnp.concatenate([d['uw', p][:, LANES:2 * LANES], d['qe'][:, ps]], axis=0), s_bd))
                yield
                yc = []
                for p, ps in enumerate(pairs):
                    v_new = d['uw', p][:, 0:LANES] - ws[p][0:CHUNK]
                    yc.append(ws[p][CHUNK:2 * CHUNK] + mm_pair(d['aq', p], v_new))
                    gdn_sc[p] = s_old[p] * jnp.exp(d['glast'][:, ps]) + _dot_tn(d['kend'][:, ps], v_new) * bdm
                yield
                yc = jnp.concatenate(yc, axis=-1)
                ss = head_sum(yc * yc)
                yield
                mix_ref[d['rs'], 512:768] = rms_finish(yc, ss, row(10, D_BRANCH), uc_ref[d['rs'], 0:256])

        def hgrn_b():
            bdm = bdm_ref[...]
            for c in range(nch):
                d = data[c]
                yd = []
                for p, ps in enumerate(pairs):
                    s_bd = hg_sc[p]
                    vd = d['vd'][:, ps]
                    yd.append(mm_pair(d['amat', p], vd) + d['diag'][:, ps] * vd + _dot(d['qed'][:, ps], s_bd))
                    hg_sc[p] = s_bd * d['dcol', p] + _dot_tn(d['kend_d'][:, ps], vd) * bdm
                yield
                yd = jnp.concatenate(yd, axis=-1)
                ss = head_sum(yd * yd)
                yield
                mix_ref[d['rs'], 768:1024] = rms_finish(yd, ss, row(14, D_BRANCH), ud_ref[d['rs'], 0:256])

        return ([prologue(c) for c in range(nch)],
                [f(c, p) for c in range(nch) for p in range(PAIRS) for f in (gdn_a, hgrn_a, ssd_a)],
                [gdn_b(), hgrn_b(), ssd_b()])

    def finish():
        @pl.when(j == pl.num_programs(1) - 1)
        def _():
            for p in range(PAIRS):
                for e in range(2):
                    es = slice(e * HEAD_DIM, (e + 1) * HEAD_DIM)
                    o_ssd[2 * p + e] = ssd_sc[p, :, es]
                    o_gdn[2 * p + e] = gdn_sc[p, es, es]
                    o_hg[2 * p + e] = hg_sc[p, es, es]

    return (*prompt_block(), finish)


def _mixer_constants():
    i = np.arange(CHUNK)[:, None]
    jn = np.arange(CHUNK)[None, :]
    tri = (jn <= i).astype(np.float32)
    strict = (jn < i).astype(np.float32)
    eye = np.eye(CHUNK, dtype=np.float32)
    wlev, mlev = [], []
    for lev in range(N_LEVELS):
        b = CHUNK >> (lev + 1)
        blk_i, pos_i = i // (2 * b), i % (2 * b)
        mid = blk_i * 2 * b + b
        upper = pos_i >= b
        w = np.where(upper, (jn >= mid) & (jn <= i), (jn > i) & (jn < mid)).astype(np.float32)
        msk = (((jn // (2 * b)) == blk_i) & upper & ((jn % (2 * b)) < b)).astype(np.float32)
        wlev.append(w)
        mlev.append(msk)
    cmat = np.stack([tri, tri] + wlev[N_LEVELS - MXU_LEVELS:])
    cmask = np.stack([tri, strict, eye] + mlev)
    cpair = np.concatenate([cmask, cmask], axis=-1)
    bdm = np.kron(np.eye(2, dtype=np.float32), np.ones((HEAD_DIM, HEAD_DIM), np.float32))
    lane = np.arange(LANES)[None, :]
    lr = np.stack([np.broadcast_to(lane < HEAD_DIM, (CHUNK, LANES)),
                   np.broadcast_to(lane >= HEAD_DIM, (CHUNK, LANES))]).astype(np.float32)
    e_a = np.zeros((LANES, 512), np.float32)
    e_b = np.zeros((LANES, 512), np.float32)
    blk = np.zeros((256, 256), np.float32)
    sel = np.zeros((2 * SUBLANES, LANES), np.float32)
    for h in range(HEADS):
        hs = slice(h * HEAD_DIM, (h + 1) * HEAD_DIM)
        e_a[h, hs] = 1.0
        e_a[HEADS + h, 256 + h * HEAD_DIM:256 + (h + 1) * HEAD_DIM] = 1.0
        e_b[2 * HEADS + h, hs] = 1.0
        e_b[3 * HEADS + h, 256 + h * HEAD_DIM:256 + (h + 1) * HEAD_DIM] = 1.0
        blk[hs, hs] = 1.0
        sel[h, 2 * HEADS + h] = 1.0
        sel[HEADS + h, 3 * HEADS + h] = 1.0
    ones = np.ones((2 * SUBLANES, LANES), np.float32)
    return (jnp.asarray(cmat, BF16), jnp.asarray(cpair, F32), jnp.asarray(bdm, F32),
            jnp.asarray(lr, BF16), jnp.asarray(e_a, BF16), jnp.asarray(e_b, BF16), jnp.asarray(blk, BF16),
            jnp.asarray(sel, BF16), jnp.asarray(ones, BF16))


def _layer_spec(a, l):
    nd = a.ndim - 1
    return pl.BlockSpec((None,) + a.shape[1:], lambda b, j: (l,) + (0,) * nd)


def _mixer_call(us, l, wts, consts, acc, *, rows):
    n_seq, t_pad, _ = us[0].shape
    nblk = t_pad // rows
    nq = SEQS_PER_STEP if n_seq % SEQS_PER_STEP == 0 else 1
    rmap = lambda b, j: (b, j, 0)

    def state_spec(tail):
        return pl.BlockSpec((None, nq) + tail, lambda b, j: (l, b) + (0,) * len(tail))

    layer_consts = [wts['rows']]
    assert len(us) == N_MIXER_U and len(layer_consts) + len(consts) == N_MIXER_CONST
    in_specs = [pl.BlockSpec((nq, rows, w), rmap) for w in (WA, WC, WD)]
    in_specs += [_layer_spec(a, l) for a in layer_consts]
    in_specs += [pl.BlockSpec(a.shape, lambda b, j, nd=a.ndim: (0,) * nd) for a in consts]
    inputs = list(us) + layer_consts + list(consts)
    mat = (HEADS, HEAD_DIM, HEAD_DIM)
    tails = [mat, mat, mat]
    scratch = [pltpu.VMEM((nq, PAIRS, HEAD_DIM, LANES), F32),
               pltpu.VMEM((nq, PAIRS, LANES, LANES), F32),
               pltpu.VMEM((nq, PAIRS, LANES, LANES), F32)]
    assert len(acc) == len(tails) == N_MIXER_STATES
    n_before_acc = len(inputs)
    inputs += list(acc)
    in_specs += [pl.BlockSpec(memory_space=pl.ANY)] * len(acc)
    out_shape = [jax.ShapeDtypeStruct((n_seq, t_pad, D_MODEL), BF16)]
    out_shape += [jax.ShapeDtypeStruct(a.shape, a.dtype) for a in acc]
    out_specs = [pl.BlockSpec((nq, rows, D_MODEL), rmap)] + [state_spec(t) for t in tails]
    aliases = {n_before_acc + i: 1 + i for i in range(len(acc))}
    return pl.pallas_call(
        functools.partial(_mixer_kernel, rows=rows, nq=nq),
        grid=(n_seq // nq, nblk),
        in_specs=in_specs,
        out_specs=out_specs,
        out_shape=out_shape,
        scratch_shapes=scratch,
        input_output_aliases=aliases,
        compiler_params=pltpu.CompilerParams(dimension_semantics=("arbitrary", "arbitrary"),
                                             vmem_limit_bytes=VMEM_LIMIT),
        name="mixer_prompt",
    )(*inputs)


S5_TIME_BLOCK = PAD_FRONT


def _s5_prompt_kernel(zu_ref, rows_ref, wbre_ref, wbim_ref, cre_ref, cim_ref, glu_ref, lb_ref, perm_ref,
                      mix_in, sre_in, sim_in, mix_ref, o_re, o_im, hre_ref, him_ref):
    del mix_in, sre_in, sim_in
    nb, tb, _ = zu_ref.shape
    rows = nb * tb
    j = pl.program_id(1)

    @pl.when(j == 0)
    def _():
        o_re[...] = jnp.zeros_like(o_re)
        o_im[...] = jnp.zeros_like(o_im)

    zu = zu_ref[...].reshape(rows, 2 * D_BRANCH)
    zu = jnp.where(j * tb >= PAD_FRONT, zu, 0.0)
    perm = perm_ref[...]
    zu = _dot01_l(perm, zu)
    z_p, u_p = zu[:, 0:D_BRANCH], zu[:, D_BRANCH:2 * D_BRANCH]
    hre_ref[...] = _dot(u_p, wbre_ref[...])
    him_ref[...] = _dot(u_p, wbim_ref[...])
    lr = jnp.broadcast_to(lb_ref[0:1, 0:S5_LANES], (nb, S5_LANES))
    li = jnp.broadcast_to(lb_ref[0:1, S5_LANES:2 * S5_LANES], (nb, S5_LANES))

    def step(tau, carry):
        cr, ci = carry
        rs = pl.ds(pl.multiple_of(tau * nb, nb), nb)
        xr = lr * cr - li * ci + hre_ref[rs, :]
        xi = lr * ci + li * cr + him_ref[rs, :]
        hre_ref[rs, :] = xr
        him_ref[rs, :] = xi
        return xr, xi

    cr, ci = lax.fori_loop(0, tb, step, (o_re[...], o_im[...]), unroll=2)
    o_re[...] = cr
    o_im[...] = ci

    row = lambda i: rows_ref[i:i + 1, 0:D_BRANCH]
    y5 = _dot(hre_ref[...], cre_ref[...]) - _dot(him_ref[...], cim_ref[...]) + row(5) * u_p
    y5 = _gelu_tanh(y5)
    y5 = y5 * _sigmoid(_dot(y5, glu_ref[...]) + row(6))
    y5 = (y5 * _silu(z_p)).astype(BF16)
    y5 = lax.dot_general(perm, y5, (((0,), (0,)), ((), ())), preferred_element_type=F32).astype(BF16)
    mix_ref[...] = y5.reshape(nb, tb, D_BRANCH)


def _s5_prompt_call(ub, l, wts, perm, mix, acc_re, acc_im):
    n_seq, t_pad, _ = ub.shape
    tb = S5_TIME_BLOCK
    assert n_seq % SUBLANES == 0 and t_pad % tb == 0 and PAD_FRONT % tb == 0
    layer_consts = [wts[k] for k in ('rows', 'wb_re', 'wb_im', 'c_re', 'c_im', 'glu_w', 's5lb')]
    in_specs = [pl.BlockSpec((SUBLANES, tb, WB), lambda g, j: (g, j, 0))]
    in_specs += [_layer_spec(a, l) for a in layer_consts]
    in_specs += [pl.BlockSpec(perm.shape, lambda g, j: (0, 0))]
    in_specs += [pl.BlockSpec(memory_space=pl.ANY)] * 3
    state_spec = pl.BlockSpec((None, SUBLANES, S5_LANES), lambda g, j: (l, g, 0))
    rows = SUBLANES * tb
    return pl.pallas_call(
        _s5_prompt_kernel,
        grid=(n_seq // SUBLANES, t_pad // tb),
        in_specs=in_specs,
        out_specs=[pl.BlockSpec((SUBLANES, tb, D_BRANCH), lambda g, j: (g, j, 1)), state_spec, state_spec],
        out_shape=[jax.ShapeDtypeStruct(mix.shape, mix.dtype), jax.ShapeDtypeStruct(acc_re.shape, F32),
                   jax.ShapeDtypeStruct(acc_im.shape, F32)],
        scratch_shapes=[pltpu.VMEM((rows, S5_LANES), F32), pltpu.VMEM((rows, S5_LANES), F32)],
        input_output_aliases={len(in_specs) - 3: 0, len(in_specs) - 2: 1, len(in_specs) - 1: 2},
        compiler_params=pltpu.CompilerParams(dimension_semantics=("arbitrary", "arbitrary"),
                                             vmem_limit_bytes=VMEM_LIMIT),
        name="s5_prompt",
    )(ub, *layer_consts, perm, mix, acc_re, acc_im)


SAMPLE_COL_WIDTHS = (SSD_XBC, 4 * HEADS, 4 * HEADS, D_BRANCH, D_BRANCH, D_BRANCH, D_BRANCH, GDN_QKV, D_BRANCH,
                     D_BRANCH, D_BRANCH, D_BRANCH, D_BRANCH)
(C_CONVB_A, C_SBIAS, C_SSCALE, C_SSD_D, C_SSD_G, C_S5_D, C_GLU_B, C_CONVB_C, C_GDN_G, C_LOGLB, C_LOG1M, C_ONEM,
 C_HG_G) = [int(v) for v in np.cumsum((0,) + SAMPLE_COL_WIDTHS[:-1])]
U_ZA, U_XBC, U_ZB, U_UB, U_ZC, U_QKV, U_SMALL, U_ZD, U_QD, U_FD, U_ID = (
    0, 256, WA, WA + 256, WA + WB, WA + WB + 256, WA + WB + 1024, WA + WB + WC, WA + WB + WC + 256,
    WA + WB + WC + 512, WA + WB + WC + 768)
V_HALF = HEAD_DIM // 2


def _sample_inproj_kernel(h_ref, wt_ref, o_ref):
    o_ref[...] = lax.dot_general(wt_ref[...], h_ref[...].astype(BF16), (((1,), (1,)), ((), ())),
                                 preferred_element_type=F32)


def _sample_outproj_kernel(mixt_ref, x_ref, w_ref, g_ref, b_ref, o_ref):
    out = lax.dot_general(mixt_ref[...], w_ref[...], (((0,), (0,)), ((), ())), preferred_element_type=F32)
    o_ref[...] = _layernorm_rows(DN_ALPHA * x_ref[...] + out, g_ref[...], b_ref[...])


def _sample_mixer_kernel(ut_ref, cols_ref, cwa_ref, cwc_ref, lb_ref, wbre_ref, wbim_ref, cre_ref, cim_ref, glu_ref,
                         i_ssd, i_ca, i_s5re, i_s5im, i_gdn, i_cc, i_hg, *rest, n_tok, n_seq):
    (mix_ref, o_ssd, o_ca, o_s5re, o_s5im, o_gdn, o_cc, o_hg) = rest[N_STATES:N_STATES + 1 + N_STATES]
    (acta_ref, actc_ref, small_ref, hre_ref, him_ref, gq_ref, gk_ref, hq_ref, hk_ref, hf_ref,
     ya_ref, yc_ref, yd_ref) = rest[N_STATES + 1 + N_STATES:]
    h = pl.program_id(0)
    vh = pl.program_id(1)
    toks = [slice(t * n_seq, (t + 1) * n_seq) for t in range(n_tok)]
    hrow = pl.multiple_of(h * HEAD_DIM, HEAD_DIM)

    def col(c0, n, off=0):
        return cols_ref[pl.ds(c0 + off, n), :]

    @pl.when((h == 0) & (vh == 0))
    def _():
        for u0, width, i_c, o_c, cw_ref, act_ref, cb in ((U_XBC, SSD_XBC, i_ca, o_ca, cwa_ref, acta_ref, C_CONVB_A),
                                                        (U_QKV, GDN_QKV, i_cc, o_cc, cwc_ref, actc_ref, C_CONVB_C)):
            xx = [i_c[i] for i in range(CONV_W - 1)] + [ut_ref[u0:u0 + width, ts] for ts in toks]
            for t in range(n_tok):
                acc = cols_ref[cb:cb + width, :]
                for w in range(CONV_W):
                    acc = acc + cw_ref[w] * xx[t + w]
                act_ref[t] = _silu(acc)
            for i in range(CONV_W - 1):
                o_c[i] = xx[n_tok + i]
        rid = lax.broadcasted_iota(jnp.int32, (4 * HEADS, n_seq), 0)
        for t, ts in enumerate(toks):
            v = ut_ref[U_SMALL:U_SMALL + 4 * HEADS, ts] + cols_ref[C_SBIAS:C_SBIAS + 4 * HEADS, :]
            small_ref[t] = jnp.where((rid >= HEADS) & (rid < 2 * HEADS), _sigmoid(v),
                                     _softplus(v) * cols_ref[C_SSCALE:C_SSCALE + 4 * HEADS, :])
        u_b = ut_ref[U_UB:U_UB + D_BRANCH, :]
        bu_re = _dot(wbre_ref[...], u_b)
        bu_im = _dot(wbim_ref[...], u_b)
        sr, si = i_s5re[...], i_s5im[...]
        lr, li = lb_ref[0], lb_ref[1]
        for t, ts in enumerate(toks):
            sr, si = lr * sr - li * si + bu_re[:, ts], lr * si + li * sr + bu_im[:, ts]
            hre_ref[:, ts] = sr
            him_ref[:, ts] = si
        o_s5re[...] = sr
        o_s5im[...] = si
        y5 = _dot(cre_ref[...], hre_ref[...]) - _dot(cim_ref[...], him_ref[...])
        for ts in toks:
            y = _gelu_tanh(y5[:, ts] + cols_ref[C_S5_D:C_S5_D + D_BRANCH, :] * u_b[:, ts])
            g = _dot(glu_ref[...], y) + cols_ref[C_GLU_B:C_GLU_B + D_BRANCH, :]
            mix_ref[D_BRANCH:2 * D_BRANCH, ts] = (y * _sigmoid(g) * _silu(ut_ref[U_ZB:U_ZB + D_BRANCH, ts])).astype(BF16)

    vrow = pl.multiple_of(vh * V_HALF, V_HALF)
    vsel = pl.ds(hrow + vrow, V_HALF)

    def scal(t, r):
        return small_ref[t, pl.ds(r * HEADS + h, 1), :]

    grp = (h // 2) * HEAD_DIM
    xdt = [acta_ref[t, vsel, :] * scal(t, 0) for t in range(n_tok)]
    ea = [jnp.exp(scal(t, 3)) for t in range(n_tok)]

    def ssd_row(n, ys):
        s = i_ssd[n]
        ys = list(ys)
        for t in range(n_tok):
            s = s * ea[t] + acta_ref[t, pl.ds(256 + grp + n, 1), :] * xdt[t]
            ys[t] = ys[t] + acta_ref[t, pl.ds(384 + grp + n, 1), :] * s
        o_ssd[n] = s
        return tuple(ys)

    zero = jnp.zeros((V_HALF, n_seq), F32)
    ys = lax.fori_loop(0, HEAD_DIM, ssd_row, (zero,) * n_tok, unroll=2)
    for t in range(n_tok):
        ya_ref[t, pl.ds(vrow, V_HALF), :] = ys[t] + col(C_SSD_D, V_HALF, hrow + vrow) * acta_ref[t, vsel, :]

    @pl.when(vh == 0)
    def _():
        for t in range(n_tok):
            q = actc_ref[t, pl.ds(hrow, HEAD_DIM), :]
            k = actc_ref[t, pl.ds(256 + hrow, HEAD_DIM), :]
            gq_ref[t] = q * lax.rsqrt(jnp.sum(q * q, axis=0, keepdims=True) + L2_EPS) * (HEAD_DIM ** -0.5)
            gk_ref[t] = k * lax.rsqrt(jnp.sum(k * k, axis=0, keepdims=True) + L2_EPS)

    for t in range(n_tok):
        alpha = jnp.exp(scal(t, 2))
        src = i_gdn if t == 0 else o_gdn

        def decay_row(kk, pred, alpha=alpha, src=src, t=t):
            s = src[kk] * alpha
            o_gdn[kk] = s
            return pred + gk_ref[t, pl.ds(kk, 1), :] * s

        pred = lax.fori_loop(0, HEAD_DIM, decay_row, zero, unroll=4)
        v_new = scal(t, 1) * (actc_ref[t, pl.ds(512 + hrow + vrow, V_HALF), :] - pred)

        def update_row(kk, o, v_new=v_new, t=t):
            s = o_gdn[kk] + gk_ref[t, pl.ds(kk, 1), :] * v_new
            o_gdn[kk] = s
            return o + gq_ref[t, pl.ds(kk, 1), :] * s

        yc_ref[t, pl.ds(vrow, V_HALF), :] = lax.fori_loop(0, HEAD_DIM, update_row, zero, unroll=4)

    @pl.when(vh == 0)
    def _():
        for t, ts in enumerate(toks):
            f_d = ut_ref[pl.ds(U_FD + hrow, HEAD_DIM), ts]
            t1 = col(C_LOGLB, HEAD_DIM, hrow)
            t2 = col(C_LOG1M, HEAD_DIM, hrow) - _softplus(-f_d)
            logf = jnp.maximum(t1, t2) + _log1p_exp_neg(jnp.abs(t1 - t2))
            hf_ref[t] = jnp.exp(logf)
            hk_ref[t] = col(C_ONEM, HEAD_DIM, hrow) * _sigmoid(-f_d)
            hq_ref[t] = _silu(ut_ref[pl.ds(U_QD + hrow, HEAD_DIM), ts])

    vd = [ut_ref[pl.ds(U_ID + hrow + vrow, V_HALF), ts] for ts in toks]

    def hgrn_row(kk, os_):
        s = i_hg[kk]
        os_ = list(os_)
        for t in range(n_tok):
            s = s * hf_ref[t, pl.ds(kk, 1), :] + hk_ref[t, pl.ds(kk, 1), :] * vd[t]
            os_[t] = os_[t] + hq_ref[t, pl.ds(kk, 1), :] * s
        o_hg[kk] = s
        return tuple(os_)

    os_ = lax.fori_loop(0, HEAD_DIM, hgrn_row, (zero,) * n_tok, unroll=2)
    for t in range(n_tok):
        yd_ref[t, pl.ds(vrow, V_HALF), :] = os_[t]

    @pl.when(vh == HEAD_DIM // V_HALF - 1)
    def _():
        for y_ref, g0, z0, m0 in ((ya_ref, C_SSD_G, U_ZA, 0), (yc_ref, C_GDN_G, U_ZC, 2 * D_BRANCH),
                                  (yd_ref, C_HG_G, U_ZD, 3 * D_BRANCH)):
            for t, ts in enumerate(toks):
                y = y_ref[t]
                ms = jnp.mean(y * y, axis=0, keepdims=True)
                z = ut_ref[pl.ds(z0 + hrow, HEAD_DIM), ts]
                mix_ref[pl.ds(m0 + hrow, HEAD_DIM), ts] = (
                    y * lax.rsqrt(ms + RMS_EPS) * col(g0, HEAD_DIM, hrow) * _silu(z)).astype(BF16)


def _sample_mixer_call(ut, l, sw, states_t, acc, *, n_tok, n_seq):
    consts = [sw[k] for k in ('cols', 'cw_a', 'cw_c', 'lb', 'wb_re', 'wb_im', 'c_re', 'c_im', 'glu_w')]
    whole = lambda a: pl.BlockSpec((None,) + a.shape[1:], lambda h, v, nd=a.ndim - 1: (l,) + (0,) * nd)
    mat_spec = pl.BlockSpec((None, None, HEAD_DIM, V_HALF, n_seq), lambda h, v: (l, h, 0, v, 0))
    state_specs = [mat_spec, whole(states_t[1]), whole(states_t[2]), whole(states_t[3]), mat_spec,
                   whole(states_t[5]), mat_spec]
    n_cols = ut.shape[1]
    in_specs = [pl.BlockSpec(ut.shape, lambda h, v: (0, 0))] + [whole(a) for a in consts] + state_specs
    in_specs += [pl.BlockSpec(memory_space=pl.ANY)] * N_STATES
    inputs = [ut] + consts + list(states_t) + list(acc)
    out_shape = [jax.ShapeDtypeStruct((D_MODEL, n_cols), BF16)] + [jax.ShapeDtypeStruct(a.shape, a.dtype) for a in acc]
    out_specs = [pl.BlockSpec((D_MODEL, n_cols), lambda h, v: (0, 0))] + state_specs
    n_before_acc = 1 + len(consts) + N_STATES
    tok_tile = lambda rows: pltpu.VMEM((n_tok, rows, n_seq), F32)
    scratch = [tok_tile(SSD_XBC), tok_tile(GDN_QKV), tok_tile(4 * HEADS),
               pltpu.VMEM((S5_LANES, n_cols), F32), pltpu.VMEM((S5_LANES, n_cols), F32)]
    scratch += [tok_tile(HEAD_DIM)] * 8
    return pl.pallas_call(
        functools.partial(_sample_mixer_kernel, n_tok=n_tok, n_seq=n_seq),
        grid=(HEADS, HEAD_DIM // V_HALF),
        in_specs=in_specs,
        out_specs=out_specs,
        out_shape=out_shape,
        scratch_shapes=scratch,
        input_output_aliases={n_before_acc + i: 1 + i for i in range(N_STATES)},
        compiler_params=pltpu.CompilerParams(dimension_semantics=("arbitrary", "arbitrary"),
                                             vmem_limit_bytes=VMEM_LIMIT),
        name="mixer_sample",
    )(*inputs)


W_IN_ROWS_PER_STEP = 128


def _regroup_w_in_kernel(w_ref, o_ref, ot_ref):
    w = w_ref[...]
    small = jnp.concatenate([w[:, 768:772], w[:, 2308:2316], w[:, 768:772],
                             jnp.zeros((w.shape[0], LANES - 4 * HEADS), F32)], axis=1)
    wr = jnp.concatenate([w[:, 0:768], w[:, 772:1284], w[:, 1284:2308], small, w[:, 2316:3340]], axis=1)
    o_ref[...] = wr.astype(BF16)
    ot_ref[...] = wr.T.astype(BF16)


def _regroup_w_in(w_in):
    depth, d_model, d_in = w_in.shape
    tk = W_IN_ROWS_PER_STEP
    w_cols = WA + WB + WC + WD
    return pl.pallas_call(
        _regroup_w_in_kernel,
        grid=(depth, d_model // tk),
        in_specs=[pl.BlockSpec((None, tk, d_in), lambda l, i: (l, i, 0))],
        out_specs=[pl.BlockSpec((None, tk, w_cols), lambda l, i: (l, i, 0)),
                   pl.BlockSpec((None, w_cols, tk), lambda l, i: (l, 0, i))],
        out_shape=[jax.ShapeDtypeStruct((depth, d_model, w_cols), BF16),
                   jax.ShapeDtypeStruct((depth, w_cols, d_model), BF16)],
        compiler_params=pltpu.CompilerParams(dimension_semantics=("arbitrary", "arbitrary"),
                                             vmem_limit_bytes=VMEM_LIMIT),
        name="regroup_w_in",
    )(w_in.astype(F32))


def _pad_rows(v, width=ROW_W):
    v = v.astype(F32).reshape(v.shape[0], -1)
    return jnp.pad(v, ((0, 0), (0, width - v.shape[1])))


def _stacked_weights(p, lbs):
    depth = lbs.shape[0]
    rep = lambda v: jnp.repeat(v.astype(F32), HEAD_DIM, axis=-1)
    zeros_h = jnp.zeros((depth, HEADS), F32)
    small_bias = jnp.concatenate([p['ssd_dt_bias'], zeros_h, p['gdn_dt_bias'], p['ssd_dt_bias']], axis=1)
    small_scale = jnp.concatenate([jnp.ones((depth, HEADS), F32), zeros_h, -jnp.exp(p['gdn_a_log']),
                                   -jnp.exp(p['ssd_a_log'])], axis=1)
    zrow = jnp.zeros((depth, ROW_W), F32)
    rows = jnp.stack([
        _pad_rows(p['ssd_conv_b']), _pad_rows(small_bias), _pad_rows(small_scale),
        _pad_rows(rep(p['ssd_d'])), _pad_rows(p['ssd_norm_g']), _pad_rows(p['s5_d']),
        _pad_rows(p['s5_glu_b']), _pad_rows(p['gdn_conv_b']), zrow, zrow,
        _pad_rows(p['gdn_norm_g']), _pad_rows(jnp.log(lbs)), _pad_rows(jnp.log1p(-lbs)), _pad_rows(1.0 - lbs),
        _pad_rows(p['hg_norm_g']), zrow], axis=1)

    pad_cw = lambda cw: jnp.pad(cw.astype(F32), ((0, 0), (0, SUBLANES - CONV_W), (0, 0)))

    lam_re, lam_im = p['s5_lam_re'].astype(F32), p['s5_lam_im'].astype(F32)
    dt = jnp.exp(p['s5_log_dt'].astype(F32))[..., None]
    mag = jnp.exp(lam_re * dt)
    ang = lam_im * dt
    lb_re, lb_im = mag * jnp.cos(ang), mag * jnp.sin(ang)
    den = jnp.square(lam_re) + jnp.square(lam_im)
    nr = lb_re - 1.0
    coef_re = (nr * lam_re + lb_im * lam_im) / den
    coef_im = (lb_im * lam_re - nr * lam_im) / den
    b_re, b_im = p['s5_b_re'].astype(F32), p['s5_b_im'].astype(F32)
    bb_re = coef_re[..., None] * b_re - coef_im[..., None] * b_im
    bb_im = coef_re[..., None] * b_im + coef_im[..., None] * b_re
    eye_g = jnp.eye(S5_GROUPS, dtype=F32)
    bd_in = lambda bb: jnp.einsum('lgnq,gh->lgqhn', bb, eye_g).reshape(depth, D_BRANCH, S5_LANES).astype(BF16)
    bd_out = lambda c: jnp.einsum('lgqn,gh->lgnhq', c.astype(F32), eye_g).reshape(depth, S5_LANES, D_BRANCH).astype(BF16)

    s5lb = jnp.concatenate([lb_re.reshape(depth, 1, -1), lb_im.reshape(depth, 1, -1)], axis=-1)

    lanes = lambda v: jnp.broadcast_to(v.astype(F32)[..., None], v.shape + (LANES,))
    col_vecs = [p['ssd_conv_b'], small_bias, small_scale, rep(p['ssd_d']), p['ssd_norm_g'], p['s5_d'], p['s5_glu_b'],
                p['gdn_conv_b'], p['gdn_norm_g'], jnp.log(lbs), jnp.log1p(-lbs), 1.0 - lbs, p['hg_norm_g']]
    assert [int(np.prod(v.shape[1:])) for v in col_vecs] == list(SAMPLE_COL_WIDTHS)
    cols_s = lanes(jnp.concatenate([v.astype(F32).reshape(depth, -1) for v in col_vecs], axis=1))
    bd_in_t = lambda bb: jnp.einsum('lgnq,gh->lgnhq', bb, eye_g).reshape(depth, S5_LANES, D_BRANCH).astype(BF16)
    bd_out_t = lambda c: jnp.einsum('lgqn,gh->lgqhn', c.astype(F32), eye_g).reshape(depth, D_BRANCH, S5_LANES).astype(BF16)
    sample = dict(cols=cols_s, cw_a=lanes(p['ssd_conv_w']), cw_c=lanes(p['gdn_conv_w']),
                  lb=lanes(jnp.stack([lb_re.reshape(depth, -1), lb_im.reshape(depth, -1)], axis=1)),
                  wb_re=bd_in_t(bb_re), wb_im=bd_in_t(bb_im), c_re=bd_out_t(p['s5_c_re']), c_im=bd_out_t(p['s5_c_im']),
                  glu_w=jnp.swapaxes(p['s5_glu_w'], 1, 2).astype(BF16))

    w_in_r, w_in_t = _regroup_w_in(p['w_in'])
    return dict(w_in=w_in_r, w_in_t=w_in_t, rows=rows, cw_a=pad_cw(p['ssd_conv_w']), cw_c=pad_cw(p['gdn_conv_w']),
                wb_re=bd_in(bb_re), wb_im=bd_in(bb_im), c_re=bd_out(p['s5_c_re']), c_im=bd_out(p['s5_c_im']),
                glu_w=p['s5_glu_w'].astype(BF16), s5lb=s5lb, w_out=p['w_out'].astype(BF16),
                ln_g=p['ln_g'].astype(F32)[:, None, :], ln_b=p['ln_b'].astype(F32)[:, None, :], sample=sample)


def _pick_tile(n_rows, candidates):
    for t in candidates:
        if n_rows % t == 0:
            return t
    raise ValueError(f"no row tile for {n_rows}")


def kernel(x_prompt, x_sample, state_ssd, state_ssd_conv, state_s5_re, state_s5_im, state_gdn, state_gdn_conv, state_hgrn, meta_tokens, ln_in_g, ln_in_b, w_in, ssd_conv_w, ssd_conv_b, ssd_dt_bias, ssd_a_log, ssd_d, ssd_norm_g, s5_lam_re, s5_lam_im, s5_log_dt, s5_b_re, s5_b_im, s5_c_re, s5_c_im, s5_d, s5_glu_w, s5_glu_b, gdn_conv_w, gdn_conv_b, gdn_a_log, gdn_dt_bias, gdn_norm_g, hg_lb_raw, hg_norm_g, w_out, ln_g, ln_b):
    p = dict(w_in=w_in, ssd_conv_w=ssd_conv_w, ssd_conv_b=ssd_conv_b, ssd_dt_bias=ssd_dt_bias,
             ssd_a_log=ssd_a_log, ssd_d=ssd_d, ssd_norm_g=ssd_norm_g, s5_lam_re=s5_lam_re,
             s5_lam_im=s5_lam_im, s5_log_dt=s5_log_dt, s5_b_re=s5_b_re, s5_b_im=s5_b_im, s5_c_re=s5_c_re,
             s5_c_im=s5_c_im, s5_d=s5_d, s5_glu_w=s5_glu_w, s5_glu_b=s5_glu_b, gdn_conv_w=gdn_conv_w,
             gdn_conv_b=gdn_conv_b, gdn_a_log=gdn_a_log, gdn_dt_bias=gdn_dt_bias, gdn_norm_g=gdn_norm_g,
             hg_norm_g=hg_norm_g, w_out=w_out, ln_g=ln_g, ln_b=ln_b)
    bp, seq, _ = x_prompt.shape
    bs, dseq, _ = x_sample.shape
    assert dseq >= CONV_W - 1 and bs % LANES == 0
    t_pad = PAD_FRONT + N_META + seq
    rows_p = 3 * CHUNK
    assert t_pad % rows_p == 0
    n_p = bp * t_pad
    n_s = bs * dseq

    soft = jax.nn.softmax(hg_lb_raw.astype(F32), axis=0)
    csum = jnp.cumsum(soft, axis=0)
    lbs = csum - csum[0]

    xs = jnp.swapaxes(x_sample.astype(F32), 0, 1).reshape(n_s, D_MODEL)

    tm_p = _pick_tile(t_pad, (704, 512, 384, 192))
    tm_s = n_s
    g_in, b_in = ln_in_g.astype(F32)[None], ln_in_b.astype(F32)[None]
    hp = pl.pallas_call(
        _ln_in_prompt_kernel,
        grid=(bp,),
        in_specs=[pl.BlockSpec((None, seq, D_MODEL), lambda b: (b, 0, 0)),
                  pl.BlockSpec((N_META, D_MODEL), lambda b: (0, 0)),
                  pl.BlockSpec((1, D_MODEL), lambda b: (0, 0)), pl.BlockSpec((1, D_MODEL), lambda b: (0, 0))],
        out_specs=pl.BlockSpec((None, t_pad, D_MODEL), lambda b: (b, 0, 0)),
        out_shape=jax.ShapeDtypeStruct((bp, t_pad, D_MODEL), F32),
        compiler_params=pltpu.CompilerParams(dimension_semantics=("arbitrary",), vmem_limit_bytes=VMEM_LIMIT),
        name="ln_in_prompt",
    )(x_prompt.astype(F32), meta_tokens.astype(F32), g_in, b_in).reshape(n_p, D_MODEL)
    hs = _row_tiled_call(_ln_in_kernel, n_s, tm_s, [xs], [g_in, b_in], [D_MODEL], [F32], "ln_in_sample")

    consts_p = _mixer_constants()
    depth = w_in.shape[0]
    wts = _stacked_weights(p, lbs)

    mat = (HEADS, HEAD_DIM, HEAD_DIM)
    zeros = lambda *shape: jnp.zeros((depth,) + shape, F32)
    acc_p = [zeros(bp, *mat), zeros(bp, *mat), zeros(bp, *mat)]
    p_re, p_im = zeros(bp, S5_LANES), zeros(bp, S5_LANES)
    conv_tails = []
    s5_rows = SUBLANES * S5_TIME_BLOCK
    s5_perm = np.zeros((s5_rows, s5_rows), np.float32)
    for b in range(SUBLANES):
        s5_perm[np.arange(S5_TIME_BLOCK) * SUBLANES + b, b * S5_TIME_BLOCK + np.arange(S5_TIME_BLOCK)] = 1.0
    s5_perm = jnp.asarray(s5_perm, BF16)
    seq_last = lambda a: jnp.moveaxis(a.astype(F32), 1, -1)
    st_in = (seq_last(state_ssd), seq_last(state_ssd_conv),
             seq_last(state_s5_re).reshape(depth, S5_LANES, bs), seq_last(state_s5_im).reshape(depth, S5_LANES, bs),
             seq_last(state_gdn), seq_last(state_gdn_conv), seq_last(state_hgrn))
    acc_s = [jnp.zeros(a.shape, F32) for a in st_in]
    out_consts = [wts['w_out'], wts['ln_g'], wts['ln_b']]
    w_cols = WA + WB + WC + WD
    for l in range(depth):
        us, tails = _inproj_conv_call(hp, l, wts, bp, tm_p)
        conv_tails.append(tails[:, SUBLANES - (CONV_W - 1):, :])
        ua, ub, uc, ud = [u.reshape(bp, t_pad, u.shape[1]) for u in us]
        mix, *acc_p = _mixer_call([ua, uc, ud], l, wts, consts_p, acc_p, rows=rows_p)
        mix, p_re, p_im = _s5_prompt_call(ub, l, wts, s5_perm, mix, p_re, p_im)
        skip = PAD_FRONT + N_META if l + 1 == depth else 0
        seq_rows = lambda b: (b, 0, 0)
        layer_blk = lambda a: pl.BlockSpec((None,) + a.shape[1:], lambda b, nd=a.ndim - 1: (l,) + (0,) * nd)
        hp = pl.pallas_call(
            functools.partial(_outproj_seq_kernel, step=rows_p, skip=skip),
            grid=(bp,),
            in_specs=[pl.BlockSpec((None, t_pad, D_MODEL), seq_rows), pl.BlockSpec((None, t_pad, D_MODEL), seq_rows)]
            + [layer_blk(a) for a in out_consts],
            out_specs=pl.BlockSpec((None, t_pad - skip, D_MODEL), seq_rows),
            out_shape=jax.ShapeDtypeStruct((bp, t_pad - skip, D_MODEL), F32),
            compiler_params=pltpu.CompilerParams(dimension_semantics=("arbitrary",), vmem_limit_bytes=VMEM_LIMIT),
            name="outproj_prompt",
        )(mix, hp.reshape(bp, t_pad, D_MODEL), *out_consts)
        if skip:
            y_prompt = hp
        else:
            hp = hp.reshape(n_p, D_MODEL)

        n_wblk = 6
        ut = pl.pallas_call(
            _sample_inproj_kernel,
            grid=(n_wblk,),
            in_specs=[pl.BlockSpec((n_s, D_MODEL), lambda i: (0, 0)),
                      pl.BlockSpec((None, w_cols // n_wblk, D_MODEL), lambda i: (l, i, 0))],
            out_specs=pl.BlockSpec((w_cols // n_wblk, n_s), lambda i: (i, 0)),
            out_shape=jax.ShapeDtypeStruct((w_cols, n_s), F32),
            compiler_params=pltpu.CompilerParams(dimension_semantics=("arbitrary",), vmem_limit_bytes=VMEM_LIMIT),
            name="inproj_sample",
        )(hs, wts['w_in_t'])
        mixt, *acc_s = _sample_mixer_call(ut, l, wts['sample'], st_in, acc_s, n_tok=dseq, n_seq=bs)
        hs = pl.pallas_call(
            _sample_outproj_kernel,
            grid=(1,),
            in_specs=[pl.BlockSpec((D_MODEL, n_s), lambda i: (0, 0)), pl.BlockSpec((n_s, D_MODEL), lambda i: (0, 0)),
                      pl.BlockSpec((None, D_MODEL, D_MODEL), lambda i: (l, 0, 0)),
                      pl.BlockSpec((None, 1, D_MODEL), lambda i: (l, 0, 0)),
                      pl.BlockSpec((None, 1, D_MODEL), lambda i: (l, 0, 0))],
            out_specs=pl.BlockSpec((n_s, D_MODEL), lambda i: (0, 0)),
            out_shape=jax.ShapeDtypeStruct((n_s, D_MODEL), F32),
            compiler_params=pltpu.CompilerParams(dimension_semantics=("arbitrary",), vmem_limit_bytes=VMEM_LIMIT),
            name="outproj_sample",
        )(mixt, hs, *out_consts)

    y_sample = jnp.swapaxes(hs.reshape(dseq, bs, D_MODEL), 0, 1)
    s5_shape = lambda a: a.reshape(depth, -1, S5_GROUPS, S5_STATE)
    p_ssd, p_gdn, p_hg = acc_p
    conv_tails = jnp.stack(conv_tails)
    p_ca, p_cc = conv_tails[..., 0:SSD_XBC], conv_tails[..., SSD_XBC:SSD_XBC + GDN_QKV]
    seq_first = lambda a: jnp.moveaxis(a, -1, 1)
    s_ssd, s_ca, s_re, s_im, s_gdn, s_cc, s_hg = [seq_first(a) for a in acc_s]
    return (y_prompt, y_sample, p_ssd, p_ca, s5_shape(p_re), s5_shape(p_im), p_gdn, p_cc, p_hg,
            s_ssd, s_ca, s5_shape(s_re), s5_shape(s_im), s_gdn, s_cc, s_hg)
```

```python
import functools
import math

import numpy as np
import jax
import jax.numpy as jnp
from jax import lax
from jax.experimental import pallas as pl
from jax.experimental.pallas import tpu as pltpu

F32 = jnp.float32
BF16 = jnp.bfloat16

D_MODEL = 1024
DEPTH = 4
N_META = 16
D_BRANCH = 256
HEADS = 4
PAIRS = HEADS // 2
HEAD_DIM = 64
SSD_XBC = 512
S5_GROUPS = 16
S5_STATE = 64
S5_LANES = S5_GROUPS * S5_STATE
GDN_QKV = 768
CONV_W = 4
DN_ALPHA = (2 * DEPTH) ** 0.25
LN_EPS = 1e-5
RMS_EPS = 1e-6
L2_EPS = 1e-6

CHUNK = 64
SUBLANES = 8
LANES = 128
PAD_FRONT = CHUNK - N_META
WA, WB, WC, WD = 768, 512, 1152, 1024
NROWS = 16
ROW_W = 768
VMEM_LIMIT = 56 * 1024 * 1024


def _dot(a, b):
    return jnp.dot(a.astype(BF16), b.astype(BF16), preferred_element_type=F32)


def _dot_nt(a, b):
    return lax.dot_general(a.astype(BF16), b.astype(BF16), (((1,), (1,)), ((), ())),
                           preferred_element_type=F32)


def _dot_tn(a, b):
    return lax.dot_general(a.astype(BF16), b.astype(BF16), (((0,), (0,)), ((), ())),
                           preferred_element_type=F32)


def _split(x, pieces):
    out = []
    r = x
    for i in range(pieces):
        xi = r.astype(BF16)
        out.append(xi)
        if i + 1 < pieces:
            r = r - xi.astype(F32)
    return out


def _dot01_l(w01, x, pieces=3):
    n = x.shape[-1]
    r = jnp.dot(w01, jnp.concatenate(_split(x, pieces), axis=-1), preferred_element_type=F32)
    return sum(r[:, i * n:(i + 1) * n] for i in range(pieces))


def _dot01_r(x, w01, pieces=3):
    m = x.shape[0]
    r = jnp.dot(jnp.concatenate(_split(x, pieces), axis=0), w01, preferred_element_type=F32)
    return sum(r[i * m:(i + 1) * m] for i in range(pieces))


def _sigmoid(x):
    return 1.0 / (1.0 + jnp.exp(-x))


def _silu(x):
    return x * _sigmoid(x)


def _log1p_exp_neg(a):
    e = jnp.exp(-a)
    u = 1.0 + e
    return jnp.log(u) - ((u - 1.0) - e) / u


def _softplus(x):
    return jnp.maximum(x, 0.0) + _log1p_exp_neg(jnp.abs(x))


def _gelu_tanh(x):
    c = math.sqrt(2.0 / math.pi)
    return 0.5 * x * (1.0 + jnp.tanh(c * (x + 0.044715 * (x * x * x))))


def _run_interleaved(chains):
    chains = list(chains)
    while chains:
        alive = []
        for ch in chains:
            try:
                next(ch)
                alive.append(ch)
            except StopIteration:
                pass
        chains = alive


def _layernorm_rows(r, g, b):
    mu = jnp.mean(r, axis=-1, keepdims=True)
    c = r - mu
    var = jnp.mean(c * c, axis=-1, keepdims=True)
    return c * lax.rsqrt(var + LN_EPS) * g + b


def _ln_in_kernel(x_ref, g_ref, b_ref, o_ref):
    o_ref[...] = _layernorm_rows(x_ref[...], g_ref[...], b_ref[...])


def _ln_in_prompt_kernel(x_ref, meta_ref, g_ref, b_ref, o_ref):
    o_ref[0:PAD_FRONT, :] = jnp.zeros((PAD_FRONT, D_MODEL), F32)
    o_ref[PAD_FRONT:CHUNK, :] = _layernorm_rows(meta_ref[...], g_ref[...], b_ref[...])
    seq = x_ref.shape[0]
    step = math.gcd(seq, 512)
    for r0 in range(0, seq, step):
        o_ref[CHUNK + r0:CHUNK + r0 + step, :] = _layernorm_rows(x_ref[r0:r0 + step, :], g_ref[...], b_ref[...])


def _inproj_conv_kernel(x_ref, w_ref, rows_ref, cwa_ref, cwc_ref, oa_ref, ob_ref, oc_ref, od_ref, tail_ref,
                        xx_ref, pp_ref, *, tiles_per_seq):
    tm = x_ref.shape[0]
    t = pl.program_id(0) % tiles_per_seq
    x = x_ref[...].astype(BF16)

    @pl.when(t == 0)
    def _():
        xx_ref[0:SUBLANES, :] = jnp.zeros((SUBLANES, SSD_XBC + GDN_QKV), F32)
        pp_ref[0:SUBLANES, :] = jnp.zeros((SUBLANES, SSD_XBC + GDN_QKV), F32)

    first_valid = jnp.where(t == 0, PAD_FRONT, 0)

    def conv(raw, c0, width, cw_ref, bias_row, o_ref):
        assert CONV_W == 4
        cols = slice(c0, c0 + width)
        r = lax.broadcasted_iota(jnp.int32, (tm, width), 0)
        raw = jnp.where(r >= first_valid, raw, 0.0)
        xx_ref[SUBLANES:SUBLANES + tm, cols] = raw
        x1 = xx_ref[SUBLANES - 1:SUBLANES - 1 + tm, cols]
        p = cw_ref[1:2, :] * raw + cw_ref[0:1, :] * x1
        pp_ref[SUBLANES:SUBLANES + tm, cols] = p
        acc = (rows_ref[bias_row:bias_row + 1, 0:width] + cw_ref[3:4, :] * raw + cw_ref[2:3, :] * x1
               + pp_ref[SUBLANES - 2:SUBLANES - 2 + tm, cols])
        o_ref[:, 256:256 + width] = _silu(acc)
        tail = raw[tm - SUBLANES:tm, :]
        tail_ref[:, cols] = tail
        xx_ref[0:SUBLANES, cols] = tail
        pp_ref[0:SUBLANES, cols] = p[tm - SUBLANES:tm, :]

    ua = jnp.dot(x, w_ref[:, 0:WA], preferred_element_type=F32)
    oa_ref[:, 0:256] = ua[:, 0:256]
    uc = jnp.dot(x, w_ref[:, WA + WB:WA + WB + WC], preferred_element_type=F32)
    conv(ua[:, 256:WA], 0, SSD_XBC, cwa_ref, 0, oa_ref)
    oc_ref[:, 0:256] = uc[:, 0:256]
    oc_ref[:, 1024:WC] = uc[:, 1024:WC]
    ob_ref[...] = jnp.dot(x, w_ref[:, WA:WA + WB], preferred_element_type=F32)
    conv(uc[:, 256:1024], SSD_XBC, GDN_QKV, cwc_ref, 7, oc_ref)
    od_ref[...] = jnp.dot(x, w_ref[:, WA + WB + WC:WA + WB + WC + WD], preferred_element_type=F32)


def _inproj_conv_call(hp, l, wts, n_seq, tm):
    n_rows = hp.shape[0]
    tiles_per_seq = n_rows // n_seq // tm
    consts = [wts['w_in'], wts['rows'], wts['cw_a'], wts['cw_c']]
    in_specs = [pl.BlockSpec((tm, D_MODEL), lambda i: (i, 0))]
    in_specs += [pl.BlockSpec((None,) + a.shape[1:], lambda i, nd=a.ndim - 1: (l,) + (0,) * nd) for a in consts]
    widths = (WA, WB, WC, WD)
    out_specs = [pl.BlockSpec((tm, w), lambda i: (i, 0)) for w in widths]
    out_specs += [pl.BlockSpec((None, SUBLANES, SSD_XBC + GDN_QKV), lambda i: (i // tiles_per_seq, 0, 0))]
    out_shape = [jax.ShapeDtypeStruct((n_rows, w), F32) for w in widths]
    out_shape += [jax.ShapeDtypeStruct((n_seq, SUBLANES, SSD_XBC + GDN_QKV), F32)]
    *us, tails = pl.pallas_call(
        functools.partial(_inproj_conv_kernel, tiles_per_seq=tiles_per_seq),
        grid=(n_rows // tm,),
        in_specs=in_specs,
        out_specs=out_specs,
        out_shape=out_shape,
        scratch_shapes=[pltpu.VMEM((SUBLANES + tm, SSD_XBC + GDN_QKV), F32)] * 2,
        compiler_params=pltpu.CompilerParams(dimension_semantics=("arbitrary",),
                                             vmem_limit_bytes=VMEM_LIMIT),
        name="inproj_conv_prompt",
    )(hp, *consts)
    return us, tails


def _outproj_seq_kernel(mix_ref, x_ref, w_ref, g_ref, b_ref, o_ref, *, step, skip):
    t_pad = mix_ref.shape[0]
    assert 0 <= skip < step

    def product(i):
        return jnp.dot(mix_ref[i * step:(i + 1) * step, :], w_ref[...], preferred_element_type=F32)

    def finish(i, out):
        res = _layernorm_rows(DN_ALPHA * x_ref[i * step:(i + 1) * step, :] + out, g_ref[...], b_ref[...])
        if i == 0:
            o_ref[0:step - skip, :] = res[skip:step, :]
        else:
            o_ref[i * step - skip:(i + 1) * step - skip, :] = res

    prev = product(0)
    for i in range(1, t_pad // step):
        cur = product(i)
        finish(i - 1, prev)
        prev = cur
    finish(t_pad // step - 1, prev)


def _row_tiled_call(kernel, n_rows, tm, row_inputs, const_inputs, out_widths, out_dtypes, name, layer=None):
    in_specs = [pl.BlockSpec((tm, a.shape[1]), lambda i: (i, 0)) for a in row_inputs]
    if layer is None:
        in_specs += [pl.BlockSpec(a.shape, lambda i, nd=a.ndim: (0,) * nd) for a in const_inputs]
    else:
        in_specs += [pl.BlockSpec((None,) + a.shape[1:], lambda i, nd=a.ndim - 1: (layer,) + (0,) * nd)
                     for a in const_inputs]
    out_specs = [pl.BlockSpec((tm, w), lambda i: (i, 0)) for w in out_widths]
    out_shape = [jax.ShapeDtypeStruct((n_rows, w), dt) for w, dt in zip(out_widths, out_dtypes)]
    single = len(out_widths) == 1
    return pl.pallas_call(
        kernel,
        grid=(n_rows // tm,),
        in_specs=in_specs,
        out_specs=out_specs[0] if single else out_specs,
        out_shape=out_shape[0] if single else out_shape,
        compiler_params=pltpu.CompilerParams(dimension_semantics=("arbitrary",),
                                             vmem_limit_bytes=VMEM_LIMIT),
        name=name,
    )(*row_inputs, *const_inputs)


N_MIXER_U = 3
N_MIXER_CONST = 10
N_STATES = 7
N_MIXER_STATES = 3
N_LEVELS = 6
MXU_LEVELS = 2
SEQS_PER_STEP = 4


def _mixer_kernel(*refs, rows, nq):
    n_in = N_MIXER_U + N_MIXER_CONST + N_MIXER_STATES
    consts = refs[N_MIXER_U:N_MIXER_U + N_MIXER_CONST]
    j = pl.program_id(1)
    stages = [[] for _ in range(nq + 2)]
    finishers = []
    for s in range(nq):
        seq_refs = [r.at[s] for r in refs[:N_MIXER_U] + refs[n_in:]]
        prologue, state_free, carried, finish = _mixer_sequence(seq_refs[:N_MIXER_U], consts, seq_refs[N_MIXER_U:],
                                                                j, rows)
        stages[s] += prologue
        stages[s + 1] += state_free
        stages[s + 2] += carried
        finishers.append(finish)
    for chains in stages:
        _run_interleaved(chains)
    for finish in finishers:
        finish()


def _mixer_sequence(u_refs, consts, out_refs, j, rows):
    nch = rows // CHUNK
    nlev = N_LEVELS
    (ua_ref, uc_ref, ud_ref) = u_refs
    (rows_ref, cmat_ref, cpair_ref, bdm_ref, lr_ref, ea_ref, eb_ref, blk_ref, sel_ref, ones_ref) = consts
    (mix_ref, o_ssd, o_gdn, o_hg, ssd_sc, gdn_sc, hg_sc) = out_refs

    @pl.when(j == 0)
    def _():
        ssd_sc[...] = jnp.zeros_like(ssd_sc)
        gdn_sc[...] = jnp.zeros_like(gdn_sc)
        hg_sc[...] = jnp.zeros_like(hg_sc)

    mix_ref[:, 256:512] = jnp.zeros((rows, D_BRANCH), BF16)
    first_valid = jnp.where(j == 0, PAD_FRONT, 0)

    def valid_rows(r):
        return r >= first_valid

    def row(i, w):
        return rows_ref[i:i + 1, 0:w]

    act_a = lambda rs, lo, hi: ua_ref[rs, 256 + lo:256 + hi]
    act_c = lambda rs, lo, hi: uc_ref[rs, 256 + lo:256 + hi]

    tri_b = cmat_ref[0]

    def head_sum(x):
        return jnp.dot(x.astype(BF16), blk_ref[...], preferred_element_type=F32)

    def rms_finish(y, ss, g_row, z):
        return (y * lax.rsqrt(ss * (1.0 / HEAD_DIM) + RMS_EPS) * g_row * _silu(z)).astype(BF16)

    def hgrn_inputs(rs, valid_w):
        f_d = ud_ref[rs, 512:768]
        lsig = -_softplus(-f_d)
        t1 = row(11, D_BRANCH)
        t2 = row(12, D_BRANCH) + lsig
        logf = jnp.maximum(t1, t2) + _log1p_exp_neg(jnp.abs(t1 - t2))
        logf = jnp.where(valid_w, logf, 0.0)
        kd = row(13, D_BRANCH) * _sigmoid(-f_d)
        qd = _silu(ud_ref[rs, 256:512])
        vd = jnp.where(valid_w, ud_ref[rs, 768:1024], 0.0)
        xmm = _dot01_l(cmat_ref[1:2 + MXU_LEVELS].reshape((1 + MXU_LEVELS) * CHUNK, CHUNK), logf)
        gcd = xmm[0:CHUNK, :]
        xlev = []
        for lev in range(nlev - MXU_LEVELS):
            half = CHUNK >> (lev + 1)
            g3 = gcd.reshape(CHUNK // (2 * half), 2 * half, D_BRANCH)
            bound = jnp.broadcast_to(g3[:, half - 1:half, :], g3.shape).reshape(CHUNK, D_BRANCH)
            xlev.append(-jnp.abs(gcd - bound))
        xlev += [xmm[(1 + i) * CHUNK:(2 + i) * CHUNK, :] for i in range(MXU_LEVELS)]
        return logf, kd, qd, vd, gcd, xlev

    def prompt_block():
        lr0, lr1 = lr_ref[0], lr_ref[1]
        pairs = [slice(p * LANES, (p + 1) * LANES) for p in range(PAIRS)]
        data = [dict() for _ in range(nch)]

        def bd(x):
            xb = x.astype(BF16)
            return jnp.concatenate([xb * lr0, xb * lr1], axis=0)

        def mm_pair(a, b):
            return _dot(a, bd(b))

        def prologue(c):
            d = data[c]
            rs = slice(c * CHUNK, (c + 1) * CHUNK)
            d['rs'] = rs
            rr = lax.broadcasted_iota(jnp.int32, (CHUNK, LANES), 0) + c * CHUNK
            lane = lax.broadcasted_iota(jnp.int32, (CHUNK, LANES), 1)
            valid = valid_rows(rr)
            t = uc_ref[rs, 1024:1152] + row(1, LANES)
            vals = jnp.where((lane >= HEADS) & (lane < 2 * HEADS), _sigmoid(t), _softplus(t) * row(2, LANES))
            vals = jnp.where(valid, vals, 0.0)
            cs = _dot01_l(tri_b, vals)
            ex_v = _dot01_r(vals, ea_ref[...], pieces=2)
            q = act_c(rs, 0, 256)
            k = act_c(rs, 256, 512)
            qss = head_sum(q * q)
            kss = head_sum(k * k)
            valid_w = jnp.concatenate([valid, valid], axis=-1)
            logf, kd, qd, vd, gcd, xlev = hgrn_inputs(rs, valid_w)
            d.update(kd=kd, qd=qd, vd=vd)
            yield
            cs_pieces = _split(cs, 3)
            ex_c_all = jnp.dot(jnp.concatenate(cs_pieces, axis=0), eb_ref[...], preferred_element_type=F32)
            ex_c = sum(ex_c_all[i * CHUNK:(i + 1) * CHUNK] for i in range(3))
            crow = sum(lax.dot_general(sel_ref[...], piece, (((1,), (1,)), ((), ())), preferred_element_type=F32)
                       for piece in cs_pieces)
            d['diag'] = head_sum(qd * kd)
            yield
            dt_full, beta_full = ex_v[:, 0:256], ex_v[:, 256:512]
            gc, acum = ex_c[:, 0:256], ex_c[:, 256:512]
            d['prow'] = lambda r: jnp.concatenate([crow[r:r + 1, :], crow[r + 1:r + 2, :]], axis=1)
            xs = act_a(rs, 0, 256)
            alast = acum[CHUNK - 1:CHUNK, :]
            xdt = xs * dt_full
            d.update(acum=acum, alast=alast, xdt=xdt, xdt_end=xdt * jnp.exp(alast - acum), eacum=jnp.exp(acum))
            v = act_c(rs, 512, 768)
            q = q * lax.rsqrt(qss + L2_EPS) * (HEAD_DIM ** -0.5)
            k = k * lax.rsqrt(kss + L2_EPS)
            glast = gc[CHUNK - 1:CHUNK, :]
            egc = jnp.exp(gc)
            d.update(q=q, k=k, gc=gc, glast=glast, beta=beta_full, vb=v * beta_full, kbe=k * beta_full * egc,
                     qe=q * egc, kend=k * jnp.exp(glast - gc))
            gld = gcd[CHUNK - 1:CHUNK, :]
            g1, g2, g3 = [t.astype(F32) for t in _split(gld, 3)]
            rid = lax.broadcasted_iota(jnp.int32, (2 * SUBLANES, D_BRANCH), 0)
            d.update(qed=qd * jnp.exp(gcd), kend_d=kd * jnp.exp(gld - gcd), xlev=xlev,
                     g16=jnp.where(rid == 0, g1, jnp.where(rid == 1, g2, jnp.where(rid == 2, g3, 0.0))))

        def ssd_a(c, p):
            d, ps = data[c], pairs[p]
            g0 = 256 + p * HEAD_DIM
            bg = act_a(d['rs'], g0, g0 + HEAD_DIM)
            cg = act_a(d['rs'], g0 + LANES, g0 + LANES + HEAD_DIM)
            cb2 = _dot_nt(cg, jnp.concatenate([bg, bg], axis=0))
            lm = jnp.exp(jnp.minimum(d['acum'][:, ps] - d['prow'](HEADS + 2 * p), 0.0)) * cpair_ref[0]
            d['cbl', p] = cb2 * lm
            d['bg', p], d['cg', p] = bg, cg
            yield

        def gdn_a(c, p):
            d, ps = data[c], pairs[p]
            k = d['k'][:, ps]
            dec = jnp.exp(jnp.minimum(d['gc'][:, ps] - d['prow'](2 * p), 0.0)) * cpair_ref[0]
            kq = _dot_nt(jnp.concatenate([k, d['q'][:, ps]], axis=0), bd(k))
            yield
            m = kq[0:CHUNK] * dec * d['beta'][:, ps] * cpair_ref[1]
            d['aq', p] = kq[CHUNK:2 * CHUNK] * dec
            acc = cpair_ref[2] - m
            mp = mm_pair(m, m)
            yield
            for _ in range(nlev - 2):
                acc_add = mm_pair(acc, mp)
                mp = mm_pair(mp, mp)
                yield
                acc = acc + acc_add
            acc = acc + mm_pair(acc, mp)
            yield
            d['uw', p] = _dot(acc, jnp.concatenate([bd(d['vb'][:, ps]), bd(d['kbe'][:, ps])], axis=1))
            yield

        def hgrn_a(c, p):
            d, ps = data[c], pairs[p]
            qd, kd = d['qd'][:, ps], d['kd'][:, ps]
            amat = jnp.zeros((CHUNK, LANES), F32)
            for lev in range(nlev):
                z = jnp.exp(d['xlev'][lev][:, ps])
                amat = amat + _dot_nt(qd * z, bd(kd * z)) * cpair_ref[3 + lev]
                if lev % 2 == 1:
                    yield
            d['amat', p] = amat
            d['dcol', p] = jnp.exp(_dot_tn(d['g16'][:, ps], ones_ref[...]))
            yield

        def ssd_b():
            for c in range(nch):
                d = data[c]
                ya = []
                for p, ps in enumerate(pairs):
                    s_pk = ssd_sc[p]
                    ya.append(mm_pair(d['cbl', p], d['xdt'][:, ps]) + _dot(d['cg', p], s_pk) * d['eacum'][:, ps])
                    ssd_sc[p] = s_pk * jnp.exp(d['alast'][:, ps]) + _dot_tn(d['bg', p], d['xdt_end'][:, ps])
                yield
                ya = jnp.concatenate(ya, axis=-1) + row(3, D_BRANCH) * act_a(d['rs'], 0, 256)
                ss = head_sum(ya * ya)
                yield
                mix_ref[d['rs'], 0:256] = rms_finish(ya, ss, row(4, D_BRANCH), ua_ref[d['rs'], 0:256])

        def gdn_b():
            bdm = bdm_ref[...]
            for c in range(nch):
                d = data[c]
                ws, s_old = [], []
                for p, ps in enumerate(pairs):
                    s_bd = gdn_sc[p]
                    s_old.append(s_bd)
                    ws.append(_dot(jnp.concatenate([d['uw', p][:, LANES:2 * LANES], d['qe'][:, ps]], axis=0), s_bd))
                yield
                yc = []
                for p, ps in enumerate(pairs):
                    v_new = d['uw', p][:, 0:LANES] - ws[p][0:CHUNK]
                    yc.append(ws[p][CHUNK:2 * CHUNK] + mm_pair(d['aq', p], v_new))
                    gdn_sc[p] = s_old[p] * jnp.exp(d['glast'][:, ps]) + _dot_tn(d['kend'][:, ps], v_new) * bdm
                yield
                yc = jnp.concatenate(yc, axis=-1)
                ss = head_sum(yc * yc)
                yield
                mix_ref[d['rs'], 512:768] = rms_finish(yc, ss, row(10, D_BRANCH), uc_ref[d['rs'], 0:256])

        def hgrn_b():
            bdm = bdm_ref[...]
            for c in range(nch):
                d = data[c]
                yd = []
                for p, ps in enumerate(pairs):
                    s_bd = hg_sc[p]
                    vd = d['vd'][:, ps]
                    yd.append(mm_pair(d['amat', p], vd) + d['diag'][:, ps] * vd + _dot(d['qed'][:, ps], s_bd))
                    hg_sc[p] = s_bd * d['dcol', p] + _dot_tn(d['kend_d'][:, ps], vd) * bdm
                yield
                yd = jnp.concatenate(yd, axis=-1)
                ss = head_sum(yd * yd)
                yield
                mix_ref[d['rs'], 768:1024] = rms_finish(yd, ss, row(14, D_BRANCH), ud_ref[d['rs'], 0:256])

        return ([prologue(c) for c in range(nch)],
                [f(c, p) for c in range(nch) for p in range(PAIRS) for f in (gdn_a, hgrn_a, ssd_a)],
                [gdn_b(), hgrn_b(), ssd_b()])

    def finish():
        @pl.when(j == pl.num_programs(1) - 1)
        def _():
            for p in range(PAIRS):
                for e in range(2):
                    es = slice(e * HEAD_DIM, (e + 1) * HEAD_DIM)
                    o_ssd[2 * p + e] = ssd_sc[p, :, es]
                    o_gdn[2 * p + e] = gdn_sc[p, es, es]
                    o_hg[2 * p + e] = hg_sc[p, es, es]

    return (*prompt_block(), finish)


def _mixer_constants():
    i = np.arange(CHUNK)[:, None]
    jn = np.arange(CHUNK)[None, :]
    tri = (jn <= i).astype(np.float32)
    strict = (jn < i).astype(np.float32)
    eye = np.eye(CHUNK, dtype=np.float32)
    wlev, mlev = [], []
    for lev in range(N_LEVELS):
        b = CHUNK >> (lev + 1)
        blk_i, pos_i = i // (2 * b), i % (2 * b)
        mid = blk_i * 2 * b + b
        upper = pos_i >= b
        w = np.where(upper, (jn >= mid) & (jn <= i), (jn > i) & (jn < mid)).astype(np.float32)
        msk = (((jn // (2 * b)) == blk_i) & upper & ((jn % (2 * b)) < b)).astype(np.float32)
        wlev.append(w)
        mlev.append(msk)
    cmat = np.stack([tri, tri] + wlev[N_LEVELS - MXU_LEVELS:])
    cmask = np.stack([tri, strict, eye] + mlev)
    cpair = np.concatenate([cmask, cmask], axis=-1)
    bdm = np.kron(np.eye(2, dtype=np.float32), np.ones((HEAD_DIM, HEAD_DIM), np.float32))
    lane = np.arange(LANES)[None, :]
    lr = np.stack([np.broadcast_to(lane < HEAD_DIM, (CHUNK, LANES)),
                   np.broadcast_to(lane >= HEAD_DIM, (CHUNK, LANES))]).astype(np.float32)
    e_a = np.zeros((LANES, 512), np.float32)
    e_b = np.zeros((LANES, 512), np.float32)
    blk = np.zeros((256, 256), np.float32)
    sel = np.zeros((2 * SUBLANES, LANES), np.float32)
    for h in range(HEADS):
        hs = slice(h * HEAD_DIM, (h + 1) * HEAD_DIM)
        e_a[h, hs] = 1.0
        e_a[HEADS + h, 256 + h * HEAD_DIM:256 + (h + 1) * HEAD_DIM] = 1.0
        e_b[2 * HEADS + h, hs] = 1.0
        e_b[3 * HEADS + h, 256 + h * HEAD_DIM:256 + (h + 1) * HEAD_DIM] = 1.0
        blk[hs, hs] = 1.0
        sel[h, 2 * HEADS + h] = 1.0
        sel[HEADS + h, 3 * HEADS + h] = 1.0
    ones = np.ones((2 * SUBLANES, LANES), np.float32)
    return (jnp.asarray(cmat, BF16), jnp.asarray(cpair, F32), jnp.asarray(bdm, F32),
            jnp.asarray(lr, BF16), jnp.asarray(e_a, BF16), jnp.asarray(e_b, BF16), jnp.asarray(blk, BF16),
            jnp.asarray(sel, BF16), jnp.asarray(ones, BF16))


def _layer_spec(a, l):
    nd = a.ndim - 1
    return pl.BlockSpec((None,) + a.shape[1:], lambda b, j: (l,) + (0,) * nd)


def _mixer_call(us, l, wts, consts, acc, *, rows):
    n_seq, t_pad, _ = us[0].shape
    nblk = t_pad // rows
    nq = SEQS_PER_STEP if n_seq % SEQS_PER_STEP == 0 else 1
    rmap = lambda b, j: (b, j, 0)

    def state_spec(tail):
        return pl.BlockSpec((None, nq) + tail, lambda b, j: (l, b) + (0,) * len(tail))

    layer_consts = [wts['rows']]
    assert len(us) == N_MIXER_U and len(layer_consts) + len(consts) == N_MIXER_CONST
    in_specs = [pl.BlockSpec((nq, rows, w), rmap) for w in (WA, WC, WD)]
    in_specs += [_layer_spec(a, l) for a in layer_consts]
    in_specs += [pl.BlockSpec(a.shape, lambda b, j, nd=a.ndim: (0,) * nd) for a in consts]
    inputs = list(us) + layer_consts + list(consts)
    mat = (HEADS, HEAD_DIM, HEAD_DIM)
    tails = [mat, mat, mat]
    scratch = [pltpu.VMEM((nq, PAIRS, HEAD_DIM, LANES), F32),
               pltpu.VMEM((nq, PAIRS, LANES, LANES), F32),
               pltpu.VMEM((nq, PAIRS, LANES, LANES), F32)]
    assert len(acc) == len(tails) == N_MIXER_STATES
    n_before_acc = len(inputs)
    inputs += list(acc)
    in_specs += [pl.BlockSpec(memory_space=pl.ANY)] * len(acc)
    out_shape = [jax.ShapeDtypeStruct((n_seq, t_pad, D_MODEL), BF16)]
    out_shape += [jax.ShapeDtypeStruct(a.shape, a.dtype) for a in acc]
    out_specs = [pl.BlockSpec((nq, rows, D_MODEL), rmap)] + [state_spec(t) for t in tails]
    aliases = {n_before_acc + i: 1 + i for i in range(len(acc))}
    return pl.pallas_call(
        functools.partial(_mixer_kernel, rows=rows, nq=nq),
        grid=(n_seq // nq, nblk),
        in_specs=in_specs,
        out_specs=out_specs,
        out_shape=out_shape,
        scratch_shapes=scratch,
        input_output_aliases=aliases,
        compiler_params=pltpu.CompilerParams(dimension_semantics=("arbitrary", "arbitrary"),
                                             vmem_limit_bytes=VMEM_LIMIT),
        name="mixer_prompt",
    )(*inputs)


S5_TIME_BLOCK = PAD_FRONT


def _s5_prompt_kernel(zu_ref, rows_ref, wbre_ref, wbim_ref, cre_ref, cim_ref, glu_ref, lb_ref, perm_ref,
                      mix_in, sre_in, sim_in, mix_ref, o_re, o_im, hre_ref, him_ref):
    del mix_in, sre_in, sim_in
    nb, tb, _ = zu_ref.shape
    rows = nb * tb
    j = pl.program_id(1)

    @pl.when(j == 0)
    def _():
        o_re[...] = jnp.zeros_like(o_re)
        o_im[...] = jnp.zeros_like(o_im)

    zu = zu_ref[...].reshape(rows, 2 * D_BRANCH)
    zu = jnp.where(j * tb >= PAD_FRONT, zu, 0.0)
    perm = perm_ref[...]
    zu = _dot01_l(perm, zu)
    z_p, u_p = zu[:, 0:D_BRANCH], zu[:, D_BRANCH:2 * D_BRANCH]
    hre_ref[...] = _dot(u_p, wbre_ref[...])
    him_ref[...] = _dot(u_p, wbim_ref[...])
    lr = jnp.broadcast_to(lb_ref[0:1, 0:S5_LANES], (nb, S5_LANES))
    li = jnp.broadcast_to(lb_ref[0:1, S5_LANES:2 * S5_LANES], (nb, S5_LANES))

    def step(tau, carry):
        cr, ci = carry
        rs = pl.ds(pl.multiple_of(tau * nb, nb), nb)
        xr = lr * cr - li * ci + hre_ref[rs, :]
        xi = lr * ci + li * cr + him_ref[rs, :]
        hre_ref[rs, :] = xr
        him_ref[rs, :] = xi
        return xr, xi

    cr, ci = lax.fori_loop(0, tb, step, (o_re[...], o_im[...]), unroll=2)
    o_re[...] = cr
    o_im[...] = ci

    row = lambda i: rows_ref[i:i + 1, 0:D_BRANCH]
    y5 = _dot(hre_ref[...], cre_ref[...]) - _dot(him_ref[...], cim_ref[...]) + row(5) * u_p
    y5 = _gelu_tanh(y5)
    y5 = y5 * _sigmoid(_dot(y5, glu_ref[...]) + row(6))
    y5 = (y5 * _silu(z_p)).astype(BF16)
    y5 = lax.dot_general(perm, y5, (((0,), (0,)), ((), ())), preferred_element_type=F32).astype(BF16)
    mix_ref[...] = y5.reshape(nb, tb, D_BRANCH)


def _s5_prompt_call(ub, l, wts, perm, mix, acc_re, acc_im):
    n_seq, t_pad, _ = ub.shape
    tb = S5_TIME_BLOCK
    assert n_seq % SUBLANES == 0 and t_pad % tb == 0 and PAD_FRONT % tb == 0
    layer_consts = [wts[k] for k in ('rows', 'wb_re', 'wb_im', 'c_re', 'c_im', 'glu_w', 's5lb')]
    in_specs = [pl.BlockSpec((SUBLANES, tb, WB), lambda g, j: (g, j, 0))]
    in_specs += [_layer_spec(a, l) for a in layer_consts]
    in_specs += [pl.BlockSpec(perm.shape, lambda g, j: (0, 0))]
    in_specs += [pl.BlockSpec(memory_space=pl.ANY)] * 3
    state_spec = pl.BlockSpec((None, SUBLANES, S5_LANES), lambda g, j: (l, g, 0))
    rows = SUBLANES * tb
    return pl.pallas_call(
        _s5_prompt_kernel,
        grid=(n_seq // SUBLANES, t_pad // tb),
        in_specs=in_specs,
        out_specs=[pl.BlockSpec((SUBLANES, tb, D_BRANCH), lambda g, j: (g, j, 1)), state_spec, state_spec],
        out_shape=[jax.ShapeDtypeStruct(mix.shape, mix.dtype), jax.ShapeDtypeStruct(acc_re.shape, F32),
                   jax.ShapeDtypeStruct(acc_im.shape, F32)],
        scratch_shapes=[pltpu.VMEM((rows, S5_LANES), F32), pltpu.VMEM((rows, S5_LANES), F32)],
        input_output_aliases={len(in_specs) - 3: 0, len(in_specs) - 2: 1, len(in_specs) - 1: 2},
        compiler_params=pltpu.CompilerParams(dimension_semantics=("arbitrary", "arbitrary"),
                                             vmem_limit_bytes=VMEM_LIMIT),
        name="s5_prompt",
    )(ub, *layer_consts, perm, mix, acc_re, acc_im)


SAMPLE_COL_WIDTHS = (SSD_XBC, 4 * HEADS, 4 * HEADS, D_BRANCH, D_BRANCH, D_BRANCH, D_BRANCH, GDN_QKV, D_BRANCH,
                     D_BRANCH, D_BRANCH, D_BRANCH, D_BRANCH)
(C_CONVB_A, C_SBIAS, C_SSCALE, C_SSD_D, C_SSD_G, C_S5_D, C_GLU_B, C_CONVB_C, C_GDN_G, C_LOGLB, C_LOG1M, C_ONEM,
 C_HG_G) = [int(v) for v in np.cumsum((0,) + SAMPLE_COL_WIDTHS[:-1])]
U_ZA, U_XBC, U_ZB, U_UB, U_ZC, U_QKV, U_SMALL, U_ZD, U_QD, U_FD, U_ID = (
    0, 256, WA, WA + 256, WA + WB, WA + WB + 256, WA + WB + 1024, WA + WB + WC, WA + WB + WC + 256,
    WA + WB + WC + 512, WA + WB + WC + 768)
V_HALF = HEAD_DIM // 2


def _sample_inproj_kernel(h_ref, wt_ref, o_ref):
    o_ref[...] = lax.dot_general(wt_ref[...], h_ref[...].astype(BF16), (((1,), (1,)), ((), ())),
                                 preferred_element_type=F32)


def _sample_outproj_kernel(mixt_ref, x_ref, w_ref, g_ref, b_ref, o_ref):
    out = lax.dot_general(mixt_ref[...], w_ref[...], (((0,), (0,)), ((), ())), preferred_element_type=F32)
    o_ref[...] = _layernorm_rows(DN_ALPHA * x_ref[...] + out, g_ref[...], b_ref[...])


def _sample_mixer_kernel(ut_ref, cols_ref, cwa_ref, cwc_ref, lb_ref, wbre_ref, wbim_ref, cre_ref, cim_ref, glu_ref,
                         i_ssd, i_ca, i_s5re, i_s5im, i_gdn, i_cc, i_hg, *rest, n_tok, n_seq):
    (mix_ref, o_ssd, o_ca, o_s5re, o_s5im, o_gdn, o_cc, o_hg) = rest[N_STATES:N_STATES + 1 + N_STATES]
    (acta_ref, actc_ref, small_ref, hre_ref, him_ref, gq_ref, gk_ref, hq_ref, hk_ref, hf_ref,
     ya_ref, yc_ref, yd_ref) = rest[N_STATES + 1 + N_STATES:]
    h = pl.program_id(0)
    vh = pl.program_id(1)
    toks = [slice(t * n_seq, (t + 1) * n_seq) for t in range(n_tok)]
    hrow = pl.multiple_of(h * HEAD_DIM, HEAD_DIM)

    def col(c0, n, off=0):
        return cols_ref[pl.ds(c0 + off, n), :]

    @pl.when((h == 0) & (vh == 0))
    def _():
        for u0, width, i_c, o_c, cw_ref, act_ref, cb in ((U_XBC, SSD_XBC, i_ca, o_ca, cwa_ref, acta_ref, C_CONVB_A),
                                                        (U_QKV, GDN_QKV, i_cc, o_cc, cwc_ref, actc_ref, C_CONVB_C)):
            xx = [i_c[i] for i in range(CONV_W - 1)] + [ut_ref[u0:u0 + width, ts] for ts in toks]
            for t in range(n_tok):
                acc = cols_ref[cb:cb + width, :]
                for w in range(CONV_W):
                    acc = acc + cw_ref[w] * xx[t + w]
                act_ref[t] = _silu(acc)
            for i in range(CONV_W - 1):
                o_c[i] = xx[n_tok + i]
        rid = lax.broadcasted_iota(jnp.int32, (4 * HEADS, n_seq), 0)
        for t, ts in enumerate(toks):
            v = ut_ref[U_SMALL:U_SMALL + 4 * HEADS, ts] + cols_ref[C_SBIAS:C_SBIAS + 4 * HEADS, :]
            small_ref[t] = jnp.where((rid >= HEADS) & (rid < 2 * HEADS), _sigmoid(v),
                                     _softplus(v) * cols_ref[C_SSCALE:C_SSCALE + 4 * HEADS, :])
        u_b = ut_ref[U_UB:U_UB + D_BRANCH, :]
        bu_re = _dot(wbre_ref[...], u_b)
        bu_im = _dot(wbim_ref[...], u_b)
        sr, si = i_s5re[...], i_s5im[...]
        lr, li = lb_ref[0], lb_ref[1]
        for t, ts in enumerate(toks):
            sr, si = lr * sr - li * si + bu_re[:, ts], lr * si + li * sr + bu_im[:, ts]
            hre_ref[:, ts] = sr
            him_ref[:, ts] = si
        o_s5re[...] = sr
        o_s5im[...] = si
        y5 = _dot(cre_ref[...], hre_ref[...]) - _dot(cim_ref[...], him_ref[...])
        for ts in toks:
            y = _gelu_tanh(y5[:, ts] + cols_ref[C_S5_D:C_S5_D + D_BRANCH, :] * u_b[:, ts])
            g = _dot(glu_ref[...], y) + cols_ref[C_GLU_B:C_GLU_B + D_BRANCH, :]
            mix_ref[D_BRANCH:2 * D_BRANCH, ts] = (y * _sigmoid(g) * _silu(ut_ref[U_ZB:U_ZB + D_BRANCH, ts])).astype(BF16)

    vrow = pl.multiple_of(vh * V_HALF, V_HALF)
    vsel = pl.ds(hrow + vrow, V_HALF)

    def scal(t, r):
        return small_ref[t, pl.ds(r * HEADS + h, 1), :]

    grp = (h // 2) * HEAD_DIM
    xdt = [acta_ref[t, vsel, :] * scal(t, 0) for t in range(n_tok)]
    ea = [jnp.exp(scal(t, 3)) for t in range(n_tok)]

    def ssd_row(n, ys):
        s = i_ssd[n]
        ys = list(ys)
        for t in range(n_tok):
            s = s * ea[t] + acta_ref[t, pl.ds(256 + grp + n, 1), :] * xdt[t]
            ys[t] = ys[t] + acta_ref[t, pl.ds(384 + grp + n, 1), :] * s
        o_ssd[n] = s
        return tuple(ys)

    zero = jnp.zeros((V_HALF, n_seq), F32)
    ys = lax.fori_loop(0, HEAD_DIM, ssd_row, (zero,) * n_tok, unroll=2)
    for t in range(n_tok):
        ya_ref[t, pl.ds(vrow, V_HALF), :] = ys[t] + col(C_SSD_D, V_HALF, hrow + vrow) * acta_ref[t, vsel, :]

    @pl.when(vh == 0)
    def _():
        for t in range(n_tok):
            q = actc_ref[t, pl.ds(hrow, HEAD_DIM), :]
            k = actc_ref[t, pl.ds(256 + hrow, HEAD_DIM), :]
            gq_ref[t] = q * lax.rsqrt(jnp.sum(q * q, axis=0, keepdims=True) + L2_EPS) * (HEAD_DIM ** -0.5)
            gk_ref[t] = k * lax.rsqrt(jnp.sum(k * k, axis=0, keepdims=True) + L2_EPS)

    for t in range(n_tok):
        alpha = jnp.exp(scal(t, 2))
        src = i_gdn if t == 0 else o_gdn

        def decay_row(kk, pred, alpha=alpha, src=src, t=t):
            s = src[kk] * alpha
            o_gdn[kk] = s
            return pred + gk_ref[t, pl.ds(kk, 1), :] * s

        pred = lax.fori_loop(0, HEAD_DIM, decay_row, zero, unroll=4)
        v_new = scal(t, 1) * (actc_ref[t, pl.ds(512 + hrow + vrow, V_HALF), :] - pred)

        def update_row(kk, o, v_new=v_new, t=t):
            s = o_gdn[kk] + gk_ref[t, pl.ds(kk, 1), :] * v_new
            o_gdn[kk] = s
            return o + gq_ref[t, pl.ds(kk, 1), :] * s

        yc_ref[t, pl.ds(vrow, V_HALF), :] = lax.fori_loop(0, HEAD_DIM, update_row, zero, unroll=4)

    @pl.when(vh == 0)
    def _():
        for t, ts in enumerate(toks):
            f_d = ut_ref[pl.ds(U_FD + hrow, HEAD_DIM), ts]
            t1 = col(C_LOGLB, HEAD_DIM, hrow)
            t2 = col(C_LOG1M, HEAD_DIM, hrow) - _softplus(-f_d)
            logf = jnp.maximum(t1, t2) + _log1p_exp_neg(jnp.abs(t1 - t2))
            hf_ref[t] = jnp.exp(logf)
            hk_ref[t] = col(C_ONEM, HEAD_DIM, hrow) * _sigmoid(-f_d)
            hq_ref[t] = _silu(ut_ref[pl.ds(U_QD + hrow, HEAD_DIM), ts])

    vd = [ut_ref[pl.ds(U_ID + hrow + vrow, V_HALF), ts] for ts in toks]

    def hgrn_row(kk, os_):
        s = i_hg[kk]
        os_ = list(os_)
        for t in range(n_tok):
            s = s * hf_ref[t, pl.ds(kk, 1), :] + hk_ref[t, pl.ds(kk, 1), :] * vd[t]
            os_[t] = os_[t] + hq_ref[t, pl.ds(kk, 1), :] * s
        o_hg[kk] = s
        return tuple(os_)

    os_ = lax.fori_loop(0, HEAD_DIM, hgrn_row, (zero,) * n_tok, unroll=2)
    for t in range(n_tok):
        yd_ref[t, pl.ds(vrow, V_HALF), :] = os_[t]

    @pl.when(vh == HEAD_DIM // V_HALF - 1)
    def _():
        for y_ref, g0, z0, m0 in ((ya_ref, C_SSD_G, U_ZA, 0), (yc_ref, C_GDN_G, U_ZC, 2 * D_BRANCH),
                                  (yd_ref, C_HG_G, U_ZD, 3 * D_BRANCH)):
            for t, ts in enumerate(toks):
                y = y_ref[t]
                ms = jnp.mean(y * y, axis=0, keepdims=True)
                z = ut_ref[pl.ds(z0 + hrow, HEAD_DIM), ts]
                mix_ref[pl.ds(m0 + hrow, HEAD_DIM), ts] = (
                    y * lax.rsqrt(ms + RMS_EPS) * col(g0, HEAD_DIM, hrow) * _silu(z)).astype(BF16)


def _sample_mixer_call(ut, l, sw, states_t, acc, *, n_tok, n_seq):
    consts = [sw[k] for k in ('cols', 'cw_a', 'cw_c', 'lb', 'wb_re', 'wb_im', 'c_re', 'c_im', 'glu_w')]
    whole = lambda a: pl.BlockSpec((None,) + a.shape[1:], lambda h, v, nd=a.ndim - 1: (l,) + (0,) * nd)
    mat_spec = pl.BlockSpec((None, None, HEAD_DIM, V_HALF, n_seq), lambda h, v: (l, h, 0, v, 0))
    state_specs = [mat_spec, whole(states_t[1]), whole(states_t[2]), whole(states_t[3]), mat_spec,
                   whole(states_t[5]), mat_spec]
    n_cols = ut.shape[1]
    in_specs = [pl.BlockSpec(ut.shape, lambda h, v: (0, 0))] + [whole(a) for a in consts] + state_specs
    in_specs += [pl.BlockSpec(memory_space=pl.ANY)] * N_STATES
    inputs = [ut] + consts + list(states_t) + list(acc)
    out_shape = [jax.ShapeDtypeStruct((D_MODEL, n_cols), BF16)] + [jax.ShapeDtypeStruct(a.shape, a.dtype) for a in acc]
    out_specs = [pl.BlockSpec((D_MODEL, n_cols), lambda h, v: (0, 0))] + state_specs
    n_before_acc = 1 + len(consts) + N_STATES
    tok_tile = lambda rows: pltpu.VMEM((n_tok, rows, n_seq), F32)
    scratch = [tok_tile(SSD_XBC), tok_tile(GDN_QKV), tok_tile(4 * HEADS),
               pltpu.VMEM((S5_LANES, n_cols), F32), pltpu.VMEM((S5_LANES, n_cols), F32)]
    scratch += [tok_tile(HEAD_DIM)] * 8
    return pl.pallas_call(
        functools.partial(_sample_mixer_kernel, n_tok=n_tok, n_seq=n_seq),
        grid=(HEADS, HEAD_DIM // V_HALF),
        in_specs=in_specs,
        out_specs=out_specs,
        out_shape=out_shape,
        scratch_shapes=scratch,
        input_output_aliases={n_before_acc + i: 1 + i for i in range(N_STATES)},
        compiler_params=pltpu.CompilerParams(dimension_semantics=("arbitrary", "arbitrary"),
                                             vmem_limit_bytes=VMEM_LIMIT),
        name="mixer_sample",
    )(*inputs)


W_IN_ROWS_PER_STEP = 128


def _regroup_w_in_kernel(w_ref, o_ref, ot_ref):
    w = w_ref[...]
    small = jnp.concatenate([w[:, 768:772], w[:, 2308:2316], w[:, 768:772],
                             jnp.zeros((w.shape[0], LANES - 4 * HEADS), F32)], axis=1)
    wr = jnp.concatenate([w[:, 0:768], w[:, 772:1284], w[:, 1284:2308], small, w[:, 2316:3340]], axis=1)
    o_ref[...] = wr.astype(BF16)
    ot_ref[...] = wr.T.astype(BF16)


def _regroup_w_in(w_in):
    depth, d_model, d_in = w_in.shape
    tk = W_IN_ROWS_PER_STEP
    w_cols = WA + WB + WC + WD
    return pl.pallas_call(
        _regroup_w_in_kernel,
        grid=(depth, d_model // tk),
        in_specs=[pl.BlockSpec((None, tk, d_in), lambda l, i: (l, i, 0))],
        out_specs=[pl.BlockSpec((None, tk, w_cols), lambda l, i: (l, i, 0)),
                   pl.BlockSpec((None, w_cols, tk), lambda l, i: (l, 0, i))],
        out_shape=[jax.ShapeDtypeStruct((depth, d_model, w_cols), BF16),
                   jax.ShapeDtypeStruct((depth, w_cols, d_model), BF16)],
        compiler_params=pltpu.CompilerParams(dimension_semantics=("arbitrary", "arbitrary"),
                                             vmem_limit_bytes=VMEM_LIMIT),
        name="regroup_w_in",
    )(w_in.astype(F32))


def _pad_rows(v, width=ROW_W):
    v = v.astype(F32).reshape(v.shape[0], -1)
    return jnp.pad(v, ((0, 0), (0, width - v.shape[1])))


def _stacked_weights(p, lbs):
    depth = lbs.shape[0]
    rep = lambda v: jnp.repeat(v.astype(F32), HEAD_DIM, axis=-1)
    zeros_h = jnp.zeros((depth, HEADS), F32)
    small_bias = jnp.concatenate([p['ssd_dt_bias'], zeros_h, p['gdn_dt_bias'], p['ssd_dt_bias']], axis=1)
    small_scale = jnp.concatenate([jnp.ones((depth, HEADS), F32), zeros_h, -jnp.exp(p['gdn_a_log']),
                                   -jnp.exp(p['ssd_a_log'])], axis=1)
    zrow = jnp.zeros((depth, ROW_W), F32)
    rows = jnp.stack([
        _pad_rows(p['ssd_conv_b']), _pad_rows(small_bias), _pad_rows(small_scale),
        _pad_rows(rep(p['ssd_d'])), _pad_rows(p['ssd_norm_g']), _pad_rows(p['s5_d']),
        _pad_rows(p['s5_glu_b']), _pad_rows(p['gdn_conv_b']), zrow, zrow,
        _pad_rows(p['gdn_norm_g']), _pad_rows(jnp.log(lbs)), _pad_rows(jnp.log1p(-lbs)), _pad_rows(1.0 - lbs),
        _pad_rows(p['hg_norm_g']), zrow], axis=1)

    pad_cw = lambda cw: jnp.pad(cw.astype(F32), ((0, 0), (0, SUBLANES - CONV_W), (0, 0)))

    lam_re, lam_im = p['s5_lam_re'].astype(F32), p['s5_lam_im'].astype(F32)
    dt = jnp.exp(p['s5_log_dt'].astype(F32))[..., None]
    mag = jnp.exp(lam_re * dt)
    ang = lam_im * dt
    lb_re, lb_im = mag * jnp.cos(ang), mag * jnp.sin(ang)
    den = jnp.square(lam_re) + jnp.square(lam_im)
    nr = lb_re - 1.0
    coef_re = (nr * lam_re + lb_im * lam_im) / den
    coef_im = (lb_im * lam_re - nr * lam_im) / den
    b_re, b_im = p['s5_b_re'].astype(F32), p['s5_b_im'].astype(F32)
    bb_re = coef_re[..., None] * b_re - coef_im[..., None] * b_im
    bb_im = coef_re[..., None] * b_im + coef_im[..., None] * b_re
    eye_g = jnp.eye(S5_GROUPS, dtype=F32)
    bd_in = lambda bb: jnp.einsum('lgnq,gh->lgqhn', bb, eye_g).reshape(depth, D_BRANCH, S5_LANES).astype(BF16)
    bd_out = lambda c: jnp.einsum('lgqn,gh->lgnhq', c.astype(F32), eye_g).reshape(depth, S5_LANES, D_BRANCH).astype(BF16)

    s5lb = jnp.concatenate([lb_re.reshape(depth, 1, -1), lb_im.reshape(depth, 1, -1)], axis=-1)

    lanes = lambda v: jnp.broadcast_to(v.astype(F32)[..., None], v.shape + (LANES,))
    col_vecs = [p['ssd_conv_b'], small_bias, small_scale, rep(p['ssd_d']), p['ssd_norm_g'], p['s5_d'], p['s5_glu_b'],
                p['gdn_conv_b'], p['gdn_norm_g'], jnp.log(lbs), jnp.log1p(-lbs), 1.0 - lbs, p['hg_norm_g']]
    assert [int(np.prod(v.shape[1:])) for v in col_vecs] == list(SAMPLE_COL_WIDTHS)
    cols_s = lanes(jnp.concatenate([v.astype(F32).reshape(depth, -1) for v in col_vecs], axis=1))
    bd_in_t = lambda bb: jnp.einsum('lgnq,gh->lgnhq', bb, eye_g).reshape(depth, S5_LANES, D_BRANCH).astype(BF16)
    bd_out_t = lambda c: jnp.einsum('lgqn,gh->lgqhn', c.astype(F32), eye_g).reshape(depth, D_BRANCH, S5_LANES).astype(BF16)
    sample = dict(cols=cols_s, cw_a=lanes(p['ssd_conv_w']), cw_c=lanes(p['gdn_conv_w']),
                  lb=lanes(jnp.stack([lb_re.reshape(depth, -1), lb_im.reshape(depth, -1)], axis=1)),
                  wb_re=bd_in_t(bb_re), wb_im=bd_in_t(bb_im), c_re=bd_out_t(p['s5_c_re']), c_im=bd_out_t(p['s5_c_im']),
                  glu_w=jnp.swapaxes(p['s5_glu_w'], 1, 2).astype(BF16))

    w_in_r, w_in_t = _regroup_w_in(p['w_in'])
    return dict(w_in=w_in_r, w_in_t=w_in_t, rows=rows, cw_a=pad_cw(p['ssd_conv_w']), cw_c=pad_cw(p['gdn_conv_w']),
                wb_re=bd_in(bb_re), wb_im=bd_in(bb_im), c_re=bd_out(p['s5_c_re']), c_im=bd_out(p['s5_c_im']),
                glu_w=p['s5_glu_w'].astype(BF16), s5lb=s5lb, w_out=p['w_out'].astype(BF16),
                ln_g=p['ln_g'].astype(F32)[:, None, :], ln_b=p['ln_b'].astype(F32)[:, None, :], sample=sample)


def _pick_tile(n_rows, candidates):
    for t in candidates:
        if n_rows % t == 0:
            return t
    raise ValueError(f"no row tile for {n_rows}")


def kernel(x_prompt, x_sample, state_ssd, state_ssd_conv, state_s5_re, state_s5_im, state_gdn, state_gdn_conv, state_hgrn, meta_tokens, ln_in_g, ln_in_b, w_in, ssd_conv_w, ssd_conv_b, ssd_dt_bias, ssd_a_log, ssd_d, ssd_norm_g, s5_lam_re, s5_lam_im, s5_log_dt, s5_b_re, s5_b_im, s5_c_re, s5_c_im, s5_d, s5_glu_w, s5_glu_b, gdn_conv_w, gdn_conv_b, gdn_a_log, gdn_dt_bias, gdn_norm_g, hg_lb_raw, hg_norm_g, w_out, ln_g, ln_b):
    p = dict(w_in=w_in, ssd_conv_w=ssd_conv_w, ssd_conv_b=ssd_conv_b, ssd_dt_bias=ssd_dt_bias,
             ssd_a_log=ssd_a_log, ssd_d=ssd_d, ssd_norm_g=ssd_norm_g, s5_lam_re=s5_lam_re,
             s5_lam_im=s5_lam_im, s5_log_dt=s5_log_dt, s5_b_re=s5_b_re, s5_b_im=s5_b_im, s5_c_re=s5_c_re,
             s5_c_im=s5_c_im, s5_d=s5_d, s5_glu_w=s5_glu_w, s5_glu_b=s5_glu_b, gdn_conv_w=gdn_conv_w,
             gdn_conv_b=gdn_conv_b, gdn_a_log=gdn_a_log, gdn_dt_bias=gdn_dt_bias, gdn_norm_g=gdn_norm_g,
             hg_norm_g=hg_norm_g, w_out=w_out, ln_g=ln_g, ln_b=ln_b)
    bp, seq, _ = x_prompt.shape
    bs, dseq, _ = x_sample.shape
    assert dseq >= CONV_W - 1 and bs % LANES == 0
    t_pad = PAD_FRONT + N_META + seq
    rows_p = 3 * CHUNK
    assert t_pad % rows_p == 0
    n_p = bp * t_pad
    n_s = bs * dseq

    soft = jax.nn.softmax(hg_lb_raw.astype(F32), axis=0)
    csum = jnp.cumsum(soft, axis=0)
    lbs = csum - csum[0]

    xs = jnp.swapaxes(x_sample.astype(F32), 0, 1).reshape(n_s, D_MODEL)

    tm_p = _pick_tile(t_pad, (704, 512, 384, 192))
    tm_s = n_s
    g_in, b_in = ln_in_g.astype(F32)[None], ln_in_b.astype(F32)[None]
    hp = pl.pallas_call(
        _ln_in_prompt_kernel,
        grid=(bp,),
        in_specs=[pl.BlockSpec((None, seq, D_MODEL), lambda b: (b, 0, 0)),
                  pl.BlockSpec((N_META, D_MODEL), lambda b: (0, 0)),
                  pl.BlockSpec((1, D_MODEL), lambda b: (0, 0)), pl.BlockSpec((1, D_MODEL), lambda b: (0, 0))],
        out_specs=pl.BlockSpec((None, t_pad, D_MODEL), lambda b: (b, 0, 0)),
        out_shape=jax.ShapeDtypeStruct((bp, t_pad, D_MODEL), F32),
        compiler_params=pltpu.CompilerParams(dimension_semantics=("arbitrary",), vmem_limit_bytes=VMEM_LIMIT),
        name="ln_in_prompt",
    )(x_prompt.astype(F32), meta_tokens.astype(F32), g_in, b_in).reshape(n_p, D_MODEL)
    hs = _row_tiled_call(_ln_in_kernel, n_s, tm_s, [xs], [g_in, b_in], [D_MODEL], [F32], "ln_in_sample")

    consts_p = _mixer_constants()
    depth = w_in.shape[0]
    wts = _stacked_weights(p, lbs)

    mat = (HEADS, HEAD_DIM, HEAD_DIM)
    zeros = lambda *shape: jnp.zeros((depth,) + shape, F32)
    acc_p = [zeros(bp, *mat), zeros(bp, *mat), zeros(bp, *mat)]
    p_re, p_im = zeros(bp, S5_LANES), zeros(bp, S5_LANES)
    conv_tails = []
    s5_rows = SUBLANES * S5_TIME_BLOCK
    s5_perm = np.zeros((s5_rows, s5_rows), np.float32)
    for b in range(SUBLANES):
        s5_perm[np.arange(S5_TIME_BLOCK) * SUBLANES + b, b * S5_TIME_BLOCK + np.arange(S5_TIME_BLOCK)] = 1.0
    s5_perm = jnp.asarray(s5_perm, BF16)
    seq_last = lambda a: jnp.moveaxis(a.astype(F32), 1, -1)
    st_in = (seq_last(state_ssd), seq_last(state_ssd_conv),
             seq_last(state_s5_re).reshape(depth, S5_LANES, bs), seq_last(state_s5_im).reshape(depth, S5_LANES, bs),
             seq_last(state_gdn), seq_last(state_gdn_conv), seq_last(state_hgrn))
    acc_s = [jnp.zeros(a.shape, F32) for a in st_in]
    out_consts = [wts['w_out'], wts['ln_g'], wts['ln_b']]
    w_cols = WA + WB + WC + WD
    for l in range(depth):
        us, tails = _inproj_conv_call(hp, l, wts, bp, tm_p)
        conv_tails.append(tails[:, SUBLANES - (CONV_W - 1):, :])
        ua, ub, uc, ud = [u.reshape(bp, t_pad, u.shape[1]) for u in us]
        mix, *acc_p = _mixer_call([ua, uc, ud], l, wts, consts_p, acc_p, rows=rows_p)
        mix, p_re, p_im = _s5_prompt_call(ub, l, wts, s5_perm, mix, p_re, p_im)
        skip = PAD_FRONT + N_META if l + 1 == depth else 0
        seq_rows = lambda b: (b, 0, 0)
        layer_blk = lambda a: pl.BlockSpec((None,) + a.shape[1:], lambda b, nd=a.ndim - 1: (l,) + (0,) * nd)
        hp = pl.pallas_call(
            functools.partial(_outproj_seq_kernel, step=rows_p, skip=skip),
            grid=(bp,),
            in_specs=[pl.BlockSpec((None, t_pad, D_MODEL), seq_rows), pl.BlockSpec((None, t_pad, D_MODEL), seq_rows)]
            + [layer_blk(a) for a in out_consts],
            out_specs=pl.BlockSpec((None, t_pad - skip, D_MODEL), seq_rows),
            out_shape=jax.ShapeDtypeStruct((bp, t_pad - skip, D_MODEL), F32),
            compiler_params=pltpu.CompilerParams(dimension_semantics=("arbitrary",), vmem_limit_bytes=VMEM_LIMIT),
            name="outproj_prompt",
        )(mix, hp.reshape(bp, t_pad, D_MODEL), *out_consts)
        if skip:
            y_prompt = hp
        else:
            hp = hp.reshape(n_p, D_MODEL)

        n_wblk = 6
        ut = pl.pallas_call(
            _sample_inproj_kernel,
            grid=(n_wblk,),
            in_specs=[pl.BlockSpec((n_s, D_MODEL), lambda i: (0, 0)),
                      pl.BlockSpec((None, w_cols // n_wblk, D_MODEL), lambda i: (l, i, 0))],
            out_specs=pl.BlockSpec((w_cols // n_wblk, n_s), lambda i: (i, 0)),
            out_shape=jax.ShapeDtypeStruct((w_cols, n_s), F32),
            compiler_params=pltpu.CompilerParams(dimension_semantics=("arbitrary",), vmem_limit_bytes=VMEM_LIMIT),
            name="inproj_sample",
        )(hs, wts['w_in_t'])
        mixt, *acc_s = _sample_mixer_call(ut, l, wts['sample'], st_in, acc_s, n_tok=dseq, n_seq=bs)
        hs = pl.pallas_call(
            _sample_outproj_kernel,
            grid=(1,),
            in_specs=[pl.BlockSpec((D_MODEL, n_s), lambda i: (0, 0)), pl.BlockSpec((n_s, D_MODEL), lambda i: (0, 0)),
                      pl.BlockSpec((None, D_MODEL, D_MODEL), lambda i: (l, 0, 0)),
                      pl.BlockSpec((None, 1, D_MODEL), lambda i: (l, 0, 0)),
                      pl.BlockSpec((None, 1, D_MODEL), lambda i: (l, 0, 0))],
            out_specs=pl.BlockSpec((n_s, D_MODEL), lambda i: (0, 0)),
            out_shape=jax.ShapeDtypeStruct((n_s, D_MODEL), F32),
            compiler_params=pltpu.CompilerParams(dimension_semantics=("arbitrary",), vmem_limit_bytes=VMEM_LIMIT),
            name="outproj_sample",
        )(mixt, hs, *out_consts)

    y_sample = jnp.swapaxes(hs.reshape(dseq, bs, D_MODEL), 0, 1)
    s5_shape = lambda a: a.reshape(depth, -1, S5_GROUPS, S5_STATE)
    p_ssd, p_gdn, p_hg = acc_p
    conv_tails = jnp.stack(conv_tails)
    p_ca, p_cc = conv_tails[..., 0:SSD_XBC], conv_tails[..., SSD_XBC:SSD_XBC + GDN_QKV]
    seq_first = lambda a: jnp.moveaxis(a, -1, 1)
    s_ssd, s_ca, s_re, s_im, s_gdn, s_cc, s_hg = [seq_first(a) for a in acc_s]
    return (y_prompt, y_sample, p_ssd, p_ca, s5_shape(p_re), s5_shape(p_im), p_gdn, p_cc, p_hg,
            s_ssd, s_ca, s5_shape(s_re), s5_shape(s_im), s_gdn, s_cc, s_hg)
```
